```python
import math
import jax, jax.numpy as jnp
from jax import lax
import numpy as np

D_MODEL = 1024
BATCH = 8
SEQ = 2048
DEPTH = 1
DEC_BATCH = 128
DEC_SEQ = 8
PAST_LEN = 16384
PAGE_SIZE = 128

MIX_WIDTH = D_MODEL
A_HEADS = 4
A_HEAD_DIM = MIX_WIDTH // 2 // A_HEADS
A_WIDTH = A_HEADS * A_HEAD_DIM
B_HEADS = 4
B_HEAD_DIM = MIX_WIDTH // 2 // B_HEADS
B_WIDTH = B_HEADS * B_HEAD_DIM
IN_COLS = 4 * A_WIDTH + 4 * B_WIDTH
D_FF = ((-(-8 * D_MODEL // 3)) + 255) // 256 * 256
PLE_DIM = 256
CHUNK = 32
ROPE_BASE = 10000.0
NORM_EPS = 1e-5
DN_ALPHA = (2.0 * DEPTH) ** 0.25
DN_BETA = (8.0 * DEPTH) ** -0.25

kernel_name = "hymba_hgrn2_retention_deepnorm_step"


def layer_norm(x, g, b):
    x32 = x.astype(jnp.float32)
    mu = jnp.mean(x32, -1, keepdims=True)
    var = jnp.mean(jnp.square(x32 - mu), -1, keepdims=True)
    return ((x32 - mu) * lax.rsqrt(var + NORM_EPS) * g + b).astype(x.dtype)


def rope(x, pos):
    half = x.shape[-1] // 2
    inv = ROPE_BASE ** (-jnp.arange(half, dtype=jnp.float32) / half)
    ang = pos[:, None] * inv[None, :]
    cos = jnp.cos(ang)[None, :, None, :]
    sin = jnp.sin(ang)[None, :, None, :]
    x1, x2 = x[..., :half], x[..., half:]
    return jnp.concatenate([x1 * cos - x2 * sin, x1 * sin + x2 * cos], -1)


def gated_recurrence(q, k, v, logf, s0):
    Bn, L, H, K = q.shape
    V = v.shape[-1]
    C = math.gcd(L, CHUNK)
    N = L // C
    rs = lambda a: a.reshape(Bn, N, C, H, a.shape[-1])
    q, k, v, logf = rs(q), rs(k), rs(v), rs(logf)
    b = jnp.cumsum(logf, axis=2)
    b_last = b[:, :, -1:]
    q_dec = q * jnp.exp(b)
    k_dec = k * jnp.exp(-b)
    k_to_end = k * jnp.exp(b_last - b)
    causal = jnp.tril(jnp.ones((C, C), dtype=bool))
    scores = jnp.einsum('bnthk,bnshk->bnhts', q_dec, k_dec)
    scores = jnp.where(causal, scores, 0.0)
    intra = jnp.einsum('bnhts,bnshv->bnthv', scores, v)
    chunk_kv = jnp.einsum('bnshk,bnshv->bnhkv', k_to_end, v)
    chunk_decay = jnp.exp(b_last[:, :, 0])

    def step(s, inp):
        dec, kv, qd = inp
        inter = jnp.einsum('bchk,bhkv->bchv', qd, s)
        return dec[..., None] * s + kv, inter

    s_final, inter = lax.scan(step, s0, (jnp.moveaxis(chunk_decay, 1, 0), jnp.moveaxis(chunk_kv, 1, 0), jnp.moveaxis(q_dec, 1, 0)))
    inter = jnp.moveaxis(inter, 0, 1)
    o = (intra + inter).reshape(Bn, L, H, V)
    return o, s_final


def token_mixer(x, pos, lb, s_a, s_b, w_in, a_norm_g, b_norm_g, b_norm_b, w_out):
    Bn, L, _ = x.shape
    f32 = jnp.float32
    proj = (x @ w_in).astype(f32)
    cuts = [A_WIDTH, 2 * A_WIDTH, 3 * A_WIDTH, 4 * A_WIDTH, 4 * A_WIDTH + B_WIDTH, 4 * A_WIDTH + 2 * B_WIDTH, 4 * A_WIDTH + 3 * B_WIDTH]
    qa, fa, ia, ga, qb, kb, vb, gb = jnp.split(proj, cuts, axis=-1)
    heads_a = lambda t: t.reshape(Bn, L, A_HEADS, A_HEAD_DIM)
    heads_b = lambda t: t.reshape(Bn, L, B_HEADS, B_HEAD_DIM)

    f = lb + (1.0 - lb) * jax.nn.sigmoid(fa)
    o_a, s_a_new = gated_recurrence(heads_a(jax.nn.silu(qa)), heads_a(1.0 - f), heads_a(ia), heads_a(jnp.log(f)), s_a.astype(f32))
    o_a = o_a * lax.rsqrt(jnp.mean(jnp.square(o_a), -1, keepdims=True) + NORM_EPS) * a_norm_g.reshape(A_HEADS, A_HEAD_DIM)
    o_a = o_a.reshape(Bn, L, A_WIDTH) * jax.nn.silu(ga)

    log_decay = jnp.log1p(-(2.0 ** (-5.0 - jnp.arange(B_HEADS, dtype=f32))))
    qr = rope(heads_b(qb), pos)
    kr = rope(heads_b(kb), pos) * (B_HEAD_DIM ** -0.5)
    logd = jnp.broadcast_to(log_decay[:, None], (Bn, L, B_HEADS, B_HEAD_DIM))
    o_b, s_b_new = gated_recurrence(qr, kr, heads_b(vb), logd, s_b.astype(f32))
    mu = jnp.mean(o_b, -1, keepdims=True)
    var = jnp.mean(jnp.square(o_b - mu), -1, keepdims=True)
    o_b = (o_b - mu) * lax.rsqrt(var + NORM_EPS) * b_norm_g.reshape(B_HEADS, B_HEAD_DIM) + b_norm_b.reshape(B_HEADS, B_HEAD_DIM)
    o_b = o_b.reshape(Bn, L, B_WIDTH) * jax.nn.silu(gb)

    mix = jnp.concatenate([o_a, o_b], -1).astype(x.dtype) @ w_out
    return mix, s_a_new.astype(s_a.dtype), s_b_new.astype(s_b.dtype)


def decoder_layer(x, p, pos, lb, s_a, s_b, w_in, a_norm_g, b_norm_g, b_norm_b, w_out, ln1_g, ln1_b, w_ffn_gate, w_ffn_up, w_ffn_down, ln2_g, ln2_b, w_ple_proj, w_ple_gate, b_ple_gate):
    mix, s_a_new, s_b_new = token_mixer(x, pos, lb, s_a, s_b, w_in, a_norm_g, b_norm_g, b_norm_b, w_out)
    h = layer_norm(DN_ALPHA * x + mix, ln1_g, ln1_b)
    ffn = (jax.nn.silu(h @ w_ffn_gate) * (h @ w_ffn_up)) @ w_ffn_down
    h = layer_norm(DN_ALPHA * h + ffn, ln2_g, ln2_b)
    gate = jax.nn.sigmoid(h @ w_ple_gate + b_ple_gate)
    y = h + gate * (p.astype(h.dtype) @ w_ple_proj)
    return y, s_a_new, s_b_new


def setup_inputs(seed: int = 0) -> dict:
    key = jax.random.key(seed)
    ks = jax.random.split(key, 24)
    nrm = lambda k, shape, scale: jax.random.normal(k, shape, jnp.float32) * scale
    return {
        "x_prompt": nrm(ks[0], (BATCH, SEQ, D_MODEL), 1.0),
        "x_sample": nrm(ks[1], (DEC_BATCH, DEC_SEQ, D_MODEL), 1.0),
        "p_prompt": nrm(ks[2], (DEPTH, BATCH, SEQ, PLE_DIM), 1.0),
        "p_sample": nrm(ks[3], (DEPTH, DEC_BATCH, DEC_SEQ, PLE_DIM), 1.0),
        "state_hgrn": nrm(ks[4], (DEPTH, DEC_BATCH, A_HEADS, A_HEAD_DIM, A_HEAD_DIM), 0.5),
        "state_ret": nrm(ks[5], (DEPTH, DEC_BATCH, B_HEADS, B_HEAD_DIM, B_HEAD_DIM), 0.5),
        "lb_logits": nrm(ks[6], (DEPTH + 1, A_WIDTH), 0.1),
        "w_in": nrm(ks[7], (DEPTH, D_MODEL, IN_COLS), D_MODEL ** -0.5),
        "a_norm_g": 1.0 + nrm(ks[8], (DEPTH, A_WIDTH), 0.01),
        "b_norm_g": 1.0 + nrm(ks[9], (DEPTH, B_WIDTH), 0.01),
        "b_norm_b": nrm(ks[10], (DEPTH, B_WIDTH), 0.01),
        "w_out": nrm(ks[11], (DEPTH, MIX_WIDTH, D_MODEL), DN_BETA * MIX_WIDTH ** -0.5),
        "ln1_g": 1.0 + nrm(ks[12], (DEPTH, D_MODEL), 0.01),
        "ln1_b": nrm(ks[13], (DEPTH, D_MODEL), 0.01),
        "w_ffn_gate": nrm(ks[14], (DEPTH, D_MODEL, D_FF), D_MODEL ** -0.5),
        "w_ffn_up": nrm(ks[15], (DEPTH, D_MODEL, D_FF), D_MODEL ** -0.5),
        "w_ffn_down": nrm(ks[16], (DEPTH, D_FF, D_MODEL), DN_BETA * D_FF ** -0.5),
        "ln2_g": 1.0 + nrm(ks[17], (DEPTH, D_MODEL), 0.01),
        "ln2_b": nrm(ks[18], (DEPTH, D_MODEL), 0.01),
        "w_ple_proj": nrm(ks[19], (DEPTH, PLE_DIM, D_MODEL), PLE_DIM ** -0.5),
        "w_ple_gate": nrm(ks[20], (DEPTH, D_MODEL, D_MODEL), D_MODEL ** -0.5),
        "b_ple_gate": nrm(ks[21], (DEPTH, D_MODEL), 0.01),
    }


def reference(x_prompt, x_sample, p_prompt, p_sample, state_hgrn, state_ret, lb_logits, w_in, a_norm_g, b_norm_g, b_norm_b, w_out, ln1_g, ln1_b, w_ffn_gate, w_ffn_up, w_ffn_down, ln2_g, ln2_b, w_ple_proj, w_ple_gate, b_ple_gate):
    f32 = jnp.float32
    lower_bounds = jnp.cumsum(jax.nn.softmax(lb_logits.astype(f32), axis=0), axis=0)
    n_prompt, len_prompt = x_prompt.shape[0], x_prompt.shape[1]
    len_sample = x_sample.shape[1]
    pos_prompt = jnp.arange(len_prompt, dtype=f32)
    pos_sample = PAST_LEN + jnp.arange(len_sample, dtype=f32)
    zero_a = jnp.zeros((n_prompt, A_HEADS, A_HEAD_DIM, A_HEAD_DIM), state_hgrn.dtype)
    zero_b = jnp.zeros((n_prompt, B_HEADS, B_HEAD_DIM, B_HEAD_DIM), state_ret.dtype)
    hp, hs = x_prompt, x_sample
    sa_p, sb_p, sa_s, sb_s = [], [], [], []
    for i in range(DEPTH):
        lw = (w_in[i], a_norm_g[i], b_norm_g[i], b_norm_b[i], w_out[i], ln1_g[i], ln1_b[i], w_ffn_gate[i], w_ffn_up[i], w_ffn_down[i], ln2_g[i], ln2_b[i], w_ple_proj[i], w_ple_gate[i], b_ple_gate[i])
        hp, a_p, b_p = decoder_layer(hp, p_prompt[i], pos_prompt, lower_bounds[i], zero_a, zero_b, *lw)
        hs, a_s, b_s = decoder_layer(hs, p_sample[i], pos_sample, lower_bounds[i], state_hgrn[i], state_ret[i], *lw)
        sa_p.append(a_p)
        sb_p.append(b_p)
        sa_s.append(a_s)
        sb_s.append(b_s)
    return (hp, hs, jnp.stack(sa_p), jnp.stack(sb_p), jnp.stack(sa_s), jnp.stack(sb_s))
```

```python
import functools
import math

import jax
import jax.numpy as jnp
from jax import lax
from jax.experimental import pallas as pl
from jax.experimental.pallas import tpu as pltpu

F32 = jnp.float32
BF16 = jnp.bfloat16

D_MODEL = 1024
N_HEADS = 4
HEAD_DIM = 128
GROUP_W = N_HEADS * HEAD_DIM
IN_COLS = 8 * GROUP_W
D_FF = 2816
PLE_DIM = 256
DEPTH = 1
PAST_LEN = 16384
REF_CHUNK = 32
ROPE_BASE = 10000.0
NORM_EPS = 1e-5
DN_ALPHA = (2.0 * DEPTH) ** 0.25
RET_LOG_DECAY = tuple(math.log1p(-(2.0 ** (-5.0 - h))) for h in range(N_HEADS))
K_SCALE = HEAD_DIM ** -0.5

V7X_VMEM_LIMIT_BYTES = 56 * 1024 * 1024

PROMPT_TILE = 256
TAIL_TILE = 256
SAMPLE_SEQS = 8
FF_CHUNK = 256


def _dot(a, b):
    return jnp.dot(a, b, preferred_element_type=F32)


def _dot_nt(a, b):
    return lax.dot_general(a, b, (((1,), (1,)), ((), ())), preferred_element_type=F32)


def _dot_tn(a, b):
    return lax.dot_general(a, b, (((0,), (0,)), ((), ())), preferred_element_type=F32)


def _split3(x):
    hi = x.astype(BF16)
    r1 = x - hi.astype(F32)
    mid = r1.astype(BF16)
    lo = (r1 - mid.astype(F32)).astype(BF16)
    return hi, mid, lo


def _dot_exact_lhs01(m01, parts):
    hi, mid, lo = parts
    return _dot(m01, hi) + _dot(m01, mid) + _dot(m01, lo)


def _sigmoid(x):
    return 1.0 / (1.0 + jnp.exp(-x))


def _silu(x):
    return x * _sigmoid(x)


def _chunk_masks(n, shift):
    r = lax.broadcasted_iota(jnp.int32, (n, n), 0)
    c = lax.broadcasted_iota(jnp.int32, (n, n), 1)
    same = (r >> shift) == (c >> shift)
    return same & (c <= r), same


def _lower_bound(lb_ref):
    rows = [lb_ref[i:i + 1, :] for i in range(lb_ref.shape[0])]
    m = functools.reduce(jnp.maximum, rows)
    e = [jnp.exp(r - m) for r in rows]
    return e[0] / functools.reduce(jnp.add, e)


def _hgrn_prepass(proj_ref, lb, shift):
    n = proj_ref.shape[0]
    causal, same = _chunk_masks(n, shift)
    tri = jnp.where(causal, 1.0, 0.0).astype(BF16)
    ones = jnp.where(same, 1.0, 0.0).astype(BF16)
    f = lb + (1.0 - lb) * _sigmoid(proj_ref[:, GROUP_W:2 * GROUP_W])
    logf = jnp.log(f)
    kk = 1.0 - f
    parts = _split3(logf)
    b = _dot_exact_lhs01(tri, parts)
    b_last = _dot_exact_lhs01(ones, parts)
    q_dec = (_silu(proj_ref[:, 0:GROUP_W]) * jnp.exp(b)).astype(BF16)
    k_dec = (kk * jnp.exp(-b)).astype(BF16)
    k_end = (kk * jnp.exp(b_last - b)).astype(BF16)
    return q_dec, k_dec, k_end, b_last, causal


def _rope(x, cos, sin_signed):
    return x * cos + pltpu.roll(x, HEAD_DIM // 2, axis=1) * sin_signed


def _rms_gate(o, g, gate):
    return o * lax.rsqrt(jnp.mean(o * o, axis=-1, keepdims=True) + NORM_EPS) * g * _silu(gate)


def _ln_gate(o, g, b, gate):
    mu = jnp.mean(o, axis=-1, keepdims=True)
    d = o - mu
    var = jnp.mean(d * d, axis=-1, keepdims=True)
    return (d * lax.rsqrt(var + NORM_EPS) * g + b) * _silu(gate)


def _rope_table_kernel(cos_ref, sin_ref, *, offset):
    n = cos_ref.shape[0]
    half = HEAD_DIM // 2
    row = lax.broadcasted_iota(jnp.int32, (n, HEAD_DIM), 0) + pl.program_id(0) * n
    lane = lax.broadcasted_iota(jnp.int32, (n, HEAD_DIM), 1)
    j = (lane & (half - 1)).astype(F32)
    inv = jnp.exp(-(j / half) * math.log(ROPE_BASE))
    ang = (row.astype(F32) + offset) * inv
    cos_ref[...] = jnp.cos(ang)
    s = jnp.sin(ang)
    sin_ref[...] = jnp.where(lane < half, -s, s)


def _rope_tables(n, offset):
    tile = min(n, 512)
    return pl.pallas_call(
        functools.partial(_rope_table_kernel, offset=float(offset)),
        grid=(n // tile,),
        in_specs=[],
        out_specs=[pl.BlockSpec((tile, HEAD_DIM), lambda i: (i, 0))] * 2,
        out_shape=[jax.ShapeDtypeStruct((n, HEAD_DIM), F32)] * 2,
        name="rope_tables",
    )()


def _prompt_mixer_kernel(x_ref, w_in_ref, lb_ref, ag_ref, bg_ref, bb_ref, cos_ref, sin_ref,
                         mix_ref, sa_ref, sb_ref, proj_ref, st_ref, oa_ref):
    tl = x_ref.shape[1]
    j = pl.program_id(1)

    @pl.when(j == 0)
    def _():
        st_ref[...] = jnp.zeros_like(st_ref)
        sb_ref[...] = jnp.zeros_like(sb_ref)

    xb = x_ref[0].astype(BF16)
    for c in range(IN_COLS // GROUP_W):
        cols = slice(c * GROUP_W, (c + 1) * GROUP_W)
        proj_ref[:, cols] = _dot(xb, w_in_ref[:, cols])

    shift = REF_CHUNK.bit_length() - 1
    q_dec, k_dec, k_end, b_last, causal = _hgrn_prepass(proj_ref, _lower_bound(lb_ref), shift)
    v_a = proj_ref[:, 2 * GROUP_W:3 * GROUP_W].astype(BF16)
    for h in range(N_HEADS):
        hs = slice(h * HEAD_DIM, (h + 1) * HEAD_DIM)
        sc = jnp.where(causal, _dot_nt(q_dec[:, hs], k_dec[:, hs]), 0.0).astype(BF16)
        oa_ref[:, hs] = _dot(sc, v_a[:, hs])
    for n in range(tl // REF_CHUNK):
        rows = slice(n * REF_CHUNK, (n + 1) * REF_CHUNK)
        for h in range(N_HEADS):
            hs = slice(h * HEAD_DIM, (h + 1) * HEAD_DIM)
            st = st_ref[h]
            oa_ref[rows, hs] += _dot_nt(q_dec[rows, hs], st.astype(BF16))
            dec = jnp.exp(b_last[n * REF_CHUNK:n * REF_CHUNK + 1, hs])
            st_ref[h] = st * dec + _dot_tn(v_a[rows, hs], k_end[rows, hs])
    ga = proj_ref[:, 3 * GROUP_W:4 * GROUP_W]
    for h in range(N_HEADS):
        hs = slice(h * HEAD_DIM, (h + 1) * HEAD_DIM)
        mix_ref[0, :, hs] = _rms_gate(oa_ref[:, hs], ag_ref[:, hs], ga[:, hs]).astype(BF16)

    @pl.when(j == pl.num_programs(1) - 1)
    def _():
        for h in range(N_HEADS):
            sa_ref[0, h] = st_ref[h].T

    cos = cos_ref[...]
    sin = sin_ref[...]
    r = lax.broadcasted_iota(jnp.int32, (tl, tl), 0)
    c = lax.broadcasted_iota(jnp.int32, (tl, tl), 1)
    diff = (r - c).astype(F32)
    row = lax.broadcasted_iota(jnp.int32, (tl, HEAD_DIM), 0).astype(F32)
    for h in range(N_HEADS):
        hs = slice(h * HEAD_DIM, (h + 1) * HEAD_DIM)
        logd = RET_LOG_DECAY[h]
        q = _rope(proj_ref[:, 4 * GROUP_W + h * HEAD_DIM:4 * GROUP_W + (h + 1) * HEAD_DIM], cos, sin)
        k = _rope(proj_ref[:, 5 * GROUP_W + h * HEAD_DIM:5 * GROUP_W + (h + 1) * HEAD_DIM], cos, sin) * K_SCALE
        v = proj_ref[:, 6 * GROUP_W + h * HEAD_DIM:6 * GROUP_W + (h + 1) * HEAD_DIM].astype(BF16)
        gate = proj_ref[:, 7 * GROUP_W + h * HEAD_DIM:7 * GROUP_W + (h + 1) * HEAD_DIM]
        dmask = jnp.where(r >= c, jnp.exp(diff * logd), 0.0)
        a = (_dot_nt(q.astype(BF16), k.astype(BF16)) * dmask).astype(BF16)
        s = sb_ref[0, h]
        q_dec = (q * jnp.exp((row + 1.0) * logd)).astype(BF16)
        o = _dot(a, v) + _dot(q_dec, s.astype(BF16))
        k_end = (k * jnp.exp((tl - 1.0 - row) * logd)).astype(BF16)
        sb_ref[0, h] = s * math.exp(tl * logd) + _dot_tn(k_end, v)
        mix_ref[0, :, GROUP_W + h * HEAD_DIM:GROUP_W + (h + 1) * HEAD_DIM] = _ln_gate(
            o, bg_ref[:, hs], bb_ref[:, hs], gate).astype(BF16)


def _prompt_mixer(x, w_in, lb_logits, a_g, b_g, b_b, cos, sin):
    bsz, seq, _ = x.shape
    tl = PROMPT_TILE
    const = lambda b, j: (0, 0)
    state_spec = pl.BlockSpec((1, N_HEADS, HEAD_DIM, HEAD_DIM), lambda b, j: (b, 0, 0, 0))
    state_shape = jax.ShapeDtypeStruct((bsz, N_HEADS, HEAD_DIM, HEAD_DIM), F32)
    return pl.pallas_call(
        _prompt_mixer_kernel,
        grid=(bsz, seq // tl),
        in_specs=[
            pl.BlockSpec((1, tl, D_MODEL), lambda b, j: (b, j, 0)),
            pl.BlockSpec((D_MODEL, IN_COLS), const, pipeline_mode=pl.Buffered(1)),
            pl.BlockSpec(lb_logits.shape, const),
            pl.BlockSpec((1, GROUP_W), const),
            pl.BlockSpec((1, GROUP_W), const),
            pl.BlockSpec((1, GROUP_W), const),
            pl.BlockSpec((tl, HEAD_DIM), lambda b, j: (j, 0)),
            pl.BlockSpec((tl, HEAD_DIM), lambda b, j: (j, 0)),
        ],
        out_specs=[
            pl.BlockSpec((1, tl, 2 * GROUP_W), lambda b, j: (b, j, 0)),
            state_spec,
            state_spec,
        ],
        out_shape=[
            jax.ShapeDtypeStruct((bsz, seq, 2 * GROUP_W), BF16),
            state_shape,
            state_shape,
        ],
        scratch_shapes=[
            pltpu.VMEM((tl, IN_COLS), F32),
            pltpu.VMEM((N_HEADS, HEAD_DIM, HEAD_DIM), F32),
            pltpu.VMEM((tl, GROUP_W), F32),
        ],
        compiler_params=pltpu.CompilerParams(
            dimension_semantics=("arbitrary", "arbitrary"),
            vmem_limit_bytes=V7X_VMEM_LIMIT_BYTES),
        name="prompt_mixer",
    )(x, w_in, lb_logits, a_g, b_g, b_b, cos, sin)


def _in_proj_kernel(x_ref, w_ref, o_ref):
    o_ref[...] = _dot(x_ref[...].astype(BF16), w_ref[...])


def _in_proj(x, w_in):
    n = x.shape[0]
    tn = 512
    return pl.pallas_call(
        _in_proj_kernel,
        grid=(IN_COLS // tn,),
        in_specs=[
            pl.BlockSpec((n, D_MODEL), lambda c: (0, 0)),
            pl.BlockSpec((D_MODEL, tn), lambda c: (0, c)),
        ],
        out_specs=pl.BlockSpec((n, tn), lambda c: (0, c)),
        out_shape=jax.ShapeDtypeStruct((n, IN_COLS), F32),
        compiler_params=pltpu.CompilerParams(
            dimension_semantics=("arbitrary",), vmem_limit_bytes=V7X_VMEM_LIMIT_BYTES),
        name="sample_in_proj",
    )(x, w_in)


def _sample_rec_kernel(proj_ref, sa_in_ref, sb_in_ref, lb_ref, ag_ref, bg_ref, bb_ref,
                       cos_ref, sin_ref, mix_ref, sa_ref, sb_ref, oa_ref, ob_ref, *, seq_len):
    rows_n = proj_ref.shape[0]
    n_seq = rows_n // seq_len
    shift = seq_len.bit_length() - 1

    q_dec, k_dec, k_end, b_last, causal = _hgrn_prepass(proj_ref, _lower_bound(lb_ref), shift)
    v_a = proj_ref[:, 2 * GROUP_W:3 * GROUP_W].astype(BF16)
    hi, mid, lo = _split3(jnp.exp(b_last))
    rr = lax.broadcasted_iota(jnp.int32, (rows_n, GROUP_W), 0) & (seq_len - 1)
    zero = jnp.zeros_like(hi)
    dec_rows = jnp.where(rr == 0, hi, jnp.where(rr == 1, mid, jnp.where(rr == 2, lo, zero)))
    ones_blk = jnp.ones((seq_len, HEAD_DIM), BF16)
    for h in range(N_HEADS):
        hs = slice(h * HEAD_DIM, (h + 1) * HEAD_DIM)
        sc = jnp.where(causal, _dot_nt(q_dec[:, hs], k_dec[:, hs]), 0.0).astype(BF16)
        oa_ref[:, hs] = _dot(sc, v_a[:, hs])
    q_dec32 = q_dec.astype(F32)
    k_end32 = k_end.astype(F32)
    v_a32 = v_a.astype(F32)
    dec32 = dec_rows.astype(F32)
    for s in range(n_seq):
        rows = slice(s * seq_len, (s + 1) * seq_len)
        for h in range(N_HEADS):
            hs = slice(h * HEAD_DIM, (h + 1) * HEAD_DIM)
            st = sa_in_ref[s, h]
            oa_ref[rows, hs] += _dot(q_dec32[rows, hs].astype(BF16), st.astype(BF16))
            dec_kv = _dot_tn(dec32[rows, hs].astype(BF16), ones_blk)
            sa_ref[s, h] = st * dec_kv + _dot_tn(k_end32[rows, hs].astype(BF16),
                                                 v_a32[rows, hs].astype(BF16))
    ga = proj_ref[:, 3 * GROUP_W:4 * GROUP_W]
    for h in range(N_HEADS):
        hs = slice(h * HEAD_DIM, (h + 1) * HEAD_DIM)
        mix_ref[:, hs] = _rms_gate(oa_ref[:, hs], ag_ref[:, hs], ga[:, hs]).astype(BF16)

    cos = jnp.concatenate([cos_ref[...]] * n_seq, axis=0)
    sin = jnp.concatenate([sin_ref[...]] * n_seq, axis=0)
    r = lax.broadcasted_iota(jnp.int32, (rows_n, rows_n), 0)
    c = lax.broadcasted_iota(jnp.int32, (rows_n, rows_n), 1)
    diff = ((r & (seq_len - 1)) - (c & (seq_len - 1))).astype(F32)
    row = (lax.broadcasted_iota(jnp.int32, (rows_n, HEAD_DIM), 0) & (seq_len - 1)).astype(F32)
    for h in range(N_HEADS):
        hs = slice(h * HEAD_DIM, (h + 1) * HEAD_DIM)
        logd = RET_LOG_DECAY[h]
        q = _rope(proj_ref[:, 4 * GROUP_W + h * HEAD_DIM:4 * GROUP_W + (h + 1) * HEAD_DIM], cos, sin)
        k = _rope(proj_ref[:, 5 * GROUP_W + h * HEAD_DIM:5 * GROUP_W + (h + 1) * HEAD_DIM], cos, sin) * K_SCALE
        v32 = proj_ref[:, 6 * GROUP_W + h * HEAD_DIM:6 * GROUP_W + (h + 1) * HEAD_DIM]
        gate = proj_ref[:, 7 * GROUP_W + h * HEAD_DIM:7 * GROUP_W + (h + 1) * HEAD_DIM]
        dmask = jnp.where(causal, jnp.exp(diff * logd), 0.0)
        a = (_dot_nt(q.astype(BF16), k.astype(BF16)) * dmask).astype(BF16)
        ob_ref[...] = _dot(a, v32.astype(BF16))
        q_dec_b = q * jnp.exp((row + 1.0) * logd)
        k_end_b = k * jnp.exp((seq_len - 1.0 - row) * logd)
        for s in range(n_seq):
            rows = slice(s * seq_len, (s + 1) * seq_len)
            st = sb_in_ref[s, h]
            ob_ref[rows, :] += _dot(q_dec_b[rows].astype(BF16), st.astype(BF16))
            sb_ref[s, h] = st * math.exp(seq_len * logd) + _dot_tn(
                k_end_b[rows].astype(BF16), v32[rows].astype(BF16))
        mix_ref[:, GROUP_W + h * HEAD_DIM:GROUP_W + (h + 1) * HEAD_DIM] = _ln_gate(
            ob_ref[...], bg_ref[:, hs], bb_ref[:, hs], gate).astype(BF16)


def _sample_rec(proj, sa, sb, lb_logits, a_g, b_g, b_b, cos, sin, seq_len):
    n_tok = proj.shape[0]
    n_seq = n_tok // seq_len
    bs = SAMPLE_SEQS
    rows = bs * seq_len
    const = lambda i: (0, 0)
    state_spec = pl.BlockSpec((bs, N_HEADS, HEAD_DIM, HEAD_DIM), lambda i: (i, 0, 0, 0))
    state_shape = jax.ShapeDtypeStruct((n_seq, N_HEADS, HEAD_DIM, HEAD_DIM), F32)
    return pl.pallas_call(
        functools.partial(_sample_rec_kernel, seq_len=seq_len),
        grid=(n_seq // bs,),
        in_specs=[
            pl.BlockSpec((rows, IN_COLS), lambda i: (i, 0)),
            state_spec,
            state_spec,
            pl.BlockSpec(lb_logits.shape, const),
            pl.BlockSpec((1, GROUP_W), const),
            pl.BlockSpec((1, GROUP_W), const),
            pl.BlockSpec((1, GROUP_W), const),
            pl.BlockSpec((seq_len, HEAD_DIM), const),
            pl.BlockSpec((seq_len, HEAD_DIM), const),
        ],
        out_specs=[
            pl.BlockSpec((rows, 2 * GROUP_W), lambda i: (i, 0)),
            state_spec,
            state_spec,
        ],
        out_shape=[
            jax.ShapeDtypeStruct((n_tok, 2 * GROUP_W), BF16),
            state_shape,
            state_shape,
        ],
        scratch_shapes=[
            pltpu.VMEM((rows, GROUP_W), F32),
            pltpu.VMEM((rows, HEAD_DIM), F32),
        ],
        compiler_params=pltpu.CompilerParams(
            dimension_semantics=("arbitrary",), vmem_limit_bytes=V7X_VMEM_LIMIT_BYTES),
        name="sample_recurrence",
    )(proj, sa, sb, lb_logits, a_g, b_g, b_b, cos, sin)


def _layer_norm(x, g, b):
    mu = jnp.mean(x, axis=-1, keepdims=True)
    d = x - mu
    var = jnp.mean(d * d, axis=-1, keepdims=True)
    return d * lax.rsqrt(var + NORM_EPS) * g + b


def _tail_kernel(x_ref, mix_ref, p_ref, w_out_ref, ln1g_ref, ln1b_ref, wg_ref, wu_ref, wd_ref,
                 ln2g_ref, ln2b_ref, wpp_ref, wpg_ref, bpg_ref, y_ref, act_ref):
    h = _layer_norm(DN_ALPHA * x_ref[...] + _dot(mix_ref[...], w_out_ref[...]),
                    ln1g_ref[...], ln1b_ref[...])
    hb = h.astype(BF16)
    for c in range(D_FF // FF_CHUNK):
        cols = slice(c * FF_CHUNK, (c + 1) * FF_CHUNK)
        act_ref[:, cols] = (_silu(_dot(hb, wg_ref[:, cols])) * _dot(hb, wu_ref[:, cols])).astype(BF16)
    ffn = _dot(act_ref[...], wd_ref[...])
    h2 = _layer_norm(DN_ALPHA * h + ffn, ln2g_ref[...], ln2b_ref[...])
    gate = _sigmoid(_dot(h2.astype(BF16), wpg_ref[...]) + bpg_ref[...])
    y_ref[...] = h2 + gate * _dot(p_ref[...].astype(BF16), wpp_ref[...])


def _tail(x, mix, p, w_out, ln1g, ln1b, wg, wu, wd, ln2g, ln2b, wpp, wpg, bpg):
    n = x.shape[0]
    tl = TAIL_TILE
    const = lambda i: (0, 0)
    resident = lambda shape: pl.BlockSpec(shape, const, pipeline_mode=pl.Buffered(1))
    vec = pl.BlockSpec((1, D_MODEL), const)
    return pl.pallas_call(
        _tail_kernel,
        grid=(n // tl,),
        in_specs=[
            pl.BlockSpec((tl, D_MODEL), lambda i: (i, 0)),
            pl.BlockSpec((tl, 2 * GROUP_W), lambda i: (i, 0)),
            pl.BlockSpec((tl, PLE_DIM), lambda i: (i, 0)),
            resident((2 * GROUP_W, D_MODEL)),
            vec, vec,
            resident((D_MODEL, D_FF)),
            resident((D_MODEL, D_FF)),
            resident((D_FF, D_MODEL)),
            vec, vec,
            resident((PLE_DIM, D_MODEL)),
            resident((D_MODEL, D_MODEL)),
            vec,
        ],
        out_specs=pl.BlockSpec((tl, D_MODEL), lambda i: (i, 0)),
        out_shape=jax.ShapeDtypeStruct((n, D_MODEL), F32),
        scratch_shapes=[pltpu.VMEM((tl, D_FF), BF16)],
        compiler_params=pltpu.CompilerParams(
            dimension_semantics=("arbitrary",), vmem_limit_bytes=V7X_VMEM_LIMIT_BYTES),
        name="dense_tail",
    )(x, mix, p, w_out, ln1g, ln1b, wg, wu, wd, ln2g, ln2b, wpp, wpg, bpg)


def kernel(x_prompt, x_sample, p_prompt, p_sample, state_hgrn, state_ret, lb_logits, w_in, a_norm_g, b_norm_g, b_norm_b, w_out, ln1_g, ln1_b, w_ffn_gate, w_ffn_up, w_ffn_down, ln2_g, ln2_b, w_ple_proj, w_ple_gate, b_ple_gate):
    assert w_in.shape[0] == DEPTH == 1
    bsz, seq, _ = x_prompt.shape
    n_dec, dec_seq, _ = x_sample.shape

    w_in_b = w_in[0].astype(BF16)
    tail_w = (w_out[0].astype(BF16), ln1_g, ln1_b, w_ffn_gate[0].astype(BF16),
              w_ffn_up[0].astype(BF16), w_ffn_down[0].astype(BF16), ln2_g, ln2_b,
              w_ple_proj[0].astype(BF16), w_ple_gate[0].astype(BF16), b_ple_gate)
    mixer_vecs = (lb_logits, a_norm_g, b_norm_g, b_norm_b)

    cos_p, sin_p = _rope_tables(seq, 0)
    cos_s, sin_s = _rope_tables(dec_seq, PAST_LEN)

    mix_p, sa_p, sb_p = _prompt_mixer(x_prompt, w_in_b, *mixer_vecs, cos_p, sin_p)
    y_p = _tail(x_prompt.reshape(bsz * seq, D_MODEL), mix_p.reshape(bsz * seq, D_MODEL),
                p_prompt[0].reshape(bsz * seq, PLE_DIM), *tail_w)

    proj_s = _in_proj(x_sample.reshape(n_dec * dec_seq, D_MODEL), w_in_b)
    mix_s, sa_s, sb_s = _sample_rec(proj_s, state_hgrn[0], state_ret[0], *mixer_vecs,
                                    cos_s, sin_s, dec_seq)
    y_s = _tail(x_sample.reshape(n_dec * dec_seq, D_MODEL), mix_s,
                p_sample[0].reshape(n_dec * dec_seq, PLE_DIM), *tail_w)

    return (y_p.reshape(bsz, seq, D_MODEL), y_s.reshape(n_dec, dec_seq, D_MODEL),
            sa_p[None], sb_p[None], sa_s[None], sb_s[None])
```

```python
import functools
import math

import jax
import jax.numpy as jnp
from jax import lax
from jax.experimental import pallas as pl
from jax.experimental.pallas import tpu as pltpu

F32 = jnp.float32
BF16 = jnp.bfloat16

D_MODEL = 1024
N_HEADS = 4
HEAD_DIM = 128
GROUP_W = N_HEADS * HEAD_DIM
IN_COLS = 8 * GROUP_W
D_FF = 2816
PLE_DIM = 256
DEPTH = 1
PAST_LEN = 16384
REF_CHUNK = 32
ROPE_BASE = 10000.0
NORM_EPS = 1e-5
DN_ALPHA = (2.0 * DEPTH) ** 0.25
RET_LOG_DECAY = tuple(math.log1p(-(2.0 ** (-5.0 - h))) for h in range(N_HEADS))
K_SCALE = HEAD_DIM ** -0.5

V7X_VMEM_LIMIT_BYTES = 56 * 1024 * 1024

TOKEN_TILE = 256
SAMPLE_SEQS = 8
FF_CHUNK = 256
DOWN_CHUNK = 256


def _dot(a, b):
    return jnp.dot(a, b, preferred_element_type=F32)


def _dot_nt(a, b):
    return lax.dot_general(a, b, (((1,), (1,)), ((), ())), preferred_element_type=F32)


def _dot_tn(a, b):
    return lax.dot_general(a, b, (((0,), (0,)), ((), ())), preferred_element_type=F32)


def _split3(x):
    hi = x.astype(BF16)
    r1 = x - hi.astype(F32)
    mid = r1.astype(BF16)
    lo = (r1 - mid.astype(F32)).astype(BF16)
    return hi, mid, lo


def _dot_exact_lhs01(m01, parts):
    hi, mid, lo = parts
    return _dot(m01, hi) + _dot(m01, mid) + _dot(m01, lo)


def _sigmoid(x):
    return 1.0 / (1.0 + jnp.exp(-x))


def _silu(x):
    return x * _sigmoid(x)


def _causal_in_chunk(n, shift):
    r = lax.broadcasted_iota(jnp.int32, (n, n), 0)
    c = lax.broadcasted_iota(jnp.int32, (n, n), 1)
    return ((r >> shift) == (c >> shift)) & (c <= r)


def _lower_bound(lb_ref):
    rows = [lb_ref[i:i + 1, :] for i in range(lb_ref.shape[0])]
    m = functools.reduce(jnp.maximum, rows)
    e = [jnp.exp(r - m) for r in rows]
    return e[0] / functools.reduce(jnp.add, e)


def _hgrn_prepass(proj_ref, lb, causal):
    tri = jnp.where(causal, 1.0, 0.0).astype(BF16)
    f = lb + (1.0 - lb) * _sigmoid(proj_ref[:, GROUP_W:2 * GROUP_W])
    kk = 1.0 - f
    b = _dot_exact_lhs01(tri, _split3(jnp.log(f)))
    q_dec = (_silu(proj_ref[:, 0:GROUP_W]) * jnp.exp(b)).astype(BF16)
    k_dec = (kk * jnp.exp(-b)).astype(BF16)
    return q_dec, k_dec, kk, b


def _rope(x, cos, sin_signed):
    return x * cos + pltpu.roll(x, HEAD_DIM // 2, axis=1) * sin_signed


def _rms_gate(o, g, gate):
    return o * lax.rsqrt(jnp.mean(o * o, axis=-1, keepdims=True) + NORM_EPS) * g * _silu(gate)


def _ln_gate(o, g, b, gate):
    mu = jnp.mean(o, axis=-1, keepdims=True)
    d = o - mu
    var = jnp.mean(d * d, axis=-1, keepdims=True)
    return (d * lax.rsqrt(var + NORM_EPS) * g + b) * _silu(gate)


def _layer_norm(x, g, b):
    mu = jnp.mean(x, axis=-1, keepdims=True)
    d = x - mu
    var = jnp.mean(d * d, axis=-1, keepdims=True)
    return d * lax.rsqrt(var + NORM_EPS) * g + b


def _head(h, group=0):
    return slice(group * GROUP_W + h * HEAD_DIM, group * GROUP_W + (h + 1) * HEAD_DIM)


def _rope_table_kernel(cos_ref, sin_ref, *, offset):
    n = cos_ref.shape[0]
    half = HEAD_DIM // 2
    row = lax.broadcasted_iota(jnp.int32, (n, HEAD_DIM), 0) + pl.program_id(0) * n
    lane = lax.broadcasted_iota(jnp.int32, (n, HEAD_DIM), 1)
    j = (lane & (half - 1)).astype(F32)
    inv = jnp.exp(-(j / half) * math.log(ROPE_BASE))
    ang = (row.astype(F32) + offset) * inv
    cos_ref[...] = jnp.cos(ang)
    s = jnp.sin(ang)
    sin_ref[...] = jnp.where(lane < half, -s, s)


def _rope_tables(n, offset):
    tile = min(n, 512)
    return pl.pallas_call(
        functools.partial(_rope_table_kernel, offset=float(offset)),
        grid=(n // tile,),
        in_specs=[],
        out_specs=[pl.BlockSpec((tile, HEAD_DIM), lambda i: (i, 0))] * 2,
        out_shape=[jax.ShapeDtypeStruct((n, HEAD_DIM), F32)] * 2,
        name="rope_tables",
    )()


def _tail_steps(x_ref, mix_ref, p_ref, y_ref, w_out_ref, ln1g_ref, ln1b_ref, wg_ref, wu_ref, wd_ref,
                ln2g_ref, ln2b_ref, wpp_ref, wpg_ref, bpg_ref, act_ref, h_ref, hb_ref):
    def out_proj():
        h = _layer_norm(DN_ALPHA * x_ref[...] + _dot(mix_ref[...], w_out_ref[...]),
                        ln1g_ref[...], ln1b_ref[...])
        h_ref[...] = h
        hb_ref[...] = h.astype(BF16)

    def ff(c):
        cols = slice(c * FF_CHUNK, (c + 1) * FF_CHUNK)
        hb = hb_ref[...]
        act_ref[:, cols] = (_silu(_dot(hb, wg_ref[:, cols])) * _dot(hb, wu_ref[:, cols])).astype(BF16)

    def down(c):
        cols = slice(c * DOWN_CHUNK, (c + 1) * DOWN_CHUNK)
        h_ref[:, cols] = DN_ALPHA * h_ref[:, cols] + _dot(act_ref[...], wd_ref[:, cols])

    def norm2():
        h2 = _layer_norm(h_ref[...], ln2g_ref[...], ln2b_ref[...])
        h_ref[...] = h2
        hb_ref[...] = h2.astype(BF16)

    def ple(c):
        cols = slice(c * DOWN_CHUNK, (c + 1) * DOWN_CHUNK)
        gate = _sigmoid(_dot(hb_ref[...], wpg_ref[:, cols]) + bpg_ref[:, cols])
        y_ref[:, cols] = h_ref[:, cols] + gate * _dot(p_ref[...].astype(BF16), wpp_ref[:, cols])

    return (out_proj,
            [functools.partial(ff, c) for c in range(D_FF // FF_CHUNK)],
            [functools.partial(down, c) for c in range(D_MODEL // DOWN_CHUNK)],
            [norm2] + [functools.partial(ple, c) for c in range(D_MODEL // DOWN_CHUNK)])


def _tail_scratch(tl):
    return [pltpu.VMEM((tl, D_FF), BF16), pltpu.VMEM((tl, D_MODEL), F32), pltpu.VMEM((tl, D_MODEL), BF16)]


def _tail_specs():
    const = lambda i: (0, 0)
    resident = lambda shape: pl.BlockSpec(shape, const, pipeline_mode=pl.Buffered(1))
    vec = pl.BlockSpec((1, D_MODEL), const)
    return [
        resident((2 * GROUP_W, D_MODEL)),
        vec, vec,
        resident((D_MODEL, D_FF)),
        resident((D_MODEL, D_FF)),
        resident((D_FF, D_MODEL)),
        vec, vec,
        resident((PLE_DIM, D_MODEL)),
        resident((D_MODEL, D_MODEL)),
        vec,
    ]


def _in_proj_steps(x_ref, w_in_ref, proj_ref):
    def in_proj(c):
        cols = slice(c * GROUP_W, (c + 1) * GROUP_W)
        proj_ref[:, cols] = _dot(x_ref[...].astype(BF16), w_in_ref[:, cols])

    return [functools.partial(in_proj, c) for c in range(IN_COLS // GROUP_W)]


def _prompt_mixer_steps(cos_ref, sin_ref, lb_ref, ag_ref, bg_ref, bb_ref,
                        sa_ref, sb_ref, proj_ref, st_ref, oa_ref, mix_ref,
                        qd_ref, kd_ref, kk_ref, b_ref, va_ref, first):
    tl = proj_ref.shape[0]
    shift = REF_CHUNK.bit_length() - 1
    st = [None] * N_HEADS

    def prepass():
        q_dec, k_dec, kk, b = _hgrn_prepass(proj_ref, _lower_bound(lb_ref), _causal_in_chunk(tl, shift))
        qd_ref[...] = q_dec
        kd_ref[...] = k_dec
        kk_ref[...] = kk
        b_ref[...] = b
        va_ref[...] = proj_ref[:, 2 * GROUP_W:3 * GROUP_W].astype(BF16)
        for h in range(N_HEADS):
            st[h] = jnp.where(first, 0.0, st_ref[h])

    def diag(h):
        hs = _head(h)
        sc = jnp.where(_causal_in_chunk(tl, shift), _dot_nt(qd_ref[:, hs], kd_ref[:, hs]), 0.0)
        oa_ref[:, hs] = _dot(sc.astype(BF16), va_ref[:, hs])

    def chunk(n):
        rows = slice(n * REF_CHUNK, (n + 1) * REF_CHUNK)
        b_last = b_ref[(n + 1) * REF_CHUNK - 1:(n + 1) * REF_CHUNK, :]
        k_end = (kk_ref[rows, :] * jnp.exp(b_last - b_ref[rows, :])).astype(BF16)
        dec = jnp.exp(b_last)
        for h in range(N_HEADS):
            hs = _head(h)
            oa_ref[rows, hs] += _dot_nt(qd_ref[rows, hs], st[h].astype(BF16))
            st[h] = st[h] * dec[:, hs] + _dot_tn(va_ref[rows, hs], k_end[:, hs])

    def hgrn_out():
        for h in range(N_HEADS):
            hs = _head(h)
            st_ref[h] = st[h]
            sa_ref[0, h] = st[h].T
            mix_ref[:, hs] = _rms_gate(oa_ref[:, hs], ag_ref[:, hs], proj_ref[:, _head(h, 3)]).astype(BF16)

    def ret(h):
        hs = _head(h)
        logd = RET_LOG_DECAY[h]
        cos = cos_ref[...]
        sin = sin_ref[...]
        r = lax.broadcasted_iota(jnp.int32, (tl, tl), 0)
        c = lax.broadcasted_iota(jnp.int32, (tl, tl), 1)
        row = lax.broadcasted_iota(jnp.int32, (tl, HEAD_DIM), 0).astype(F32)
        q = _rope(proj_ref[:, _head(h, 4)], cos, sin)
        k = _rope(proj_ref[:, _head(h, 5)], cos, sin) * K_SCALE
        v = proj_ref[:, _head(h, 6)].astype(BF16)
        dmask = jnp.where(r >= c, jnp.exp((r - c).astype(F32) * logd), 0.0)
        a = (_dot_nt(q.astype(BF16), k.astype(BF16)) * dmask).astype(BF16)
        s = jnp.where(first, 0.0, sb_ref[0, h])
        q_dec_b = (q * jnp.exp((row + 1.0) * logd)).astype(BF16)
        o = _dot(a, v) + _dot(q_dec_b, s.astype(BF16))
        k_end_b = (k * jnp.exp((tl - 1.0 - row) * logd)).astype(BF16)
        sb_ref[0, h] = s * math.exp(tl * logd) + _dot_tn(k_end_b, v)
        mix_ref[:, _head(h, 1)] = _ln_gate(o, bg_ref[:, hs], bb_ref[:, hs],
                                           proj_ref[:, _head(h, 7)]).astype(BF16)

    return (prepass,
            [functools.partial(diag, h) for h in range(N_HEADS)],
            [functools.partial(chunk, n) for n in range(tl // REF_CHUNK)],
            hgrn_out,
            [functools.partial(ret, h) for h in range(N_HEADS)])


def _interleave(a, b):
    out = []
    for i in range(max(len(a), len(b))):
        out += a[i:i + 1] + b[i:i + 1]
    return out


def _prompt_layer_kernel(xn_ref, xp_ref, p_ref, cos_ref, sin_ref, w_in_ref, lb_ref, ag_ref, bg_ref,
                         bb_ref, *rest, tiles_per_seq):
    tail_w = rest[:11]
    y_ref, sa_ref, sb_ref = rest[11:14]
    proj_ref, st_ref, oa_ref, mix_ref, qd_ref, kd_ref, kk_ref, b_ref, va_ref = rest[14:23]
    tail_scratch = rest[23:]
    g = pl.program_id(0)
    n_tiles = pl.num_programs(0) - 1
    slot = lax.rem(g, 2)
    proj_cur = proj_ref.at[slot]
    proj_next = proj_ref.at[1 - slot]

    def mixer_steps():
        return _prompt_mixer_steps(cos_ref, sin_ref, lb_ref, ag_ref, bg_ref, bb_ref,
                                   sa_ref, sb_ref, proj_cur, st_ref, oa_ref, mix_ref,
                                   qd_ref, kd_ref, kk_ref, b_ref, va_ref,
                                   lax.rem(g, tiles_per_seq) == 0)

    def tail_steps():
        return _tail_steps(xp_ref, mix_ref, p_ref, y_ref, *tail_w, *tail_scratch)

    @pl.when(g == 0)
    def _():
        prepass, diag, chunk, hgrn_out, ret = mixer_steps()
        steps = _in_proj_steps(xp_ref, w_in_ref, proj_cur)
        steps += [prepass] + diag + chunk + [hgrn_out] + ret
        steps += _in_proj_steps(xn_ref, w_in_ref, proj_next)
        for step in steps:
            step()

    @pl.when((g > 0) & (g < n_tiles))
    def _():
        prepass, diag, chunk, hgrn_out, ret = mixer_steps()
        out_proj, ff, down, final = tail_steps()
        in_proj = _in_proj_steps(xn_ref, w_in_ref, proj_next)
        steps = [out_proj, in_proj[0], prepass, in_proj[1]]
        steps += _interleave(ff + down, ret + diag + chunk + [hgrn_out])
        steps += _interleave(final, in_proj[2:])
        for step in steps:
            step()

    @pl.when(g == n_tiles)
    def _():
        out_proj, ff, down, final = tail_steps()
        for step in [out_proj] + ff + down + final:
            step()


def _prompt_layer(x, p, cos, sin, w_in, lb_logits, a_g, b_g, b_b, tail_w):
    bsz, seq, _ = x.shape
    tl = TOKEN_TILE
    tps = seq // tl
    n_tiles = bsz * tps
    x2 = x.reshape(bsz * seq, D_MODEL)
    p2 = p.reshape(bsz * seq, PLE_DIM)
    const = lambda g: (0, 0)
    nxt = lambda g: (jnp.minimum(g + 1, n_tiles - 1), 0)
    prev = lambda g: (jnp.maximum(g - 1, 0), 0)
    seq_tile = lambda g: (lax.rem(jnp.minimum(g, n_tiles - 1), tps), 0)
    state_spec = pl.BlockSpec((1, N_HEADS, HEAD_DIM, HEAD_DIM),
                              lambda g: (jnp.minimum(g, n_tiles - 1) // tps, 0, 0, 0))
    state_shape = jax.ShapeDtypeStruct((bsz, N_HEADS, HEAD_DIM, HEAD_DIM), F32)
    return pl.pallas_call(
        functools.partial(_prompt_layer_kernel, tiles_per_seq=tps),
        grid=(n_tiles + 1,),
        in_specs=[
            pl.BlockSpec((tl, D_MODEL), nxt),
            pl.BlockSpec((tl, D_MODEL), prev),
            pl.BlockSpec((tl, PLE_DIM), prev),
            pl.BlockSpec((tl, HEAD_DIM), seq_tile),
            pl.BlockSpec((tl, HEAD_DIM), seq_tile),
            pl.BlockSpec((D_MODEL, IN_COLS), const, pipeline_mode=pl.Buffered(1)),
            pl.BlockSpec(lb_logits.shape, const),
            pl.BlockSpec((1, GROUP_W), const),
            pl.BlockSpec((1, GROUP_W), const),
            pl.BlockSpec((1, GROUP_W), const),
        ] + _tail_specs(),
        out_specs=[
            pl.BlockSpec((tl, D_MODEL), prev),
            state_spec,
            state_spec,
        ],
        out_shape=[
            jax.ShapeDtypeStruct((bsz * seq, D_MODEL), F32),
            state_shape,
            state_shape,
        ],
        scratch_shapes=[
            pltpu.VMEM((2, tl, IN_COLS), F32),
            pltpu.VMEM((N_HEADS, HEAD_DIM, HEAD_DIM), F32),
            pltpu.VMEM((tl, GROUP_W), F32),
            pltpu.VMEM((tl, 2 * GROUP_W), BF16),
            pltpu.VMEM((tl, GROUP_W), BF16),
            pltpu.VMEM((tl, GROUP_W), BF16),
            pltpu.VMEM((tl, GROUP_W), F32),
            pltpu.VMEM((tl, GROUP_W), F32),
            pltpu.VMEM((tl, GROUP_W), BF16),
        ] + _tail_scratch(tl),
        compiler_params=pltpu.CompilerParams(
            dimension_semantics=("arbitrary",), vmem_limit_bytes=V7X_VMEM_LIMIT_BYTES),
        name="prompt_layer",
    )(x2, x2, p2, cos, sin, w_in, lb_logits, a_g, b_g, b_b, *tail_w)


def _in_proj_kernel(x_ref, w_ref, o_ref):
    o_ref[...] = _dot(x_ref[...].astype(BF16), w_ref[...])


def _in_proj(x, w_in):
    n = x.shape[0]
    tn = 512
    return pl.pallas_call(
        _in_proj_kernel,
        grid=(IN_COLS // tn,),
        in_specs=[
            pl.BlockSpec((n, D_MODEL), lambda c: (0, 0)),
            pl.BlockSpec((D_MODEL, tn), lambda c: (0, c)),
        ],
        out_specs=pl.BlockSpec((n, tn), lambda c: (0, c)),
        out_shape=jax.ShapeDtypeStruct((n, IN_COLS), F32),
        compiler_params=pltpu.CompilerParams(
            dimension_semantics=("arbitrary",), vmem_limit_bytes=V7X_VMEM_LIMIT_BYTES),
        name="sample_in_proj",
    )(x, w_in)


def _sample_rec_kernel(proj_ref, sa_in_ref, sb_in_ref, lb_ref, ag_ref, bg_ref, bb_ref,
                       cos_ref, sin_ref, mix_ref, sa_ref, sb_ref, oa_ref, ob_ref, *, seq_len):
    rows_n = proj_ref.shape[0]
    n_seq = rows_n // seq_len
    causal = _causal_in_chunk(rows_n, seq_len.bit_length() - 1)

    q_dec, k_dec, kk, b = _hgrn_prepass(proj_ref, _lower_bound(lb_ref), causal)
    v_a = proj_ref[:, 2 * GROUP_W:3 * GROUP_W]
    for h in range(N_HEADS):
        hs = _head(h)
        sc = jnp.where(causal, _dot_nt(q_dec[:, hs], k_dec[:, hs]), 0.0).astype(BF16)
        oa_ref[:, hs] = _dot(sc, v_a[:, hs].astype(BF16))
    q_dec32 = q_dec.astype(F32)
    rr = lax.broadcasted_iota(jnp.int32, (seq_len, GROUP_W), 0)
    ones_blk = jnp.ones((seq_len, HEAD_DIM), BF16)
    for s in range(n_seq):
        rows = slice(s * seq_len, (s + 1) * seq_len)
        b_last = b[(s + 1) * seq_len - 1:(s + 1) * seq_len, :]
        k_end = (kk[rows] * jnp.exp(b_last - b[rows])).astype(BF16)
        hi, mid, lo = [t.astype(F32) for t in _split3(jnp.exp(b_last))]
        dec_rows = jnp.where(rr == 0, hi, jnp.where(rr == 1, mid, jnp.where(rr == 2, lo, 0.0)))
        dec_rows = dec_rows.astype(BF16)
        for h in range(N_HEADS):
            hs = _head(h)
            st = sa_in_ref[s, h]
            oa_ref[rows, hs] += _dot(q_dec32[rows, hs].astype(BF16), st.astype(BF16))
            dec_kv = _dot_tn(dec_rows[:, hs], ones_blk)
            sa_ref[s, h] = st * dec_kv + _dot_tn(k_end[:, hs], v_a[rows, hs].astype(BF16))
    for h in range(N_HEADS):
        hs = _head(h)
        mix_ref[:, hs] = _rms_gate(oa_ref[:, hs], ag_ref[:, hs], proj_ref[:, _head(h, 3)]).astype(BF16)

    cos = jnp.concatenate([cos_ref[...]] * n_seq, axis=0)
    sin = jnp.concatenate([sin_ref[...]] * n_seq, axis=0)
    r = lax.broadcasted_iota(jnp.int32, (rows_n, rows_n), 0)
    c = lax.broadcasted_iota(jnp.int32, (rows_n, rows_n), 1)
    diff = ((r & (seq_len - 1)) - (c & (seq_len - 1))).astype(F32)
    row = (lax.broadcasted_iota(jnp.int32, (rows_n, HEAD_DIM), 0) & (seq_len - 1)).astype(F32)
    for h in range(N_HEADS):
        hs = _head(h)
        logd = RET_LOG_DECAY[h]
        q = _rope(proj_ref[:, _head(h, 4)], cos, sin)
        k = _rope(proj_ref[:, _head(h, 5)], cos, sin) * K_SCALE
        v32 = proj_ref[:, _head(h, 6)]
        dmask = jnp.where(causal, jnp.exp(diff * logd), 0.0)
        a = (_dot_nt(q.astype(BF16), k.astype(BF16)) * dmask).astype(BF16)
        ob_ref[...] = _dot(a, v32.astype(BF16))
        q_dec_b = q * jnp.exp((row + 1.0) * logd)
        k_end_b = k * jnp.exp((seq_len - 1.0 - row) * logd)
        for s in range(n_seq):
            rows = slice(s * seq_len, (s + 1) * seq_len)
            st = sb_in_ref[s, h]
            ob_ref[rows, :] += _dot(q_dec_b[rows].astype(BF16), st.astype(BF16))
            sb_ref[s, h] = st * math.exp(seq_len * logd) + _dot_tn(
                k_end_b[rows].astype(BF16), v32[rows].astype(BF16))
        mix_ref[:, _head(h, 1)] = _ln_gate(ob_ref[...], bg_ref[:, hs], bb_ref[:, hs],
                                           proj_ref[:, _head(h, 7)]).astype(BF16)


def _sample_rec(proj, sa, sb, lb_logits, a_g, b_g, b_b, cos, sin, seq_len):
    n_tok = proj.shape[0]
    n_seq = n_tok // seq_len
    bs = SAMPLE_SEQS
    rows = bs * seq_len
    const = lambda i: (0, 0)
    state_spec = pl.BlockSpec((bs, N_HEADS, HEAD_DIM, HEAD_DIM), lambda i: (i, 0, 0, 0))
    state_shape = jax.ShapeDtypeStruct((n_seq, N_HEADS, HEAD_DIM, HEAD_DIM), F32)
    return pl.pallas_call(
        functools.partial(_sample_rec_kernel, seq_len=seq_len),
        grid=(n_seq // bs,),
        in_specs=[
            pl.BlockSpec((rows, IN_COLS), lambda i: (i, 0)),
            state_spec,
            state_spec,
            pl.BlockSpec(lb_logits.shape, const),
            pl.BlockSpec((1, GROUP_W), const),
            pl.BlockSpec((1, GROUP_W), const),
            pl.BlockSpec((1, GROUP_W), const),
            pl.BlockSpec((seq_len, HEAD_DIM), const),
            pl.BlockSpec((seq_len, HEAD_DIM), const),
        ],
        out_specs=[
            pl.BlockSpec((rows, 2 * GROUP_W), lambda i: (i, 0)),
            state_spec,
            state_spec,
        ],
        out_shape=[
            jax.ShapeDtypeStruct((n_tok, 2 * GROUP_W), BF16),
            state_shape,
            state_shape,
        ],
        scratch_shapes=[
            pltpu.VMEM((rows, GROUP_W), F32),
            pltpu.VMEM((rows, HEAD_DIM), F32),
        ],
        compiler_params=pltpu.CompilerParams(
            dimension_semantics=("arbitrary",), vmem_limit_bytes=V7X_VMEM_LIMIT_BYTES),
        name="sample_recurrence",
    )(proj, sa, sb, lb_logits, a_g, b_g, b_b, cos, sin)


def _tail_kernel(x_ref, mix_ref, p_ref, *rest):
    tail_w, y_ref, tail_scratch = rest[:11], rest[11], rest[12:]
    out_proj, ff, down, final = _tail_steps(x_ref, mix_ref, p_ref, y_ref, *tail_w, *tail_scratch)
    for step in [out_proj] + ff + down + final:
        step()


def _tail(x, mix, p, tail_w):
    n = x.shape[0]
    tl = TOKEN_TILE
    return pl.pallas_call(
        _tail_kernel,
        grid=(n // tl,),
        in_specs=[
            pl.BlockSpec((tl, D_MODEL), lambda i: (i, 0)),
            pl.BlockSpec((tl, 2 * GROUP_W), lambda i: (i, 0)),
            pl.BlockSpec((tl, PLE_DIM), lambda i: (i, 0)),
        ] + _tail_specs(),
        out_specs=pl.BlockSpec((tl, D_MODEL), lambda i: (i, 0)),
        out_shape=jax.ShapeDtypeStruct((n, D_MODEL), F32),
        scratch_shapes=_tail_scratch(tl),
        compiler_params=pltpu.CompilerParams(
            dimension_semantics=("arbitrary",), vmem_limit_bytes=V7X_VMEM_LIMIT_BYTES),
        name="sample_tail",
    )(x, mix, p, *tail_w)


def kernel(x_prompt, x_sample, p_prompt, p_sample, state_hgrn, state_ret, lb_logits, w_in, a_norm_g, b_norm_g, b_norm_b, w_out, ln1_g, ln1_b, w_ffn_gate, w_ffn_up, w_ffn_down, ln2_g, ln2_b, w_ple_proj, w_ple_gate, b_ple_gate):
    assert w_in.shape[0] == DEPTH == 1
    bsz, seq, _ = x_prompt.shape
    n_dec, dec_seq, _ = x_sample.shape

    w_in_b = w_in[0].astype(BF16)
    tail_w = (w_out[0].astype(BF16), ln1_g, ln1_b, w_ffn_gate[0].astype(BF16),
              w_ffn_up[0].astype(BF16), w_ffn_down[0].astype(BF16), ln2_g, ln2_b,
              w_ple_proj[0].astype(BF16), w_ple_gate[0].astype(BF16), b_ple_gate)
    mixer_vecs = (lb_logits, a_norm_g, b_norm_g, b_norm_b)

    cos_p, sin_p = _rope_tables(seq, 0)
    cos_s, sin_s = _rope_tables(dec_seq, PAST_LEN)

    y_p, sa_p, sb_p = _prompt_layer(x_prompt, p_prompt[0], cos_p, sin_p, w_in_b, *mixer_vecs, tail_w)

    proj_s = _in_proj(x_sample.reshape(n_dec * dec_seq, D_MODEL), w_in_b)
    mix_s, sa_s, sb_s = _sample_rec(proj_s, state_hgrn[0], state_ret[0], *mixer_vecs,
                                    cos_s, sin_s, dec_seq)
    y_s = _tail(x_sample.reshape(n_dec * dec_seq, D_MODEL), mix_s,
                p_sample[0].reshape(n_dec * dec_seq, PLE_DIM), tail_w)

    return (y_p.reshape(bsz, seq, D_MODEL), y_s.reshape(n_dec, dec_seq, D_MODEL),
            sa_p[None], sb_p[None], sa_s[None], sb_s[None])
```

```python
import functools
import math

import jax
import jax.numpy as jnp
from jax import lax
from jax.experimental import pallas as pl
from jax.experimental.pallas import tpu as pltpu

F32 = jnp.float32
BF16 = jnp.bfloat16

D_MODEL = 1024
N_HEADS = 4
HEAD_DIM = 128
GROUP_W = N_HEADS * HEAD_DIM
IN_COLS = 8 * GROUP_W
D_FF = 2816
PLE_DIM = 256
DEPTH = 1
PAST_LEN = 16384
REF_CHUNK = 32
ROPE_BASE = 10000.0
NORM_EPS = 1e-5
DN_ALPHA = (2.0 * DEPTH) ** 0.25
RET_LOG_DECAY = tuple(math.log1p(-(2.0 ** (-5.0 - h))) for h in range(N_HEADS))
K_SCALE = HEAD_DIM ** -0.5

V7X_VMEM_LIMIT_BYTES = 56 * 1024 * 1024

TOKEN_TILE = 256
SAMPLE_SEQS = 8
FF_CHUNK = 256
DOWN_CHUNK = 256


def _dot(a, b):
    return jnp.dot(a, b, preferred_element_type=F32)


def _dot_nt(a, b):
    return lax.dot_general(a, b, (((1,), (1,)), ((), ())), preferred_element_type=F32)


def _dot_tn(a, b):
    return lax.dot_general(a, b, (((0,), (0,)), ((), ())), preferred_element_type=F32)


def _split3(x):
    hi = x.astype(BF16)
    r1 = x - hi.astype(F32)
    mid = r1.astype(BF16)
    lo = (r1 - mid.astype(F32)).astype(BF16)
    return hi, mid, lo


def _dot_exact_lhs01(m01, parts):
    hi, mid, lo = parts
    return _dot(m01, hi) + _dot(m01, mid) + _dot(m01, lo)


def _sigmoid(x):
    return 1.0 / (1.0 + jnp.exp(-x))


def _silu(x):
    return x * _sigmoid(x)


def _causal_in_chunk(n, shift):
    r = lax.broadcasted_iota(jnp.int32, (n, n), 0)
    c = lax.broadcasted_iota(jnp.int32, (n, n), 1)
    return ((r >> shift) == (c >> shift)) & (c <= r)


def _lower_bound(lb_ref):
    rows = [lb_ref[i:i + 1, :] for i in range(lb_ref.shape[0])]
    m = functools.reduce(jnp.maximum, rows)
    e = [jnp.exp(r - m) for r in rows]
    return e[0] / functools.reduce(jnp.add, e)


def _hgrn_prepass(proj_ref, lb, causal):
    tri = jnp.where(causal, 1.0, 0.0).astype(BF16)
    f = lb + (1.0 - lb) * _sigmoid(proj_ref[:, GROUP_W:2 * GROUP_W])
    kk = 1.0 - f
    b = _dot_exact_lhs01(tri, _split3(jnp.log(f)))
    q_dec = (_silu(proj_ref[:, 0:GROUP_W]) * jnp.exp(b)).astype(BF16)
    k_dec = (kk * jnp.exp(-b)).astype(BF16)
    return q_dec, k_dec, kk, b


def _rope(x, cos, sin_signed):
    return x * cos + pltpu.roll(x, HEAD_DIM // 2, axis=1) * sin_signed


def _rms_gate(o, g, gate):
    return o * lax.rsqrt(jnp.mean(o * o, axis=-1, keepdims=True) + NORM_EPS) * g * _silu(gate)


def _ln_gate(o, g, b, gate):
    mu = jnp.mean(o, axis=-1, keepdims=True)
    d = o - mu
    var = jnp.mean(d * d, axis=-1, keepdims=True)
    return (d * lax.rsqrt(var + NORM_EPS) * g + b) * _silu(gate)


def _layer_norm(x, g, b):
    mu = jnp.mean(x, axis=-1, keepdims=True)
    d = x - mu
    var = jnp.mean(d * d, axis=-1, keepdims=True)
    return d * lax.rsqrt(var + NORM_EPS) * g + b


def _head(h, group=0):
    return slice(group * GROUP_W + h * HEAD_DIM, group * GROUP_W + (h + 1) * HEAD_DIM)


def _rope_table_kernel(cos_ref, sin_ref, *, offset):
    n = cos_ref.shape[0]
    half = HEAD_DIM // 2
    row = lax.broadcasted_iota(jnp.int32, (n, HEAD_DIM), 0) + pl.program_id(0) * n
    lane = lax.broadcasted_iota(jnp.int32, (n, HEAD_DIM), 1)
    j = (lane & (half - 1)).astype(F32)
    inv = jnp.exp(-(j / half) * math.log(ROPE_BASE))
    ang = (row.astype(F32) + offset) * inv
    cos_ref[...] = jnp.cos(ang)
    s = jnp.sin(ang)
    sin_ref[...] = jnp.where(lane < half, -s, s)


def _rope_tables(n, offset):
    tile = min(n, 512)
    return pl.pallas_call(
        functools.partial(_rope_table_kernel, offset=float(offset)),
        grid=(n // tile,),
        in_specs=[],
        out_specs=[pl.BlockSpec((tile, HEAD_DIM), lambda i: (i, 0))] * 2,
        out_shape=[jax.ShapeDtypeStruct((n, HEAD_DIM), F32)] * 2,
        name="rope_tables",
    )()


def _tail_steps(x_ref, mix_ref, p_ref, y_ref, w_out_ref, ln1g_ref, ln1b_ref, wg_ref, wu_ref, wd_ref,
                ln2g_ref, ln2b_ref, wpp_ref, wpg_ref, bpg_ref, act_ref, h_ref, hb_ref):
    def out_proj():
        h = _layer_norm(DN_ALPHA * x_ref[...] + _dot(mix_ref[...], w_out_ref[...]),
                        ln1g_ref[...], ln1b_ref[...])
        h_ref[...] = h
        hb_ref[...] = h.astype(BF16)

    def ff(c):
        cols = slice(c * FF_CHUNK, (c + 1) * FF_CHUNK)
        hb = hb_ref[...]
        act_ref[:, cols] = (_silu(_dot(hb, wg_ref[:, cols])) * _dot(hb, wu_ref[:, cols])).astype(BF16)

    def down(c):
        cols = slice(c * DOWN_CHUNK, (c + 1) * DOWN_CHUNK)
        h_ref[:, cols] = DN_ALPHA * h_ref[:, cols] + _dot(act_ref[...], wd_ref[:, cols])

    def norm2():
        h2 = _layer_norm(h_ref[...], ln2g_ref[...], ln2b_ref[...])
        h_ref[...] = h2
        hb_ref[...] = h2.astype(BF16)

    def ple(c):
        cols = slice(c * DOWN_CHUNK, (c + 1) * DOWN_CHUNK)
        gate = _sigmoid(_dot(hb_ref[...], wpg_ref[:, cols]) + bpg_ref[:, cols])
        y_ref[:, cols] = h_ref[:, cols] + gate * _dot(p_ref[...].astype(BF16), wpp_ref[:, cols])

    return (out_proj,
            [functools.partial(ff, c) for c in range(D_FF // FF_CHUNK)],
            [functools.partial(down, c) for c in range(D_MODEL // DOWN_CHUNK)],
            [norm2] + [functools.partial(ple, c) for c in range(D_MODEL // DOWN_CHUNK)])


def _tail_scratch(tl):
    return [pltpu.VMEM((tl, D_FF), BF16), pltpu.VMEM((tl, D_MODEL), F32), pltpu.VMEM((tl, D_MODEL), BF16)]


def _tail_specs():
    const = lambda i: (0, 0)
    resident = lambda shape: pl.BlockSpec(shape, const, pipeline_mode=pl.Buffered(1))
    vec = pl.BlockSpec((1, D_MODEL), const)
    return [
        resident((2 * GROUP_W, D_MODEL)),
        vec, vec,
        resident((D_MODEL, D_FF)),
        resident((D_MODEL, D_FF)),
        resident((D_FF, D_MODEL)),
        vec, vec,
        resident((PLE_DIM, D_MODEL)),
        resident((D_MODEL, D_MODEL)),
        vec,
    ]


def _in_proj_steps(x_ref, w_in_ref, proj_ref):
    def in_proj(c):
        cols = slice(c * GROUP_W, (c + 1) * GROUP_W)
        proj_ref[:, cols] = _dot(x_ref[...].astype(BF16), w_in_ref[:, cols])

    return [functools.partial(in_proj, c) for c in range(IN_COLS // GROUP_W)]


def _prompt_mixer_steps(cos_ref, sin_ref, lb_ref, ag_ref, bg_ref, bb_ref,
                        sa_ref, sb_ref, proj_ref, st_ref, oa_ref, mix_ref,
                        qd_ref, kd_ref, kk_ref, b_ref, va_ref, first):
    tl = proj_ref.shape[0]
    shift = REF_CHUNK.bit_length() - 1
    st = [None] * N_HEADS

    def prepass():
        q_dec, k_dec, kk, b = _hgrn_prepass(proj_ref, _lower_bound(lb_ref), _causal_in_chunk(tl, shift))
        qd_ref[...] = q_dec
        kd_ref[...] = k_dec
        kk_ref[...] = kk
        b_ref[...] = b
        va_ref[...] = proj_ref[:, 2 * GROUP_W:3 * GROUP_W].astype(BF16)
        for h in range(N_HEADS):
            st[h] = jnp.where(first, 0.0, st_ref[h])

    def diag(h):
        hs = _head(h)
        sc = jnp.where(_causal_in_chunk(tl, shift), _dot_nt(qd_ref[:, hs], kd_ref[:, hs]), 0.0)
        oa_ref[:, hs] = _dot(sc.astype(BF16), va_ref[:, hs])

    def chunk(n):
        rows = slice(n * REF_CHUNK, (n + 1) * REF_CHUNK)
        b_last = b_ref[(n + 1) * REF_CHUNK - 1:(n + 1) * REF_CHUNK, :]
        k_end = (kk_ref[rows, :] * jnp.exp(b_last - b_ref[rows, :])).astype(BF16)
        dec = jnp.exp(b_last)
        for h in range(N_HEADS):
            hs = _head(h)
            oa_ref[rows, hs] += _dot_nt(qd_ref[rows, hs], st[h].astype(BF16))
            st[h] = st[h] * dec[:, hs] + _dot_tn(va_ref[rows, hs], k_end[:, hs])

    def hgrn_out():
        for h in range(N_HEADS):
            hs = _head(h)
            st_ref[h] = st[h]
            sa_ref[0, h] = st[h].T
            mix_ref[:, hs] = _rms_gate(oa_ref[:, hs], ag_ref[:, hs], proj_ref[:, _head(h, 3)]).astype(BF16)

    def ret(h):
        hs = _head(h)
        logd = RET_LOG_DECAY[h]
        cos = cos_ref[...]
        sin = sin_ref[...]
        r = lax.broadcasted_iota(jnp.int32, (tl, tl), 0)
        c = lax.broadcasted_iota(jnp.int32, (tl, tl), 1)
        row = lax.broadcasted_iota(jnp.int32, (tl, HEAD_DIM), 0).astype(F32)
        q = _rope(proj_ref[:, _head(h, 4)], cos, sin)
        k = _rope(proj_ref[:, _head(h, 5)], cos, sin) * K_SCALE
        v = proj_ref[:, _head(h, 6)].astype(BF16)
        dmask = jnp.where(r >= c, jnp.exp((r - c).astype(F32) * logd), 0.0)
        a = (_dot_nt(q.astype(BF16), k.astype(BF16)) * dmask).astype(BF16)
        s = jnp.where(first, 0.0, sb_ref[0, h])
        q_dec_b = (q * jnp.exp((row + 1.0) * logd)).astype(BF16)
        o = _dot(a, v) + _dot(q_dec_b, s.astype(BF16))
        k_end_b = (k * jnp.exp((tl - 1.0 - row) * logd)).astype(BF16)
        sb_ref[0, h] = s * math.exp(tl * logd) + _dot_tn(k_end_b, v)
        mix_ref[:, _head(h, 1)] = _ln_gate(o, bg_ref[:, hs], bb_ref[:, hs],
                                           proj_ref[:, _head(h, 7)]).astype(BF16)

    return (prepass,
            [functools.partial(diag, h) for h in range(N_HEADS)],
            [functools.partial(chunk, n) for n in range(tl // REF_CHUNK)],
            hgrn_out,
            [functools.partial(ret, h) for h in range(N_HEADS)])


def _interleave(a, b):
    out = []
    for i in range(max(len(a), len(b))):
        out += a[i:i + 1] + b[i:i + 1]
    return out


def _prompt_layer_kernel(xn_ref, xp_ref, p_ref, cos_ref, sin_ref, w_in_ref, lb_ref, ag_ref, bg_ref,
                         bb_ref, *rest, tiles_per_seq):
    tail_w = rest[:11]
    y_ref, sa_ref, sb_ref = rest[11:14]
    proj_ref, st_ref, oa_ref, mix_ref, qd_ref, kd_ref, kk_ref, b_ref, va_ref = rest[14:23]
    tail_scratch = rest[23:]
    g = pl.program_id(0)
    n_tiles = pl.num_programs(0) - 1
    odd = lax.rem(g, 2) == 1

    def mixer_steps(slot):
        return _prompt_mixer_steps(cos_ref, sin_ref, lb_ref, ag_ref, bg_ref, bb_ref,
                                   sa_ref, sb_ref, proj_ref.at[slot], st_ref, oa_ref, mix_ref,
                                   qd_ref, kd_ref, kk_ref, b_ref, va_ref,
                                   lax.rem(g, tiles_per_seq) == 0)

    def tail_steps():
        return _tail_steps(xp_ref, mix_ref, p_ref, y_ref, *tail_w, *tail_scratch)

    @pl.when(g == 0)
    def _():
        prepass, diag, chunk, hgrn_out, ret = mixer_steps(0)
        steps = _in_proj_steps(xp_ref, w_in_ref, proj_ref.at[0])
        steps += [prepass] + diag + chunk + [hgrn_out] + ret
        steps += _in_proj_steps(xn_ref, w_in_ref, proj_ref.at[1])
        for step in steps:
            step()

    def steady(slot):
        prepass, diag, chunk, hgrn_out, ret = mixer_steps(slot)
        out_proj, ff, down, final = tail_steps()
        in_proj = _in_proj_steps(xn_ref, w_in_ref, proj_ref.at[1 - slot])
        steps = [out_proj, in_proj[0], prepass, in_proj[1]]
        steps += _interleave(ff + down, ret + diag + chunk + [hgrn_out])
        steps += _interleave(final, in_proj[2:])
        for step in steps:
            step()

    in_steady = (g > 0) & (g < n_tiles)
    pl.when(in_steady & odd)(functools.partial(steady, 1))
    pl.when(in_steady & jnp.logical_not(odd))(functools.partial(steady, 0))

    @pl.when(g == n_tiles)
    def _():
        out_proj, ff, down, final = tail_steps()
        for step in [out_proj] + ff + down + final:
            step()


def _prompt_layer(x, p, cos, sin, w_in, lb_logits, a_g, b_g, b_b, tail_w):
    bsz, seq, _ = x.shape
    tl = TOKEN_TILE
    tps = seq // tl
    n_tiles = bsz * tps
    x2 = x.reshape(bsz * seq, D_MODEL)
    p2 = p.reshape(bsz * seq, PLE_DIM)
    const = lambda g: (0, 0)
    nxt = lambda g: (jnp.minimum(g + 1, n_tiles - 1), 0)
    prev = lambda g: (jnp.maximum(g - 1, 0), 0)
    seq_tile = lambda g: (lax.rem(jnp.minimum(g, n_tiles - 1), tps), 0)
    state_spec = pl.BlockSpec((1, N_HEADS, HEAD_DIM, HEAD_DIM),
                              lambda g: (jnp.minimum(g, n_tiles - 1) // tps, 0, 0, 0))
    state_shape = jax.ShapeDtypeStruct((bsz, N_HEADS, HEAD_DIM, HEAD_DIM), F32)
    return pl.pallas_call(
        functools.partial(_prompt_layer_kernel, tiles_per_seq=tps),
        grid=(n_tiles + 1,),
        in_specs=[
            pl.BlockSpec((tl, D_MODEL), nxt),
            pl.BlockSpec((tl, D_MODEL), prev),
            pl.BlockSpec((tl, PLE_DIM), prev),
            pl.BlockSpec((tl, HEAD_DIM), seq_tile),
            pl.BlockSpec((tl, HEAD_DIM), seq_tile),
            pl.BlockSpec((D_MODEL, IN_COLS), const, pipeline_mode=pl.Buffered(1)),
            pl.BlockSpec(lb_logits.shape, const),
            pl.BlockSpec((1, GROUP_W), const),
            pl.BlockSpec((1, GROUP_W), const),
            pl.BlockSpec((1, GROUP_W), const),
        ] + _tail_specs(),
        out_specs=[
            pl.BlockSpec((tl, D_MODEL), prev),
            state_spec,
            state_spec,
        ],
        out_shape=[
            jax.ShapeDtypeStruct((bsz * seq, D_MODEL), F32),
            state_shape,
            state_shape,
        ],
        scratch_shapes=[
            pltpu.VMEM((2, tl, IN_COLS), F32),
            pltpu.VMEM((N_HEADS, HEAD_DIM, HEAD_DIM), F32),
            pltpu.VMEM((tl, GROUP_W), F32),
            pltpu.VMEM((tl, 2 * GROUP_W), BF16),
            pltpu.VMEM((tl, GROUP_W), BF16),
            pltpu.VMEM((tl, GROUP_W), BF16),
            pltpu.VMEM((tl, GROUP_W), F32),
            pltpu.VMEM((tl, GROUP_W), F32),
            pltpu.VMEM((tl, GROUP_W), BF16),
        ] + _tail_scratch(tl),
        compiler_params=pltpu.CompilerParams(
            dimension_semantics=("arbitrary",), vmem_limit_bytes=V7X_VMEM_LIMIT_BYTES),
        name="prompt_layer",
    )(x2, x2, p2, cos, sin, w_in, lb_logits, a_g, b_g, b_b, *tail_w)


def _in_proj_kernel(x_ref, w_ref, o_ref):
    o_ref[...] = _dot(x_ref[...].astype(BF16), w_ref[...])


def _in_proj(x, w_in):
    n = x.shape[0]
    tn = 512
    return pl.pallas_call(
        _in_proj_kernel,
        grid=(IN_COLS // tn,),
        in_specs=[
            pl.BlockSpec((n, D_MODEL), lambda c: (0, 0)),
            pl.BlockSpec((D_MODEL, tn), lambda c: (0, c)),
        ],
        out_specs=pl.BlockSpec((n, tn), lambda c: (0, c)),
        out_shape=jax.ShapeDtypeStruct((n, IN_COLS), F32),
        compiler_params=pltpu.CompilerParams(
            dimension_semantics=("arbitrary",), vmem_limit_bytes=V7X_VMEM_LIMIT_BYTES),
        name="sample_in_proj",
    )(x, w_in)


def _sample_rec_kernel(proj_ref, sa_in_ref, sb_in_ref, lb_ref, ag_ref, bg_ref, bb_ref,
                       cos_ref, sin_ref, mix_ref, sa_ref, sb_ref, oa_ref, ob_ref, *, seq_len):
    rows_n = proj_ref.shape[0]
    n_seq = rows_n // seq_len
    causal = _causal_in_chunk(rows_n, seq_len.bit_length() - 1)

    q_dec, k_dec, kk, b = _hgrn_prepass(proj_ref, _lower_bound(lb_ref), causal)
    v_a = proj_ref[:, 2 * GROUP_W:3 * GROUP_W]
    for h in range(N_HEADS):
        hs = _head(h)
        sc = jnp.where(causal, _dot_nt(q_dec[:, hs], k_dec[:, hs]), 0.0).astype(BF16)
        oa_ref[:, hs] = _dot(sc, v_a[:, hs].astype(BF16))
    q_dec32 = q_dec.astype(F32)
    rr = lax.broadcasted_iota(jnp.int32, (seq_len, GROUP_W), 0)
    ones_blk = jnp.ones((seq_len, HEAD_DIM), BF16)
    for s in range(n_seq):
        rows = slice(s * seq_len, (s + 1) * seq_len)
        b_last = b[(s + 1) * seq_len - 1:(s + 1) * seq_len, :]
        k_end = (kk[rows] * jnp.exp(b_last - b[rows])).astype(BF16)
        hi, mid, lo = [t.astype(F32) for t in _split3(jnp.exp(b_last))]
        dec_rows = jnp.where(rr == 0, hi, jnp.where(rr == 1, mid, jnp.where(rr == 2, lo, 0.0)))
        dec_rows = dec_rows.astype(BF16)
        for h in range(N_HEADS):
            hs = _head(h)
            st = sa_in_ref[s, h]
            oa_ref[rows, hs] += _dot(q_dec32[rows, hs].astype(BF16), st.astype(BF16))
            dec_kv = _dot_tn(dec_rows[:, hs], ones_blk)
            sa_ref[s, h] = st * dec_kv + _dot_tn(k_end[:, hs], v_a[rows, hs].astype(BF16))
    for h in range(N_HEADS):
        hs = _head(h)
        mix_ref[:, hs] = _rms_gate(oa_ref[:, hs], ag_ref[:, hs], proj_ref[:, _head(h, 3)]).astype(BF16)

    cos = jnp.concatenate([cos_ref[...]] * n_seq, axis=0)
    sin = jnp.concatenate([sin_ref[...]] * n_seq, axis=0)
    r = lax.broadcasted_iota(jnp.int32, (rows_n, rows_n), 0)
    c = lax.broadcasted_iota(jnp.int32, (rows_n, rows_n), 1)
    diff = ((r & (seq_len - 1)) - (c & (seq_len - 1))).astype(F32)
    row = (lax.broadcasted_iota(jnp.int32, (rows_n, HEAD_DIM), 0) & (seq_len - 1)).astype(F32)
    for h in range(N_HEADS):
        hs = _head(h)
        logd = RET_LOG_DECAY[h]
        q = _rope(proj_ref[:, _head(h, 4)], cos, sin)
        k = _rope(proj_ref[:, _head(h, 5)], cos, sin) * K_SCALE
        v32 = proj_ref[:, _head(h, 6)]
        dmask = jnp.where(causal, jnp.exp(diff * logd), 0.0)
        a = (_dot_nt(q.astype(BF16), k.astype(BF16)) * dmask).astype(BF16)
        ob_ref[...] = _dot(a, v32.astype(BF16))
        q_dec_b = q * jnp.exp((row + 1.0) * logd)
        k_end_b = k * jnp.exp((seq_len - 1.0 - row) * logd)
        for s in range(n_seq):
            rows = slice(s * seq_len, (s + 1) * seq_len)
            st = sb_in_ref[s, h]
            ob_ref[rows, :] += _dot(q_dec_b[rows].astype(BF16), st.astype(BF16))
            sb_ref[s, h] = st * math.exp(seq_len * logd) + _dot_tn(
                k_end_b[rows].astype(BF16), v32[rows].astype(BF16))
        mix_ref[:, _head(h, 1)] = _ln_gate(ob_ref[...], bg_ref[:, hs], bb_ref[:, hs],
                                           proj_ref[:, _head(h, 7)]).astype(BF16)


def _sample_rec(proj, sa, sb, lb_logits, a_g, b_g, b_b, cos, sin, seq_len):
    n_tok = proj.shape[0]
    n_seq = n_tok // seq_len
    bs = SAMPLE_SEQS
    rows = bs * seq_len
    const = lambda i: (0, 0)
    state_spec = pl.BlockSpec((bs, N_HEADS, HEAD_DIM, HEAD_DIM), lambda i: (i, 0, 0, 0))
    state_shape = jax.ShapeDtypeStruct((n_seq, N_HEADS, HEAD_DIM, HEAD_DIM), F32)
    return pl.pallas_call(
        functools.partial(_sample_rec_kernel, seq_len=seq_len),
        grid=(n_seq // bs,),
        in_specs=[
            pl.BlockSpec((rows, IN_COLS), lambda i: (i, 0)),
            state_spec,
            state_spec,
            pl.BlockSpec(lb_logits.shape, const),
            pl.BlockSpec((1, GROUP_W), const),
            pl.BlockSpec((1, GROUP_W), const),
            pl.BlockSpec((1, GROUP_W), const),
            pl.BlockSpec((seq_len, HEAD_DIM), const),
            pl.BlockSpec((seq_len, HEAD_DIM), const),
        ],
        out_specs=[
            pl.BlockSpec((rows, 2 * GROUP_W), lambda i: (i, 0)),
            state_spec,
            state_spec,
        ],
        out_shape=[
            jax.ShapeDtypeStruct((n_tok, 2 * GROUP_W), BF16),
            state_shape,
            state_shape,
        ],
        scratch_shapes=[
            pltpu.VMEM((rows, GROUP_W), F32),
            pltpu.VMEM((rows, HEAD_DIM), F32),
        ],
        compiler_params=pltpu.CompilerParams(
            dimension_semantics=("arbitrary",), vmem_limit_bytes=V7X_VMEM_LIMIT_BYTES),
        name="sample_recurrence",
    )(proj, sa, sb, lb_logits, a_g, b_g, b_b, cos, sin)


def _tail_kernel(x_ref, mix_ref, p_ref, *rest):
    tail_w, y_ref, tail_scratch = rest[:11], rest[11], rest[12:]
    out_proj, ff, down, final = _tail_steps(x_ref, mix_ref, p_ref, y_ref, *tail_w, *tail_scratch)
    for step in [out_proj] + ff + down + final:
        step()


def _tail(x, mix, p, tail_w):
    n = x.shape[0]
    tl = TOKEN_TILE
    return pl.pallas_call(
        _tail_kernel,
        grid=(n // tl,),
        in_specs=[
            pl.BlockSpec((tl, D_MODEL), lambda i: (i, 0)),
            pl.BlockSpec((tl, 2 * GROUP_W), lambda i: (i, 0)),
            pl.BlockSpec((tl, PLE_DIM), lambda i: (i, 0)),
        ] + _tail_specs(),
        out_specs=pl.BlockSpec((tl, D_MODEL), lambda i: (i, 0)),
        out_shape=jax.ShapeDtypeStruct((n, D_MODEL), F32),
        scratch_shapes=_tail_scratch(tl),
        compiler_params=pltpu.CompilerParams(
            dimension_semantics=("arbitrary",), vmem_limit_bytes=V7X_VMEM_LIMIT_BYTES),
        name="sample_tail",
    )(x, mix, p, *tail_w)


def kernel(x_prompt, x_sample, p_prompt, p_sample, state_hgrn, state_ret, lb_logits, w_in, a_norm_g, b_norm_g, b_norm_b, w_out, ln1_g, ln1_b, w_ffn_gate, w_ffn_up, w_ffn_down, ln2_g, ln2_b, w_ple_proj, w_ple_gate, b_ple_gate):
    assert w_in.shape[0] == DEPTH == 1
    bsz, seq, _ = x_prompt.shape
    n_dec, dec_seq, _ = x_sample.shape

    w_in_b = w_in[0].astype(BF16)
    tail_w = (w_out[0].astype(BF16), ln1_g, ln1_b, w_ffn_gate[0].astype(BF16),
              w_ffn_up[0].astype(BF16), w_ffn_down[0].astype(BF16), ln2_g, ln2_b,
              w_ple_proj[0].astype(BF16), w_ple_gate[0].astype(BF16), b_ple_gate)
    mixer_vecs = (lb_logits, a_norm_g, b_norm_g, b_norm_b)

    cos_p, sin_p = _rope_tables(seq, 0)
    cos_s, sin_s = _rope_tables(dec_seq, PAST_LEN)

    y_p, sa_p, sb_p = _prompt_layer(x_prompt, p_prompt[0], cos_p, sin_p, w_in_b, *mixer_vecs, tail_w)

    proj_s = _in_proj(x_sample.reshape(n_dec * dec_seq, D_MODEL), w_in_b)
    mix_s, sa_s, sb_s = _sample_rec(proj_s, state_hgrn[0], state_ret[0], *mixer_vecs,
                                    cos_s, sin_s, dec_seq)
    y_s = _tail(x_sample.reshape(n_dec * dec_seq, D_MODEL), mix_s,
                p_sample[0].reshape(n_dec * dec_seq, PLE_DIM), tail_w)

    return (y_p.reshape(bsz, seq, D_MODEL), y_s.reshape(n_dec, dec_seq, D_MODEL),
            sa_p[None], sb_p[None], sa_s[None], sb_s[None])
```

```python
import functools
import math

import jax
import jax.numpy as jnp
from jax import lax
from jax.experimental import pallas as pl
from jax.experimental.pallas import tpu as pltpu

F32 = jnp.float32
BF16 = jnp.bfloat16

D_MODEL = 1024
N_HEADS = 4
HEAD_DIM = 128
GROUP_W = N_HEADS * HEAD_DIM
IN_COLS = 8 * GROUP_W
D_FF = 2816
PLE_DIM = 256
DEPTH = 1
PAST_LEN = 16384
REF_CHUNK = 32
ROPE_BASE = 10000.0
NORM_EPS = 1e-5
DN_ALPHA = (2.0 * DEPTH) ** 0.25
RET_LOG_DECAY = tuple(math.log1p(-(2.0 ** (-5.0 - h))) for h in range(N_HEADS))
K_SCALE = HEAD_DIM ** -0.5

V7X_VMEM_LIMIT_BYTES = 56 * 1024 * 1024

TOKEN_TILE = 256
SAMPLE_SEQS = 8
FF_CHUNK = 256
DOWN_CHUNK = 256


def _dot(a, b):
    return jnp.dot(a, b, preferred_element_type=F32)


def _dot_nt(a, b):
    return lax.dot_general(a, b, (((1,), (1,)), ((), ())), preferred_element_type=F32)


def _dot_tn(a, b):
    return lax.dot_general(a, b, (((0,), (0,)), ((), ())), preferred_element_type=F32)


def _split3(x):
    hi = x.astype(BF16)
    r1 = x - hi.astype(F32)
    mid = r1.astype(BF16)
    lo = (r1 - mid.astype(F32)).astype(BF16)
    return hi, mid, lo


def _dot_exact_lhs01(m01, parts):
    hi, mid, lo = parts
    return _dot(m01, hi) + _dot(m01, mid) + _dot(m01, lo)


def _sigmoid(x):
    return 1.0 / (1.0 + jnp.exp(-x))


def _silu(x):
    return x * _sigmoid(x)


def _causal_in_chunk(n, shift):
    r = lax.broadcasted_iota(jnp.int32, (n, n), 0)
    c = lax.broadcasted_iota(jnp.int32, (n, n), 1)
    return ((r >> shift) == (c >> shift)) & (c <= r)


def _lower_bound(lb_ref):
    rows = [lb_ref[i:i + 1, :] for i in range(lb_ref.shape[0])]
    m = functools.reduce(jnp.maximum, rows)
    e = [jnp.exp(r - m) for r in rows]
    return e[0] / functools.reduce(jnp.add, e)


def _hgrn_prepass(proj_ref, lb, causal):
    tri = jnp.where(causal, 1.0, 0.0).astype(BF16)
    f = lb + (1.0 - lb) * _sigmoid(proj_ref[:, GROUP_W:2 * GROUP_W])
    kk = 1.0 - f
    b = _dot_exact_lhs01(tri, _split3(jnp.log(f)))
    q_dec = (_silu(proj_ref[:, 0:GROUP_W]) * jnp.exp(b)).astype(BF16)
    k_dec = (kk * jnp.exp(-b)).astype(BF16)
    return q_dec, k_dec, kk, b


def _rope(x, cos, sin_signed):
    return x * cos + pltpu.roll(x, HEAD_DIM // 2, axis=1) * sin_signed


def _rms_gate(o, g, gate):
    return o * lax.rsqrt(jnp.mean(o * o, axis=-1, keepdims=True) + NORM_EPS) * g * _silu(gate)


def _ln_gate(o, g, b, gate):
    mu = jnp.mean(o, axis=-1, keepdims=True)
    d = o - mu
    var = jnp.mean(d * d, axis=-1, keepdims=True)
    return (d * lax.rsqrt(var + NORM_EPS) * g + b) * _silu(gate)


def _layer_norm(x, g, b):
    mu = jnp.mean(x, axis=-1, keepdims=True)
    d = x - mu
    var = jnp.mean(d * d, axis=-1, keepdims=True)
    return d * lax.rsqrt(var + NORM_EPS) * g + b


def _head(h, group=0):
    return slice(group * GROUP_W + h * HEAD_DIM, group * GROUP_W + (h + 1) * HEAD_DIM)


def _rope_table_kernel(cos_ref, sin_ref, *, offset):
    n = cos_ref.shape[0]
    half = HEAD_DIM // 2
    row = lax.broadcasted_iota(jnp.int32, (n, HEAD_DIM), 0) + pl.program_id(0) * n
    lane = lax.broadcasted_iota(jnp.int32, (n, HEAD_DIM), 1)
    j = (lane & (half - 1)).astype(F32)
    inv = jnp.exp(-(j / half) * math.log(ROPE_BASE))
    ang = (row.astype(F32) + offset) * inv
    cos_ref[...] = jnp.cos(ang)
    s = jnp.sin(ang)
    sin_ref[...] = jnp.where(lane < half, -s, s)


def _rope_tables(n, offset):
    tile = min(n, 512)
    return pl.pallas_call(
        functools.partial(_rope_table_kernel, offset=float(offset)),
        grid=(n // tile,),
        in_specs=[],
        out_specs=[pl.BlockSpec((tile, HEAD_DIM), lambda i: (i, 0))] * 2,
        out_shape=[jax.ShapeDtypeStruct((n, HEAD_DIM), F32)] * 2,
        name="rope_tables",
    )()


def _tail_steps(x_ref, mix_ref, p_ref, y_ref, w_out_ref, ln1g_ref, ln1b_ref, wg_ref, wu_ref, wd_ref,
                ln2g_ref, ln2b_ref, wpp_ref, wpg_ref, bpg_ref, act_ref, h_ref, hb_ref):
    def out_proj():
        h = _layer_norm(DN_ALPHA * x_ref[...] + _dot(mix_ref[...], w_out_ref[...]),
                        ln1g_ref[...], ln1b_ref[...])
        h_ref[...] = h
        hb_ref[...] = h.astype(BF16)

    def ff(c):
        cols = slice(c * FF_CHUNK, (c + 1) * FF_CHUNK)
        hb = hb_ref[...]
        act_ref[:, cols] = (_silu(_dot(hb, wg_ref[:, cols])) * _dot(hb, wu_ref[:, cols])).astype(BF16)

    def down(c):
        cols = slice(c * DOWN_CHUNK, (c + 1) * DOWN_CHUNK)
        h_ref[:, cols] = DN_ALPHA * h_ref[:, cols] + _dot(act_ref[...], wd_ref[:, cols])

    def norm2():
        h2 = _layer_norm(h_ref[...], ln2g_ref[...], ln2b_ref[...])
        h_ref[...] = h2
        hb_ref[...] = h2.astype(BF16)

    def ple(c):
        cols = slice(c * DOWN_CHUNK, (c + 1) * DOWN_CHUNK)
        gate = _sigmoid(_dot(hb_ref[...], wpg_ref[:, cols]) + bpg_ref[:, cols])
        y_ref[:, cols] = h_ref[:, cols] + gate * _dot(p_ref[...].astype(BF16), wpp_ref[:, cols])

    return (out_proj,
            [functools.partial(ff, c) for c in range(D_FF // FF_CHUNK)],
            [functools.partial(down, c) for c in range(D_MODEL // DOWN_CHUNK)],
            [norm2] + [functools.partial(ple, c) for c in range(D_MODEL // DOWN_CHUNK)])


def _tail_scratch(tl):
    return [pltpu.VMEM((tl, D_FF), BF16), pltpu.VMEM((tl, D_MODEL), F32), pltpu.VMEM((tl, D_MODEL), BF16)]


def _tail_specs():
    const = lambda i: (0, 0)
    resident = lambda shape: pl.BlockSpec(shape, const, pipeline_mode=pl.Buffered(1))
    vec = pl.BlockSpec((1, D_MODEL), const)
    return [
        resident((2 * GROUP_W, D_MODEL)),
        vec, vec,
        resident((D_MODEL, D_FF)),
        resident((D_MODEL, D_FF)),
        resident((D_FF, D_MODEL)),
        vec, vec,
        resident((PLE_DIM, D_MODEL)),
        resident((D_MODEL, D_MODEL)),
        vec,
    ]


def _in_proj_steps(x_ref, w_in_ref, proj_ref):
    def in_proj(c):
        cols = slice(c * GROUP_W, (c + 1) * GROUP_W)
        proj_ref[:, cols] = _dot(x_ref[...].astype(BF16), w_in_ref[:, cols])

    return [functools.partial(in_proj, c) for c in range(IN_COLS // GROUP_W)]


def _prompt_mixer_steps(cos_ref, sin_ref, lb_ref, ag_ref, bg_ref, bb_ref,
                        sa_ref, sb_ref, proj_ref, st_ref, oa_ref, mix_ref,
                        qd_ref, kd_ref, kk_ref, b_ref, va_ref, first, valid):
    tl = proj_ref.shape[0]
    shift = REF_CHUNK.bit_length() - 1
    st = [None] * N_HEADS

    def prepass():
        q_dec, k_dec, kk, b = _hgrn_prepass(proj_ref, _lower_bound(lb_ref), _causal_in_chunk(tl, shift))
        qd_ref[...] = q_dec
        kd_ref[...] = k_dec
        kk_ref[...] = kk
        b_ref[...] = b
        va_ref[...] = proj_ref[:, 2 * GROUP_W:3 * GROUP_W].astype(BF16)
        for h in range(N_HEADS):
            st[h] = jnp.where(first, 0.0, st_ref[h])

    def diag(h):
        hs = _head(h)
        sc = jnp.where(_causal_in_chunk(tl, shift), _dot_nt(qd_ref[:, hs], kd_ref[:, hs]), 0.0)
        oa_ref[:, hs] = _dot(sc.astype(BF16), va_ref[:, hs])

    def chunk(n):
        rows = slice(n * REF_CHUNK, (n + 1) * REF_CHUNK)
        b_last = b_ref[(n + 1) * REF_CHUNK - 1:(n + 1) * REF_CHUNK, :]
        k_end = (kk_ref[rows, :] * jnp.exp(b_last - b_ref[rows, :])).astype(BF16)
        dec = jnp.exp(b_last)
        for h in range(N_HEADS):
            hs = _head(h)
            oa_ref[rows, hs] += _dot_nt(qd_ref[rows, hs], st[h].astype(BF16))
            st[h] = st[h] * dec[:, hs] + _dot_tn(va_ref[rows, hs], k_end[:, hs])

    def hgrn_out():
        for h in range(N_HEADS):
            hs = _head(h)
            st_ref[h] = st[h]
            sa_ref[0, h] = jnp.where(valid, st[h].T, sa_ref[0, h])
            mix_ref[:, hs] = _rms_gate(oa_ref[:, hs], ag_ref[:, hs], proj_ref[:, _head(h, 3)]).astype(BF16)

    def ret(h):
        hs = _head(h)
        logd = RET_LOG_DECAY[h]
        cos = cos_ref[...]
        sin = sin_ref[...]
        r = lax.broadcasted_iota(jnp.int32, (tl, tl), 0)
        c = lax.broadcasted_iota(jnp.int32, (tl, tl), 1)
        row = lax.broadcasted_iota(jnp.int32, (tl, HEAD_DIM), 0).astype(F32)
        q = _rope(proj_ref[:, _head(h, 4)], cos, sin)
        k = _rope(proj_ref[:, _head(h, 5)], cos, sin) * K_SCALE
        v = proj_ref[:, _head(h, 6)].astype(BF16)
        dmask = jnp.where(r >= c, jnp.exp((r - c).astype(F32) * logd), 0.0)
        a = (_dot_nt(q.astype(BF16), k.astype(BF16)) * dmask).astype(BF16)
        s_old = sb_ref[0, h]
        s = jnp.where(first, 0.0, s_old)
        q_dec_b = (q * jnp.exp((row + 1.0) * logd)).astype(BF16)
        o = _dot(a, v) + _dot(q_dec_b, s.astype(BF16))
        k_end_b = (k * jnp.exp((tl - 1.0 - row) * logd)).astype(BF16)
        sb_ref[0, h] = jnp.where(valid, s * math.exp(tl * logd) + _dot_tn(k_end_b, v), s_old)
        mix_ref[:, _head(h, 1)] = _ln_gate(o, bg_ref[:, hs], bb_ref[:, hs],
                                           proj_ref[:, _head(h, 7)]).astype(BF16)

    return (prepass,
            [functools.partial(diag, h) for h in range(N_HEADS)],
            [functools.partial(chunk, n) for n in range(tl // REF_CHUNK)],
            hgrn_out,
            [functools.partial(ret, h) for h in range(N_HEADS)])


def _interleave(a, b):
    out = []
    for i in range(max(len(a), len(b))):
        out += a[i:i + 1] + b[i:i + 1]
    return out


def _prompt_layer_kernel(xn_ref, xp_ref, p_ref, cos_ref, sin_ref, w_in_ref, lb_ref, ag_ref, bg_ref,
                         bb_ref, *rest, tiles_per_seq):
    tail_w = rest[:11]
    y_ref, sa_ref, sb_ref = rest[11:14]
    proj_ref, st_ref, oa_ref, mix_ref, qd_ref, kd_ref, kk_ref, b_ref, va_ref = rest[14:23]
    tail_scratch = rest[23:]
    g = pl.program_id(0)
    n_tiles = pl.num_programs(0) - 1
    slot = lax.rem(g, 2)

    @pl.when(g == 0)
    def _():
        mix_ref[...] = jnp.zeros_like(mix_ref)
        for step in _in_proj_steps(xp_ref, w_in_ref, proj_ref.at[0]):
            step()

    prepass, diag, chunk, hgrn_out, ret = _prompt_mixer_steps(
        cos_ref, sin_ref, lb_ref, ag_ref, bg_ref, bb_ref, sa_ref, sb_ref, proj_ref.at[slot],
        st_ref, oa_ref, mix_ref, qd_ref, kd_ref, kk_ref, b_ref, va_ref,
        first=lax.rem(g, tiles_per_seq) == 0, valid=g < n_tiles)
    out_proj, ff, down, final = _tail_steps(xp_ref, mix_ref, p_ref, y_ref, *tail_w, *tail_scratch)
    in_proj = _in_proj_steps(xn_ref, w_in_ref, proj_ref.at[1 - slot])
    steps = [out_proj, in_proj[0], prepass, in_proj[1]]
    steps += _interleave(ff + down, ret + diag + chunk + [hgrn_out])
    steps += _interleave(final, in_proj[2:])
    for step in steps:
        step()


def _prompt_layer(x, p, cos, sin, w_in, lb_logits, a_g, b_g, b_b, tail_w):
    bsz, seq, _ = x.shape
    tl = TOKEN_TILE
    tps = seq // tl
    n_tiles = bsz * tps
    x2 = x.reshape(bsz * seq, D_MODEL)
    p2 = p.reshape(bsz * seq, PLE_DIM)
    const = lambda g: (0, 0)
    nxt = lambda g: (jnp.minimum(g + 1, n_tiles - 1), 0)
    prev = lambda g: (jnp.maximum(g - 1, 0), 0)
    seq_tile = lambda g: (lax.rem(jnp.minimum(g, n_tiles - 1), tps), 0)
    state_spec = pl.BlockSpec((1, N_HEADS, HEAD_DIM, HEAD_DIM),
                              lambda g: (jnp.minimum(g, n_tiles - 1) // tps, 0, 0, 0))
    state_shape = jax.ShapeDtypeStruct((bsz, N_HEADS, HEAD_DIM, HEAD_DIM), F32)
    return pl.pallas_call(
        functools.partial(_prompt_layer_kernel, tiles_per_seq=tps),
        grid=(n_tiles + 1,),
        in_specs=[
            pl.BlockSpec((tl, D_MODEL), nxt),
            pl.BlockSpec((tl, D_MODEL), prev),
            pl.BlockSpec((tl, PLE_DIM), prev),
            pl.BlockSpec((tl, HEAD_DIM), seq_tile),
            pl.BlockSpec((tl, HEAD_DIM), seq_tile),
            pl.BlockSpec((D_MODEL, IN_COLS), const, pipeline_mode=pl.Buffered(1)),
            pl.BlockSpec(lb_logits.shape, const),
            pl.BlockSpec((1, GROUP_W), const),
            pl.BlockSpec((1, GROUP_W), const),
            pl.BlockSpec((1, GROUP_W), const),
        ] + _tail_specs(),
        out_specs=[
            pl.BlockSpec((tl, D_MODEL), prev),
            state_spec,
            state_spec,
        ],
        out_shape=[
            jax.ShapeDtypeStruct((bsz * seq, D_MODEL), F32),
            state_shape,
            state_shape,
        ],
        scratch_shapes=[
            pltpu.VMEM((2, tl, IN_COLS), F32),
            pltpu.VMEM((N_HEADS, HEAD_DIM, HEAD_DIM), F32),
            pltpu.VMEM((tl, GROUP_W), F32),
            pltpu.VMEM((tl, 2 * GROUP_W), BF16),
            pltpu.VMEM((tl, GROUP_W), BF16),
            pltpu.VMEM((tl, GROUP_W), BF16),
            pltpu.VMEM((tl, GROUP_W), F32),
            pltpu.VMEM((tl, GROUP_W), F32),
            pltpu.VMEM((tl, GROUP_W), BF16),
        ] + _tail_scratch(tl),
        compiler_params=pltpu.CompilerParams(
            dimension_semantics=("arbitrary",), vmem_limit_bytes=V7X_VMEM_LIMIT_BYTES),
        name="prompt_layer",
    )(x2, x2, p2, cos, sin, w_in, lb_logits, a_g, b_g, b_b, *tail_w)


def _in_proj_kernel(x_ref, w_ref, o_ref):
    o_ref[...] = _dot(x_ref[...].astype(BF16), w_ref[...])


def _in_proj(x, w_in):
    n = x.shape[0]
    tn = 512
    return pl.pallas_call(
        _in_proj_kernel,
        grid=(IN_COLS // tn,),
        in_specs=[
            pl.BlockSpec((n, D_MODEL), lambda c: (0, 0)),
            pl.BlockSpec((D_MODEL, tn), lambda c: (0, c)),
        ],
        out_specs=pl.BlockSpec((n, tn), lambda c: (0, c)),
        out_shape=jax.ShapeDtypeStruct((n, IN_COLS), F32),
        compiler_params=pltpu.CompilerParams(
            dimension_semantics=("arbitrary",), vmem_limit_bytes=V7X_VMEM_LIMIT_BYTES),
        name="sample_in_proj",
    )(x, w_in)


def _sample_rec_kernel(proj_ref, sa_in_ref, sb_in_ref, lb_ref, ag_ref, bg_ref, bb_ref,
                       cos_ref, sin_ref, mix_ref, sa_ref, sb_ref, oa_ref, ob_ref, *, seq_len):
    rows_n = proj_ref.shape[0]
    n_seq = rows_n // seq_len
    causal = _causal_in_chunk(rows_n, seq_len.bit_length() - 1)

    q_dec, k_dec, kk, b = _hgrn_prepass(proj_ref, _lower_bound(lb_ref), causal)
    v_a = proj_ref[:, 2 * GROUP_W:3 * GROUP_W]
    for h in range(N_HEADS):
        hs = _head(h)
        sc = jnp.where(causal, _dot_nt(q_dec[:, hs], k_dec[:, hs]), 0.0).astype(BF16)
        oa_ref[:, hs] = _dot(sc, v_a[:, hs].astype(BF16))
    q_dec32 = q_dec.astype(F32)
    rr = lax.broadcasted_iota(jnp.int32, (seq_len, GROUP_W), 0)
    ones_blk = jnp.ones((seq_len, HEAD_DIM), BF16)
    for s in range(n_seq):
        rows = slice(s * seq_len, (s + 1) * seq_len)
        b_last = b[(s + 1) * seq_len - 1:(s + 1) * seq_len, :]
        k_end = (kk[rows] * jnp.exp(b_last - b[rows])).astype(BF16)
        hi, mid, lo = [t.astype(F32) for t in _split3(jnp.exp(b_last))]
        dec_rows = jnp.where(rr == 0, hi, jnp.where(rr == 1, mid, jnp.where(rr == 2, lo, 0.0)))
        dec_rows = dec_rows.astype(BF16)
        for h in range(N_HEADS):
            hs = _head(h)
            st = sa_in_ref[s, h]
            oa_ref[rows, hs] += _dot(q_dec32[rows, hs].astype(BF16), st.astype(BF16))
            dec_kv = _dot_tn(dec_rows[:, hs], ones_blk)
            sa_ref[s, h] = st * dec_kv + _dot_tn(k_end[:, hs], v_a[rows, hs].astype(BF16))
    for h in range(N_HEADS):
        hs = _head(h)
        mix_ref[:, hs] = _rms_gate(oa_ref[:, hs], ag_ref[:, hs], proj_ref[:, _head(h, 3)]).astype(BF16)

    cos = jnp.concatenate([cos_ref[...]] * n_seq, axis=0)
    sin = jnp.concatenate([sin_ref[...]] * n_seq, axis=0)
    r = lax.broadcasted_iota(jnp.int32, (rows_n, rows_n), 0)
    c = lax.broadcasted_iota(jnp.int32, (rows_n, rows_n), 1)
    diff = ((r & (seq_len - 1)) - (c & (seq_len - 1))).astype(F32)
    row = (lax.broadcasted_iota(jnp.int32, (rows_n, HEAD_DIM), 0) & (seq_len - 1)).astype(F32)
    for h in range(N_HEADS):
        hs = _head(h)
        logd = RET_LOG_DECAY[h]
        q = _rope(proj_ref[:, _head(h, 4)], cos, sin)
        k = _rope(proj_ref[:, _head(h, 5)], cos, sin) * K_SCALE
        v32 = proj_ref[:, _head(h, 6)]
        dmask = jnp.where(causal, jnp.exp(diff * logd), 0.0)
        a = (_dot_nt(q.astype(BF16), k.astype(BF16)) * dmask).astype(BF16)
        ob_ref[...] = _dot(a, v32.astype(BF16))
        q_dec_b = q * jnp.exp((row + 1.0) * logd)
        k_end_b = k * jnp.exp((seq_len - 1.0 - row) * logd)
        for s in range(n_seq):
            rows = slice(s * seq_len, (s + 1) * seq_len)
            st = sb_in_ref[s, h]
            ob_ref[rows, :] += _dot(q_dec_b[rows].astype(BF16), st.astype(BF16))
            sb_ref[s, h] = st * math.exp(seq_len * logd) + _dot_tn(
                k_end_b[rows].astype(BF16), v32[rows].astype(BF16))
        mix_ref[:, _head(h, 1)] = _ln_gate(ob_ref[...], bg_ref[:, hs], bb_ref[:, hs],
                                           proj_ref[:, _head(h, 7)]).astype(BF16)


def _sample_rec(proj, sa, sb, lb_logits, a_g, b_g, b_b, cos, sin, seq_len):
    n_tok = proj.shape[0]
    n_seq = n_tok // seq_len
    bs = SAMPLE_SEQS
    rows = bs * seq_len
    const = lambda i: (0, 0)
    state_spec = pl.BlockSpec((bs, N_HEADS, HEAD_DIM, HEAD_DIM), lambda i: (i, 0, 0, 0))
    state_shape = jax.ShapeDtypeStruct((n_seq, N_HEADS, HEAD_DIM, HEAD_DIM), F32)
    return pl.pallas_call(
        functools.partial(_sample_rec_kernel, seq_len=seq_len),
        grid=(n_seq // bs,),
        in_specs=[
            pl.BlockSpec((rows, IN_COLS), lambda i: (i, 0)),
            state_spec,
            state_spec,
            pl.BlockSpec(lb_logits.shape, const),
            pl.BlockSpec((1, GROUP_W), const),
            pl.BlockSpec((1, GROUP_W), const),
            pl.BlockSpec((1, GROUP_W), const),
            pl.BlockSpec((seq_len, HEAD_DIM), const),
            pl.BlockSpec((seq_len, HEAD_DIM), const),
        ],
        out_specs=[
            pl.BlockSpec((rows, 2 * GROUP_W), lambda i: (i, 0)),
            state_spec,
            state_spec,
        ],
        out_shape=[
            jax.ShapeDtypeStruct((n_tok, 2 * GROUP_W), BF16),
            state_shape,
            state_shape,
        ],
        scratch_shapes=[
            pltpu.VMEM((rows, GROUP_W), F32),
            pltpu.VMEM((rows, HEAD_DIM), F32),
        ],
        compiler_params=pltpu.CompilerParams(
            dimension_semantics=("arbitrary",), vmem_limit_bytes=V7X_VMEM_LIMIT_BYTES),
        name="sample_recurrence",
    )(proj, sa, sb, lb_logits, a_g, b_g, b_b, cos, sin)


def _tail_kernel(x_ref, mix_ref, p_ref, *rest):
    tail_w, y_ref, tail_scratch = rest[:11], rest[11], rest[12:]
    out_proj, ff, down, final = _tail_steps(x_ref, mix_ref, p_ref, y_ref, *tail_w, *tail_scratch)
    for step in [out_proj] + ff + down + final:
        step()


def _tail(x, mix, p, tail_w):
    n = x.shape[0]
    tl = TOKEN_TILE
    return pl.pallas_call(
        _tail_kernel,
        grid=(n // tl,),
        in_specs=[
            pl.BlockSpec((tl, D_MODEL), lambda i: (i, 0)),
            pl.BlockSpec((tl, 2 * GROUP_W), lambda i: (i, 0)),
            pl.BlockSpec((tl, PLE_DIM), lambda i: (i, 0)),
        ] + _tail_specs(),
        out_specs=pl.BlockSpec((tl, D_MODEL), lambda i: (i, 0)),
        out_shape=jax.ShapeDtypeStruct((n, D_MODEL), F32),
        scratch_shapes=_tail_scratch(tl),
        compiler_params=pltpu.CompilerParams(
            dimension_semantics=("arbitrary",), vmem_limit_bytes=V7X_VMEM_LIMIT_BYTES),
        name="sample_tail",
    )(x, mix, p, *tail_w)


def kernel(x_prompt, x_sample, p_prompt, p_sample, state_hgrn, state_ret, lb_logits, w_in, a_norm_g, b_norm_g, b_norm_b, w_out, ln1_g, ln1_b, w_ffn_gate, w_ffn_up, w_ffn_down, ln2_g, ln2_b, w_ple_proj, w_ple_gate, b_ple_gate):
    assert w_in.shape[0] == DEPTH == 1
    bsz, seq, _ = x_prompt.shape
    n_dec, dec_seq, _ = x_sample.shape

    w_in_b = w_in[0].astype(BF16)
    tail_w = (w_out[0].astype(BF16), ln1_g, ln1_b, w_ffn_gate[0].astype(BF16),
              w_ffn_up[0].astype(BF16), w_ffn_down[0].astype(BF16), ln2_g, ln2_b,
              w_ple_proj[0].astype(BF16), w_ple_gate[0].astype(BF16), b_ple_gate)
    mixer_vecs = (lb_logits, a_norm_g, b_norm_g, b_norm_b)

    cos_p, sin_p = _rope_tables(seq, 0)
    cos_s, sin_s = _rope_tables(dec_seq, PAST_LEN)

    y_p, sa_p, sb_p = _prompt_layer(x_prompt, p_prompt[0], cos_p, sin_p, w_in_b, *mixer_vecs, tail_w)

    proj_s = _in_proj(x_sample.reshape(n_dec * dec_seq, D_MODEL), w_in_b)
    mix_s, sa_s, sb_s = _sample_rec(proj_s, state_hgrn[0], state_ret[0], *mixer_vecs,
                                    cos_s, sin_s, dec_seq)
    y_s = _tail(x_sample.reshape(n_dec * dec_seq, D_MODEL), mix_s,
                p_sample[0].reshape(n_dec * dec_seq, PLE_DIM), tail_w)

    return (y_p.reshape(bsz, seq, D_MODEL), y_s.reshape(n_dec, dec_seq, D_MODEL),
            sa_p[None], sb_p[None], sa_s[None], sb_s[None])
```

```python
import functools
import math

import jax
import jax.numpy as jnp
from jax import lax
from jax.experimental import pallas as pl
from jax.experimental.pallas import tpu as pltpu

F32 = jnp.float32
BF16 = jnp.bfloat16

D_MODEL = 1024
N_HEADS = 4
HEAD_DIM = 128
GROUP_W = N_HEADS * HEAD_DIM
IN_COLS = 8 * GROUP_W
D_FF = 2816
PLE_DIM = 256
DEPTH = 1
PAST_LEN = 16384
REF_CHUNK = 32
ROPE_BASE = 10000.0
NORM_EPS = 1e-5
DN_ALPHA = (2.0 * DEPTH) ** 0.25
RET_LOG_DECAY = tuple(math.log1p(-(2.0 ** (-5.0 - h))) for h in range(N_HEADS))
K_SCALE = HEAD_DIM ** -0.5

V7X_VMEM_LIMIT_BYTES = 58 * 1024 * 1024

TOKEN_TILE = 256
SAMPLE_SEQS = 8
FF_CHUNK = 256
DOWN_CHUNK = 256


def _dot(a, b):
    return jnp.dot(a, b, preferred_element_type=F32)


def _dot_nt(a, b):
    return lax.dot_general(a, b, (((1,), (1,)), ((), ())), preferred_element_type=F32)


def _dot_tn(a, b):
    return lax.dot_general(a, b, (((0,), (0,)), ((), ())), preferred_element_type=F32)


def _split3(x):
    hi = x.astype(BF16)
    r1 = x - hi.astype(F32)
    mid = r1.astype(BF16)
    lo = (r1 - mid.astype(F32)).astype(BF16)
    return hi, mid, lo


def _dot_exact_lhs01(m01, parts):
    hi, mid, lo = parts
    return _dot(m01, hi) + _dot(m01, mid) + _dot(m01, lo)


def _sigmoid(x):
    return 1.0 / (1.0 + jnp.exp(-x))


def _silu(x):
    return x * _sigmoid(x)


def _causal_in_chunk(n, shift):
    r = lax.broadcasted_iota(jnp.int32, (n, n), 0)
    c = lax.broadcasted_iota(jnp.int32, (n, n), 1)
    return ((r >> shift) == (c >> shift)) & (c <= r)


def _lower_bound(lb_ref):
    rows = [lb_ref[i:i + 1, :] for i in range(lb_ref.shape[0])]
    m = functools.reduce(jnp.maximum, rows)
    e = [jnp.exp(r - m) for r in rows]
    return e[0] / functools.reduce(jnp.add, e)


def _hgrn_prepass(proj_ref, lb, causal):
    tri = jnp.where(causal, 1.0, 0.0).astype(BF16)
    f = lb + (1.0 - lb) * _sigmoid(proj_ref[:, GROUP_W:2 * GROUP_W])
    kk = 1.0 - f
    b = _dot_exact_lhs01(tri, _split3(jnp.log(f)))
    q_dec = (_silu(proj_ref[:, 0:GROUP_W]) * jnp.exp(b)).astype(BF16)
    k_dec = (kk * jnp.exp(-b)).astype(BF16)
    return q_dec, k_dec, kk, b


def _rope(x, cos, sin_signed):
    return x * cos + pltpu.roll(x, HEAD_DIM // 2, axis=1) * sin_signed


def _rms_gate(o, g, gate):
    return o * lax.rsqrt(jnp.mean(o * o, axis=-1, keepdims=True) + NORM_EPS) * g * _silu(gate)


def _ln_gate(o, g, b, gate):
    mu = jnp.mean(o, axis=-1, keepdims=True)
    d = o - mu
    var = jnp.mean(d * d, axis=-1, keepdims=True)
    return (d * lax.rsqrt(var + NORM_EPS) * g + b) * _silu(gate)


def _layer_norm(x, g, b):
    mu = jnp.mean(x, axis=-1, keepdims=True)
    d = x - mu
    var = jnp.mean(d * d, axis=-1, keepdims=True)
    return d * lax.rsqrt(var + NORM_EPS) * g + b


def _head(h, group=0):
    return slice(group * GROUP_W + h * HEAD_DIM, group * GROUP_W + (h + 1) * HEAD_DIM)


def _rope_table_kernel(cos_ref, sin_ref, *, offset):
    n = cos_ref.shape[0]
    half = HEAD_DIM // 2
    row = lax.broadcasted_iota(jnp.int32, (n, HEAD_DIM), 0) + pl.program_id(0) * n
    lane = lax.broadcasted_iota(jnp.int32, (n, HEAD_DIM), 1)
    j = (lane & (half - 1)).astype(F32)
    inv = jnp.exp(-(j / half) * math.log(ROPE_BASE))
    ang = (row.astype(F32) + offset) * inv
    cos_ref[...] = jnp.cos(ang)
    s = jnp.sin(ang)
    sin_ref[...] = jnp.where(lane < half, -s, s)


def _rope_tables(n, offset):
    tile = min(n, 512)
    return pl.pallas_call(
        functools.partial(_rope_table_kernel, offset=float(offset)),
        grid=(n // tile,),
        in_specs=[],
        out_specs=[pl.BlockSpec((tile, HEAD_DIM), lambda i: (i, 0))] * 2,
        out_shape=[jax.ShapeDtypeStruct((n, HEAD_DIM), F32)] * 2,
        name="rope_tables",
    )()


def _tail_steps(x_ref, mix_ref, p_ref, y_ref, w_out_ref, ln1g_ref, ln1b_ref, wg_ref, wu_ref, wd_ref,
                ln2g_ref, ln2b_ref, wpp_ref, wpg_ref, bpg_ref, act_ref, h_ref, hb_ref):
    def out_proj():
        h = _layer_norm(DN_ALPHA * x_ref[...] + _dot(mix_ref[...], w_out_ref[...]),
                        ln1g_ref[...], ln1b_ref[...])
        h_ref[...] = h
        hb_ref[...] = h.astype(BF16)

    def ff(c):
        cols = slice(c * FF_CHUNK, (c + 1) * FF_CHUNK)
        hb = hb_ref[...]
        act_ref[:, cols] = (_silu(_dot(hb, wg_ref[:, cols])) * _dot(hb, wu_ref[:, cols])).astype(BF16)

    def down(c):
        cols = slice(c * DOWN_CHUNK, (c + 1) * DOWN_CHUNK)
        h_ref[:, cols] = DN_ALPHA * h_ref[:, cols] + _dot(act_ref[...], wd_ref[:, cols])

    def norm2():
        h2 = _layer_norm(h_ref[...], ln2g_ref[...], ln2b_ref[...])
        h_ref[...] = h2
        hb_ref[...] = h2.astype(BF16)

    def ple(c):
        cols = slice(c * DOWN_CHUNK, (c + 1) * DOWN_CHUNK)
        gate = _sigmoid(_dot(hb_ref[...], wpg_ref[:, cols]) + bpg_ref[:, cols])
        y_ref[:, cols] = h_ref[:, cols] + gate * _dot(p_ref[...].astype(BF16), wpp_ref[:, cols])

    return (out_proj,
            [functools.partial(ff, c) for c in range(D_FF // FF_CHUNK)],
            [functools.partial(down, c) for c in range(D_MODEL // DOWN_CHUNK)],
            [norm2] + [functools.partial(ple, c) for c in range(D_MODEL // DOWN_CHUNK)])


def _tail_scratch(tl):
    return [pltpu.VMEM((tl, D_FF), BF16), pltpu.VMEM((tl, D_MODEL), F32), pltpu.VMEM((tl, D_MODEL), BF16)]


def _tail_specs():
    const = lambda i: (0, 0)
    resident = lambda shape: pl.BlockSpec(shape, const, pipeline_mode=pl.Buffered(1))
    vec = pl.BlockSpec((1, D_MODEL), const)
    return [
        resident((2 * GROUP_W, D_MODEL)),
        vec, vec,
        resident((D_MODEL, D_FF)),
        resident((D_MODEL, D_FF)),
        resident((D_FF, D_MODEL)),
        vec, vec,
        resident((PLE_DIM, D_MODEL)),
        resident((D_MODEL, D_MODEL)),
        vec,
    ]


def _in_proj_steps(x_ref, w_in_ref, proj_ref):
    def in_proj(c):
        cols = slice(c * GROUP_W, (c + 1) * GROUP_W)
        proj_ref[:, cols] = _dot(x_ref[...].astype(BF16), w_in_ref[:, cols])

    return [functools.partial(in_proj, c) for c in range(IN_COLS // GROUP_W)]


def _prompt_mixer_steps(cos_ref, sin_ref, lb_ref, ag_ref, bg_ref, bb_ref,
                        sa_ref, sb_ref, proj_ref, st_ref, oa_ref, mix_ref,
                        qd_ref, kd_ref, ke_ref, dec_ref, va_ref, kv_ref, sbf_ref, first):
    tl = proj_ref.shape[0]
    shift = REF_CHUNK.bit_length() - 1
    n_chunks = tl // REF_CHUNK

    def prepass():
        q_dec, k_dec, kk, b = _hgrn_prepass(proj_ref, _lower_bound(lb_ref), _causal_in_chunk(tl, shift))
        qd_ref[...] = q_dec
        kd_ref[...] = k_dec
        va_ref[...] = proj_ref[:, 2 * GROUP_W:3 * GROUP_W].astype(BF16)
        last = [b[(n + 1) * REF_CHUNK - 1:(n + 1) * REF_CHUNK, :] for n in range(n_chunks)]
        b_last = jnp.concatenate([jnp.broadcast_to(r, (REF_CHUNK, GROUP_W)) for r in last], axis=0)
        k_end = kk * jnp.exp(b_last - b)
        for n in range(n_chunks):
            dec_ref[n:n + 1, :] = jnp.exp(last[n])
        odd = ((lax.broadcasted_iota(jnp.int32, (tl, GROUP_W), 0) >> shift) & 1) == 1
        k_even = jnp.where(odd, 0.0, k_end).astype(BF16)
        k_odd = jnp.where(odd, k_end, 0.0).astype(BF16)
        for h in range(N_HEADS):
            ke_ref[:, 2 * h * HEAD_DIM:(2 * h + 1) * HEAD_DIM] = k_even[:, _head(h)]
            ke_ref[:, (2 * h + 1) * HEAD_DIM:(2 * h + 2) * HEAD_DIM] = k_odd[:, _head(h)]

    def kv_scan(h):
        hs = _head(h)
        pair = 2 * REF_CHUNK
        for r in range(n_chunks // 2):
            rows = slice(r * pair, (r + 1) * pair)
            kv_ref[r] = _dot_tn(va_ref[rows, hs], ke_ref[rows, 2 * h * HEAD_DIM:(2 * h + 2) * HEAD_DIM])
        st = jnp.where(first, 0.0, st_ref[h])
        for n in range(n_chunks):
            sbf_ref[n, h] = st.astype(BF16)
            st = st * dec_ref[n:n + 1, hs] + kv_ref[n // 2, :, (n % 2) * HEAD_DIM:(n % 2 + 1) * HEAD_DIM]
        st_ref[h] = st
        sa_ref[0, h] = st.T

    def diag(h):
        hs = _head(h)
        sc = jnp.where(_causal_in_chunk(tl, shift), _dot_nt(qd_ref[:, hs], kd_ref[:, hs]), 0.0)
        oa_ref[:, hs] = _dot(sc.astype(BF16), va_ref[:, hs])

    def inter(n):
        rows = slice(n * REF_CHUNK, (n + 1) * REF_CHUNK)
        for h in range(N_HEADS):
            hs = _head(h)
            oa_ref[rows, hs] += _dot_nt(qd_ref[rows, hs], sbf_ref[n, h])

    def hgrn_out():
        for h in range(N_HEADS):
            hs = _head(h)
            mix_ref[:, hs] = _rms_gate(oa_ref[:, hs], ag_ref[:, hs], proj_ref[:, _head(h, 3)]).astype(BF16)

    def ret(h):
        hs = _head(h)
        logd = RET_LOG_DECAY[h]
        cos = cos_ref[...]
        sin = sin_ref[...]
        r = lax.broadcasted_iota(jnp.int32, (tl, tl), 0)
        c = lax.broadcasted_iota(jnp.int32, (tl, tl), 1)
        row = lax.broadcasted_iota(jnp.int32, (tl, HEAD_DIM), 0).astype(F32)
        q = _rope(proj_ref[:, _head(h, 4)], cos, sin)
        k = _rope(proj_ref[:, _head(h, 5)], cos, sin) * K_SCALE
        v = proj_ref[:, _head(h, 6)].astype(BF16)
        dmask = jnp.where(r >= c, jnp.exp((r - c).astype(F32) * logd), 0.0)
        a = (_dot_nt(q.astype(BF16), k.astype(BF16)) * dmask).astype(BF16)
        s = jnp.where(first, 0.0, sb_ref[0, h])
        q_dec_b = (q * jnp.exp((row + 1.0) * logd)).astype(BF16)
        o = _dot(a, v) + _dot(q_dec_b, s.astype(BF16))
        k_end_b = (k * jnp.exp((tl - 1.0 - row) * logd)).astype(BF16)
        sb_ref[0, h] = s * math.exp(tl * logd) + _dot_tn(k_end_b, v)
        mix_ref[:, _head(h, 1)] = _ln_gate(o, bg_ref[:, hs], bb_ref[:, hs],
                                           proj_ref[:, _head(h, 7)]).astype(BF16)

    return (prepass,
            [functools.partial(kv_scan, h) for h in range(N_HEADS)],
            [functools.partial(diag, h) for h in range(N_HEADS)],
            [functools.partial(inter, n) for n in range(n_chunks)],
            hgrn_out,
            [functools.partial(ret, h) for h in range(N_HEADS)])


def _interleave(a, b):
    out = []
    for i in range(max(len(a), len(b))):
        out += a[i:i + 1] + b[i:i + 1]
    return out


def _prompt_layer_kernel(xn_ref, xp_ref, p_ref, cos_ref, sin_ref, w_in_ref, lb_ref, ag_ref, bg_ref,
                         bb_ref, *rest, tiles_per_seq):
    tail_w = rest[:11]
    y_ref, sa_ref, sb_ref = rest[11:14]
    mixer_scratch = rest[14:25]
    proj_ref, mix_ref = mixer_scratch[0], mixer_scratch[3]
    tail_scratch = rest[25:]
    g = pl.program_id(0)
    n_tiles = pl.num_programs(0) - 1
    slot = lax.rem(g, 2)

    def mixer_steps():
        return _prompt_mixer_steps(cos_ref, sin_ref, lb_ref, ag_ref, bg_ref, bb_ref, sa_ref, sb_ref,
                                   proj_ref.at[slot], *mixer_scratch[1:],
                                   first=lax.rem(g, tiles_per_seq) == 0)

    def tail_steps():
        return _tail_steps(xp_ref, mix_ref, p_ref, y_ref, *tail_w, *tail_scratch)

    @pl.when(g == 0)
    def _():
        prepass, kv_scan, diag, inter, hgrn_out, ret = mixer_steps()
        steps = _in_proj_steps(xp_ref, w_in_ref, proj_ref.at[slot])
        steps += [prepass] + kv_scan + diag + inter + [hgrn_out] + ret
        steps += _in_proj_steps(xn_ref, w_in_ref, proj_ref.at[1 - slot])
        for step in steps:
            step()

    @pl.when((g > 0) & (g < n_tiles))
    def _():
        prepass, kv_scan, diag, inter, hgrn_out, ret = mixer_steps()
        out_proj, ff, down, final = tail_steps()
        in_proj = _in_proj_steps(xn_ref, w_in_ref, proj_ref.at[1 - slot])
        inter_pairs = [lambda a=a, b=b: (a(), b()) for a, b in zip(inter[0::2], inter[1::2])]
        steps = [out_proj, in_proj[0], prepass, in_proj[1]]
        steps += _interleave(ff + down, kv_scan + diag + ret + inter_pairs + [hgrn_out])
        steps += _interleave(final, in_proj[2:])
        for step in steps:
            step()

    @pl.when(g == n_tiles)
    def _():
        out_proj, ff, down, final = tail_steps()
        for step in [out_proj] + ff + down + final:
            step()


def _prompt_layer(x, p, cos, sin, w_in, lb_logits, a_g, b_g, b_b, tail_w):
    bsz, seq, _ = x.shape
    tl = TOKEN_TILE
    tps = seq // tl
    n_tiles = bsz * tps
    x2 = x.reshape(bsz * seq, D_MODEL)
    p2 = p.reshape(bsz * seq, PLE_DIM)
    const = lambda g: (0, 0)
    nxt = lambda g: (jnp.minimum(g + 1, n_tiles - 1), 0)
    prev = lambda g: (jnp.maximum(g - 1, 0), 0)
    seq_tile = lambda g: (lax.rem(jnp.minimum(g, n_tiles - 1), tps), 0)
    state_spec = pl.BlockSpec((1, N_HEADS, HEAD_DIM, HEAD_DIM),
                              lambda g: (jnp.minimum(g, n_tiles - 1) // tps, 0, 0, 0))
    state_shape = jax.ShapeDtypeStruct((bsz, N_HEADS, HEAD_DIM, HEAD_DIM), F32)
    return pl.pallas_call(
        functools.partial(_prompt_layer_kernel, tiles_per_seq=tps),
        grid=(n_tiles + 1,),
        in_specs=[
            pl.BlockSpec((tl, D_MODEL), nxt),
            pl.BlockSpec((tl, D_MODEL), prev),
            pl.BlockSpec((tl, PLE_DIM), prev),
            pl.BlockSpec((tl, HEAD_DIM), seq_tile),
            pl.BlockSpec((tl, HEAD_DIM), seq_tile),
            pl.BlockSpec((D_MODEL, IN_COLS), const, pipeline_mode=pl.Buffered(1)),
            pl.BlockSpec(lb_logits.shape, const),
            pl.BlockSpec((1, GROUP_W), const),
            pl.BlockSpec((1, GROUP_W), const),
            pl.BlockSpec((1, GROUP_W), const),
        ] + _tail_specs(),
        out_specs=[
            pl.BlockSpec((tl, D_MODEL), prev),
            state_spec,
            state_spec,
        ],
        out_shape=[
            jax.ShapeDtypeStruct((bsz * seq, D_MODEL), F32),
            state_shape,
            state_shape,
        ],
        scratch_shapes=[
            pltpu.VMEM((2, tl, IN_COLS), F32),
            pltpu.VMEM((N_HEADS, HEAD_DIM, HEAD_DIM), F32),
            pltpu.VMEM((tl, GROUP_W), F32),
            pltpu.VMEM((tl, 2 * GROUP_W), BF16),
            pltpu.VMEM((tl, GROUP_W), BF16),
            pltpu.VMEM((tl, GROUP_W), BF16),
            pltpu.VMEM((tl, 2 * GROUP_W), BF16),
            pltpu.VMEM((tl // REF_CHUNK, GROUP_W), F32),
            pltpu.VMEM((tl, GROUP_W), BF16),
            pltpu.VMEM((tl // (2 * REF_CHUNK), HEAD_DIM, 2 * HEAD_DIM), F32),
            pltpu.VMEM((tl // REF_CHUNK, N_HEADS, HEAD_DIM, HEAD_DIM), BF16),
        ] + _tail_scratch(tl),
        compiler_params=pltpu.CompilerParams(
            dimension_semantics=("arbitrary",), vmem_limit_bytes=V7X_VMEM_LIMIT_BYTES),
        name="prompt_layer",
    )(x2, x2, p2, cos, sin, w_in, lb_logits, a_g, b_g, b_b, *tail_w)


def _in_proj_kernel(x_ref, w_ref, o_ref):
    o_ref[...] = _dot(x_ref[...].astype(BF16), w_ref[...])


def _in_proj(x, w_in):
    n = x.shape[0]
    tn = 512
    return pl.pallas_call(
        _in_proj_kernel,
        grid=(IN_COLS // tn,),
        in_specs=[
            pl.BlockSpec((n, D_MODEL), lambda c: (0, 0)),
            pl.BlockSpec((D_MODEL, tn), lambda c: (0, c)),
        ],
        out_specs=pl.BlockSpec((n, tn), lambda c: (0, c)),
        out_shape=jax.ShapeDtypeStruct((n, IN_COLS), F32),
        compiler_params=pltpu.CompilerParams(
            dimension_semantics=("arbitrary",), vmem_limit_bytes=V7X_VMEM_LIMIT_BYTES),
        name="sample_in_proj",
    )(x, w_in)


def _sample_rec_kernel(proj_ref, sa_in_ref, sb_in_ref, lb_ref, ag_ref, bg_ref, bb_ref,
                       cos_ref, sin_ref, mix_ref, sa_ref, sb_ref, oa_ref, ob_ref, *, seq_len):
    rows_n = proj_ref.shape[0]
    n_seq = rows_n // seq_len
    causal = _causal_in_chunk(rows_n, seq_len.bit_length() - 1)

    q_dec, k_dec, kk, b = _hgrn_prepass(proj_ref, _lower_bound(lb_ref), causal)
    v_a = proj_ref[:, 2 * GROUP_W:3 * GROUP_W]
    for h in range(N_HEADS):
        hs = _head(h)
        sc = jnp.where(causal, _dot_nt(q_dec[:, hs], k_dec[:, hs]), 0.0).astype(BF16)
        oa_ref[:, hs] = _dot(sc, v_a[:, hs].astype(BF16))
    q_dec32 = q_dec.astype(F32)
    rr = lax.broadcasted_iota(jnp.int32, (seq_len, GROUP_W), 0)
    ones_blk = jnp.ones((seq_len, HEAD_DIM), BF16)
    for s in range(n_seq):
        rows = slice(s * seq_len, (s + 1) * seq_len)
        b_last = b[(s + 1) * seq_len - 1:(s + 1) * seq_len, :]
        k_end = (kk[rows] * jnp.exp(b_last - b[rows])).astype(BF16)
        hi, mid, lo = [t.astype(F32) for t in _split3(jnp.exp(b_last))]
        dec_rows = jnp.where(rr == 0, hi, jnp.where(rr == 1, mid, jnp.where(rr == 2, lo, 0.0)))
        dec_rows = dec_rows.astype(BF16)
        for h in range(N_HEADS):
            hs = _head(h)
            st = sa_in_ref[s, h]
            oa_ref[rows, hs] += _dot(q_dec32[rows, hs].astype(BF16), st.astype(BF16))
            dec_kv = _dot_tn(dec_rows[:, hs], ones_blk)
            sa_ref[s, h] = st * dec_kv + _dot_tn(k_end[:, hs], v_a[rows, hs].astype(BF16))
    for h in range(N_HEADS):
        hs = _head(h)
        mix_ref[:, hs] = _rms_gate(oa_ref[:, hs], ag_ref[:, hs], proj_ref[:, _head(h, 3)]).astype(BF16)

    cos = jnp.concatenate([cos_ref[...]] * n_seq, axis=0)
    sin = jnp.concatenate([sin_ref[...]] * n_seq, axis=0)
    r = lax.broadcasted_iota(jnp.int32, (rows_n, rows_n), 0)
    c = lax.broadcasted_iota(jnp.int32, (rows_n, rows_n), 1)
    diff = ((r & (seq_len - 1)) - (c & (seq_len - 1))).astype(F32)
    row = (lax.broadcasted_iota(jnp.int32, (rows_n, HEAD_DIM), 0) & (seq_len - 1)).astype(F32)
    for h in range(N_HEADS):
        hs = _head(h)
        logd = RET_LOG_DECAY[h]
        q = _rope(proj_ref[:, _head(h, 4)], cos, sin)
        k = _rope(proj_ref[:, _head(h, 5)], cos, sin) * K_SCALE
        v32 = proj_ref[:, _head(h, 6)]
        dmask = jnp.where(causal, jnp.exp(diff * logd), 0.0)
        a = (_dot_nt(q.astype(BF16), k.astype(BF16)) * dmask).astype(BF16)
        ob_ref[...] = _dot(a, v32.astype(BF16))
        q_dec_b = q * jnp.exp((row + 1.0) * logd)
        k_end_b = k * jnp.exp((seq_len - 1.0 - row) * logd)
        for s in range(n_seq):
            rows = slice(s * seq_len, (s + 1) * seq_len)
            st = sb_in_ref[s, h]
            ob_ref[rows, :] += _dot(q_dec_b[rows].astype(BF16), st.astype(BF16))
            sb_ref[s, h] = st * math.exp(seq_len * logd) + _dot_tn(
                k_end_b[rows].astype(BF16), v32[rows].astype(BF16))
        mix_ref[:, _head(h, 1)] = _ln_gate(ob_ref[...], bg_ref[:, hs], bb_ref[:, hs],
                                           proj_ref[:, _head(h, 7)]).astype(BF16)


def _sample_rec(proj, sa, sb, lb_logits, a_g, b_g, b_b, cos, sin, seq_len):
    n_tok = proj.shape[0]
    n_seq = n_tok // seq_len
    bs = SAMPLE_SEQS
    rows = bs * seq_len
    const = lambda i: (0, 0)
    state_spec = pl.BlockSpec((bs, N_HEADS, HEAD_DIM, HEAD_DIM), lambda i: (i, 0, 0, 0))
    state_shape = jax.ShapeDtypeStruct((n_seq, N_HEADS, HEAD_DIM, HEAD_DIM), F32)
    return pl.pallas_call(
        functools.partial(_sample_rec_kernel, seq_len=seq_len),
        grid=(n_seq // bs,),
        in_specs=[
            pl.BlockSpec((rows, IN_COLS), lambda i: (i, 0)),
            state_spec,
            state_spec,
            pl.BlockSpec(lb_logits.shape, const),
            pl.BlockSpec((1, GROUP_W), const),
            pl.BlockSpec((1, GROUP_W), const),
            pl.BlockSpec((1, GROUP_W), const),
            pl.BlockSpec((seq_len, HEAD_DIM), const),
            pl.BlockSpec((seq_len, HEAD_DIM), const),
        ],
        out_specs=[
            pl.BlockSpec((rows, 2 * GROUP_W), lambda i: (i, 0)),
            state_spec,
            state_spec,
        ],
        out_shape=[
            jax.ShapeDtypeStruct((n_tok, 2 * GROUP_W), BF16),
            state_shape,
            state_shape,
        ],
        scratch_shapes=[
            pltpu.VMEM((rows, GROUP_W), F32),
            pltpu.VMEM((rows, HEAD_DIM), F32),
        ],
        compiler_params=pltpu.CompilerParams(
            dimension_semantics=("arbitrary",), vmem_limit_bytes=V7X_VMEM_LIMIT_BYTES),
        name="sample_recurrence",
    )(proj, sa, sb, lb_logits, a_g, b_g, b_b, cos, sin)


def _tail_kernel(x_ref, mix_ref, p_ref, *rest):
    tail_w, y_ref, tail_scratch = rest[:11], rest[11], rest[12:]
    out_proj, ff, down, final = _tail_steps(x_ref, mix_ref, p_ref, y_ref, *tail_w, *tail_scratch)
    for step in [out_proj] + ff + down + final:
        step()


def _tail(x, mix, p, tail_w):
    n = x.shape[0]
    tl = TOKEN_TILE
    return pl.pallas_call(
        _tail_kernel,
        grid=(n // tl,),
        in_specs=[
            pl.BlockSpec((tl, D_MODEL), lambda i: (i, 0)),
            pl.BlockSpec((tl, 2 * GROUP_W), lambda i: (i, 0)),
            pl.BlockSpec((tl, PLE_DIM), lambda i: (i, 0)),
        ] + _tail_specs(),
        out_specs=pl.BlockSpec((tl, D_MODEL), lambda i: (i, 0)),
        out_shape=jax.ShapeDtypeStruct((n, D_MODEL), F32),
        scratch_shapes=_tail_scratch(tl),
        compiler_params=pltpu.CompilerParams(
            dimension_semantics=("arbitrary",), vmem_limit_bytes=V7X_VMEM_LIMIT_BYTES),
        name="sample_tail",
    )(x, mix, p, *tail_w)


def kernel(x_prompt, x_sample, p_prompt, p_sample, state_hgrn, state_ret, lb_logits, w_in, a_norm_g, b_norm_g, b_norm_b, w_out, ln1_g, ln1_b, w_ffn_gate, w_ffn_up, w_ffn_down, ln2_g, ln2_b, w_ple_proj, w_ple_gate, b_ple_gate):
    assert w_in.shape[0] == DEPTH == 1
    bsz, seq, _ = x_prompt.shape
    n_dec, dec_seq, _ = x_sample.shape

    w_in_b = w_in[0].astype(BF16)
    tail_w = (w_out[0].astype(BF16), ln1_g, ln1_b, w_ffn_gate[0].astype(BF16),
              w_ffn_up[0].astype(BF16), w_ffn_down[0].astype(BF16), ln2_g, ln2_b,
              w_ple_proj[0].astype(BF16), w_ple_gate[0].astype(BF16), b_ple_gate)
    mixer_vecs = (lb_logits, a_norm_g, b_norm_g, b_norm_b)

    cos_p, sin_p = _rope_tables(seq, 0)
    cos_s, sin_s = _rope_tables(dec_seq, PAST_LEN)

    y_p, sa_p, sb_p = _prompt_layer(x_prompt, p_prompt[0], cos_p, sin_p, w_in_b, *mixer_vecs, tail_w)

    proj_s = _in_proj(x_sample.reshape(n_dec * dec_seq, D_MODEL), w_in_b)
    mix_s, sa_s, sb_s = _sample_rec(proj_s, state_hgrn[0], state_ret[0], *mixer_vecs,
                                    cos_s, sin_s, dec_seq)
    y_s = _tail(x_sample.reshape(n_dec * dec_seq, D_MODEL), mix_s,
                p_sample[0].reshape(n_dec * dec_seq, PLE_DIM), tail_w)

    return (y_p.reshape(bsz, seq, D_MODEL), y_s.reshape(n_dec, dec_seq, D_MODEL),
            sa_p[None], sb_p[None], sa_s[None], sb_s[None])
```

```python
import functools
import math

import jax
import jax.numpy as jnp
from jax import lax
from jax.experimental import pallas as pl
from jax.experimental.pallas import tpu as pltpu

F32 = jnp.float32
BF16 = jnp.bfloat16

D_MODEL = 1024
N_HEADS = 4
HEAD_DIM = 128
GROUP_W = N_HEADS * HEAD_DIM
IN_COLS = 8 * GROUP_W
D_FF = 2816
PLE_DIM = 256
DEPTH = 1
PAST_LEN = 16384
REF_CHUNK = 32
ROPE_BASE = 10000.0
NORM_EPS = 1e-5
DN_ALPHA = (2.0 * DEPTH) ** 0.25
RET_LOG_DECAY = tuple(math.log1p(-(2.0 ** (-5.0 - h))) for h in range(N_HEADS))
K_SCALE = HEAD_DIM ** -0.5

V7X_VMEM_LIMIT_BYTES = 60 * 1024 * 1024

TOKEN_TILE = 256
SAMPLE_TAIL_TILE = 512
SAMPLE_PROJ_COLS = 1024
SAMPLE_SEQS = 8
FF_CHUNK = 256
DOWN_CHUNK = 256


def _dot(a, b):
    return jnp.dot(a, b, preferred_element_type=F32)


def _dot_nt(a, b):
    return lax.dot_general(a, b, (((1,), (1,)), ((), ())), preferred_element_type=F32)


def _dot_tn(a, b):
    return lax.dot_general(a, b, (((0,), (0,)), ((), ())), preferred_element_type=F32)


def _split3(x):
    hi = x.astype(BF16)
    r1 = x - hi.astype(F32)
    mid = r1.astype(BF16)
    lo = (r1 - mid.astype(F32)).astype(BF16)
    return hi, mid, lo


def _dot_exact_lhs01(m01, parts):
    hi, mid, lo = parts
    return _dot(m01, hi) + _dot(m01, mid) + _dot(m01, lo)


def _sigmoid(x):
    return 1.0 / (1.0 + jnp.exp(-x))


def _silu(x):
    return x * _sigmoid(x)


def _causal_in_chunk(n, shift):
    r = lax.broadcasted_iota(jnp.int32, (n, n), 0)
    c = lax.broadcasted_iota(jnp.int32, (n, n), 1)
    return ((r >> shift) == (c >> shift)) & (c <= r)


def _lower_bound(lb_ref):
    rows = [lb_ref[i:i + 1, :] for i in range(lb_ref.shape[0])]
    m = functools.reduce(jnp.maximum, rows)
    e = [jnp.exp(r - m) for r in rows]
    return e[0] / functools.reduce(jnp.add, e)


def _hgrn_prepass(proj_ref, lb, causal):
    tri = jnp.where(causal, 1.0, 0.0).astype(BF16)
    f = lb + (1.0 - lb) * _sigmoid(proj_ref[:, GROUP_W:2 * GROUP_W])
    kk = 1.0 - f
    b = _dot_exact_lhs01(tri, _split3(jnp.log(f)))
    q_dec = (_silu(proj_ref[:, 0:GROUP_W]) * jnp.exp(b)).astype(BF16)
    k_dec = (kk * jnp.exp(-b)).astype(BF16)
    return q_dec, k_dec, kk, b


def _rope(x, cos, sin_signed):
    return x * cos + pltpu.roll(x, HEAD_DIM // 2, axis=1) * sin_signed


def _rms_gate(o, g, gate):
    return o * lax.rsqrt(jnp.mean(o * o, axis=-1, keepdims=True) + NORM_EPS) * g * _silu(gate)


def _ln_gate(o, g, b, gate):
    mu = jnp.mean(o, axis=-1, keepdims=True)
    d = o - mu
    var = jnp.mean(d * d, axis=-1, keepdims=True)
    return (d * lax.rsqrt(var + NORM_EPS) * g + b) * _silu(gate)


def _layer_norm(x, g, b):
    mu = jnp.mean(x, axis=-1, keepdims=True)
    d = x - mu
    var = jnp.mean(d * d, axis=-1, keepdims=True)
    return d * lax.rsqrt(var + NORM_EPS) * g + b


def _head(h, group=0):
    return slice(group * GROUP_W + h * HEAD_DIM, group * GROUP_W + (h + 1) * HEAD_DIM)


def _rope_table_kernel(cos_ref, sin_ref, *, offset):
    n = cos_ref.shape[0]
    half = HEAD_DIM // 2
    row = lax.broadcasted_iota(jnp.int32, (n, HEAD_DIM), 0) + pl.program_id(0) * n
    lane = lax.broadcasted_iota(jnp.int32, (n, HEAD_DIM), 1)
    j = (lane & (half - 1)).astype(F32)
    inv = jnp.exp(-(j / half) * math.log(ROPE_BASE))
    ang = (row.astype(F32) + offset) * inv
    cos_ref[...] = jnp.cos(ang)
    s = jnp.sin(ang)
    sin_ref[...] = jnp.where(lane < half, -s, s)


def _rope_tables(n, offset):
    tile = min(n, 512)
    return pl.pallas_call(
        functools.partial(_rope_table_kernel, offset=float(offset)),
        grid=(n // tile,),
        in_specs=[],
        out_specs=[pl.BlockSpec((tile, HEAD_DIM), lambda i: (i, 0))] * 2,
        out_shape=[jax.ShapeDtypeStruct((n, HEAD_DIM), F32)] * 2,
        name="rope_tables",
    )()


def _tail_steps(x_ref, mix_ref, p_ref, y_ref, w_out_ref, ln1g_ref, ln1b_ref, wg_ref, wu_ref, wd_ref,
                ln2g_ref, ln2b_ref, wpp_ref, wpg_ref, bpg_ref, act_ref, h_ref, hb_ref):
    def out_proj():
        h = _layer_norm(DN_ALPHA * x_ref[...] + _dot(mix_ref[...], w_out_ref[...]),
                        ln1g_ref[...], ln1b_ref[...])
        h_ref[...] = h
        hb_ref[...] = h.astype(BF16)

    def ff(c):
        cols = slice(c * FF_CHUNK, (c + 1) * FF_CHUNK)
        hb = hb_ref[...]
        act_ref[:, cols] = (_silu(_dot(hb, wg_ref[:, cols])) * _dot(hb, wu_ref[:, cols])).astype(BF16)

    def down(c):
        cols = slice(c * DOWN_CHUNK, (c + 1) * DOWN_CHUNK)
        h_ref[:, cols] = DN_ALPHA * h_ref[:, cols] + _dot(act_ref[...], wd_ref[:, cols])

    def norm2():
        h2 = _layer_norm(h_ref[...], ln2g_ref[...], ln2b_ref[...])
        h_ref[...] = h2
        hb_ref[...] = h2.astype(BF16)

    def ple(c):
        cols = slice(c * DOWN_CHUNK, (c + 1) * DOWN_CHUNK)
        gate = _sigmoid(_dot(hb_ref[...], wpg_ref[:, cols]) + bpg_ref[:, cols])
        y_ref[:, cols] = h_ref[:, cols] + gate * _dot(p_ref[...].astype(BF16), wpp_ref[:, cols])

    return (out_proj,
            [functools.partial(ff, c) for c in range(D_FF // FF_CHUNK)],
            [functools.partial(down, c) for c in range(D_MODEL // DOWN_CHUNK)],
            [norm2] + [functools.partial(ple, c) for c in range(D_MODEL // DOWN_CHUNK)])


def _tail_scratch(tl):
    return [pltpu.VMEM((tl, D_FF), BF16), pltpu.VMEM((tl, D_MODEL), F32), pltpu.VMEM((tl, D_MODEL), BF16)]


def _tail_specs():
    const = lambda i: (0, 0)
    resident = lambda shape: pl.BlockSpec(shape, const, pipeline_mode=pl.Buffered(1))
    vec = pl.BlockSpec((1, D_MODEL), const)
    return [
        resident((2 * GROUP_W, D_MODEL)),
        vec, vec,
        resident((D_MODEL, D_FF)),
        resident((D_MODEL, D_FF)),
        resident((D_FF, D_MODEL)),
        vec, vec,
        resident((PLE_DIM, D_MODEL)),
        resident((D_MODEL, D_MODEL)),
        vec,
    ]


def _in_proj_steps(x_ref, w_in_ref, proj_ref, xb_ref):
    def in_proj(c):
        if c == 0:
            xb_ref[...] = x_ref[...].astype(BF16)
        cols = slice(c * GROUP_W, (c + 1) * GROUP_W)
        proj_ref[:, cols] = _dot(xb_ref[...], w_in_ref[:, cols])

    return [functools.partial(in_proj, c) for c in range(IN_COLS // GROUP_W)]


def _ret_tables(dm_ref, rd_ref):
    tl = dm_ref.shape[1]
    r = lax.broadcasted_iota(jnp.int32, (tl, tl), 0)
    c = lax.broadcasted_iota(jnp.int32, (tl, tl), 1)
    row = lax.broadcasted_iota(jnp.int32, (tl, HEAD_DIM), 0).astype(F32)
    for h in range(N_HEADS):
        logd = RET_LOG_DECAY[h]
        dm_ref[h] = jnp.where(r >= c, jnp.exp((r - c).astype(F32) * logd), 0.0)
        rd_ref[0, h] = jnp.exp((row + 1.0) * logd)
        rd_ref[1, h] = jnp.exp((tl - 1.0 - row) * logd)


def _prompt_mixer_steps(cos_ref, sin_ref, lb_ref, ag_ref, bg_ref, bb_ref,
                        sa_ref, sb_ref, proj_ref, st_ref, oa_ref, mix_ref,
                        qd_ref, kd_ref, ke_ref, dec_ref, va_ref, kv_ref, sbf_ref, dm_ref, rd_ref, first):
    tl = proj_ref.shape[0]
    shift = REF_CHUNK.bit_length() - 1
    n_chunks = tl // REF_CHUNK

    def prepass():
        q_dec, k_dec, kk, b = _hgrn_prepass(proj_ref, _lower_bound(lb_ref), _causal_in_chunk(tl, shift))
        qd_ref[...] = q_dec
        kd_ref[...] = k_dec
        va_ref[...] = proj_ref[:, 2 * GROUP_W:3 * GROUP_W].astype(BF16)
        last = [b[(n + 1) * REF_CHUNK - 1:(n + 1) * REF_CHUNK, :] for n in range(n_chunks)]
        b_last = jnp.concatenate([jnp.broadcast_to(r, (REF_CHUNK, GROUP_W)) for r in last], axis=0)
        k_end = kk * jnp.exp(b_last - b)
        for n in range(n_chunks):
            dec_ref[n:n + 1, :] = jnp.exp(last[n])
        odd = ((lax.broadcasted_iota(jnp.int32, (tl, GROUP_W), 0) >> shift) & 1) == 1
        k_even = jnp.where(odd, 0.0, k_end).astype(BF16)
        k_odd = jnp.where(odd, k_end, 0.0).astype(BF16)
        for h in range(N_HEADS):
            ke_ref[:, 2 * h * HEAD_DIM:(2 * h + 1) * HEAD_DIM] = k_even[:, _head(h)]
            ke_ref[:, (2 * h + 1) * HEAD_DIM:(2 * h + 2) * HEAD_DIM] = k_odd[:, _head(h)]

    def kv_scan(h):
        hs = _head(h)
        pair = 2 * REF_CHUNK
        for r in range(n_chunks // 2):
            rows = slice(r * pair, (r + 1) * pair)
            kv_ref[r] = _dot_tn(va_ref[rows, hs], ke_ref[rows, 2 * h * HEAD_DIM:(2 * h + 2) * HEAD_DIM])
        st = jnp.where(first, 0.0, st_ref[h])
        for n in range(n_chunks):
            sbf_ref[n, h] = st.astype(BF16)
            st = st * dec_ref[n:n + 1, hs] + kv_ref[n // 2, :, (n % 2) * HEAD_DIM:(n % 2 + 1) * HEAD_DIM]
        st_ref[h] = st
        sa_ref[0, h] = st.T

    def diag(h):
        hs = _head(h)
        sc = jnp.where(_causal_in_chunk(tl, shift), _dot_nt(qd_ref[:, hs], kd_ref[:, hs]), 0.0)
        oa_ref[:, hs] = _dot(sc.astype(BF16), va_ref[:, hs])

    def inter(n):
        rows = slice(n * REF_CHUNK, (n + 1) * REF_CHUNK)
        for h in range(N_HEADS):
            hs = _head(h)
            oa_ref[rows, hs] += _dot_nt(qd_ref[rows, hs], sbf_ref[n, h])

    def hgrn_out():
        for h in range(N_HEADS):
            hs = _head(h)
            mix_ref[:, hs] = _rms_gate(oa_ref[:, hs], ag_ref[:, hs], proj_ref[:, _head(h, 3)]).astype(BF16)

    def ret(h):
        hs = _head(h)
        cos = cos_ref[...]
        sin = sin_ref[...]
        q = _rope(proj_ref[:, _head(h, 4)], cos, sin)
        k = _rope(proj_ref[:, _head(h, 5)], cos, sin) * K_SCALE
        v = proj_ref[:, _head(h, 6)].astype(BF16)
        a = (_dot_nt(q.astype(BF16), k.astype(BF16)) * dm_ref[h]).astype(BF16)
        s = jnp.where(first, 0.0, sb_ref[0, h])
        o = _dot(a, v) + _dot((q * rd_ref[0, h]).astype(BF16), s.astype(BF16))
        k_end_b = (k * rd_ref[1, h]).astype(BF16)
        sb_ref[0, h] = s * math.exp(tl * RET_LOG_DECAY[h]) + _dot_tn(k_end_b, v)
        mix_ref[:, _head(h, 1)] = _ln_gate(o, bg_ref[:, hs], bb_ref[:, hs],
                                           proj_ref[:, _head(h, 7)]).astype(BF16)

    return (prepass,
            [functools.partial(kv_scan, h) for h in range(N_HEADS)],
            [functools.partial(diag, h) for h in range(N_HEADS)],
            [functools.partial(inter, n) for n in range(n_chunks)],
            hgrn_out,
            [functools.partial(ret, h) for h in range(N_HEADS)])


def _interleave(a, b):
    out = []
    for i in range(max(len(a), len(b))):
        out += a[i:i + 1] + b[i:i + 1]
    return out


def _prompt_layer_kernel(xn_ref, xp_ref, p_ref, cos_ref, sin_ref, w_in_ref, lb_ref, ag_ref, bg_ref,
                         bb_ref, *rest, tiles_per_seq):
    tail_w = rest[:11]
    y_ref, sa_ref, sb_ref = rest[11:14]
    mixer_scratch = rest[14:27]
    proj_ref, mix_ref, dm_ref, rd_ref = mixer_scratch[0], mixer_scratch[3], mixer_scratch[11], mixer_scratch[12]
    xb_ref = rest[27]
    tail_scratch = rest[28:]
    g = pl.program_id(0)
    n_tiles = pl.num_programs(0) - 1
    slot = lax.rem(g, 2)

    def mixer_steps():
        return _prompt_mixer_steps(cos_ref, sin_ref, lb_ref, ag_ref, bg_ref, bb_ref, sa_ref, sb_ref,
                                   proj_ref.at[slot], *mixer_scratch[1:],
                                   first=lax.rem(g, tiles_per_seq) == 0)

    def tail_steps():
        return _tail_steps(xp_ref, mix_ref, p_ref, y_ref, *tail_w, *tail_scratch)

    @pl.when(g == 0)
    def _():
        _ret_tables(dm_ref, rd_ref)
        prepass, kv_scan, diag, inter, hgrn_out, ret = mixer_steps()
        steps = _in_proj_steps(xp_ref, w_in_ref, proj_ref.at[slot], xb_ref)
        steps += [prepass] + kv_scan + diag + inter + [hgrn_out] + ret
        steps += _in_proj_steps(xn_ref, w_in_ref, proj_ref.at[1 - slot], xb_ref)
        for step in steps:
            step()

    @pl.when((g > 0) & (g < n_tiles))
    def _():
        prepass, kv_scan, diag, inter, hgrn_out, ret = mixer_steps()
        out_proj, ff, down, final = tail_steps()
        in_proj = _in_proj_steps(xn_ref, w_in_ref, proj_ref.at[1 - slot], xb_ref)
        inter_pairs = [lambda a=a, b=b: (a(), b()) for a, b in zip(inter[0::2], inter[1::2])]
        steps = [out_proj, in_proj[0], prepass, in_proj[1]]
        steps += _interleave(ff + down, kv_scan + diag + ret + inter_pairs + [hgrn_out])
        steps += _interleave(final, in_proj[2:])
        for step in steps:
            step()

    @pl.when(g == n_tiles)
    def _():
        out_proj, ff, down, final = tail_steps()
        for step in [out_proj] + ff + down + final:
            step()


def _prompt_layer(x, p, cos, sin, w_in, lb_logits, a_g, b_g, b_b, tail_w):
    bsz, seq, _ = x.shape
    tl = TOKEN_TILE
    tps = seq // tl
    n_tiles = bsz * tps
    x2 = x.reshape(bsz * seq, D_MODEL)
    p2 = p.reshape(bsz * seq, PLE_DIM)
    const = lambda g: (0, 0)
    nxt = lambda g: (jnp.minimum(g + 1, n_tiles - 1), 0)
    prev = lambda g: (jnp.maximum(g - 1, 0), 0)
    seq_tile = lambda g: (lax.rem(jnp.minimum(g, n_tiles - 1), tps), 0)
    state_spec = pl.BlockSpec((1, N_HEADS, HEAD_DIM, HEAD_DIM),
                              lambda g: (jnp.minimum(g, n_tiles - 1) // tps, 0, 0, 0))
    state_shape = jax.ShapeDtypeStruct((bsz, N_HEADS, HEAD_DIM, HEAD_DIM), F32)
    return pl.pallas_call(
        functools.partial(_prompt_layer_kernel, tiles_per_seq=tps),
        grid=(n_tiles + 1,),
        in_specs=[
            pl.BlockSpec((tl, D_MODEL), nxt),
            pl.BlockSpec((tl, D_MODEL), prev),
            pl.BlockSpec((tl, PLE_DIM), prev),
            pl.BlockSpec((tl, HEAD_DIM), seq_tile),
            pl.BlockSpec((tl, HEAD_DIM), seq_tile),
            pl.BlockSpec((D_MODEL, IN_COLS), const, pipeline_mode=pl.Buffered(1)),
            pl.BlockSpec(lb_logits.shape, const),
            pl.BlockSpec((1, GROUP_W), const),
            pl.BlockSpec((1, GROUP_W), const),
            pl.BlockSpec((1, GROUP_W), const),
        ] + _tail_specs(),
        out_specs=[
            pl.BlockSpec((tl, D_MODEL), prev),
            state_spec,
            state_spec,
        ],
        out_shape=[
            jax.ShapeDtypeStruct((bsz * seq, D_MODEL), F32),
            state_shape,
            state_shape,
        ],
        scratch_shapes=[
            pltpu.VMEM((2, tl, IN_COLS), F32),
            pltpu.VMEM((N_HEADS, HEAD_DIM, HEAD_DIM), F32),
            pltpu.VMEM((tl, GROUP_W), F32),
            pltpu.VMEM((tl, 2 * GROUP_W), BF16),
            pltpu.VMEM((tl, GROUP_W), BF16),
            pltpu.VMEM((tl, GROUP_W), BF16),
            pltpu.VMEM((tl, 2 * GROUP_W), BF16),
            pltpu.VMEM((tl // REF_CHUNK, GROUP_W), F32),
            pltpu.VMEM((tl, GROUP_W), BF16),
            pltpu.VMEM((tl // (2 * REF_CHUNK), HEAD_DIM, 2 * HEAD_DIM), F32),
            pltpu.VMEM((tl // REF_CHUNK, N_HEADS, HEAD_DIM, HEAD_DIM), BF16),
            pltpu.VMEM((N_HEADS, tl, tl), F32),
            pltpu.VMEM((2, N_HEADS, tl, HEAD_DIM), F32),
            pltpu.VMEM((tl, D_MODEL), BF16),
        ] + _tail_scratch(tl),
        compiler_params=pltpu.CompilerParams(
            dimension_semantics=("arbitrary",), vmem_limit_bytes=V7X_VMEM_LIMIT_BYTES),
        name="prompt_layer",
    )(x2, x2, p2, cos, sin, w_in, lb_logits, a_g, b_g, b_b, *tail_w)


def _in_proj_kernel(x_ref, w_ref, o_ref):
    o_ref[...] = _dot(x_ref[...].astype(BF16), w_ref[...])


def _in_proj(x, w_in):
    n = x.shape[0]
    tn = SAMPLE_PROJ_COLS
    return pl.pallas_call(
        _in_proj_kernel,
        grid=(IN_COLS // tn,),
        in_specs=[
            pl.BlockSpec((n, D_MODEL), lambda c: (0, 0)),
            pl.BlockSpec((D_MODEL, tn), lambda c: (0, c)),
        ],
        out_specs=pl.BlockSpec((n, tn), lambda c: (0, c)),
        out_shape=jax.ShapeDtypeStruct((n, IN_COLS), F32),
        compiler_params=pltpu.CompilerParams(
            dimension_semantics=("arbitrary",), vmem_limit_bytes=V7X_VMEM_LIMIT_BYTES),
        name="sample_in_proj",
    )(x, w_in)


def _sample_rec_kernel(proj_ref, sa_in_ref, sb_in_ref, lb_ref, ag_ref, bg_ref, bb_ref,
                       cos_ref, sin_ref, mix_ref, sa_ref, sb_ref, oa_ref, ob_ref, *, seq_len):
    rows_n = proj_ref.shape[0]
    n_seq = rows_n // seq_len
    causal = _causal_in_chunk(rows_n, seq_len.bit_length() - 1)

    q_dec, k_dec, kk, b = _hgrn_prepass(proj_ref, _lower_bound(lb_ref), causal)
    v_a = proj_ref[:, 2 * GROUP_W:3 * GROUP_W]
    for h in range(N_HEADS):
        hs = _head(h)
        sc = jnp.where(causal, _dot_nt(q_dec[:, hs], k_dec[:, hs]), 0.0).astype(BF16)
        oa_ref[:, hs] = _dot(sc, v_a[:, hs].astype(BF16))
    q_dec32 = q_dec.astype(F32)
    rr = lax.broadcasted_iota(jnp.int32, (seq_len, GROUP_W), 0)
    ones_blk = jnp.ones((seq_len, HEAD_DIM), BF16)
    for s in range(n_seq):
        rows = slice(s * seq_len, (s + 1) * seq_len)
        b_last = b[(s + 1) * seq_len - 1:(s + 1) * seq_len, :]
        k_end = (kk[rows] * jnp.exp(b_last - b[rows])).astype(BF16)
        hi, mid, lo = [t.astype(F32) for t in _split3(jnp.exp(b_last))]
        dec_rows = jnp.where(rr == 0, hi, jnp.where(rr == 1, mid, jnp.where(rr == 2, lo, 0.0)))
        dec_rows = dec_rows.astype(BF16)
        for h in range(N_HEADS):
            hs = _head(h)
            st = sa_in_ref[s, h]
            oa_ref[rows, hs] += _dot(q_dec32[rows, hs].astype(BF16), st.astype(BF16))
            dec_kv = _dot_tn(dec_rows[:, hs], ones_blk)
            sa_ref[s, h] = st * dec_kv + _dot_tn(k_end[:, hs], v_a[rows, hs].astype(BF16))
    for h in range(N_HEADS):
        hs = _head(h)
        mix_ref[:, hs] = _rms_gate(oa_ref[:, hs], ag_ref[:, hs], proj_ref[:, _head(h, 3)]).astype(BF16)

    cos = jnp.concatenate([cos_ref[...]] * n_seq, axis=0)
    sin = jnp.concatenate([sin_ref[...]] * n_seq, axis=0)
    r = lax.broadcasted_iota(jnp.int32, (rows_n, rows_n), 0)
    c = lax.broadcasted_iota(jnp.int32, (rows_n, rows_n), 1)
    diff = ((r & (seq_len - 1)) - (c & (seq_len - 1))).astype(F32)
    row = (lax.broadcasted_iota(jnp.int32, (rows_n, HEAD_DIM), 0) & (seq_len - 1)).astype(F32)
    for h in range(N_HEADS):
        hs = _head(h)
        logd = RET_LOG_DECAY[h]
        q = _rope(proj_ref[:, _head(h, 4)], cos, sin)
        k = _rope(proj_ref[:, _head(h, 5)], cos, sin) * K_SCALE
        v32 = proj_ref[:, _head(h, 6)]
        dmask = jnp.where(causal, jnp.exp(diff * logd), 0.0)
        a = (_dot_nt(q.astype(BF16), k.astype(BF16)) * dmask).astype(BF16)
        ob_ref[...] = _dot(a, v32.astype(BF16))
        q_dec_b = q * jnp.exp((row + 1.0) * logd)
        k_end_b = k * jnp.exp((seq_len - 1.0 - row) * logd)
        for s in range(n_seq):
            rows = slice(s * seq_len, (s + 1) * seq_len)
            st = sb_in_ref[s, h]
            ob_ref[rows, :] += _dot(q_dec_b[rows].astype(BF16), st.astype(BF16))
            sb_ref[s, h] = st * math.exp(seq_len * logd) + _dot_tn(
                k_end_b[rows].astype(BF16), v32[rows].astype(BF16))
        mix_ref[:, _head(h, 1)] = _ln_gate(ob_ref[...], bg_ref[:, hs], bb_ref[:, hs],
                                           proj_ref[:, _head(h, 7)]).astype(BF16)


def _sample_rec(proj, sa, sb, lb_logits, a_g, b_g, b_b, cos, sin, seq_len):
    n_tok = proj.shape[0]
    n_seq = n_tok // seq_len
    bs = SAMPLE_SEQS
    rows = bs * seq_len
    const = lambda i: (0, 0)
    state_spec = pl.BlockSpec((bs, N_HEADS, HEAD_DIM, HEAD_DIM), lambda i: (i, 0, 0, 0))
    state_shape = jax.ShapeDtypeStruct((n_seq, N_HEADS, HEAD_DIM, HEAD_DIM), F32)
    return pl.pallas_call(
        functools.partial(_sample_rec_kernel, seq_len=seq_len),
        grid=(n_seq // bs,),
        in_specs=[
            pl.BlockSpec((rows, IN_COLS), lambda i: (i, 0)),
            state_spec,
            state_spec,
            pl.BlockSpec(lb_logits.shape, const),
            pl.BlockSpec((1, GROUP_W), const),
            pl.BlockSpec((1, GROUP_W), const),
            pl.BlockSpec((1, GROUP_W), const),
            pl.BlockSpec((seq_len, HEAD_DIM), const),
            pl.BlockSpec((seq_len, HEAD_DIM), const),
        ],
        out_specs=[
            pl.BlockSpec((rows, 2 * GROUP_W), lambda i: (i, 0)),
            state_spec,
            state_spec,
        ],
        out_shape=[
            jax.ShapeDtypeStruct((n_tok, 2 * GROUP_W), BF16),
            state_shape,
            state_shape,
        ],
        scratch_shapes=[
            pltpu.VMEM((rows, GROUP_W), F32),
            pltpu.VMEM((rows, HEAD_DIM), F32),
        ],
        compiler_params=pltpu.CompilerParams(
            dimension_semantics=("arbitrary",), vmem_limit_bytes=V7X_VMEM_LIMIT_BYTES),
        name="sample_recurrence",
    )(proj, sa, sb, lb_logits, a_g, b_g, b_b, cos, sin)


def _tail_kernel(x_ref, mix_ref, p_ref, *rest):
    tail_w, y_ref, tail_scratch = rest[:11], rest[11], rest[12:]
    out_proj, ff, down, final = _tail_steps(x_ref, mix_ref, p_ref, y_ref, *tail_w, *tail_scratch)
    for step in [out_proj] + ff + down + final:
        step()


def _tail(x, mix, p, tail_w):
    n = x.shape[0]
    tl = SAMPLE_TAIL_TILE
    return pl.pallas_call(
        _tail_kernel,
        grid=(n // tl,),
        in_specs=[
            pl.BlockSpec((tl, D_MODEL), lambda i: (i, 0)),
            pl.BlockSpec((tl, 2 * GROUP_W), lambda i: (i, 0)),
            pl.BlockSpec((tl, PLE_DIM), lambda i: (i, 0)),
        ] + _tail_specs(),
        out_specs=pl.BlockSpec((tl, D_MODEL), lambda i: (i, 0)),
        out_shape=jax.ShapeDtypeStruct((n, D_MODEL), F32),
        scratch_shapes=_tail_scratch(tl),
        compiler_params=pltpu.CompilerParams(
            dimension_semantics=("arbitrary",), vmem_limit_bytes=V7X_VMEM_LIMIT_BYTES),
        name="sample_tail",
    )(x, mix, p, *tail_w)


def kernel(x_prompt, x_sample, p_prompt, p_sample, state_hgrn, state_ret, lb_logits, w_in, a_norm_g, b_norm_g, b_norm_b, w_out, ln1_g, ln1_b, w_ffn_gate, w_ffn_up, w_ffn_down, ln2_g, ln2_b, w_ple_proj, w_ple_gate, b_ple_gate):
    assert w_in.shape[0] == DEPTH == 1
    bsz, seq, _ = x_prompt.shape
    n_dec, dec_seq, _ = x_sample.shape

    w_in_b = w_in[0].astype(BF16)
    tail_w = (w_out[0].astype(BF16), ln1_g, ln1_b, w_ffn_gate[0].astype(BF16),
              w_ffn_up[0].astype(BF16), w_ffn_down[0].astype(BF16), ln2_g, ln2_b,
              w_ple_proj[0].astype(BF16), w_ple_gate[0].astype(BF16), b_ple_gate)
    mixer_vecs = (lb_logits, a_norm_g, b_norm_g, b_norm_b)

    cos_p, sin_p = _rope_tables(seq, 0)
    cos_s, sin_s = _rope_tables(dec_seq, PAST_LEN)

    y_p, sa_p, sb_p = _prompt_layer(x_prompt, p_prompt[0], cos_p, sin_p, w_in_b, *mixer_vecs, tail_w)

    proj_s = _in_proj(x_sample.reshape(n_dec * dec_seq, D_MODEL), w_in_b)
    mix_s, sa_s, sb_s = _sample_rec(proj_s, state_hgrn[0], state_ret[0], *mixer_vecs,
                                    cos_s, sin_s, dec_seq)
    y_s = _tail(x_sample.reshape(n_dec * dec_seq, D_MODEL), mix_s,
                p_sample[0].reshape(n_dec * dec_seq, PLE_DIM), tail_w)

    return (y_p.reshape(bsz, seq, D_MODEL), y_s.reshape(n_dec, dec_seq, D_MODEL),
            sa_p[None], sb_p[None], sa_s[None], sb_s[None])
```

```python
import functools
import math

import jax
import jax.numpy as jnp
from jax import lax
from jax.experimental import pallas as pl
from jax.experimental.pallas import tpu as pltpu

F32 = jnp.float32
BF16 = jnp.bfloat16

D_MODEL = 1024
N_HEADS = 4
HEAD_DIM = 128
GROUP_W = N_HEADS * HEAD_DIM
IN_COLS = 8 * GROUP_W
D_FF = 2816
PLE_DIM = 256
DEPTH = 1
PAST_LEN = 16384
REF_CHUNK = 32
ROPE_BASE = 10000.0
NORM_EPS = 1e-5
DN_ALPHA = (2.0 * DEPTH) ** 0.25
RET_LOG_DECAY = tuple(math.log1p(-(2.0 ** (-5.0 - h))) for h in range(N_HEADS))
K_SCALE = HEAD_DIM ** -0.5

V7X_VMEM_LIMIT_BYTES = 60 * 1024 * 1024

TOKEN_TILE = 256
SAMPLE_TAIL_TILE = 512
SAMPLE_PROJ_COLS = 1024
SAMPLE_SEQS = 8
FF_CHUNK = 256
DOWN_CHUNK = 256


def _dot(a, b):
    return jnp.dot(a, b, preferred_element_type=F32)


def _dot_nt(a, b):
    return lax.dot_general(a, b, (((1,), (1,)), ((), ())), preferred_element_type=F32)


def _dot_tn(a, b):
    return lax.dot_general(a, b, (((0,), (0,)), ((), ())), preferred_element_type=F32)


def _split3(x):
    hi = x.astype(BF16)
    r1 = x - hi.astype(F32)
    mid = r1.astype(BF16)
    lo = (r1 - mid.astype(F32)).astype(BF16)
    return hi, mid, lo


def _dot_exact_lhs01(m01, parts):
    hi, mid, lo = parts
    return _dot(m01, hi) + _dot(m01, mid) + _dot(m01, lo)


def _sigmoid(x):
    return 1.0 / (1.0 + jnp.exp(-x))


def _silu(x):
    return x * _sigmoid(x)


def _causal_in_chunk(n, shift):
    r = lax.broadcasted_iota(jnp.int32, (n, n), 0)
    c = lax.broadcasted_iota(jnp.int32, (n, n), 1)
    return ((r >> shift) == (c >> shift)) & (c <= r)


def _lower_bound(lb_ref):
    rows = [lb_ref[i:i + 1, :] for i in range(lb_ref.shape[0])]
    m = functools.reduce(jnp.maximum, rows)
    e = [jnp.exp(r - m) for r in rows]
    return e[0] / functools.reduce(jnp.add, e)


def _hgrn_prepass(proj_ref, lb, causal):
    tri = jnp.where(causal, 1.0, 0.0).astype(BF16)
    f = lb + (1.0 - lb) * _sigmoid(proj_ref[:, GROUP_W:2 * GROUP_W])
    kk = 1.0 - f
    b = _dot_exact_lhs01(tri, _split3(jnp.log(f)))
    q_dec = (_silu(proj_ref[:, 0:GROUP_W]) * jnp.exp(b)).astype(BF16)
    k_dec = kk * jnp.exp(-b)
    return q_dec, k_dec, kk, b


def _rope(x, cos, sin_signed):
    return x * cos + pltpu.roll(x, HEAD_DIM // 2, axis=1) * sin_signed


def _rms_gate(o, g, gate):
    return o * lax.rsqrt(jnp.mean(o * o, axis=-1, keepdims=True) + NORM_EPS) * g * _silu(gate)


def _ln_gate(o, g, b, gate):
    mu = jnp.mean(o, axis=-1, keepdims=True)
    d = o - mu
    var = jnp.mean(d * d, axis=-1, keepdims=True)
    return (d * lax.rsqrt(var + NORM_EPS) * g + b) * _silu(gate)


def _layer_norm(x, g, b):
    mu = jnp.mean(x, axis=-1, keepdims=True)
    d = x - mu
    var = jnp.mean(d * d, axis=-1, keepdims=True)
    return d * lax.rsqrt(var + NORM_EPS) * g + b


def _head(h, group=0):
    return slice(group * GROUP_W + h * HEAD_DIM, group * GROUP_W + (h + 1) * HEAD_DIM)


def _rope_table_kernel(cos_ref, sin_ref, *, offset):
    n = cos_ref.shape[0]
    half = HEAD_DIM // 2
    row = lax.broadcasted_iota(jnp.int32, (n, HEAD_DIM), 0) + pl.program_id(0) * n
    lane = lax.broadcasted_iota(jnp.int32, (n, HEAD_DIM), 1)
    j = (lane & (half - 1)).astype(F32)
    inv = jnp.exp(-(j / half) * math.log(ROPE_BASE))
    ang = (row.astype(F32) + offset) * inv
    cos_ref[...] = jnp.cos(ang)
    s = jnp.sin(ang)
    sin_ref[...] = jnp.where(lane < half, -s, s)


def _rope_tables(n, offset):
    tile = min(n, 512)
    return pl.pallas_call(
        functools.partial(_rope_table_kernel, offset=float(offset)),
        grid=(n // tile,),
        in_specs=[],
        out_specs=[pl.BlockSpec((tile, HEAD_DIM), lambda i: (i, 0))] * 2,
        out_shape=[jax.ShapeDtypeStruct((n, HEAD_DIM), F32)] * 2,
        name="rope_tables",
    )()


def _tail_steps(x_ref, mix_ref, p_ref, y_ref, w_out_ref, ln1g_ref, ln1b_ref, wg_ref, wu_ref, wd_ref,
                ln2g_ref, ln2b_ref, wpp_ref, wpg_ref, bpg_ref, act_ref, h_ref, hb_ref):
    def out_proj():
        h = _layer_norm(DN_ALPHA * x_ref[...] + _dot(mix_ref[...], w_out_ref[...]),
                        ln1g_ref[...], ln1b_ref[...])
        h_ref[...] = h
        hb_ref[...] = h.astype(BF16)

    def ff(c):
        cols = slice(c * FF_CHUNK, (c + 1) * FF_CHUNK)
        hb = hb_ref[...]
        act_ref[:, cols] = (_silu(_dot(hb, wg_ref[:, cols])) * _dot(hb, wu_ref[:, cols])).astype(BF16)

    def down(c):
        cols = slice(c * DOWN_CHUNK, (c + 1) * DOWN_CHUNK)
        h_ref[:, cols] = DN_ALPHA * h_ref[:, cols] + _dot(act_ref[...], wd_ref[:, cols])

    def norm2():
        h2 = _layer_norm(h_ref[...], ln2g_ref[...], ln2b_ref[...])
        h_ref[...] = h2
        hb_ref[...] = h2.astype(BF16)

    def ple(c):
        cols = slice(c * DOWN_CHUNK, (c + 1) * DOWN_CHUNK)
        gate = _sigmoid(_dot(hb_ref[...], wpg_ref[:, cols]) + bpg_ref[:, cols])
        y_ref[:, cols] = h_ref[:, cols] + gate * _dot(p_ref[...].astype(BF16), wpp_ref[:, cols])

    return (out_proj,
            [functools.partial(ff, c) for c in range(D_FF // FF_CHUNK)],
            [functools.partial(down, c) for c in range(D_MODEL // DOWN_CHUNK)],
            [norm2] + [functools.partial(ple, c) for c in range(D_MODEL // DOWN_CHUNK)])


def _tail_scratch(tl):
    return [pltpu.VMEM((tl, D_FF), BF16), pltpu.VMEM((tl, D_MODEL), F32), pltpu.VMEM((tl, D_MODEL), BF16)]


def _tail_specs():
    const = lambda i: (0, 0)
    resident = lambda shape: pl.BlockSpec(shape, const, pipeline_mode=pl.Buffered(1))
    vec = pl.BlockSpec((1, D_MODEL), const)
    return [
        resident((2 * GROUP_W, D_MODEL)),
        vec, vec,
        resident((D_MODEL, D_FF)),
        resident((D_MODEL, D_FF)),
        resident((D_FF, D_MODEL)),
        vec, vec,
        resident((PLE_DIM, D_MODEL)),
        resident((D_MODEL, D_MODEL)),
        vec,
    ]


def _in_proj_steps(x_ref, w_in_ref, proj_ref, xb_ref):
    def in_proj(c):
        if c == 0:
            xb_ref[...] = x_ref[...].astype(BF16)
        cols = slice(c * GROUP_W, (c + 1) * GROUP_W)
        proj_ref[:, cols] = _dot(xb_ref[...], w_in_ref[:, cols])

    return [functools.partial(in_proj, c) for c in range(IN_COLS // GROUP_W)]


def _ret_tables(dm_ref, rd_ref):
    tl = dm_ref.shape[1]
    r = lax.broadcasted_iota(jnp.int32, (tl, tl), 0)
    c = lax.broadcasted_iota(jnp.int32, (tl, tl), 1)
    row = lax.broadcasted_iota(jnp.int32, (tl, HEAD_DIM), 0).astype(F32)
    for h in range(N_HEADS):
        logd = RET_LOG_DECAY[h]
        dm_ref[h] = jnp.where(r >= c, jnp.exp((r - c).astype(F32) * logd), 0.0)
        rd_ref[0, h] = jnp.exp((row + 1.0) * logd)
        rd_ref[1, h] = jnp.exp((tl - 1.0 - row) * logd)


def _prompt_mixer_steps(cos_ref, sin_ref, lb_ref, ag_ref, bg_ref, bb_ref,
                        sa_ref, sb_ref, proj_ref, st_ref, oa_ref, mix_ref,
                        qd_ref, kd_ref, ke_ref, dec_ref, va_ref, kv_ref, sbf_ref, dm_ref, rd_ref, first):
    tl = proj_ref.shape[0]
    shift = REF_CHUNK.bit_length() - 1
    n_chunks = tl // REF_CHUNK

    def prepass():
        q_dec, k_dec, kk, b = _hgrn_prepass(proj_ref, _lower_bound(lb_ref), _causal_in_chunk(tl, shift))
        qd_ref[...] = q_dec
        for h in range(N_HEADS):
            kd_ref[h] = k_dec[:, _head(h)].T.astype(BF16)
        va_ref[...] = proj_ref[:, 2 * GROUP_W:3 * GROUP_W].astype(BF16)
        last = [b[(n + 1) * REF_CHUNK - 1:(n + 1) * REF_CHUNK, :] for n in range(n_chunks)]
        b_last = jnp.concatenate([jnp.broadcast_to(r, (REF_CHUNK, GROUP_W)) for r in last], axis=0)
        k_end = kk * jnp.exp(b_last - b)
        for n in range(n_chunks):
            dec_ref[n:n + 1, :] = jnp.exp(last[n])
        odd = ((lax.broadcasted_iota(jnp.int32, (tl, GROUP_W), 0) >> shift) & 1) == 1
        k_even = jnp.where(odd, 0.0, k_end).astype(BF16)
        k_odd = jnp.where(odd, k_end, 0.0).astype(BF16)
        for h in range(N_HEADS):
            ke_ref[:, 2 * h * HEAD_DIM:(2 * h + 1) * HEAD_DIM] = k_even[:, _head(h)]
            ke_ref[:, (2 * h + 1) * HEAD_DIM:(2 * h + 2) * HEAD_DIM] = k_odd[:, _head(h)]

    def kv_scan(h):
        hs = _head(h)
        pair = 2 * REF_CHUNK
        for r in range(n_chunks // 2):
            rows = slice(r * pair, (r + 1) * pair)
            kv_ref[r] = _dot_tn(va_ref[rows, hs], ke_ref[rows, 2 * h * HEAD_DIM:(2 * h + 2) * HEAD_DIM])
        st = jnp.where(first, 0.0, st_ref[h])
        for n in range(n_chunks):
            sbf_ref[n, h] = st.T.astype(BF16)
            st = st * dec_ref[n:n + 1, hs] + kv_ref[n // 2, :, (n % 2) * HEAD_DIM:(n % 2 + 1) * HEAD_DIM]
        st_ref[h] = st
        sa_ref[0, h] = st.T

    def diag(h):
        hs = _head(h)
        sc = jnp.where(_causal_in_chunk(tl, shift), _dot(qd_ref[:, hs], kd_ref[h]), 0.0)
        oa_ref[:, hs] = _dot(sc.astype(BF16), va_ref[:, hs])

    def inter(n):
        rows = slice(n * REF_CHUNK, (n + 1) * REF_CHUNK)
        for h in range(N_HEADS):
            hs = _head(h)
            oa_ref[rows, hs] += _dot(qd_ref[rows, hs], sbf_ref[n, h])

    def hgrn_out():
        for h in range(N_HEADS):
            hs = _head(h)
            mix_ref[:, hs] = _rms_gate(oa_ref[:, hs], ag_ref[:, hs], proj_ref[:, _head(h, 3)]).astype(BF16)

    def ret(h):
        hs = _head(h)
        cos = cos_ref[...]
        sin = sin_ref[...]
        q = _rope(proj_ref[:, _head(h, 4)], cos, sin)
        k = _rope(proj_ref[:, _head(h, 5)], cos, sin) * K_SCALE
        v = proj_ref[:, _head(h, 6)].astype(BF16)
        a = (_dot(q.astype(BF16), k.T.astype(BF16)) * dm_ref[h]).astype(BF16)
        s = jnp.where(first, 0.0, sb_ref[0, h])
        o = _dot(a, v) + _dot((q * rd_ref[0, h]).astype(BF16), s.astype(BF16))
        k_end_b = (k * rd_ref[1, h]).astype(BF16)
        sb_ref[0, h] = s * math.exp(tl * RET_LOG_DECAY[h]) + _dot_tn(k_end_b, v)
        mix_ref[:, _head(h, 1)] = _ln_gate(o, bg_ref[:, hs], bb_ref[:, hs],
                                           proj_ref[:, _head(h, 7)]).astype(BF16)

    return (prepass,
            [functools.partial(kv_scan, h) for h in range(N_HEADS)],
            [functools.partial(diag, h) for h in range(N_HEADS)],
            [functools.partial(inter, n) for n in range(n_chunks)],
            hgrn_out,
            [functools.partial(ret, h) for h in range(N_HEADS)])


def _interleave(a, b):
    out = []
    for i in range(max(len(a), len(b))):
        out += a[i:i + 1] + b[i:i + 1]
    return out


def _prompt_layer_kernel(xn_ref, xp_ref, p_ref, cos_ref, sin_ref, w_in_ref, lb_ref, ag_ref, bg_ref,
                         bb_ref, *rest, tiles_per_seq):
    tail_w = rest[:11]
    y_ref, sa_ref, sb_ref = rest[11:14]
    mixer_scratch = rest[14:27]
    proj_ref, mix_ref, dm_ref, rd_ref = mixer_scratch[0], mixer_scratch[3], mixer_scratch[11], mixer_scratch[12]
    xb_ref = rest[27]
    tail_scratch = rest[28:]
    g = pl.program_id(0)
    n_tiles = pl.num_programs(0) - 1
    slot = lax.rem(g, 2)

    def mixer_steps():
        return _prompt_mixer_steps(cos_ref, sin_ref, lb_ref, ag_ref, bg_ref, bb_ref, sa_ref, sb_ref,
                                   proj_ref.at[slot], *mixer_scratch[1:],
                                   first=lax.rem(g, tiles_per_seq) == 0)

    def tail_steps():
        return _tail_steps(xp_ref, mix_ref, p_ref, y_ref, *tail_w, *tail_scratch)

    @pl.when(g == 0)
    def _():
        _ret_tables(dm_ref, rd_ref)
        prepass, kv_scan, diag, inter, hgrn_out, ret = mixer_steps()
        steps = _in_proj_steps(xp_ref, w_in_ref, proj_ref.at[slot], xb_ref)
        steps += [prepass] + kv_scan + diag + inter + [hgrn_out] + ret
        steps += _in_proj_steps(xn_ref, w_in_ref, proj_ref.at[1 - slot], xb_ref)
        for step in steps:
            step()

    @pl.when((g > 0) & (g < n_tiles))
    def _():
        prepass, kv_scan, diag, inter, hgrn_out, ret = mixer_steps()
        out_proj, ff, down, final = tail_steps()
        in_proj = _in_proj_steps(xn_ref, w_in_ref, proj_ref.at[1 - slot], xb_ref)
        inter_pairs = [lambda a=a, b=b: (a(), b()) for a, b in zip(inter[0::2], inter[1::2])]
        steps = [out_proj, in_proj[0], prepass, in_proj[1]]
        steps += _interleave(ff + down, kv_scan + diag + ret + inter_pairs + [hgrn_out])
        steps += _interleave(final, in_proj[2:])
        for step in steps:
            step()

    @pl.when(g == n_tiles)
    def _():
        out_proj, ff, down, final = tail_steps()
        for step in [out_proj] + ff + down + final:
            step()


def _prompt_layer(x, p, cos, sin, w_in, lb_logits, a_g, b_g, b_b, tail_w):
    bsz, seq, _ = x.shape
    tl = TOKEN_TILE
    tps = seq // tl
    n_tiles = bsz * tps
    x2 = x.reshape(bsz * seq, D_MODEL)
    p2 = p.reshape(bsz * seq, PLE_DIM)
    const = lambda g: (0, 0)
    nxt = lambda g: (jnp.minimum(g + 1, n_tiles - 1), 0)
    prev = lambda g: (jnp.maximum(g - 1, 0), 0)
    seq_tile = lambda g: (lax.rem(jnp.minimum(g, n_tiles - 1), tps), 0)
    state_spec = pl.BlockSpec((1, N_HEADS, HEAD_DIM, HEAD_DIM),
                              lambda g: (jnp.minimum(g, n_tiles - 1) // tps, 0, 0, 0))
    state_shape = jax.ShapeDtypeStruct((bsz, N_HEADS, HEAD_DIM, HEAD_DIM), F32)
    return pl.pallas_call(
        functools.partial(_prompt_layer_kernel, tiles_per_seq=tps),
        grid=(n_tiles + 1,),
        in_specs=[
            pl.BlockSpec((tl, D_MODEL), nxt),
            pl.BlockSpec((tl, D_MODEL), prev),
            pl.BlockSpec((tl, PLE_DIM), prev),
            pl.BlockSpec((tl, HEAD_DIM), seq_tile),
            pl.BlockSpec((tl, HEAD_DIM), seq_tile),
            pl.BlockSpec((D_MODEL, IN_COLS), const, pipeline_mode=pl.Buffered(1)),
            pl.BlockSpec(lb_logits.shape, const),
            pl.BlockSpec((1, GROUP_W), const),
            pl.BlockSpec((1, GROUP_W), const),
            pl.BlockSpec((1, GROUP_W), const),
        ] + _tail_specs(),
        out_specs=[
            pl.BlockSpec((tl, D_MODEL), prev),
            state_spec,
            state_spec,
        ],
        out_shape=[
            jax.ShapeDtypeStruct((bsz * seq, D_MODEL), F32),
            state_shape,
            state_shape,
        ],
        scratch_shapes=[
            pltpu.VMEM((2, tl, IN_COLS), F32),
            pltpu.VMEM((N_HEADS, HEAD_DIM, HEAD_DIM), F32),
            pltpu.VMEM((tl, GROUP_W), F32),
            pltpu.VMEM((tl, 2 * GROUP_W), BF16),
            pltpu.VMEM((tl, GROUP_W), BF16),
            pltpu.VMEM((N_HEADS, HEAD_DIM, tl), BF16),
            pltpu.VMEM((tl, 2 * GROUP_W), BF16),
            pltpu.VMEM((tl // REF_CHUNK, GROUP_W), F32),
            pltpu.VMEM((tl, GROUP_W), BF16),
            pltpu.VMEM((tl // (2 * REF_CHUNK), HEAD_DIM, 2 * HEAD_DIM), F32),
            pltpu.VMEM((tl // REF_CHUNK, N_HEADS, HEAD_DIM, HEAD_DIM), BF16),
            pltpu.VMEM((N_HEADS, tl, tl), F32),
            pltpu.VMEM((2, N_HEADS, tl, HEAD_DIM), F32),
            pltpu.VMEM((tl, D_MODEL), BF16),
        ] + _tail_scratch(tl),
        compiler_params=pltpu.CompilerParams(
            dimension_semantics=("arbitrary",), vmem_limit_bytes=V7X_VMEM_LIMIT_BYTES),
        name="prompt_layer",
    )(x2, x2, p2, cos, sin, w_in, lb_logits, a_g, b_g, b_b, *tail_w)


def _in_proj_kernel(x_ref, w_ref, o_ref):
    o_ref[...] = _dot(x_ref[...].astype(BF16), w_ref[...])


def _in_proj(x, w_in):
    n = x.shape[0]
    tn = SAMPLE_PROJ_COLS
    return pl.pallas_call(
        _in_proj_kernel,
        grid=(IN_COLS // tn,),
        in_specs=[
            pl.BlockSpec((n, D_MODEL), lambda c: (0, 0)),
            pl.BlockSpec((D_MODEL, tn), lambda c: (0, c)),
        ],
        out_specs=pl.BlockSpec((n, tn), lambda c: (0, c)),
        out_shape=jax.ShapeDtypeStruct((n, IN_COLS), F32),
        compiler_params=pltpu.CompilerParams(
            dimension_semantics=("arbitrary",), vmem_limit_bytes=V7X_VMEM_LIMIT_BYTES),
        name="sample_in_proj",
    )(x, w_in)


def _sample_rec_kernel(proj_ref, sa_in_ref, sb_in_ref, lb_ref, ag_ref, bg_ref, bb_ref,
                       cos_ref, sin_ref, mix_ref, sa_ref, sb_ref, oa_ref, ob_ref, *, seq_len):
    rows_n = proj_ref.shape[0]
    n_seq = rows_n // seq_len
    causal = _causal_in_chunk(rows_n, seq_len.bit_length() - 1)

    q_dec, k_dec, kk, b = _hgrn_prepass(proj_ref, _lower_bound(lb_ref), causal)
    v_a = proj_ref[:, 2 * GROUP_W:3 * GROUP_W]
    for h in range(N_HEADS):
        hs = _head(h)
        sc = jnp.where(causal, _dot_nt(q_dec[:, hs], k_dec[:, hs].astype(BF16)), 0.0).astype(BF16)
        oa_ref[:, hs] = _dot(sc, v_a[:, hs].astype(BF16))
    q_dec32 = q_dec.astype(F32)
    rr = lax.broadcasted_iota(jnp.int32, (seq_len, GROUP_W), 0)
    ones_blk = jnp.ones((seq_len, HEAD_DIM), BF16)
    for s in range(n_seq):
        rows = slice(s * seq_len, (s + 1) * seq_len)
        b_last = b[(s + 1) * seq_len - 1:(s + 1) * seq_len, :]
        k_end = (kk[rows] * jnp.exp(b_last - b[rows])).astype(BF16)
        hi, mid, lo = [t.astype(F32) for t in _split3(jnp.exp(b_last))]
        dec_rows = jnp.where(rr == 0, hi, jnp.where(rr == 1, mid, jnp.where(rr == 2, lo, 0.0)))
        dec_rows = dec_rows.astype(BF16)
        for h in range(N_HEADS):
            hs = _head(h)
            st = sa_in_ref[s, h]
            oa_ref[rows, hs] += _dot(q_dec32[rows, hs].astype(BF16), st.astype(BF16))
            dec_kv = _dot_tn(dec_rows[:, hs], ones_blk)
            sa_ref[s, h] = st * dec_kv + _dot_tn(k_end[:, hs], v_a[rows, hs].astype(BF16))
    for h in range(N_HEADS):
        hs = _head(h)
        mix_ref[:, hs] = _rms_gate(oa_ref[:, hs], ag_ref[:, hs], proj_ref[:, _head(h, 3)]).astype(BF16)

    cos = jnp.concatenate([cos_ref[...]] * n_seq, axis=0)
    sin = jnp.concatenate([sin_ref[...]] * n_seq, axis=0)
    r = lax.broadcasted_iota(jnp.int32, (rows_n, rows_n), 0)
    c = lax.broadcasted_iota(jnp.int32, (rows_n, rows_n), 1)
    diff = ((r & (seq_len - 1)) - (c & (seq_len - 1))).astype(F32)
    row = (lax.broadcasted_iota(jnp.int32, (rows_n, HEAD_DIM), 0) & (seq_len - 1)).astype(F32)
    for h in range(N_HEADS):
        hs = _head(h)
        logd = RET_LOG_DECAY[h]
        q = _rope(proj_ref[:, _head(h, 4)], cos, sin)
        k = _rope(proj_ref[:, _head(h, 5)], cos, sin) * K_SCALE
        v32 = proj_ref[:, _head(h, 6)]
        dmask = jnp.where(causal, jnp.exp(diff * logd), 0.0)
        a = (_dot_nt(q.astype(BF16), k.astype(BF16)) * dmask).astype(BF16)
        ob_ref[...] = _dot(a, v32.astype(BF16))
        q_dec_b = q * jnp.exp((row + 1.0) * logd)
        k_end_b = k * jnp.exp((seq_len - 1.0 - row) * logd)
        for s in range(n_seq):
            rows = slice(s * seq_len, (s + 1) * seq_len)
            st = sb_in_ref[s, h]
            ob_ref[rows, :] += _dot(q_dec_b[rows].astype(BF16), st.astype(BF16))
            sb_ref[s, h] = st * math.exp(seq_len * logd) + _dot_tn(
                k_end_b[rows].astype(BF16), v32[rows].astype(BF16))
        mix_ref[:, _head(h, 1)] = _ln_gate(ob_ref[...], bg_ref[:, hs], bb_ref[:, hs],
                                           proj_ref[:, _head(h, 7)]).astype(BF16)


def _sample_rec(proj, sa, sb, lb_logits, a_g, b_g, b_b, cos, sin, seq_len):
    n_tok = proj.shape[0]
    n_seq = n_tok // seq_len
    bs = SAMPLE_SEQS
    rows = bs * seq_len
    const = lambda i: (0, 0)
    state_spec = pl.BlockSpec((bs, N_HEADS, HEAD_DIM, HEAD_DIM), lambda i: (i, 0, 0, 0))
    state_shape = jax.ShapeDtypeStruct((n_seq, N_HEADS, HEAD_DIM, HEAD_DIM), F32)
    return pl.pallas_call(
        functools.partial(_sample_rec_kernel, seq_len=seq_len),
        grid=(n_seq // bs,),
        in_specs=[
            pl.BlockSpec((rows, IN_COLS), lambda i: (i, 0)),
            state_spec,
            state_spec,
            pl.BlockSpec(lb_logits.shape, const),
            pl.BlockSpec((1, GROUP_W), const),
            pl.BlockSpec((1, GROUP_W), const),
            pl.BlockSpec((1, GROUP_W), const),
            pl.BlockSpec((seq_len, HEAD_DIM), const),
            pl.BlockSpec((seq_len, HEAD_DIM), const),
        ],
        out_specs=[
            pl.BlockSpec((rows, 2 * GROUP_W), lambda i: (i, 0)),
            state_spec,
            state_spec,
        ],
        out_shape=[
            jax.ShapeDtypeStruct((n_tok, 2 * GROUP_W), BF16),
            state_shape,
            state_shape,
        ],
        scratch_shapes=[
            pltpu.VMEM((rows, GROUP_W), F32),
            pltpu.VMEM((rows, HEAD_DIM), F32),
        ],
        compiler_params=pltpu.CompilerParams(
            dimension_semantics=("arbitrary",), vmem_limit_bytes=V7X_VMEM_LIMIT_BYTES),
        name="sample_recurrence",
    )(proj, sa, sb, lb_logits, a_g, b_g, b_b, cos, sin)


def _tail_kernel(x_ref, mix_ref, p_ref, *rest):
    tail_w, y_ref, tail_scratch = rest[:11], rest[11], rest[12:]
    out_proj, ff, down, final = _tail_steps(x_ref, mix_ref, p_ref, y_ref, *tail_w, *tail_scratch)
    for step in [out_proj] + ff + down + final:
        step()


def _tail(x, mix, p, tail_w):
    n = x.shape[0]
    tl = SAMPLE_TAIL_TILE
    return pl.pallas_call(
        _tail_kernel,
        grid=(n // tl,),
        in_specs=[
            pl.BlockSpec((tl, D_MODEL), lambda i: (i, 0)),
            pl.BlockSpec((tl, 2 * GROUP_W), lambda i: (i, 0)),
            pl.BlockSpec((tl, PLE_DIM), lambda i: (i, 0)),
        ] + _tail_specs(),
        out_specs=pl.BlockSpec((tl, D_MODEL), lambda i: (i, 0)),
        out_shape=jax.ShapeDtypeStruct((n, D_MODEL), F32),
        scratch_shapes=_tail_scratch(tl),
        compiler_params=pltpu.CompilerParams(
            dimension_semantics=("arbitrary",), vmem_limit_bytes=V7X_VMEM_LIMIT_BYTES),
        name="sample_tail",
    )(x, mix, p, *tail_w)


def kernel(x_prompt, x_sample, p_prompt, p_sample, state_hgrn, state_ret, lb_logits, w_in, a_norm_g, b_norm_g, b_norm_b, w_out, ln1_g, ln1_b, w_ffn_gate, w_ffn_up, w_ffn_down, ln2_g, ln2_b, w_ple_proj, w_ple_gate, b_ple_gate):
    assert w_in.shape[0] == DEPTH == 1
    bsz, seq, _ = x_prompt.shape
    n_dec, dec_seq, _ = x_sample.shape

    w_in_b = w_in[0].astype(BF16)
    tail_w = (w_out[0].astype(BF16), ln1_g, ln1_b, w_ffn_gate[0].astype(BF16),
              w_ffn_up[0].astype(BF16), w_ffn_down[0].astype(BF16), ln2_g, ln2_b,
              w_ple_proj[0].astype(BF16), w_ple_gate[0].astype(BF16), b_ple_gate)
    mixer_vecs = (lb_logits, a_norm_g, b_norm_g, b_norm_b)

    cos_p, sin_p = _rope_tables(seq, 0)
    cos_s, sin_s = _rope_tables(dec_seq, PAST_LEN)

    y_p, sa_p, sb_p = _prompt_layer(x_prompt, p_prompt[0], cos_p, sin_p, w_in_b, *mixer_vecs, tail_w)

    proj_s = _in_proj(x_sample.reshape(n_dec * dec_seq, D_MODEL), w_in_b)
    mix_s, sa_s, sb_s = _sample_rec(proj_s, state_hgrn[0], state_ret[0], *mixer_vecs,
                                    cos_s, sin_s, dec_seq)
    y_s = _tail(x_sample.reshape(n_dec * dec_seq, D_MODEL), mix_s,
                p_sample[0].reshape(n_dec * dec_seq, PLE_DIM), tail_w)

    return (y_p.reshape(bsz, seq, D_MODEL), y_s.reshape(n_dec, dec_seq, D_MODEL),
            sa_p[None], sb_p[None], sa_s[None], sb_s[None])
```

```python
import functools
import math

import jax
import jax.numpy as jnp
from jax import lax
from jax.experimental import pallas as pl
from jax.experimental.pallas import tpu as pltpu

F32 = jnp.float32
BF16 = jnp.bfloat16

D_MODEL = 1024
N_HEADS = 4
HEAD_DIM = 128
GROUP_W = N_HEADS * HEAD_DIM
IN_COLS = 8 * GROUP_W
D_FF = 2816
PLE_DIM = 256
DEPTH = 1
PAST_LEN = 16384
REF_CHUNK = 32
ROPE_BASE = 10000.0
NORM_EPS = 1e-5
DN_ALPHA = (2.0 * DEPTH) ** 0.25
RET_LOG_DECAY = tuple(math.log1p(-(2.0 ** (-5.0 - h))) for h in range(N_HEADS))
K_SCALE = HEAD_DIM ** -0.5

V7X_VMEM_LIMIT_BYTES = 60 * 1024 * 1024

TOKEN_TILE = 256
SAMPLE_PROJ_COLS = 1024
SAMPLE_SEQS = 8
FF_CHUNK = 256
DOWN_CHUNK = 256


def _dot(a, b):
    return jnp.dot(a, b, preferred_element_type=F32)


def _dot_nt(a, b):
    return lax.dot_general(a, b, (((1,), (1,)), ((), ())), preferred_element_type=F32)


def _dot_tn(a, b):
    return lax.dot_general(a, b, (((0,), (0,)), ((), ())), preferred_element_type=F32)


def _split3(x):
    hi = x.astype(BF16)
    r1 = x - hi.astype(F32)
    mid = r1.astype(BF16)
    lo = (r1 - mid.astype(F32)).astype(BF16)
    return hi, mid, lo


def _dot_exact_lhs01(m01, parts):
    hi, mid, lo = parts
    return _dot(m01, hi) + _dot(m01, mid) + _dot(m01, lo)


def _sigmoid(x):
    return 1.0 / (1.0 + jnp.exp(-x))


def _silu(x):
    return x * _sigmoid(x)


def _causal_in_chunk(n, shift):
    r = lax.broadcasted_iota(jnp.int32, (n, n), 0)
    c = lax.broadcasted_iota(jnp.int32, (n, n), 1)
    return ((r >> shift) == (c >> shift)) & (c <= r)


def _lower_bound(lb_ref):
    rows = [lb_ref[i:i + 1, :] for i in range(lb_ref.shape[0])]
    m = functools.reduce(jnp.maximum, rows)
    e = [jnp.exp(r - m) for r in rows]
    return e[0] / functools.reduce(jnp.add, e)


def _hgrn_prepass(proj_ref, lb, causal):
    tri = jnp.where(causal, 1.0, 0.0).astype(BF16)
    f = lb + (1.0 - lb) * _sigmoid(proj_ref[:, GROUP_W:2 * GROUP_W])
    kk = 1.0 - f
    b = _dot_exact_lhs01(tri, _split3(jnp.log(f)))
    q_dec = (_silu(proj_ref[:, 0:GROUP_W]) * jnp.exp(b)).astype(BF16)
    k_dec = kk * jnp.exp(-b)
    return q_dec, k_dec, kk, b


def _rope(x, cos, sin_signed):
    return x * cos + pltpu.roll(x, HEAD_DIM // 2, axis=1) * sin_signed


def _rms_gate(o, g, gate):
    return o * lax.rsqrt(jnp.mean(o * o, axis=-1, keepdims=True) + NORM_EPS) * g * _silu(gate)


def _ln_gate(o, g, b, gate):
    mu = jnp.mean(o, axis=-1, keepdims=True)
    d = o - mu
    var = jnp.mean(d * d, axis=-1, keepdims=True)
    return (d * lax.rsqrt(var + NORM_EPS) * g + b) * _silu(gate)


def _layer_norm(x, g, b):
    mu = jnp.mean(x, axis=-1, keepdims=True)
    d = x - mu
    var = jnp.mean(d * d, axis=-1, keepdims=True)
    return d * lax.rsqrt(var + NORM_EPS) * g + b


def _head(h, group=0):
    return slice(group * GROUP_W + h * HEAD_DIM, group * GROUP_W + (h + 1) * HEAD_DIM)


def _rope_table_kernel(cos_ref, sin_ref, *, offset):
    n = cos_ref.shape[0]
    half = HEAD_DIM // 2
    row = lax.broadcasted_iota(jnp.int32, (n, HEAD_DIM), 0) + pl.program_id(0) * n
    lane = lax.broadcasted_iota(jnp.int32, (n, HEAD_DIM), 1)
    j = (lane & (half - 1)).astype(F32)
    inv = jnp.exp(-(j / half) * math.log(ROPE_BASE))
    ang = (row.astype(F32) + offset) * inv
    cos_ref[...] = jnp.cos(ang)
    s = jnp.sin(ang)
    sin_ref[...] = jnp.where(lane < half, -s, s)


def _rope_tables(n, offset):
    tile = min(n, 512)
    return pl.pallas_call(
        functools.partial(_rope_table_kernel, offset=float(offset)),
        grid=(n // tile,),
        in_specs=[],
        out_specs=[pl.BlockSpec((tile, HEAD_DIM), lambda i: (i, 0))] * 2,
        out_shape=[jax.ShapeDtypeStruct((n, HEAD_DIM), F32)] * 2,
        name="rope_tables",
    )()


def _tail_steps(x_ref, mix_ref, p_ref, y_ref, w_out_ref, ln1g_ref, ln1b_ref, wg_ref, wu_ref, wd_ref,
                ln2g_ref, ln2b_ref, wpp_ref, wpg_ref, bpg_ref, act_ref, h_ref, hb_ref):
    def out_proj():
        h = _layer_norm(DN_ALPHA * x_ref[...] + _dot(mix_ref[...], w_out_ref[...]),
                        ln1g_ref[...], ln1b_ref[...])
        h_ref[...] = h
        hb_ref[...] = h.astype(BF16)

    def ff(c):
        cols = slice(c * FF_CHUNK, (c + 1) * FF_CHUNK)
        hb = hb_ref[...]
        act_ref[:, cols] = (_silu(_dot(hb, wg_ref[:, cols])) * _dot(hb, wu_ref[:, cols])).astype(BF16)

    def down(c):
        cols = slice(c * DOWN_CHUNK, (c + 1) * DOWN_CHUNK)
        h_ref[:, cols] = DN_ALPHA * h_ref[:, cols] + _dot(act_ref[...], wd_ref[:, cols])

    def norm2():
        h2 = _layer_norm(h_ref[...], ln2g_ref[...], ln2b_ref[...])
        h_ref[...] = h2
        hb_ref[...] = h2.astype(BF16)

    def ple(c):
        cols = slice(c * DOWN_CHUNK, (c + 1) * DOWN_CHUNK)
        gate = _sigmoid(_dot(hb_ref[...], wpg_ref[:, cols]) + bpg_ref[:, cols])
        y_ref[:, cols] = h_ref[:, cols] + gate * _dot(p_ref[...].astype(BF16), wpp_ref[:, cols])

    return (out_proj,
            [functools.partial(ff, c) for c in range(D_FF // FF_CHUNK)],
            [functools.partial(down, c) for c in range(D_MODEL // DOWN_CHUNK)],
            [norm2] + [functools.partial(ple, c) for c in range(D_MODEL // DOWN_CHUNK)])


def _tail_scratch(tl):
    return [pltpu.VMEM((tl, D_FF), BF16), pltpu.VMEM((tl, D_MODEL), F32), pltpu.VMEM((tl, D_MODEL), BF16)]


def _tail_specs():
    const = lambda i: (0, 0)
    resident = lambda shape: pl.BlockSpec(shape, const, pipeline_mode=pl.Buffered(1))
    vec = pl.BlockSpec((1, D_MODEL), const)
    return [
        resident((2 * GROUP_W, D_MODEL)),
        vec, vec,
        resident((D_MODEL, D_FF)),
        resident((D_MODEL, D_FF)),
        resident((D_FF, D_MODEL)),
        vec, vec,
        resident((PLE_DIM, D_MODEL)),
        resident((D_MODEL, D_MODEL)),
        vec,
    ]


def _in_proj_steps(x_ref, w_in_ref, proj_ref, xb_ref):
    def in_proj(c):
        if c == 0:
            xb_ref[...] = x_ref[...].astype(BF16)
        cols = slice(c * GROUP_W, (c + 1) * GROUP_W)
        proj_ref[:, cols] = _dot(xb_ref[...], w_in_ref[:, cols])

    return [functools.partial(in_proj, c) for c in range(IN_COLS // GROUP_W)]


def _ret_tables(dm_ref, rd_ref):
    tl = dm_ref.shape[1]
    r = lax.broadcasted_iota(jnp.int32, (tl, tl), 0)
    c = lax.broadcasted_iota(jnp.int32, (tl, tl), 1)
    row = lax.broadcasted_iota(jnp.int32, (tl, HEAD_DIM), 0).astype(F32)
    for h in range(N_HEADS):
        logd = RET_LOG_DECAY[h]
        dm_ref[h] = jnp.where(r >= c, jnp.exp((r - c).astype(F32) * logd), 0.0)
        rd_ref[0, h] = jnp.exp((row + 1.0) * logd)
        rd_ref[1, h] = jnp.exp((tl - 1.0 - row) * logd)


def _prompt_mixer_steps(cos_ref, sin_ref, lb_ref, ag_ref, bg_ref, bb_ref,
                        sa_ref, sb_ref, proj_ref, st_ref, oa_ref, mix_ref,
                        qd_ref, kd_ref, ke_ref, dec_ref, va_ref, kv_ref, sbf_ref, dm_ref, rd_ref, first):
    tl = proj_ref.shape[0]
    shift = REF_CHUNK.bit_length() - 1
    n_chunks = tl // REF_CHUNK

    def prepass():
        q_dec, k_dec, kk, b = _hgrn_prepass(proj_ref, _lower_bound(lb_ref), _causal_in_chunk(tl, shift))
        qd_ref[...] = q_dec
        for h in range(N_HEADS):
            kd_ref[h] = k_dec[:, _head(h)].T.astype(BF16)
        va_ref[...] = proj_ref[:, 2 * GROUP_W:3 * GROUP_W].astype(BF16)
        last = [b[(n + 1) * REF_CHUNK - 1:(n + 1) * REF_CHUNK, :] for n in range(n_chunks)]
        b_last = jnp.concatenate([jnp.broadcast_to(r, (REF_CHUNK, GROUP_W)) for r in last], axis=0)
        k_end = kk * jnp.exp(b_last - b)
        for n in range(n_chunks):
            dec_ref[n:n + 1, :] = jnp.exp(last[n])
        odd = ((lax.broadcasted_iota(jnp.int32, (tl, GROUP_W), 0) >> shift) & 1) == 1
        k_even = jnp.where(odd, 0.0, k_end).astype(BF16)
        k_odd = jnp.where(odd, k_end, 0.0).astype(BF16)
        for h in range(N_HEADS):
            ke_ref[:, 2 * h * HEAD_DIM:(2 * h + 1) * HEAD_DIM] = k_even[:, _head(h)]
            ke_ref[:, (2 * h + 1) * HEAD_DIM:(2 * h + 2) * HEAD_DIM] = k_odd[:, _head(h)]

    def kv_scan(h):
        hs = _head(h)
        pair = 2 * REF_CHUNK
        for r in range(n_chunks // 2):
            rows = slice(r * pair, (r + 1) * pair)
            kv_ref[r] = _dot_tn(va_ref[rows, hs], ke_ref[rows, 2 * h * HEAD_DIM:(2 * h + 2) * HEAD_DIM])
        st = jnp.where(first, 0.0, st_ref[h])
        for n in range(n_chunks):
            sbf_ref[n, h] = st.T.astype(BF16)
            st = st * dec_ref[n:n + 1, hs] + kv_ref[n // 2, :, (n % 2) * HEAD_DIM:(n % 2 + 1) * HEAD_DIM]
        st_ref[h] = st
        sa_ref[0, h] = st.T

    def diag(h):
        hs = _head(h)
        sc = jnp.where(_causal_in_chunk(tl, shift), _dot(qd_ref[:, hs], kd_ref[h]), 0.0)
        oa_ref[:, hs] = _dot(sc.astype(BF16), va_ref[:, hs])

    def inter(n):
        rows = slice(n * REF_CHUNK, (n + 1) * REF_CHUNK)
        for h in range(N_HEADS):
            hs = _head(h)
            oa_ref[rows, hs] += _dot(qd_ref[rows, hs], sbf_ref[n, h])

    def hgrn_out():
        for h in range(N_HEADS):
            hs = _head(h)
            mix_ref[:, hs] = _rms_gate(oa_ref[:, hs], ag_ref[:, hs], proj_ref[:, _head(h, 3)]).astype(BF16)

    def ret(h):
        hs = _head(h)
        cos = cos_ref[...]
        sin = sin_ref[...]
        q = _rope(proj_ref[:, _head(h, 4)], cos, sin)
        k = _rope(proj_ref[:, _head(h, 5)], cos, sin) * K_SCALE
        v = proj_ref[:, _head(h, 6)].astype(BF16)
        a = (_dot(q.astype(BF16), k.T.astype(BF16)) * dm_ref[h]).astype(BF16)
        s = jnp.where(first, 0.0, sb_ref[0, h])
        o = _dot(a, v) + _dot((q * rd_ref[0, h]).astype(BF16), s.astype(BF16))
        k_end_b = (k * rd_ref[1, h]).astype(BF16)
        sb_ref[0, h] = s * math.exp(tl * RET_LOG_DECAY[h]) + _dot_tn(k_end_b, v)
        mix_ref[:, _head(h, 1)] = _ln_gate(o, bg_ref[:, hs], bb_ref[:, hs],
                                           proj_ref[:, _head(h, 7)]).astype(BF16)

    return (prepass,
            [functools.partial(kv_scan, h) for h in range(N_HEADS)],
            [functools.partial(diag, h) for h in range(N_HEADS)],
            [functools.partial(inter, n) for n in range(n_chunks)],
            hgrn_out,
            [functools.partial(ret, h) for h in range(N_HEADS)])


def _interleave(a, b):
    out = []
    for i in range(max(len(a), len(b))):
        out += a[i:i + 1] + b[i:i + 1]
    return out


def _prompt_layer_kernel(xn_ref, xp_ref, p_ref, cos_ref, sin_ref, w_in_ref, lb_ref, ag_ref, bg_ref,
                         bb_ref, *rest, tiles_per_seq):
    tail_w = rest[:11]
    y_ref, sa_ref, sb_ref = rest[11:14]
    mixer_scratch = rest[14:27]
    proj_ref, mix_ref, dm_ref, rd_ref = mixer_scratch[0], mixer_scratch[3], mixer_scratch[11], mixer_scratch[12]
    xb_ref = rest[27]
    tail_scratch = rest[28:]
    g = pl.program_id(0)
    n_tiles = pl.num_programs(0) - 1
    slot = lax.rem(g, 2)

    def mixer_steps():
        return _prompt_mixer_steps(cos_ref, sin_ref, lb_ref, ag_ref, bg_ref, bb_ref, sa_ref, sb_ref,
                                   proj_ref.at[slot], *mixer_scratch[1:],
                                   first=lax.rem(g, tiles_per_seq) == 0)

    def tail_steps():
        return _tail_steps(xp_ref, mix_ref, p_ref, y_ref, *tail_w, *tail_scratch)

    @pl.when(g == 0)
    def _():
        _ret_tables(dm_ref, rd_ref)
        prepass, kv_scan, diag, inter, hgrn_out, ret = mixer_steps()
        steps = _in_proj_steps(xp_ref, w_in_ref, proj_ref.at[slot], xb_ref)
        steps += [prepass] + kv_scan + diag + inter + [hgrn_out] + ret
        steps += _in_proj_steps(xn_ref, w_in_ref, proj_ref.at[1 - slot], xb_ref)
        for step in steps:
            step()

    @pl.when((g > 0) & (g < n_tiles))
    def _():
        prepass, kv_scan, diag, inter, hgrn_out, ret = mixer_steps()
        out_proj, ff, down, final = tail_steps()
        in_proj = _in_proj_steps(xn_ref, w_in_ref, proj_ref.at[1 - slot], xb_ref)
        inter_pairs = [lambda a=a, b=b: (a(), b()) for a, b in zip(inter[0::2], inter[1::2])]
        steps = [out_proj, in_proj[0], prepass, in_proj[1]]
        steps += _interleave(ff + down, kv_scan + diag + ret + inter_pairs + [hgrn_out])
        steps += _interleave(final, in_proj[2:])
        for step in steps:
            step()

    @pl.when(g == n_tiles)
    def _():
        out_proj, ff, down, final = tail_steps()
        for step in [out_proj] + ff + down + final:
            step()


def _prompt_layer(x, p, cos, sin, w_in, lb_logits, a_g, b_g, b_b, tail_w):
    bsz, seq, _ = x.shape
    tl = TOKEN_TILE
    tps = seq // tl
    n_tiles = bsz * tps
    x2 = x.reshape(bsz * seq, D_MODEL)
    p2 = p.reshape(bsz * seq, PLE_DIM)
    const = lambda g: (0, 0)
    nxt = lambda g: (jnp.minimum(g + 1, n_tiles - 1), 0)
    prev = lambda g: (jnp.maximum(g - 1, 0), 0)
    seq_tile = lambda g: (lax.rem(jnp.minimum(g, n_tiles - 1), tps), 0)
    state_spec = pl.BlockSpec((1, N_HEADS, HEAD_DIM, HEAD_DIM),
                              lambda g: (jnp.minimum(g, n_tiles - 1) // tps, 0, 0, 0))
    state_shape = jax.ShapeDtypeStruct((bsz, N_HEADS, HEAD_DIM, HEAD_DIM), F32)
    return pl.pallas_call(
        functools.partial(_prompt_layer_kernel, tiles_per_seq=tps),
        grid=(n_tiles + 1,),
        in_specs=[
            pl.BlockSpec((tl, D_MODEL), nxt),
            pl.BlockSpec((tl, D_MODEL), prev),
            pl.BlockSpec((tl, PLE_DIM), prev),
            pl.BlockSpec((tl, HEAD_DIM), seq_tile),
            pl.BlockSpec((tl, HEAD_DIM), seq_tile),
            pl.BlockSpec((D_MODEL, IN_COLS), const, pipeline_mode=pl.Buffered(1)),
            pl.BlockSpec(lb_logits.shape, const),
            pl.BlockSpec((1, GROUP_W), const),
            pl.BlockSpec((1, GROUP_W), const),
            pl.BlockSpec((1, GROUP_W), const),
        ] + _tail_specs(),
        out_specs=[
            pl.BlockSpec((tl, D_MODEL), prev),
            state_spec,
            state_spec,
        ],
        out_shape=[
            jax.ShapeDtypeStruct((bsz * seq, D_MODEL), F32),
            state_shape,
            state_shape,
        ],
        scratch_shapes=[
            pltpu.VMEM((2, tl, IN_COLS), F32),
            pltpu.VMEM((N_HEADS, HEAD_DIM, HEAD_DIM), F32),
            pltpu.VMEM((tl, GROUP_W), F32),
            pltpu.VMEM((tl, 2 * GROUP_W), BF16),
            pltpu.VMEM((tl, GROUP_W), BF16),
            pltpu.VMEM((N_HEADS, HEAD_DIM, tl), BF16),
            pltpu.VMEM((tl, 2 * GROUP_W), BF16),
            pltpu.VMEM((tl // REF_CHUNK, GROUP_W), F32),
            pltpu.VMEM((tl, GROUP_W), BF16),
            pltpu.VMEM((tl // (2 * REF_CHUNK), HEAD_DIM, 2 * HEAD_DIM), F32),
            pltpu.VMEM((tl // REF_CHUNK, N_HEADS, HEAD_DIM, HEAD_DIM), BF16),
            pltpu.VMEM((N_HEADS, tl, tl), F32),
            pltpu.VMEM((2, N_HEADS, tl, HEAD_DIM), F32),
            pltpu.VMEM((tl, D_MODEL), BF16),
        ] + _tail_scratch(tl),
        compiler_params=pltpu.CompilerParams(
            dimension_semantics=("arbitrary",), vmem_limit_bytes=V7X_VMEM_LIMIT_BYTES),
        name="prompt_layer",
    )(x2, x2, p2, cos, sin, w_in, lb_logits, a_g, b_g, b_b, *tail_w)


def _in_proj_kernel(x_ref, w_ref, o_ref, wb_ref):
    wb_ref[...] = w_ref[...].astype(BF16)
    o_ref[...] = _dot(x_ref[...].astype(BF16), wb_ref[...])


def _in_proj(x, w_in):
    n = x.shape[0]
    tn = SAMPLE_PROJ_COLS
    return pl.pallas_call(
        _in_proj_kernel,
        grid=(IN_COLS // tn,),
        in_specs=[
            pl.BlockSpec((n, D_MODEL), lambda c: (0, 0)),
            pl.BlockSpec((D_MODEL, tn), lambda c: (0, c)),
        ],
        out_specs=[
            pl.BlockSpec((n, tn), lambda c: (0, c)),
            pl.BlockSpec((D_MODEL, tn), lambda c: (0, c)),
        ],
        out_shape=[
            jax.ShapeDtypeStruct((n, IN_COLS), F32),
            jax.ShapeDtypeStruct((D_MODEL, IN_COLS), BF16),
        ],
        compiler_params=pltpu.CompilerParams(
            dimension_semantics=("arbitrary",), vmem_limit_bytes=V7X_VMEM_LIMIT_BYTES),
        name="sample_in_proj",
    )(x, w_in)


def _sample_rec_kernel(proj_ref, sa_in_ref, sb_in_ref, lb_ref, ag_ref, bg_ref, bb_ref,
                       cos_ref, sin_ref, mix_ref, sa_ref, sb_ref, oa_ref, ob_ref, *, seq_len):
    rows_n = proj_ref.shape[0]
    n_seq = rows_n // seq_len
    causal = _causal_in_chunk(rows_n, seq_len.bit_length() - 1)

    q_dec, k_dec, kk, b = _hgrn_prepass(proj_ref, _lower_bound(lb_ref), causal)
    v_a = proj_ref[:, 2 * GROUP_W:3 * GROUP_W]
    for h in range(N_HEADS):
        hs = _head(h)
        sc = jnp.where(causal, _dot_nt(q_dec[:, hs], k_dec[:, hs].astype(BF16)), 0.0).astype(BF16)
        oa_ref[:, hs] = _dot(sc, v_a[:, hs].astype(BF16))
    q_dec32 = q_dec.astype(F32)
    rr = lax.broadcasted_iota(jnp.int32, (seq_len, GROUP_W), 0)
    ones_blk = jnp.ones((seq_len, HEAD_DIM), BF16)
    for s in range(n_seq):
        rows = slice(s * seq_len, (s + 1) * seq_len)
        b_last = b[(s + 1) * seq_len - 1:(s + 1) * seq_len, :]
        k_end = (kk[rows] * jnp.exp(b_last - b[rows])).astype(BF16)
        hi, mid, lo = [t.astype(F32) for t in _split3(jnp.exp(b_last))]
        dec_rows = jnp.where(rr == 0, hi, jnp.where(rr == 1, mid, jnp.where(rr == 2, lo, 0.0)))
        dec_rows = dec_rows.astype(BF16)
        for h in range(N_HEADS):
            hs = _head(h)
            st = sa_in_ref[s, h]
            oa_ref[rows, hs] += _dot(q_dec32[rows, hs].astype(BF16), st.astype(BF16))
            dec_kv = _dot_tn(dec_rows[:, hs], ones_blk)
            sa_ref[s, h] = st * dec_kv + _dot_tn(k_end[:, hs], v_a[rows, hs].astype(BF16))
    for h in range(N_HEADS):
        hs = _head(h)
        mix_ref[:, hs] = _rms_gate(oa_ref[:, hs], ag_ref[:, hs], proj_ref[:, _head(h, 3)]).astype(BF16)

    cos = jnp.concatenate([cos_ref[...]] * n_seq, axis=0)
    sin = jnp.concatenate([sin_ref[...]] * n_seq, axis=0)
    r = lax.broadcasted_iota(jnp.int32, (rows_n, rows_n), 0)
    c = lax.broadcasted_iota(jnp.int32, (rows_n, rows_n), 1)
    diff = ((r & (seq_len - 1)) - (c & (seq_len - 1))).astype(F32)
    row = (lax.broadcasted_iota(jnp.int32, (rows_n, HEAD_DIM), 0) & (seq_len - 1)).astype(F32)
    for h in range(N_HEADS):
        hs = _head(h)
        logd = RET_LOG_DECAY[h]
        q = _rope(proj_ref[:, _head(h, 4)], cos, sin)
        k = _rope(proj_ref[:, _head(h, 5)], cos, sin) * K_SCALE
        v32 = proj_ref[:, _head(h, 6)]
        dmask = jnp.where(causal, jnp.exp(diff * logd), 0.0)
        a = (_dot_nt(q.astype(BF16), k.astype(BF16)) * dmask).astype(BF16)
        ob_ref[...] = _dot(a, v32.astype(BF16))
        q_dec_b = q * jnp.exp((row + 1.0) * logd)
        k_end_b = k * jnp.exp((seq_len - 1.0 - row) * logd)
        for s in range(n_seq):
            rows = slice(s * seq_len, (s + 1) * seq_len)
            st = sb_in_ref[s, h]
            ob_ref[rows, :] += _dot(q_dec_b[rows].astype(BF16), st.astype(BF16))
            sb_ref[s, h] = st * math.exp(seq_len * logd) + _dot_tn(
                k_end_b[rows].astype(BF16), v32[rows].astype(BF16))
        mix_ref[:, _head(h, 1)] = _ln_gate(ob_ref[...], bg_ref[:, hs], bb_ref[:, hs],
                                           proj_ref[:, _head(h, 7)]).astype(BF16)


def _sample_rec(proj, sa, sb, lb_logits, a_g, b_g, b_b, cos, sin, seq_len):
    n_tok = proj.shape[0]
    n_seq = n_tok // seq_len
    bs = SAMPLE_SEQS
    rows = bs * seq_len
    const = lambda i: (0, 0)
    state_spec = pl.BlockSpec((bs, N_HEADS, HEAD_DIM, HEAD_DIM), lambda i: (i, 0, 0, 0))
    state_shape = jax.ShapeDtypeStruct((n_seq, N_HEADS, HEAD_DIM, HEAD_DIM), F32)
    return pl.pallas_call(
        functools.partial(_sample_rec_kernel, seq_len=seq_len),
        grid=(n_seq // bs,),
        in_specs=[
            pl.BlockSpec((rows, IN_COLS), lambda i: (i, 0)),
            state_spec,
            state_spec,
            pl.BlockSpec(lb_logits.shape, const),
            pl.BlockSpec((1, GROUP_W), const),
            pl.BlockSpec((1, GROUP_W), const),
            pl.BlockSpec((1, GROUP_W), const),
            pl.BlockSpec((seq_len, HEAD_DIM), const),
            pl.BlockSpec((seq_len, HEAD_DIM), const),
        ],
        out_specs=[
            pl.BlockSpec((rows, 2 * GROUP_W), lambda i: (i, 0)),
            state_spec,
            state_spec,
        ],
        out_shape=[
            jax.ShapeDtypeStruct((n_tok, 2 * GROUP_W), BF16),
            state_shape,
            state_shape,
        ],
        scratch_shapes=[
            pltpu.VMEM((rows, GROUP_W), F32),
            pltpu.VMEM((rows, HEAD_DIM), F32),
        ],
        compiler_params=pltpu.CompilerParams(
            dimension_semantics=("arbitrary",), vmem_limit_bytes=V7X_VMEM_LIMIT_BYTES),
        name="sample_recurrence",
    )(proj, sa, sb, lb_logits, a_g, b_g, b_b, cos, sin)


_TAIL_PHASE_STEPS = (D_MODEL // DOWN_CHUNK, D_FF // FF_CHUNK, D_MODEL // DOWN_CHUNK, D_MODEL // DOWN_CHUNK)
_TAIL_PHASE_START = tuple(sum(_TAIL_PHASE_STEPS[:i]) for i in range(4))


def _sample_tail_kernel(x_ref, mix_ref, p_ref, wo_ref, wg_ref, wu_ref, wd_ref, wpg_ref, wpp_ref,
                        ln1g_ref, ln1b_ref, ln2g_ref, ln2b_ref, bpg_ref,
                        y_ref, wo_b, wg_b, wu_b, wd_b, wpg_b, wpp_b, act_ref, h_ref, hb_ref):
    s = pl.program_id(0)
    a0, b0, c0, d0 = _TAIL_PHASE_START
    blk = DOWN_CHUNK

    @pl.when(s < b0)
    def _():
        wo_b[...] = wo_ref[...].astype(BF16)
        cols = pl.ds(pl.multiple_of((s - a0) * blk, blk), blk)
        h_ref[:, cols] = DN_ALPHA * x_ref[:, cols] + _dot(mix_ref[...], wo_b[...])

        @pl.when(s == b0 - 1)
        def _():
            h = _layer_norm(h_ref[...], ln1g_ref[...], ln1b_ref[...])
            h_ref[...] = h
            hb_ref[...] = h.astype(BF16)

    @pl.when((s >= b0) & (s < c0))
    def _():
        wg_b[...] = wg_ref[...].astype(BF16)
        wu_b[...] = wu_ref[...].astype(BF16)
        cols = pl.ds(pl.multiple_of((s - b0) * FF_CHUNK, FF_CHUNK), FF_CHUNK)
        hb = hb_ref[...]
        act_ref[:, cols] = (_silu(_dot(hb, wg_b[...])) * _dot(hb, wu_b[...])).astype(BF16)

    @pl.when((s >= c0) & (s < d0))
    def _():
        wd_b[...] = wd_ref[...].astype(BF16)
        cols = pl.ds(pl.multiple_of((s - c0) * blk, blk), blk)
        h_ref[:, cols] = DN_ALPHA * h_ref[:, cols] + _dot(act_ref[...], wd_b[...])

        @pl.when(s == d0 - 1)
        def _():
            h2 = _layer_norm(h_ref[...], ln2g_ref[...], ln2b_ref[...])
            h_ref[...] = h2
            hb_ref[...] = h2.astype(BF16)

    @pl.when(s >= d0)
    def _():
        wpg_b[...] = wpg_ref[...].astype(BF16)
        wpp_b[...] = wpp_ref[...].astype(BF16)
        cols = pl.ds(pl.multiple_of((s - d0) * blk, blk), blk)
        gate = _sigmoid(_dot(hb_ref[...], wpg_b[...]) + bpg_ref[:, cols])
        y_ref[...] = h_ref[:, cols] + gate * _dot(p_ref[...].astype(BF16), wpp_b[...])


def _sample_tail(x, mix, p, w_out, ln1g, ln1b, wg, wu, wd, ln2g, ln2b, wpp, wpg, bpg):
    n = x.shape[0]
    a0, b0, c0, d0 = _TAIL_PHASE_START
    na, nb, nc, nd = _TAIL_PHASE_STEPS
    const = lambda s: (0, 0)
    col = lambda start, count: (lambda s: (0, jnp.clip(s - start, 0, count - 1)))
    vec = pl.BlockSpec((1, D_MODEL), const)
    weight_specs = [
        pl.BlockSpec((2 * GROUP_W, DOWN_CHUNK), col(a0, na)),
        pl.BlockSpec((D_MODEL, FF_CHUNK), col(b0, nb)),
        pl.BlockSpec((D_MODEL, FF_CHUNK), col(b0, nb)),
        pl.BlockSpec((D_FF, DOWN_CHUNK), col(c0, nc)),
        pl.BlockSpec((D_MODEL, DOWN_CHUNK), col(d0, nd)),
        pl.BlockSpec((PLE_DIM, DOWN_CHUNK), col(d0, nd)),
    ]
    weights = (w_out, wg, wu, wd, wpg, wpp)
    return pl.pallas_call(
        _sample_tail_kernel,
        grid=(sum(_TAIL_PHASE_STEPS),),
        in_specs=[
            pl.BlockSpec((n, D_MODEL), const),
            pl.BlockSpec((n, 2 * GROUP_W), const),
            pl.BlockSpec((n, PLE_DIM), const),
        ] + weight_specs + [vec, vec, vec, vec, vec],
        out_specs=[pl.BlockSpec((n, DOWN_CHUNK), col(d0, nd))] + weight_specs,
        out_shape=[jax.ShapeDtypeStruct((n, D_MODEL), F32)]
        + [jax.ShapeDtypeStruct(w.shape, BF16) for w in weights],
        scratch_shapes=_tail_scratch(n),
        compiler_params=pltpu.CompilerParams(
            dimension_semantics=("arbitrary",), vmem_limit_bytes=V7X_VMEM_LIMIT_BYTES),
        name="sample_tail",
    )(x, mix, p, *weights, ln1g, ln1b, ln2g, ln2b, bpg)


def kernel(x_prompt, x_sample, p_prompt, p_sample, state_hgrn, state_ret, lb_logits, w_in, a_norm_g, b_norm_g, b_norm_b, w_out, ln1_g, ln1_b, w_ffn_gate, w_ffn_up, w_ffn_down, ln2_g, ln2_b, w_ple_proj, w_ple_gate, b_ple_gate):
    assert w_in.shape[0] == DEPTH == 1
    bsz, seq, _ = x_prompt.shape
    n_dec, dec_seq, _ = x_sample.shape

    mixer_vecs = (lb_logits, a_norm_g, b_norm_g, b_norm_b)
    cos_p, sin_p = _rope_tables(seq, 0)
    cos_s, sin_s = _rope_tables(dec_seq, PAST_LEN)

    x_s = x_sample.reshape(n_dec * dec_seq, D_MODEL)
    proj_s, w_in_b = _in_proj(x_s, w_in[0])
    mix_s, sa_s, sb_s = _sample_rec(proj_s, state_hgrn[0], state_ret[0], *mixer_vecs,
                                    cos_s, sin_s, dec_seq)
    y_s, w_out_b, wg_b, wu_b, wd_b, wpg_b, wpp_b = _sample_tail(
        x_s, mix_s, p_sample[0].reshape(n_dec * dec_seq, PLE_DIM), w_out[0], ln1_g, ln1_b,
        w_ffn_gate[0], w_ffn_up[0], w_ffn_down[0], ln2_g, ln2_b, w_ple_proj[0], w_ple_gate[0],
        b_ple_gate)

    tail_w = (w_out_b, ln1_g, ln1_b, wg_b, wu_b, wd_b, ln2_g, ln2_b, wpp_b, wpg_b, b_ple_gate)
    y_p, sa_p, sb_p = _prompt_layer(x_prompt, p_prompt[0], cos_p, sin_p, w_in_b, *mixer_vecs, tail_w)

    return (y_p.reshape(bsz, seq, D_MODEL), y_s.reshape(n_dec, dec_seq, D_MODEL),
            sa_p[None], sb_p[None], sa_s[None], sb_s[None])
```

```python
import functools
import math

import jax
import jax.numpy as jnp
from jax import lax
from jax.experimental import pallas as pl
from jax.experimental.pallas import tpu as pltpu

F32 = jnp.float32
BF16 = jnp.bfloat16

D_MODEL = 1024
N_HEADS = 4
HEAD_DIM = 128
GROUP_W = N_HEADS * HEAD_DIM
IN_COLS = 8 * GROUP_W
D_FF = 2816
PLE_DIM = 256
DEPTH = 1
PAST_LEN = 16384
REF_CHUNK = 32
ROPE_BASE = 10000.0
NORM_EPS = 1e-5
DN_ALPHA = (2.0 * DEPTH) ** 0.25
RET_LOG_DECAY = tuple(math.log1p(-(2.0 ** (-5.0 - h))) for h in range(N_HEADS))
K_SCALE = HEAD_DIM ** -0.5

V7X_VMEM_LIMIT_BYTES = 60 * 1024 * 1024

TOKEN_TILE = 256
SAMPLE_PROJ_COLS = 1024
SAMPLE_SEQS = 16
FF_CHUNK = 256
DOWN_CHUNK = 256


def _dot(a, b):
    return jnp.dot(a, b, preferred_element_type=F32)


def _dot_nt(a, b):
    return lax.dot_general(a, b, (((1,), (1,)), ((), ())), preferred_element_type=F32)


def _dot_tn(a, b):
    return lax.dot_general(a, b, (((0,), (0,)), ((), ())), preferred_element_type=F32)


def _split3(x):
    hi = x.astype(BF16)
    r1 = x - hi.astype(F32)
    mid = r1.astype(BF16)
    lo = (r1 - mid.astype(F32)).astype(BF16)
    return hi, mid, lo


def _dot_exact_lhs01(m01, parts):
    hi, mid, lo = parts
    return _dot(m01, hi) + _dot(m01, mid) + _dot(m01, lo)


def _sigmoid(x):
    return 1.0 / (1.0 + jnp.exp(-x))


def _silu(x):
    return x * _sigmoid(x)


def _causal_in_chunk(n, shift):
    r = lax.broadcasted_iota(jnp.int32, (n, n), 0)
    c = lax.broadcasted_iota(jnp.int32, (n, n), 1)
    return ((r >> shift) == (c >> shift)) & (c <= r)


def _lower_bound(lb_ref):
    rows = [lb_ref[i:i + 1, :] for i in range(lb_ref.shape[0])]
    m = functools.reduce(jnp.maximum, rows)
    e = [jnp.exp(r - m) for r in rows]
    return e[0] / functools.reduce(jnp.add, e)


def _hgrn_prepass(proj_ref, lb, causal):
    tri = jnp.where(causal, 1.0, 0.0).astype(BF16)
    f = lb + (1.0 - lb) * _sigmoid(proj_ref[:, GROUP_W:2 * GROUP_W])
    kk = 1.0 - f
    b = _dot_exact_lhs01(tri, _split3(jnp.log(f)))
    q_dec = (_silu(proj_ref[:, 0:GROUP_W]) * jnp.exp(b)).astype(BF16)
    k_dec = kk * jnp.exp(-b)
    return q_dec, k_dec, kk, b


def _rope(x, cos, sin_signed):
    return x * cos + pltpu.roll(x, HEAD_DIM // 2, axis=1) * sin_signed


def _rms_gate(o, g, gate):
    return o * lax.rsqrt(jnp.mean(o * o, axis=-1, keepdims=True) + NORM_EPS) * g * _silu(gate)


def _ln_gate(o, g, b, gate):
    mu = jnp.mean(o, axis=-1, keepdims=True)
    d = o - mu
    var = jnp.mean(d * d, axis=-1, keepdims=True)
    return (d * lax.rsqrt(var + NORM_EPS) * g + b) * _silu(gate)


def _layer_norm(x, g, b):
    mu = jnp.mean(x, axis=-1, keepdims=True)
    d = x - mu
    var = jnp.mean(d * d, axis=-1, keepdims=True)
    return d * lax.rsqrt(var + NORM_EPS) * g + b


def _head(h, group=0):
    return slice(group * GROUP_W + h * HEAD_DIM, group * GROUP_W + (h + 1) * HEAD_DIM)


def _rope_table_kernel(cos_ref, sin_ref, *, offset):
    n = cos_ref.shape[0]
    half = HEAD_DIM // 2
    row = lax.broadcasted_iota(jnp.int32, (n, HEAD_DIM), 0) + pl.program_id(0) * n
    lane = lax.broadcasted_iota(jnp.int32, (n, HEAD_DIM), 1)
    j = (lane & (half - 1)).astype(F32)
    inv = jnp.exp(-(j / half) * math.log(ROPE_BASE))
    ang = (row.astype(F32) + offset) * inv
    cos_ref[...] = jnp.cos(ang)
    s = jnp.sin(ang)
    sin_ref[...] = jnp.where(lane < half, -s, s)


def _rope_tables(n, offset):
    tile = min(n, 512)
    return pl.pallas_call(
        functools.partial(_rope_table_kernel, offset=float(offset)),
        grid=(n // tile,),
        in_specs=[],
        out_specs=[pl.BlockSpec((tile, HEAD_DIM), lambda i: (i, 0))] * 2,
        out_shape=[jax.ShapeDtypeStruct((n, HEAD_DIM), F32)] * 2,
        name="rope_tables",
    )()


def _tail_steps(x_ref, mix_ref, p_ref, y_ref, w_out_ref, ln1g_ref, ln1b_ref, wg_ref, wu_ref, wd_ref,
                ln2g_ref, ln2b_ref, wpp_ref, wpg_ref, bpg_ref, act_ref, h_ref, hb_ref):
    def out_proj():
        for c in range(D_MODEL // DOWN_CHUNK):
            cols = slice(c * DOWN_CHUNK, (c + 1) * DOWN_CHUNK)
            h_ref[:, cols] = DN_ALPHA * x_ref[:, cols] + _dot(mix_ref[...], w_out_ref[:, cols])
        h = _layer_norm(h_ref[...], ln1g_ref[...], ln1b_ref[...])
        h_ref[...] = h
        hb_ref[...] = h.astype(BF16)

    def ff(c):
        cols = slice(c * FF_CHUNK, (c + 1) * FF_CHUNK)
        hb = hb_ref[...]
        act_ref[:, cols] = (_silu(_dot(hb, wg_ref[:, cols])) * _dot(hb, wu_ref[:, cols])).astype(BF16)

    def down(c):
        cols = slice(c * DOWN_CHUNK, (c + 1) * DOWN_CHUNK)
        h_ref[:, cols] = DN_ALPHA * h_ref[:, cols] + _dot(act_ref[...], wd_ref[:, cols])

    def norm2():
        h2 = _layer_norm(h_ref[...], ln2g_ref[...], ln2b_ref[...])
        h_ref[...] = h2
        hb_ref[...] = h2.astype(BF16)

    def ple(c):
        cols = slice(c * DOWN_CHUNK, (c + 1) * DOWN_CHUNK)
        gate = _sigmoid(_dot(hb_ref[...], wpg_ref[:, cols]) + bpg_ref[:, cols])
        y_ref[:, cols] = h_ref[:, cols] + gate * _dot(p_ref[...].astype(BF16), wpp_ref[:, cols])

    return (out_proj,
            [functools.partial(ff, c) for c in range(D_FF // FF_CHUNK)],
            [functools.partial(down, c) for c in range(D_MODEL // DOWN_CHUNK)],
            [norm2] + [functools.partial(ple, c) for c in range(D_MODEL // DOWN_CHUNK)])


def _tail_scratch(tl):
    return [pltpu.VMEM((tl, D_FF), BF16), pltpu.VMEM((tl, D_MODEL), F32), pltpu.VMEM((tl, D_MODEL), BF16)]


def _tail_specs():
    const = lambda i: (0, 0)
    resident = lambda shape: pl.BlockSpec(shape, const, pipeline_mode=pl.Buffered(1))
    vec = pl.BlockSpec((1, D_MODEL), const)
    return [
        resident((2 * GROUP_W, D_MODEL)),
        vec, vec,
        resident((D_MODEL, D_FF)),
        resident((D_MODEL, D_FF)),
        resident((D_FF, D_MODEL)),
        vec, vec,
        resident((PLE_DIM, D_MODEL)),
        resident((D_MODEL, D_MODEL)),
        vec,
    ]


def _in_proj_steps(x_ref, w_in_ref, proj_ref, xb_ref):
    def in_proj(c):
        if c == 0:
            xb_ref[...] = x_ref[...].astype(BF16)
        cols = slice(c * GROUP_W, (c + 1) * GROUP_W)
        proj_ref[:, cols] = _dot(xb_ref[...], w_in_ref[:, cols])

    return [functools.partial(in_proj, c) for c in range(IN_COLS // GROUP_W)]


def _ret_tables(dm_ref, rd_ref):
    tl = dm_ref.shape[1]
    r = lax.broadcasted_iota(jnp.int32, (tl, tl), 0)
    c = lax.broadcasted_iota(jnp.int32, (tl, tl), 1)
    row = lax.broadcasted_iota(jnp.int32, (tl, HEAD_DIM), 0).astype(F32)
    for h in range(N_HEADS):
        logd = RET_LOG_DECAY[h]
        dm_ref[h] = jnp.where(r >= c, jnp.exp((r - c).astype(F32) * logd), 0.0)
        rd_ref[0, h] = jnp.exp((row + 1.0) * logd)
        rd_ref[1, h] = jnp.exp((tl - 1.0 - row) * logd)


def _prompt_mixer_steps(cos_ref, sin_ref, lb_ref, ag_ref, bg_ref, bb_ref,
                        sa_ref, sb_ref, proj_ref, st_ref, oa_ref, mix_ref,
                        qd_ref, kd_ref, ke_ref, dec_ref, va_ref, kv_ref, sbf_ref, dm_ref, rd_ref, first):
    tl = proj_ref.shape[0]
    shift = REF_CHUNK.bit_length() - 1
    n_chunks = tl // REF_CHUNK

    def prepass():
        q_dec, k_dec, kk, b = _hgrn_prepass(proj_ref, _lower_bound(lb_ref), _causal_in_chunk(tl, shift))
        qd_ref[...] = q_dec
        for h in range(N_HEADS):
            kd_ref[h] = k_dec[:, _head(h)].T.astype(BF16)
        va_ref[...] = proj_ref[:, 2 * GROUP_W:3 * GROUP_W].astype(BF16)
        last = [b[(n + 1) * REF_CHUNK - 1:(n + 1) * REF_CHUNK, :] for n in range(n_chunks)]
        b_last = jnp.concatenate([jnp.broadcast_to(r, (REF_CHUNK, GROUP_W)) for r in last], axis=0)
        k_end = kk * jnp.exp(b_last - b)
        for n in range(n_chunks):
            dec_ref[n:n + 1, :] = jnp.exp(last[n])
        odd = ((lax.broadcasted_iota(jnp.int32, (tl, GROUP_W), 0) >> shift) & 1) == 1
        k_even = jnp.where(odd, 0.0, k_end).astype(BF16)
        k_odd = jnp.where(odd, k_end, 0.0).astype(BF16)
        for h in range(N_HEADS):
            ke_ref[:, 2 * h * HEAD_DIM:(2 * h + 1) * HEAD_DIM] = k_even[:, _head(h)]
            ke_ref[:, (2 * h + 1) * HEAD_DIM:(2 * h + 2) * HEAD_DIM] = k_odd[:, _head(h)]

    def kv_scan(h):
        hs = _head(h)
        pair = 2 * REF_CHUNK
        for r in range(n_chunks // 2):
            rows = slice(r * pair, (r + 1) * pair)
            kv_ref[r] = _dot_tn(va_ref[rows, hs], ke_ref[rows, 2 * h * HEAD_DIM:(2 * h + 2) * HEAD_DIM])
        st = jnp.where(first, 0.0, st_ref[h])
        for n in range(n_chunks):
            sbf_ref[n, h] = st.T.astype(BF16)
            st = st * dec_ref[n:n + 1, hs] + kv_ref[n // 2, :, (n % 2) * HEAD_DIM:(n % 2 + 1) * HEAD_DIM]
        st_ref[h] = st
        sa_ref[0, h] = st.T

    def diag(h):
        hs = _head(h)
        sc = jnp.where(_causal_in_chunk(tl, shift), _dot(qd_ref[:, hs], kd_ref[h]), 0.0)
        oa_ref[:, hs] = _dot(sc.astype(BF16), va_ref[:, hs])

    def inter(n):
        rows = slice(n * REF_CHUNK, (n + 1) * REF_CHUNK)
        for h in range(N_HEADS):
            hs = _head(h)
            oa_ref[rows, hs] += _dot(qd_ref[rows, hs], sbf_ref[n, h])

    def hgrn_out():
        for h in range(N_HEADS):
            hs = _head(h)
            mix_ref[:, hs] = _rms_gate(oa_ref[:, hs], ag_ref[:, hs], proj_ref[:, _head(h, 3)]).astype(BF16)

    def ret(h):
        hs = _head(h)
        cos = cos_ref[...]
        sin = sin_ref[...]
        q = _rope(proj_ref[:, _head(h, 4)], cos, sin)
        k = _rope(proj_ref[:, _head(h, 5)], cos, sin) * K_SCALE
        v = proj_ref[:, _head(h, 6)].astype(BF16)
        a = (_dot(q.astype(BF16), k.T.astype(BF16)) * dm_ref[h]).astype(BF16)
        s = jnp.where(first, 0.0, sb_ref[0, h])
        o = _dot(a, v) + _dot((q * rd_ref[0, h]).astype(BF16), s.astype(BF16))
        k_end_b = (k * rd_ref[1, h]).astype(BF16)
        sb_ref[0, h] = s * math.exp(tl * RET_LOG_DECAY[h]) + _dot_tn(k_end_b, v)
        mix_ref[:, _head(h, 1)] = _ln_gate(o, bg_ref[:, hs], bb_ref[:, hs],
                                           proj_ref[:, _head(h, 7)]).astype(BF16)

    return (prepass,
            [functools.partial(kv_scan, h) for h in range(N_HEADS)],
            [functools.partial(diag, h) for h in range(N_HEADS)],
            [functools.partial(inter, n) for n in range(n_chunks)],
            hgrn_out,
            [functools.partial(ret, h) for h in range(N_HEADS)])


def _interleave(a, b):
    out = []
    for i in range(max(len(a), len(b))):
        out += a[i:i + 1] + b[i:i + 1]
    return out


def _prompt_layer_kernel(xn_ref, xp_ref, p_ref, cos_ref, sin_ref, w_in_ref, lb_ref, ag_ref, bg_ref,
                         bb_ref, *rest, tiles_per_seq):
    tail_w = rest[:11]
    y_ref, sa_ref, sb_ref = rest[11:14]
    mixer_scratch = rest[14:27]
    proj_ref, mix_ref, dm_ref, rd_ref = mixer_scratch[0], mixer_scratch[3], mixer_scratch[11], mixer_scratch[12]
    xb_ref = rest[27]
    tail_scratch = rest[28:]
    g = pl.program_id(0)
    n_tiles = pl.num_programs(0) - 1
    slot = lax.rem(g, 2)

    def mixer_steps():
        return _prompt_mixer_steps(cos_ref, sin_ref, lb_ref, ag_ref, bg_ref, bb_ref, sa_ref, sb_ref,
                                   proj_ref.at[slot], *mixer_scratch[1:],
                                   first=lax.rem(g, tiles_per_seq) == 0)

    def tail_steps():
        return _tail_steps(xp_ref, mix_ref, p_ref, y_ref, *tail_w, *tail_scratch)

    @pl.when(g == 0)
    def _():
        _ret_tables(dm_ref, rd_ref)
        prepass, kv_scan, diag, inter, hgrn_out, ret = mixer_steps()
        steps = _in_proj_steps(xp_ref, w_in_ref, proj_ref.at[slot], xb_ref)
        steps += [prepass] + kv_scan + diag + inter + [hgrn_out] + ret
        steps += _in_proj_steps(xn_ref, w_in_ref, proj_ref.at[1 - slot], xb_ref)
        for step in steps:
            step()

    @pl.when((g > 0) & (g < n_tiles))
    def _():
        prepass, kv_scan, diag, inter, hgrn_out, ret = mixer_steps()
        out_proj, ff, down, final = tail_steps()
        in_proj = _in_proj_steps(xn_ref, w_in_ref, proj_ref.at[1 - slot], xb_ref)
        inter_pairs = [lambda a=a, b=b: (a(), b()) for a, b in zip(inter[0::2], inter[1::2])]
        steps = [out_proj, in_proj[0], prepass, in_proj[1]]
        steps += _interleave(ff + down, kv_scan + diag + ret + inter_pairs + [hgrn_out])
        steps += _interleave(final, in_proj[2:])
        for step in steps:
            step()

    @pl.when(g == n_tiles)
    def _():
        out_proj, ff, down, final = tail_steps()
        for step in [out_proj] + ff + down + final:
            step()


def _prompt_layer(x, p, cos, sin, w_in, lb_logits, a_g, b_g, b_b, tail_w):
    bsz, seq, _ = x.shape
    tl = TOKEN_TILE
    tps = seq // tl
    n_tiles = bsz * tps
    x2 = x.reshape(bsz * seq, D_MODEL)
    p2 = p.reshape(bsz * seq, PLE_DIM)
    const = lambda g: (0, 0)
    nxt = lambda g: (jnp.minimum(g + 1, n_tiles - 1), 0)
    prev = lambda g: (jnp.maximum(g - 1, 0), 0)
    seq_tile = lambda g: (lax.rem(jnp.minimum(g, n_tiles - 1), tps), 0)
    state_spec = pl.BlockSpec((1, N_HEADS, HEAD_DIM, HEAD_DIM),
                              lambda g: (jnp.minimum(g, n_tiles - 1) // tps, 0, 0, 0))
    state_shape = jax.ShapeDtypeStruct((bsz, N_HEADS, HEAD_DIM, HEAD_DIM), F32)
    return pl.pallas_call(
        functools.partial(_prompt_layer_kernel, tiles_per_seq=tps),
        grid=(n_tiles + 1,),
        in_specs=[
            pl.BlockSpec((tl, D_MODEL), nxt),
            pl.BlockSpec((tl, D_MODEL), prev),
            pl.BlockSpec((tl, PLE_DIM), prev),
            pl.BlockSpec((tl, HEAD_DIM), seq_tile),
            pl.BlockSpec((tl, HEAD_DIM), seq_tile),
            pl.BlockSpec((D_MODEL, IN_COLS), const, pipeline_mode=pl.Buffered(1)),
            pl.BlockSpec(lb_logits.shape, const),
            pl.BlockSpec((1, GROUP_W), const),
            pl.BlockSpec((1, GROUP_W), const),
            pl.BlockSpec((1, GROUP_W), const),
        ] + _tail_specs(),
        out_specs=[
            pl.BlockSpec((tl, D_MODEL), prev),
            state_spec,
            state_spec,
        ],
        out_shape=[
            jax.ShapeDtypeStruct((bsz * seq, D_MODEL), F32),
            state_shape,
            state_shape,
        ],
        scratch_shapes=[
            pltpu.VMEM((2, tl, IN_COLS), F32),
            pltpu.VMEM((N_HEADS, HEAD_DIM, HEAD_DIM), F32),
            pltpu.VMEM((tl, GROUP_W), F32),
            pltpu.VMEM((tl, 2 * GROUP_W), BF16),
            pltpu.VMEM((tl, GROUP_W), BF16),
            pltpu.VMEM((N_HEADS, HEAD_DIM, tl), BF16),
            pltpu.VMEM((tl, 2 * GROUP_W), BF16),
            pltpu.VMEM((tl // REF_CHUNK, GROUP_W), F32),
            pltpu.VMEM((tl, GROUP_W), BF16),
            pltpu.VMEM((tl // (2 * REF_CHUNK), HEAD_DIM, 2 * HEAD_DIM), F32),
            pltpu.VMEM((tl // REF_CHUNK, N_HEADS, HEAD_DIM, HEAD_DIM), BF16),
            pltpu.VMEM((N_HEADS, tl, tl), F32),
            pltpu.VMEM((2, N_HEADS, tl, HEAD_DIM), F32),
            pltpu.VMEM((tl, D_MODEL), BF16),
        ] + _tail_scratch(tl),
        compiler_params=pltpu.CompilerParams(
            dimension_semantics=("arbitrary",), vmem_limit_bytes=V7X_VMEM_LIMIT_BYTES),
        name="prompt_layer",
    )(x2, x2, p2, cos, sin, w_in, lb_logits, a_g, b_g, b_b, *tail_w)


def _in_proj_kernel(x_ref, w_ref, o_ref, wb_ref):
    wb_ref[...] = w_ref[...].astype(BF16)
    o_ref[...] = _dot(x_ref[...].astype(BF16), wb_ref[...])


def _in_proj(x, w_in):
    n = x.shape[0]
    tn = SAMPLE_PROJ_COLS
    return pl.pallas_call(
        _in_proj_kernel,
        grid=(IN_COLS // tn,),
        in_specs=[
            pl.BlockSpec((n, D_MODEL), lambda c: (0, 0)),
            pl.BlockSpec((D_MODEL, tn), lambda c: (0, c)),
        ],
        out_specs=[
            pl.BlockSpec((n, tn), lambda c: (0, c)),
            pl.BlockSpec((D_MODEL, tn), lambda c: (0, c)),
        ],
        out_shape=[
            jax.ShapeDtypeStruct((n, IN_COLS), F32),
            jax.ShapeDtypeStruct((D_MODEL, IN_COLS), BF16),
        ],
        compiler_params=pltpu.CompilerParams(
            dimension_semantics=("arbitrary",), vmem_limit_bytes=V7X_VMEM_LIMIT_BYTES),
        name="sample_in_proj",
    )(x, w_in)


def _sample_rec_kernel(proj_ref, sa_in_ref, sb_in_ref, lb_ref, ag_ref, bg_ref, bb_ref,
                       cos_ref, sin_ref, mix_ref, sa_ref, sb_ref, oa_ref, ob_ref, *, seq_len):
    rows_n = proj_ref.shape[0]
    n_seq = rows_n // seq_len
    causal = _causal_in_chunk(rows_n, seq_len.bit_length() - 1)

    q_dec, k_dec, kk, b = _hgrn_prepass(proj_ref, _lower_bound(lb_ref), causal)
    v_a = proj_ref[:, 2 * GROUP_W:3 * GROUP_W]
    for h in range(N_HEADS):
        hs = _head(h)
        sc = jnp.where(causal, _dot_nt(q_dec[:, hs], k_dec[:, hs].astype(BF16)), 0.0).astype(BF16)
        oa_ref[:, hs] = _dot(sc, v_a[:, hs].astype(BF16))
    q_dec32 = q_dec.astype(F32)
    rr = lax.broadcasted_iota(jnp.int32, (seq_len, GROUP_W), 0)
    ones_blk = jnp.ones((seq_len, HEAD_DIM), BF16)
    for s in range(n_seq):
        rows = slice(s * seq_len, (s + 1) * seq_len)
        b_last = b[(s + 1) * seq_len - 1:(s + 1) * seq_len, :]
        k_end = (kk[rows] * jnp.exp(b_last - b[rows])).astype(BF16)
        hi, mid, lo = [t.astype(F32) for t in _split3(jnp.exp(b_last))]
        dec_rows = jnp.where(rr == 0, hi, jnp.where(rr == 1, mid, jnp.where(rr == 2, lo, 0.0)))
        dec_rows = dec_rows.astype(BF16)
        for h in range(N_HEADS):
            hs = _head(h)
            st = sa_in_ref[s, h]
            oa_ref[rows, hs] += _dot(q_dec32[rows, hs].astype(BF16), st.astype(BF16))
            dec_kv = _dot_tn(dec_rows[:, hs], ones_blk)
            sa_ref[s, h] = st * dec_kv + _dot_tn(k_end[:, hs], v_a[rows, hs].astype(BF16))
    for h in range(N_HEADS):
        hs = _head(h)
        mix_ref[:, hs] = _rms_gate(oa_ref[:, hs], ag_ref[:, hs], proj_ref[:, _head(h, 3)]).astype(BF16)

    cos = jnp.concatenate([cos_ref[...]] * n_seq, axis=0)
    sin = jnp.concatenate([sin_ref[...]] * n_seq, axis=0)
    r = lax.broadcasted_iota(jnp.int32, (rows_n, rows_n), 0)
    c = lax.broadcasted_iota(jnp.int32, (rows_n, rows_n), 1)
    diff = ((r & (seq_len - 1)) - (c & (seq_len - 1))).astype(F32)
    row = (lax.broadcasted_iota(jnp.int32, (rows_n, HEAD_DIM), 0) & (seq_len - 1)).astype(F32)
    for h in range(N_HEADS):
        hs = _head(h)
        logd = RET_LOG_DECAY[h]
        q = _rope(proj_ref[:, _head(h, 4)], cos, sin)
        k = _rope(proj_ref[:, _head(h, 5)], cos, sin) * K_SCALE
        v32 = proj_ref[:, _head(h, 6)]
        dmask = jnp.where(causal, jnp.exp(diff * logd), 0.0)
        a = (_dot_nt(q.astype(BF16), k.astype(BF16)) * dmask).astype(BF16)
        ob_ref[...] = _dot(a, v32.astype(BF16))
        q_dec_b = q * jnp.exp((row + 1.0) * logd)
        k_end_b = k * jnp.exp((seq_len - 1.0 - row) * logd)
        for s in range(n_seq):
            rows = slice(s * seq_len, (s + 1) * seq_len)
            st = sb_in_ref[s, h]
            ob_ref[rows, :] += _dot(q_dec_b[rows].astype(BF16), st.astype(BF16))
            sb_ref[s, h] = st * math.exp(seq_len * logd) + _dot_tn(
                k_end_b[rows].astype(BF16), v32[rows].astype(BF16))
        mix_ref[:, _head(h, 1)] = _ln_gate(ob_ref[...], bg_ref[:, hs], bb_ref[:, hs],
                                           proj_ref[:, _head(h, 7)]).astype(BF16)


def _sample_rec(proj, sa, sb, lb_logits, a_g, b_g, b_b, cos, sin, seq_len):
    n_tok = proj.shape[0]
    n_seq = n_tok // seq_len
    bs = SAMPLE_SEQS
    rows = bs * seq_len
    const = lambda i: (0, 0)
    state_spec = pl.BlockSpec((bs, N_HEADS, HEAD_DIM, HEAD_DIM), lambda i: (i, 0, 0, 0))
    state_shape = jax.ShapeDtypeStruct((n_seq, N_HEADS, HEAD_DIM, HEAD_DIM), F32)
    return pl.pallas_call(
        functools.partial(_sample_rec_kernel, seq_len=seq_len),
        grid=(n_seq // bs,),
        in_specs=[
            pl.BlockSpec((rows, IN_COLS), lambda i: (i, 0)),
            state_spec,
            state_spec,
            pl.BlockSpec(lb_logits.shape, const),
            pl.BlockSpec((1, GROUP_W), const),
            pl.BlockSpec((1, GROUP_W), const),
            pl.BlockSpec((1, GROUP_W), const),
            pl.BlockSpec((seq_len, HEAD_DIM), const),
            pl.BlockSpec((seq_len, HEAD_DIM), const),
        ],
        out_specs=[
            pl.BlockSpec((rows, 2 * GROUP_W), lambda i: (i, 0)),
            state_spec,
            state_spec,
        ],
        out_shape=[
            jax.ShapeDtypeStruct((n_tok, 2 * GROUP_W), BF16),
            state_shape,
            state_shape,
        ],
        scratch_shapes=[
            pltpu.VMEM((rows, GROUP_W), F32),
            pltpu.VMEM((rows, HEAD_DIM), F32),
        ],
        compiler_params=pltpu.CompilerParams(
            dimension_semantics=("arbitrary",), vmem_limit_bytes=V7X_VMEM_LIMIT_BYTES),
        name="sample_recurrence",
    )(proj, sa, sb, lb_logits, a_g, b_g, b_b, cos, sin)


_TAIL_PHASE_STEPS = (D_MODEL // DOWN_CHUNK, D_FF // FF_CHUNK, D_MODEL // DOWN_CHUNK, D_MODEL // DOWN_CHUNK)
_TAIL_PHASE_START = tuple(sum(_TAIL_PHASE_STEPS[:i]) for i in range(4))


def _sample_tail_kernel(x_ref, mix_ref, p_ref, wo_ref, wg_ref, wu_ref, wd_ref, wpg_ref, wpp_ref,
                        ln1g_ref, ln1b_ref, ln2g_ref, ln2b_ref, bpg_ref,
                        y_ref, wo_b, wg_b, wu_b, wd_b, wpg_b, wpp_b, act_ref, h_ref, hb_ref):
    s = pl.program_id(0)
    a0, b0, c0, d0 = _TAIL_PHASE_START
    blk = DOWN_CHUNK

    @pl.when(s < b0)
    def _():
        wo_b[...] = wo_ref[...].astype(BF16)
        cols = pl.ds(pl.multiple_of((s - a0) * blk, blk), blk)
        h_ref[:, cols] = DN_ALPHA * x_ref[:, cols] + _dot(mix_ref[...], wo_b[...])

        @pl.when(s == b0 - 1)
        def _():
            h = _layer_norm(h_ref[...], ln1g_ref[...], ln1b_ref[...])
            h_ref[...] = h
            hb_ref[...] = h.astype(BF16)

    @pl.when((s >= b0) & (s < c0))
    def _():
        wg_b[...] = wg_ref[...].astype(BF16)
        wu_b[...] = wu_ref[...].astype(BF16)
        cols = pl.ds(pl.multiple_of((s - b0) * FF_CHUNK, FF_CHUNK), FF_CHUNK)
        hb = hb_ref[...]
        act_ref[:, cols] = (_silu(_dot(hb, wg_b[...])) * _dot(hb, wu_b[...])).astype(BF16)

    @pl.when((s >= c0) & (s < d0))
    def _():
        wd_b[...] = wd_ref[...].astype(BF16)
        cols = pl.ds(pl.multiple_of((s - c0) * blk, blk), blk)
        h_ref[:, cols] = DN_ALPHA * h_ref[:, cols] + _dot(act_ref[...], wd_b[...])

        @pl.when(s == d0 - 1)
        def _():
            h2 = _layer_norm(h_ref[...], ln2g_ref[...], ln2b_ref[...])
            h_ref[...] = h2
            hb_ref[...] = h2.astype(BF16)

    @pl.when(s >= d0)
    def _():
        wpg_b[...] = wpg_ref[...].astype(BF16)
        wpp_b[...] = wpp_ref[...].astype(BF16)
        cols = pl.ds(pl.multiple_of((s - d0) * blk, blk), blk)
        gate = _sigmoid(_dot(hb_ref[...], wpg_b[...]) + bpg_ref[:, cols])
        y_ref[...] = h_ref[:, cols] + gate * _dot(p_ref[...].astype(BF16), wpp_b[...])


def _sample_tail(x, mix, p, w_out, ln1g, ln1b, wg, wu, wd, ln2g, ln2b, wpp, wpg, bpg):
    n = x.shape[0]
    a0, b0, c0, d0 = _TAIL_PHASE_START
    na, nb, nc, nd = _TAIL_PHASE_STEPS
    const = lambda s: (0, 0)
    col = lambda start, count: (lambda s: (0, jnp.clip(s - start, 0, count - 1)))
    vec = pl.BlockSpec((1, D_MODEL), const)
    weight_specs = [
        pl.BlockSpec((2 * GROUP_W, DOWN_CHUNK), col(a0, na)),
        pl.BlockSpec((D_MODEL, FF_CHUNK), col(b0, nb)),
        pl.BlockSpec((D_MODEL, FF_CHUNK), col(b0, nb)),
        pl.BlockSpec((D_FF, DOWN_CHUNK), col(c0, nc)),
        pl.BlockSpec((D_MODEL, DOWN_CHUNK), col(d0, nd)),
        pl.BlockSpec((PLE_DIM, DOWN_CHUNK), col(d0, nd)),
    ]
    weights = (w_out, wg, wu, wd, wpg, wpp)
    return pl.pallas_call(
        _sample_tail_kernel,
        grid=(sum(_TAIL_PHASE_STEPS),),
        in_specs=[
            pl.BlockSpec((n, D_MODEL), const),
            pl.BlockSpec((n, 2 * GROUP_W), const),
            pl.BlockSpec((n, PLE_DIM), const),
        ] + weight_specs + [vec, vec, vec, vec, vec],
        out_specs=[pl.BlockSpec((n, DOWN_CHUNK), col(d0, nd))] + weight_specs,
        out_shape=[jax.ShapeDtypeStruct((n, D_MODEL), F32)]
        + [jax.ShapeDtypeStruct(w.shape, BF16) for w in weights],
        scratch_shapes=_tail_scratch(n),
        compiler_params=pltpu.CompilerParams(
            dimension_semantics=("arbitrary",), vmem_limit_bytes=V7X_VMEM_LIMIT_BYTES),
        name="sample_tail",
    )(x, mix, p, *weights, ln1g, ln1b, ln2g, ln2b, bpg)


def kernel(x_prompt, x_sample, p_prompt, p_sample, state_hgrn, state_ret, lb_logits, w_in, a_norm_g, b_norm_g, b_norm_b, w_out, ln1_g, ln1_b, w_ffn_gate, w_ffn_up, w_ffn_down, ln2_g, ln2_b, w_ple_proj, w_ple_gate, b_ple_gate):
    assert w_in.shape[0] == DEPTH == 1
    bsz, seq, _ = x_prompt.shape
    n_dec, dec_seq, _ = x_sample.shape

    mixer_vecs = (lb_logits, a_norm_g, b_norm_g, b_norm_b)
    cos_p, sin_p = _rope_tables(seq, 0)
    cos_s, sin_s = _rope_tables(dec_seq, PAST_LEN)

    x_s = x_sample.reshape(n_dec * dec_seq, D_MODEL)
    proj_s, w_in_b = _in_proj(x_s, w_in[0])
    mix_s, sa_s, sb_s = _sample_rec(proj_s, state_hgrn[0], state_ret[0], *mixer_vecs,
                                    cos_s, sin_s, dec_seq)
    y_s, w_out_b, wg_b, wu_b, wd_b, wpg_b, wpp_b = _sample_tail(
        x_s, mix_s, p_sample[0].reshape(n_dec * dec_seq, PLE_DIM), w_out[0], ln1_g, ln1_b,
        w_ffn_gate[0], w_ffn_up[0], w_ffn_down[0], ln2_g, ln2_b, w_ple_proj[0], w_ple_gate[0],
        b_ple_gate)

    tail_w = (w_out_b, ln1_g, ln1_b, wg_b, wu_b, wd_b, ln2_g, ln2_b, wpp_b, wpg_b, b_ple_gate)
    y_p, sa_p, sb_p = _prompt_layer(x_prompt, p_prompt[0], cos_p, sin_p, w_in_b, *mixer_vecs, tail_w)

    return (y_p.reshape(bsz, seq, D_MODEL), y_s.reshape(n_dec, dec_seq, D_MODEL),
            sa_p[None], sb_p[None], sa_s[None], sb_s[None])
```

```python
import functools
import math

import jax
import jax.numpy as jnp
from jax import lax
from jax.experimental import pallas as pl
from jax.experimental.pallas import tpu as pltpu

F32 = jnp.float32
BF16 = jnp.bfloat16

D_MODEL = 1024
N_HEADS = 4
HEAD_DIM = 128
GROUP_W = N_HEADS * HEAD_DIM
IN_COLS = 8 * GROUP_W
D_FF = 2816
PLE_DIM = 256
DEPTH = 1
PAST_LEN = 16384
REF_CHUNK = 32
ROPE_BASE = 10000.0
NORM_EPS = 1e-5
DN_ALPHA = (2.0 * DEPTH) ** 0.25
RET_LOG_DECAY = tuple(math.log1p(-(2.0 ** (-5.0 - h))) for h in range(N_HEADS))
K_SCALE = HEAD_DIM ** -0.5

V7X_VMEM_LIMIT_BYTES = 60 * 1024 * 1024

TOKEN_TILE = 256
SAMPLE_PROJ_COLS = 1024
SAMPLE_SEQS = 16
FF_CHUNK = 256
DOWN_CHUNK = 256


def _dot(a, b):
    return jnp.dot(a, b, preferred_element_type=F32)


def _dot_nt(a, b):
    return lax.dot_general(a, b, (((1,), (1,)), ((), ())), preferred_element_type=F32)


def _dot_tn(a, b):
    return lax.dot_general(a, b, (((0,), (0,)), ((), ())), preferred_element_type=F32)


def _split3(x):
    hi = x.astype(BF16)
    r1 = x - hi.astype(F32)
    mid = r1.astype(BF16)
    lo = (r1 - mid.astype(F32)).astype(BF16)
    return hi, mid, lo


def _dot_exact_lhs01(m01, parts):
    hi, mid, lo = parts
    return _dot(m01, hi) + _dot(m01, mid) + _dot(m01, lo)


def _sigmoid(x):
    return 1.0 / (1.0 + jnp.exp(-x))


def _silu(x):
    return x * _sigmoid(x)


def _causal_in_chunk(n, shift):
    r = lax.broadcasted_iota(jnp.int32, (n, n), 0)
    c = lax.broadcasted_iota(jnp.int32, (n, n), 1)
    return ((r >> shift) == (c >> shift)) & (c <= r)


def _lower_bound(lb_ref):
    rows = [lb_ref[i:i + 1, :] for i in range(lb_ref.shape[0])]
    m = functools.reduce(jnp.maximum, rows)
    e = [jnp.exp(r - m) for r in rows]
    return e[0] / functools.reduce(jnp.add, e)


def _hgrn_prepass(proj_ref, lb, causal):
    tri = jnp.where(causal, 1.0, 0.0).astype(BF16)
    f = lb + (1.0 - lb) * _sigmoid(proj_ref[:, GROUP_W:2 * GROUP_W])
    kk = 1.0 - f
    b = _dot_exact_lhs01(tri, _split3(jnp.log(f)))
    q_dec = (_silu(proj_ref[:, 0:GROUP_W]) * jnp.exp(b)).astype(BF16)
    k_dec = kk * jnp.exp(-b)
    return q_dec, k_dec, kk, b


def _rope(x, cos, sin_signed):
    return x * cos + pltpu.roll(x, HEAD_DIM // 2, axis=1) * sin_signed


def _rms_gate(o, g, gate):
    return o * lax.rsqrt(jnp.mean(o * o, axis=-1, keepdims=True) + NORM_EPS) * g * _silu(gate)


def _ln_gate(o, g, b, gate):
    mu = jnp.mean(o, axis=-1, keepdims=True)
    d = o - mu
    var = jnp.mean(d * d, axis=-1, keepdims=True)
    return (d * lax.rsqrt(var + NORM_EPS) * g + b) * _silu(gate)


def _layer_norm(x, g, b):
    mu = jnp.mean(x, axis=-1, keepdims=True)
    d = x - mu
    var = jnp.mean(d * d, axis=-1, keepdims=True)
    return d * lax.rsqrt(var + NORM_EPS) * g + b


def _head(h, group=0):
    return slice(group * GROUP_W + h * HEAD_DIM, group * GROUP_W + (h + 1) * HEAD_DIM)


def _rope_table_kernel(cos_ref, sin_ref, *, offset):
    n = cos_ref.shape[0]
    half = HEAD_DIM // 2
    row = lax.broadcasted_iota(jnp.int32, (n, HEAD_DIM), 0) + pl.program_id(0) * n
    lane = lax.broadcasted_iota(jnp.int32, (n, HEAD_DIM), 1)
    j = (lane & (half - 1)).astype(F32)
    inv = jnp.exp(-(j / half) * math.log(ROPE_BASE))
    ang = (row.astype(F32) + offset) * inv
    cos_ref[...] = jnp.cos(ang)
    s = jnp.sin(ang)
    sin_ref[...] = jnp.where(lane < half, -s, s)


def _rope_tables(n, offset):
    tile = min(n, 512)
    return pl.pallas_call(
        functools.partial(_rope_table_kernel, offset=float(offset)),
        grid=(n // tile,),
        in_specs=[],
        out_specs=[pl.BlockSpec((tile, HEAD_DIM), lambda i: (i, 0))] * 2,
        out_shape=[jax.ShapeDtypeStruct((n, HEAD_DIM), F32)] * 2,
        name="rope_tables",
    )()


def _tail_steps(x_ref, mix_ref, p_ref, y_ref, w_out_ref, ln1g_ref, ln1b_ref, wg_ref, wu_ref, wd_ref,
                ln2g_ref, ln2b_ref, wpp_ref, wpg_ref, bpg_ref, act_ref, h_ref, hb_ref):
    def out_proj():
        for c in range(D_MODEL // DOWN_CHUNK):
            cols = slice(c * DOWN_CHUNK, (c + 1) * DOWN_CHUNK)
            h_ref[:, cols] = DN_ALPHA * x_ref[:, cols] + _dot(mix_ref[...], w_out_ref[:, cols])
        h = _layer_norm(h_ref[...], ln1g_ref[...], ln1b_ref[...])
        h_ref[...] = h
        hb_ref[...] = h.astype(BF16)

    def ff(c):
        cols = slice(c * FF_CHUNK, (c + 1) * FF_CHUNK)
        hb = hb_ref[...]
        act_ref[:, cols] = (_silu(_dot(hb, wg_ref[:, cols])) * _dot(hb, wu_ref[:, cols])).astype(BF16)

    def down(c):
        cols = slice(c * DOWN_CHUNK, (c + 1) * DOWN_CHUNK)
        h_ref[:, cols] = DN_ALPHA * h_ref[:, cols] + _dot(act_ref[...], wd_ref[:, cols])

    def norm2():
        h2 = _layer_norm(h_ref[...], ln2g_ref[...], ln2b_ref[...])
        h_ref[...] = h2
        hb_ref[...] = h2.astype(BF16)

    def ple(c):
        cols = slice(c * DOWN_CHUNK, (c + 1) * DOWN_CHUNK)
        gate = _sigmoid(_dot(hb_ref[...], wpg_ref[:, cols]) + bpg_ref[:, cols])
        y_ref[:, cols] = h_ref[:, cols] + gate * _dot(p_ref[...].astype(BF16), wpp_ref[:, cols])

    return (out_proj,
            [functools.partial(ff, c) for c in range(D_FF // FF_CHUNK)],
            [functools.partial(down, c) for c in range(D_MODEL // DOWN_CHUNK)],
            [norm2] + [functools.partial(ple, c) for c in range(D_MODEL // DOWN_CHUNK)])


def _tail_scratch(tl):
    return [pltpu.VMEM((tl, D_FF), BF16), pltpu.VMEM((tl, D_MODEL), F32), pltpu.VMEM((tl, D_MODEL), BF16)]


def _tail_specs():
    const = lambda i: (0, 0)
    resident = lambda shape: pl.BlockSpec(shape, const, pipeline_mode=pl.Buffered(1))
    vec = pl.BlockSpec((1, D_MODEL), const)
    return [
        resident((2 * GROUP_W, D_MODEL)),
        vec, vec,
        resident((D_MODEL, D_FF)),
        resident((D_MODEL, D_FF)),
        resident((D_FF, D_MODEL)),
        vec, vec,
        resident((PLE_DIM, D_MODEL)),
        resident((D_MODEL, D_MODEL)),
        vec,
    ]


def _in_proj_steps(x_ref, w_in_ref, proj_ref, xb_ref):
    def in_proj(c):
        if c == 0:
            xb_ref[...] = x_ref[...].astype(BF16)
        cols = slice(c * GROUP_W, (c + 1) * GROUP_W)
        proj_ref[:, cols] = _dot(xb_ref[...], w_in_ref[:, cols])

    return [functools.partial(in_proj, c) for c in range(IN_COLS // GROUP_W)]


def _ret_tables(dm_ref, rd_ref):
    tl = dm_ref.shape[1]
    r = lax.broadcasted_iota(jnp.int32, (tl, tl), 0)
    c = lax.broadcasted_iota(jnp.int32, (tl, tl), 1)
    row = lax.broadcasted_iota(jnp.int32, (tl, HEAD_DIM), 0).astype(F32)
    for h in range(N_HEADS):
        logd = RET_LOG_DECAY[h]
        dm_ref[h] = jnp.where(r >= c, jnp.exp((r - c).astype(F32) * logd), 0.0)
        rd_ref[0, h] = jnp.exp((row + 1.0) * logd)
        rd_ref[1, h] = jnp.exp((tl - 1.0 - row) * logd)


def _prompt_mixer_steps(cos_ref, sin_ref, lb_ref, ag_ref, bg_ref, bb_ref,
                        sa_ref, sb_ref, proj_ref, st_ref, oa_ref, mix_ref,
                        qd_ref, kd_ref, ke_ref, dec_ref, va_ref, kv_ref, sbf_ref, dm_ref, rd_ref, first):
    tl = proj_ref.shape[0]
    shift = REF_CHUNK.bit_length() - 1
    n_chunks = tl // REF_CHUNK

    def prepass(src_ref):
        q_dec, k_dec, kk, b = _hgrn_prepass(src_ref, _lower_bound(lb_ref), _causal_in_chunk(tl, shift))
        qd_ref[...] = q_dec
        for h in range(N_HEADS):
            kd_ref[h] = k_dec[:, _head(h)].T.astype(BF16)
        va_ref[...] = src_ref[:, 2 * GROUP_W:3 * GROUP_W].astype(BF16)
        last = [b[(n + 1) * REF_CHUNK - 1:(n + 1) * REF_CHUNK, :] for n in range(n_chunks)]
        b_last = jnp.concatenate([jnp.broadcast_to(r, (REF_CHUNK, GROUP_W)) for r in last], axis=0)
        k_end = kk * jnp.exp(b_last - b)
        for n in range(n_chunks):
            dec_ref[n:n + 1, :] = jnp.exp(last[n])
        odd = ((lax.broadcasted_iota(jnp.int32, (tl, GROUP_W), 0) >> shift) & 1) == 1
        k_even = jnp.where(odd, 0.0, k_end).astype(BF16)
        k_odd = jnp.where(odd, k_end, 0.0).astype(BF16)
        for h in range(N_HEADS):
            ke_ref[:, 2 * h * HEAD_DIM:(2 * h + 1) * HEAD_DIM] = k_even[:, _head(h)]
            ke_ref[:, (2 * h + 1) * HEAD_DIM:(2 * h + 2) * HEAD_DIM] = k_odd[:, _head(h)]

    def kv_scan(h):
        hs = _head(h)
        pair = 2 * REF_CHUNK
        for r in range(n_chunks // 2):
            rows = slice(r * pair, (r + 1) * pair)
            kv_ref[r] = _dot_tn(va_ref[rows, hs], ke_ref[rows, 2 * h * HEAD_DIM:(2 * h + 2) * HEAD_DIM])
        st = jnp.where(first, 0.0, st_ref[h])
        for n in range(n_chunks):
            sbf_ref[n, h] = st.T.astype(BF16)
            st = st * dec_ref[n:n + 1, hs] + kv_ref[n // 2, :, (n % 2) * HEAD_DIM:(n % 2 + 1) * HEAD_DIM]
        st_ref[h] = st
        sa_ref[0, h] = st.T

    def diag(h):
        hs = _head(h)
        sc = jnp.where(_causal_in_chunk(tl, shift), _dot(qd_ref[:, hs], kd_ref[h]), 0.0)
        oa_ref[:, hs] = _dot(sc.astype(BF16), va_ref[:, hs])

    def inter(n):
        rows = slice(n * REF_CHUNK, (n + 1) * REF_CHUNK)
        for h in range(N_HEADS):
            hs = _head(h)
            oa_ref[rows, hs] += _dot(qd_ref[rows, hs], sbf_ref[n, h])

    def hgrn_out():
        for h in range(N_HEADS):
            hs = _head(h)
            mix_ref[:, hs] = _rms_gate(oa_ref[:, hs], ag_ref[:, hs], proj_ref[:, _head(h, 3)]).astype(BF16)

    def ret(h):
        hs = _head(h)
        cos = cos_ref[...]
        sin = sin_ref[...]
        q = _rope(proj_ref[:, _head(h, 4)], cos, sin)
        k = _rope(proj_ref[:, _head(h, 5)], cos, sin) * K_SCALE
        v = proj_ref[:, _head(h, 6)].astype(BF16)
        a = (_dot(q.astype(BF16), k.T.astype(BF16)) * dm_ref[h]).astype(BF16)
        s = jnp.where(first, 0.0, sb_ref[0, h])
        o = _dot(a, v) + _dot((q * rd_ref[0, h]).astype(BF16), s.astype(BF16))
        k_end_b = (k * rd_ref[1, h]).astype(BF16)
        sb_ref[0, h] = s * math.exp(tl * RET_LOG_DECAY[h]) + _dot_tn(k_end_b, v)
        mix_ref[:, _head(h, 1)] = _ln_gate(o, bg_ref[:, hs], bb_ref[:, hs],
                                           proj_ref[:, _head(h, 7)]).astype(BF16)

    return (prepass,
            [functools.partial(kv_scan, h) for h in range(N_HEADS)],
            [functools.partial(diag, h) for h in range(N_HEADS)],
            [functools.partial(inter, n) for n in range(n_chunks)],
            hgrn_out,
            [functools.partial(ret, h) for h in range(N_HEADS)])


def _interleave(a, b):
    out = []
    for i in range(max(len(a), len(b))):
        out += a[i:i + 1] + b[i:i + 1]
    return out


def _prompt_layer_kernel(xn_ref, xp_ref, p_ref, cos_ref, sin_ref, w_in_ref, lb_ref, ag_ref, bg_ref,
                         bb_ref, *rest, tiles_per_seq):
    tail_w = rest[:11]
    y_ref, sa_ref, sb_ref = rest[11:14]
    mixer_scratch = rest[14:27]
    proj_ref, mix_ref, dm_ref, rd_ref = mixer_scratch[0], mixer_scratch[3], mixer_scratch[11], mixer_scratch[12]
    xb_ref = rest[27]
    tail_scratch = rest[28:]
    g = pl.program_id(0)
    n_tiles = pl.num_programs(0) - 1
    slot = lax.rem(g, 2)

    def mixer_steps():
        return _prompt_mixer_steps(cos_ref, sin_ref, lb_ref, ag_ref, bg_ref, bb_ref, sa_ref, sb_ref,
                                   proj_ref.at[slot], *mixer_scratch[1:],
                                   first=lax.rem(g, tiles_per_seq) == 0)

    def tail_steps():
        return _tail_steps(xp_ref, mix_ref, p_ref, y_ref, *tail_w, *tail_scratch)

    @pl.when(g == 0)
    def _():
        _ret_tables(dm_ref, rd_ref)
        prepass, kv_scan, diag, inter, hgrn_out, ret = mixer_steps()
        steps = _in_proj_steps(xp_ref, w_in_ref, proj_ref.at[slot], xb_ref)
        steps += [functools.partial(prepass, proj_ref.at[slot])] + kv_scan + diag + inter + [hgrn_out] + ret
        steps += _in_proj_steps(xn_ref, w_in_ref, proj_ref.at[1 - slot], xb_ref)
        steps += [functools.partial(prepass, proj_ref.at[1 - slot])]
        for step in steps:
            step()

    @pl.when((g > 0) & (g < n_tiles))
    def _():
        prepass, kv_scan, diag, inter, hgrn_out, ret = mixer_steps()
        out_proj, ff, down, final = tail_steps()
        in_proj = _in_proj_steps(xn_ref, w_in_ref, proj_ref.at[1 - slot], xb_ref)
        inter_pairs = [lambda a=a, b=b: (a(), b()) for a, b in zip(inter[0::2], inter[1::2])]
        prepass_next = functools.partial(prepass, proj_ref.at[1 - slot])
        steps = [out_proj, in_proj[0], in_proj[1]]
        steps += _interleave(ff + in_proj[2:4] + down,
                             kv_scan + diag + inter_pairs + [prepass_next] + ret + [hgrn_out])
        steps += _interleave(final, in_proj[4:])
        for step in steps:
            step()

    @pl.when(g == n_tiles)
    def _():
        out_proj, ff, down, final = tail_steps()
        for step in [out_proj] + ff + down + final:
            step()


def _prompt_layer(x, p, cos, sin, w_in, lb_logits, a_g, b_g, b_b, tail_w):
    bsz, seq, _ = x.shape
    tl = TOKEN_TILE
    tps = seq // tl
    n_tiles = bsz * tps
    x2 = x.reshape(bsz * seq, D_MODEL)
    p2 = p.reshape(bsz * seq, PLE_DIM)
    const = lambda g: (0, 0)
    nxt = lambda g: (jnp.minimum(g + 1, n_tiles - 1), 0)
    prev = lambda g: (jnp.maximum(g - 1, 0), 0)
    seq_tile = lambda g: (lax.rem(jnp.minimum(g, n_tiles - 1), tps), 0)
    state_spec = pl.BlockSpec((1, N_HEADS, HEAD_DIM, HEAD_DIM),
                              lambda g: (jnp.minimum(g, n_tiles - 1) // tps, 0, 0, 0))
    state_shape = jax.ShapeDtypeStruct((bsz, N_HEADS, HEAD_DIM, HEAD_DIM), F32)
    return pl.pallas_call(
        functools.partial(_prompt_layer_kernel, tiles_per_seq=tps),
        grid=(n_tiles + 1,),
        in_specs=[
            pl.BlockSpec((tl, D_MODEL), nxt),
            pl.BlockSpec((tl, D_MODEL), prev),
            pl.BlockSpec((tl, PLE_DIM), prev),
            pl.BlockSpec((tl, HEAD_DIM), seq_tile),
            pl.BlockSpec((tl, HEAD_DIM), seq_tile),
            pl.BlockSpec((D_MODEL, IN_COLS), const, pipeline_mode=pl.Buffered(1)),
            pl.BlockSpec(lb_logits.shape, const),
            pl.BlockSpec((1, GROUP_W), const),
            pl.BlockSpec((1, GROUP_W), const),
            pl.BlockSpec((1, GROUP_W), const),
        ] + _tail_specs(),
        out_specs=[
            pl.BlockSpec((tl, D_MODEL), prev),
            state_spec,
            state_spec,
        ],
        out_shape=[
            jax.ShapeDtypeStruct((bsz * seq, D_MODEL), F32),
            state_shape,
            state_shape,
        ],
        scratch_shapes=[
            pltpu.VMEM((2, tl, IN_COLS), F32),
            pltpu.VMEM((N_HEADS, HEAD_DIM, HEAD_DIM), F32),
            pltpu.VMEM((tl, GROUP_W), F32),
            pltpu.VMEM((tl, 2 * GROUP_W), BF16),
            pltpu.VMEM((tl, GROUP_W), BF16),
            pltpu.VMEM((N_HEADS, HEAD_DIM, tl), BF16),
            pltpu.VMEM((tl, 2 * GROUP_W), BF16),
            pltpu.VMEM((tl // REF_CHUNK, GROUP_W), F32),
            pltpu.VMEM((tl, GROUP_W), BF16),
            pltpu.VMEM((tl // (2 * REF_CHUNK), HEAD_DIM, 2 * HEAD_DIM), F32),
            pltpu.VMEM((tl // REF_CHUNK, N_HEADS, HEAD_DIM, HEAD_DIM), BF16),
            pltpu.VMEM((N_HEADS, tl, tl), F32),
            pltpu.VMEM((2, N_HEADS, tl, HEAD_DIM), F32),
            pltpu.VMEM((tl, D_MODEL), BF16),
        ] + _tail_scratch(tl),
        compiler_params=pltpu.CompilerParams(
            dimension_semantics=("arbitrary",), vmem_limit_bytes=V7X_VMEM_LIMIT_BYTES),
        name="prompt_layer",
    )(x2, x2, p2, cos, sin, w_in, lb_logits, a_g, b_g, b_b, *tail_w)


def _in_proj_kernel(x_ref, w_ref, o_ref, wb_ref):
    wb_ref[...] = w_ref[...].astype(BF16)
    o_ref[...] = _dot(x_ref[...].astype(BF16), wb_ref[...])


def _in_proj(x, w_in):
    n = x.shape[0]
    tn = SAMPLE_PROJ_COLS
    return pl.pallas_call(
        _in_proj_kernel,
        grid=(IN_COLS // tn,),
        in_specs=[
            pl.BlockSpec((n, D_MODEL), lambda c: (0, 0)),
            pl.BlockSpec((D_MODEL, tn), lambda c: (0, c)),
        ],
        out_specs=[
            pl.BlockSpec((n, tn), lambda c: (0, c)),
            pl.BlockSpec((D_MODEL, tn), lambda c: (0, c)),
        ],
        out_shape=[
            jax.ShapeDtypeStruct((n, IN_COLS), F32),
            jax.ShapeDtypeStruct((D_MODEL, IN_COLS), BF16),
        ],
        compiler_params=pltpu.CompilerParams(
            dimension_semantics=("arbitrary",), vmem_limit_bytes=V7X_VMEM_LIMIT_BYTES),
        name="sample_in_proj",
    )(x, w_in)


def _sample_rec_kernel(proj_ref, sa_in_ref, sb_in_ref, lb_ref, ag_ref, bg_ref, bb_ref,
                       cos_ref, sin_ref, mix_ref, sa_ref, sb_ref, oa_ref, ob_ref, *, seq_len):
    rows_n = proj_ref.shape[0]
    n_seq = rows_n // seq_len
    causal = _causal_in_chunk(rows_n, seq_len.bit_length() - 1)

    q_dec, k_dec, kk, b = _hgrn_prepass(proj_ref, _lower_bound(lb_ref), causal)
    v_a = proj_ref[:, 2 * GROUP_W:3 * GROUP_W]
    for h in range(N_HEADS):
        hs = _head(h)
        sc = jnp.where(causal, _dot_nt(q_dec[:, hs], k_dec[:, hs].astype(BF16)), 0.0).astype(BF16)
        oa_ref[:, hs] = _dot(sc, v_a[:, hs].astype(BF16))
    q_dec32 = q_dec.astype(F32)
    rr = lax.broadcasted_iota(jnp.int32, (seq_len, GROUP_W), 0)
    ones_blk = jnp.ones((seq_len, HEAD_DIM), BF16)
    for s in range(n_seq):
        rows = slice(s * seq_len, (s + 1) * seq_len)
        b_last = b[(s + 1) * seq_len - 1:(s + 1) * seq_len, :]
        k_end = (kk[rows] * jnp.exp(b_last - b[rows])).astype(BF16)
        hi, mid, lo = [t.astype(F32) for t in _split3(jnp.exp(b_last))]
        dec_rows = jnp.where(rr == 0, hi, jnp.where(rr == 1, mid, jnp.where(rr == 2, lo, 0.0)))
        dec_rows = dec_rows.astype(BF16)
        for h in range(N_HEADS):
            hs = _head(h)
            st = sa_in_ref[s, h]
            oa_ref[rows, hs] += _dot(q_dec32[rows, hs].astype(BF16), st.astype(BF16))
            dec_kv = _dot_tn(dec_rows[:, hs], ones_blk)
            sa_ref[s, h] = st * dec_kv + _dot_tn(k_end[:, hs], v_a[rows, hs].astype(BF16))
    for h in range(N_HEADS):
        hs = _head(h)
        mix_ref[:, hs] = _rms_gate(oa_ref[:, hs], ag_ref[:, hs], proj_ref[:, _head(h, 3)]).astype(BF16)

    cos = jnp.concatenate([cos_ref[...]] * n_seq, axis=0)
    sin = jnp.concatenate([sin_ref[...]] * n_seq, axis=0)
    r = lax.broadcasted_iota(jnp.int32, (rows_n, rows_n), 0)
    c = lax.broadcasted_iota(jnp.int32, (rows_n, rows_n), 1)
    diff = ((r & (seq_len - 1)) - (c & (seq_len - 1))).astype(F32)
    row = (lax.broadcasted_iota(jnp.int32, (rows_n, HEAD_DIM), 0) & (seq_len - 1)).astype(F32)
    for h in range(N_HEADS):
        hs = _head(h)
        logd = RET_LOG_DECAY[h]
        q = _rope(proj_ref[:, _head(h, 4)], cos, sin)
        k = _rope(proj_ref[:, _head(h, 5)], cos, sin) * K_SCALE
        v32 = proj_ref[:, _head(h, 6)]
        dmask = jnp.where(causal, jnp.exp(diff * logd), 0.0)
        a = (_dot_nt(q.astype(BF16), k.astype(BF16)) * dmask).astype(BF16)
        ob_ref[...] = _dot(a, v32.astype(BF16))
        q_dec_b = q * jnp.exp((row + 1.0) * logd)
        k_end_b = k * jnp.exp((seq_len - 1.0 - row) * logd)
        for s in range(n_seq):
            rows = slice(s * seq_len, (s + 1) * seq_len)
            st = sb_in_ref[s, h]
            ob_ref[rows, :] += _dot(q_dec_b[rows].astype(BF16), st.astype(BF16))
            sb_ref[s, h] = st * math.exp(seq_len * logd) + _dot_tn(
                k_end_b[rows].astype(BF16), v32[rows].astype(BF16))
        mix_ref[:, _head(h, 1)] = _ln_gate(ob_ref[...], bg_ref[:, hs], bb_ref[:, hs],
                                           proj_ref[:, _head(h, 7)]).astype(BF16)


def _sample_rec(proj, sa, sb, lb_logits, a_g, b_g, b_b, cos, sin, seq_len):
    n_tok = proj.shape[0]
    n_seq = n_tok // seq_len
    bs = SAMPLE_SEQS
    rows = bs * seq_len
    const = lambda i: (0, 0)
    state_spec = pl.BlockSpec((bs, N_HEADS, HEAD_DIM, HEAD_DIM), lambda i: (i, 0, 0, 0))
    state_shape = jax.ShapeDtypeStruct((n_seq, N_HEADS, HEAD_DIM, HEAD_DIM), F32)
    return pl.pallas_call(
        functools.partial(_sample_rec_kernel, seq_len=seq_len),
        grid=(n_seq // bs,),
        in_specs=[
            pl.BlockSpec((rows, IN_COLS), lambda i: (i, 0)),
            state_spec,
            state_spec,
            pl.BlockSpec(lb_logits.shape, const),
            pl.BlockSpec((1, GROUP_W), const),
            pl.BlockSpec((1, GROUP_W), const),
            pl.BlockSpec((1, GROUP_W), const),
            pl.BlockSpec((seq_len, HEAD_DIM), const),
            pl.BlockSpec((seq_len, HEAD_DIM), const),
        ],
        out_specs=[
            pl.BlockSpec((rows, 2 * GROUP_W), lambda i: (i, 0)),
            state_spec,
            state_spec,
        ],
        out_shape=[
            jax.ShapeDtypeStruct((n_tok, 2 * GROUP_W), BF16),
            state_shape,
            state_shape,
        ],
        scratch_shapes=[
            pltpu.VMEM((rows, GROUP_W), F32),
            pltpu.VMEM((rows, HEAD_DIM), F32),
        ],
        compiler_params=pltpu.CompilerParams(
            dimension_semantics=("arbitrary",), vmem_limit_bytes=V7X_VMEM_LIMIT_BYTES),
        name="sample_recurrence",
    )(proj, sa, sb, lb_logits, a_g, b_g, b_b, cos, sin)


_TAIL_PHASE_STEPS = (D_MODEL // DOWN_CHUNK, D_FF // FF_CHUNK, D_MODEL // DOWN_CHUNK, D_MODEL // DOWN_CHUNK)
_TAIL_PHASE_START = tuple(sum(_TAIL_PHASE_STEPS[:i]) for i in range(4))


def _sample_tail_kernel(x_ref, mix_ref, p_ref, wo_ref, wg_ref, wu_ref, wd_ref, wpg_ref, wpp_ref,
                        ln1g_ref, ln1b_ref, ln2g_ref, ln2b_ref, bpg_ref,
                        y_ref, wo_b, wg_b, wu_b, wd_b, wpg_b, wpp_b, act_ref, h_ref, hb_ref):
    s = pl.program_id(0)
    a0, b0, c0, d0 = _TAIL_PHASE_START
    blk = DOWN_CHUNK

    @pl.when(s < b0)
    def _():
        wo_b[...] = wo_ref[...].astype(BF16)
        cols = pl.ds(pl.multiple_of((s - a0) * blk, blk), blk)
        h_ref[:, cols] = DN_ALPHA * x_ref[:, cols] + _dot(mix_ref[...], wo_b[...])

        @pl.when(s == b0 - 1)
        def _():
            h = _layer_norm(h_ref[...], ln1g_ref[...], ln1b_ref[...])
            h_ref[...] = h
            hb_ref[...] = h.astype(BF16)

    @pl.when((s >= b0) & (s < c0))
    def _():
        wg_b[...] = wg_ref[...].astype(BF16)
        wu_b[...] = wu_ref[...].astype(BF16)
        cols = pl.ds(pl.multiple_of((s - b0) * FF_CHUNK, FF_CHUNK), FF_CHUNK)
        hb = hb_ref[...]
        act_ref[:, cols] = (_silu(_dot(hb, wg_b[...])) * _dot(hb, wu_b[...])).astype(BF16)

    @pl.when((s >= c0) & (s < d0))
    def _():
        wd_b[...] = wd_ref[...].astype(BF16)
        cols = pl.ds(pl.multiple_of((s - c0) * blk, blk), blk)
        h_ref[:, cols] = DN_ALPHA * h_ref[:, cols] + _dot(act_ref[...], wd_b[...])

        @pl.when(s == d0 - 1)
        def _():
            h2 = _layer_norm(h_ref[...], ln2g_ref[...], ln2b_ref[...])
            h_ref[...] = h2
            hb_ref[...] = h2.astype(BF16)

    @pl.when(s >= d0)
    def _():
        wpg_b[...] = wpg_ref[...].astype(BF16)
        wpp_b[...] = wpp_ref[...].astype(BF16)
        cols = pl.ds(pl.multiple_of((s - d0) * blk, blk), blk)
        gate = _sigmoid(_dot(hb_ref[...], wpg_b[...]) + bpg_ref[:, cols])
        y_ref[...] = h_ref[:, cols] + gate * _dot(p_ref[...].astype(BF16), wpp_b[...])


def _sample_tail(x, mix, p, w_out, ln1g, ln1b, wg, wu, wd, ln2g, ln2b, wpp, wpg, bpg):
    n = x.shape[0]
    a0, b0, c0, d0 = _TAIL_PHASE_START
    na, nb, nc, nd = _TAIL_PHASE_STEPS
    const = lambda s: (0, 0)
    col = lambda start, count: (lambda s: (0, jnp.clip(s - start, 0, count - 1)))
    vec = pl.BlockSpec((1, D_MODEL), const)
    weight_specs = [
        pl.BlockSpec((2 * GROUP_W, DOWN_CHUNK), col(a0, na)),
        pl.BlockSpec((D_MODEL, FF_CHUNK), col(b0, nb)),
        pl.BlockSpec((D_MODEL, FF_CHUNK), col(b0, nb)),
        pl.BlockSpec((D_FF, DOWN_CHUNK), col(c0, nc)),
        pl.BlockSpec((D_MODEL, DOWN_CHUNK), col(d0, nd)),
        pl.BlockSpec((PLE_DIM, DOWN_CHUNK), col(d0, nd)),
    ]
    weights = (w_out, wg, wu, wd, wpg, wpp)
    return pl.pallas_call(
        _sample_tail_kernel,
        grid=(sum(_TAIL_PHASE_STEPS),),
        in_specs=[
            pl.BlockSpec((n, D_MODEL), const),
            pl.BlockSpec((n, 2 * GROUP_W), const),
            pl.BlockSpec((n, PLE_DIM), const),
        ] + weight_specs + [vec, vec, vec, vec, vec],
        out_specs=[pl.BlockSpec((n, DOWN_CHUNK), col(d0, nd))] + weight_specs,
        out_shape=[jax.ShapeDtypeStruct((n, D_MODEL), F32)]
        + [jax.ShapeDtypeStruct(w.shape, BF16) for w in weights],
        scratch_shapes=_tail_scratch(n),
        compiler_params=pltpu.CompilerParams(
            dimension_semantics=("arbitrary",), vmem_limit_bytes=V7X_VMEM_LIMIT_BYTES),
        name="sample_tail",
    )(x, mix, p, *weights, ln1g, ln1b, ln2g, ln2b, bpg)


def kernel(x_prompt, x_sample, p_prompt, p_sample, state_hgrn, state_ret, lb_logits, w_in, a_norm_g, b_norm_g, b_norm_b, w_out, ln1_g, ln1_b, w_ffn_gate, w_ffn_up, w_ffn_down, ln2_g, ln2_b, w_ple_proj, w_ple_gate, b_ple_gate):
    assert w_in.shape[0] == DEPTH == 1
    bsz, seq, _ = x_prompt.shape
    n_dec, dec_seq, _ = x_sample.shape

    mixer_vecs = (lb_logits, a_norm_g, b_norm_g, b_norm_b)
    cos_p, sin_p = _rope_tables(seq, 0)
    cos_s, sin_s = _rope_tables(dec_seq, PAST_LEN)

    x_s = x_sample.reshape(n_dec * dec_seq, D_MODEL)
    proj_s, w_in_b = _in_proj(x_s, w_in[0])
    mix_s, sa_s, sb_s = _sample_rec(proj_s, state_hgrn[0], state_ret[0], *mixer_vecs,
                                    cos_s, sin_s, dec_seq)
    y_s, w_out_b, wg_b, wu_b, wd_b, wpg_b, wpp_b = _sample_tail(
        x_s, mix_s, p_sample[0].reshape(n_dec * dec_seq, PLE_DIM), w_out[0], ln1_g, ln1_b,
        w_ffn_gate[0], w_ffn_up[0], w_ffn_down[0], ln2_g, ln2_b, w_ple_proj[0], w_ple_gate[0],
        b_ple_gate)

    tail_w = (w_out_b, ln1_g, ln1_b, wg_b, wu_b, wd_b, ln2_g, ln2_b, wpp_b, wpg_b, b_ple_gate)
    y_p, sa_p, sb_p = _prompt_layer(x_prompt, p_prompt[0], cos_p, sin_p, w_in_b, *mixer_vecs, tail_w)

    return (y_p.reshape(bsz, seq, D_MODEL), y_s.reshape(n_dec, dec_seq, D_MODEL),
            sa_p[None], sb_p[None], sa_s[None], sb_s[None])
```

```python
import functools
import math

import jax
import jax.numpy as jnp
from jax import lax
from jax.experimental import pallas as pl
from jax.experimental.pallas import tpu as pltpu

F32 = jnp.float32
BF16 = jnp.bfloat16

D_MODEL = 1024
N_HEADS = 4
HEAD_DIM = 128
GROUP_W = N_HEADS * HEAD_DIM
IN_COLS = 8 * GROUP_W
D_FF = 2816
PLE_DIM = 256
DEPTH = 1
PAST_LEN = 16384
REF_CHUNK = 32
ROPE_BASE = 10000.0
NORM_EPS = 1e-5
DN_ALPHA = (2.0 * DEPTH) ** 0.25
RET_LOG_DECAY = tuple(math.log1p(-(2.0 ** (-5.0 - h))) for h in range(N_HEADS))
K_SCALE = HEAD_DIM ** -0.5

V7X_VMEM_LIMIT_BYTES = 60 * 1024 * 1024

TOKEN_TILE = 256
SAMPLE_PROJ_COLS = 1024
SAMPLE_SEQS = 16
FF_CHUNK = 256
DOWN_CHUNK = 256


def _dot(a, b):
    return jnp.dot(a, b, preferred_element_type=F32)


def _dot_nt(a, b):
    return lax.dot_general(a, b, (((1,), (1,)), ((), ())), preferred_element_type=F32)


def _dot_tn(a, b):
    return lax.dot_general(a, b, (((0,), (0,)), ((), ())), preferred_element_type=F32)


def _split3(x):
    hi = x.astype(BF16)
    r1 = x - hi.astype(F32)
    mid = r1.astype(BF16)
    lo = (r1 - mid.astype(F32)).astype(BF16)
    return hi, mid, lo


def _dot_exact_lhs01(m01, parts):
    return _dot(jnp.concatenate([m01] * 3, axis=1), jnp.concatenate(list(parts), axis=0))


def _sigmoid(x):
    return 1.0 / (1.0 + jnp.exp(-x))


def _silu(x):
    return x * _sigmoid(x)


def _causal_in_chunk(n, shift):
    r = lax.broadcasted_iota(jnp.int32, (n, n), 0)
    c = lax.broadcasted_iota(jnp.int32, (n, n), 1)
    return ((r >> shift) == (c >> shift)) & (c <= r)


def _lower_bound(lb_ref):
    rows = [lb_ref[i:i + 1, :] for i in range(lb_ref.shape[0])]
    m = functools.reduce(jnp.maximum, rows)
    e = [jnp.exp(r - m) for r in rows]
    return e[0] / functools.reduce(jnp.add, e)


def _hgrn_prepass(proj_ref, lb, causal):
    tri = jnp.where(causal, 1.0, 0.0).astype(BF16)
    f = lb + (1.0 - lb) * _sigmoid(proj_ref[:, GROUP_W:2 * GROUP_W])
    kk = 1.0 - f
    b = _dot_exact_lhs01(tri, _split3(jnp.log(f)))
    q_dec = (_silu(proj_ref[:, 0:GROUP_W]) * jnp.exp(b)).astype(BF16)
    k_dec = kk * jnp.exp(-b)
    return q_dec, k_dec, kk, b


def _rope(x, cos, sin_signed):
    return x * cos + pltpu.roll(x, HEAD_DIM // 2, axis=1) * sin_signed


def _rms_gate(o, g, gate):
    return o * lax.rsqrt(jnp.mean(o * o, axis=-1, keepdims=True) + NORM_EPS) * g * _silu(gate)


def _ln_gate(o, g, b, gate):
    mu = jnp.mean(o, axis=-1, keepdims=True)
    d = o - mu
    var = jnp.mean(d * d, axis=-1, keepdims=True)
    return (d * lax.rsqrt(var + NORM_EPS) * g + b) * _silu(gate)


def _layer_norm(x, g, b):
    mu = jnp.mean(x, axis=-1, keepdims=True)
    d = x - mu
    var = jnp.mean(d * d, axis=-1, keepdims=True)
    return d * lax.rsqrt(var + NORM_EPS) * g + b


def _head(h, group=0):
    return slice(group * GROUP_W + h * HEAD_DIM, group * GROUP_W + (h + 1) * HEAD_DIM)


def _rope_table_kernel(cos_ref, sin_ref, *, offset):
    n = cos_ref.shape[0]
    half = HEAD_DIM // 2
    packed = n % 16 == 0
    m = n // 2 if packed else n
    row = lax.broadcasted_iota(jnp.int32, (m, HEAD_DIM), 0) + pl.program_id(0) * n
    lane = lax.broadcasted_iota(jnp.int32, (m, HEAD_DIM), 1)
    low = lane < half
    if packed:
        row = row + jnp.where(low, 0, m)
    j = (lane & (half - 1)).astype(F32)
    inv = jnp.exp(-(j / half) * math.log(ROPE_BASE))
    ang = (row.astype(F32) + offset) * inv
    c = jnp.cos(ang)
    s = jnp.sin(ang)
    if not packed:
        cos_ref[...] = c
        sin_ref[...] = jnp.where(low, -s, s)
        return
    c_sw = pltpu.roll(c, half, axis=1)
    s_sw = pltpu.roll(s, half, axis=1)
    cos_ref[0:m, :] = jnp.where(low, c, c_sw)
    cos_ref[m:n, :] = jnp.where(low, c_sw, c)
    sin_ref[0:m, :] = jnp.where(low, -s, s_sw)
    sin_ref[m:n, :] = jnp.where(low, -s_sw, s)


def _rope_tables(n, offset):
    tile = min(n, 512)
    return pl.pallas_call(
        functools.partial(_rope_table_kernel, offset=float(offset)),
        grid=(n // tile,),
        in_specs=[],
        out_specs=[pl.BlockSpec((tile, HEAD_DIM), lambda i: (i, 0))] * 2,
        out_shape=[jax.ShapeDtypeStruct((n, HEAD_DIM), F32)] * 2,
        name="rope_tables",
    )()


def _tail_steps(x_ref, mix_ref, p_ref, y_ref, w_out_ref, ln1g_ref, ln1b_ref, wg_ref, wu_ref, wd_ref,
                ln2g_ref, ln2b_ref, wpp_ref, wpg_ref, bpg_ref, act_ref, h_ref, hb_ref):
    def out_proj():
        for c in range(D_MODEL // DOWN_CHUNK):
            cols = slice(c * DOWN_CHUNK, (c + 1) * DOWN_CHUNK)
            h_ref[:, cols] = DN_ALPHA * x_ref[:, cols] + _dot(mix_ref[...], w_out_ref[:, cols])
        h = _layer_norm(h_ref[...], ln1g_ref[...], ln1b_ref[...])
        h_ref[...] = h
        hb_ref[...] = h.astype(BF16)

    def ff(c):
        cols = slice(c * FF_CHUNK, (c + 1) * FF_CHUNK)
        hb = hb_ref[...]
        act_ref[:, cols] = (_silu(_dot(hb, wg_ref[:, cols])) * _dot(hb, wu_ref[:, cols])).astype(BF16)

    def down(c):
        cols = slice(c * DOWN_CHUNK, (c + 1) * DOWN_CHUNK)
        h_ref[:, cols] = DN_ALPHA * h_ref[:, cols] + _dot(act_ref[...], wd_ref[:, cols])

    def norm2():
        h2 = _layer_norm(h_ref[...], ln2g_ref[...], ln2b_ref[...])
        h_ref[...] = h2
        hb_ref[...] = h2.astype(BF16)

    def ple(c):
        cols = slice(c * DOWN_CHUNK, (c + 1) * DOWN_CHUNK)
        gate = _sigmoid(_dot(hb_ref[...], wpg_ref[:, cols]) + bpg_ref[:, cols])
        y_ref[:, cols] = h_ref[:, cols] + gate * _dot(p_ref[...].astype(BF16), wpp_ref[:, cols])

    return (out_proj,
            [functools.partial(ff, c) for c in range(D_FF // FF_CHUNK)],
            [functools.partial(down, c) for c in range(D_MODEL // DOWN_CHUNK)],
            [norm2] + [functools.partial(ple, c) for c in range(D_MODEL // DOWN_CHUNK)])


def _tail_scratch(tl):
    return [pltpu.VMEM((tl, D_FF), BF16), pltpu.VMEM((tl, D_MODEL), F32), pltpu.VMEM((tl, D_MODEL), BF16)]


def _tail_specs():
    const = lambda i: (0, 0)
    resident = lambda shape: pl.BlockSpec(shape, const, pipeline_mode=pl.Buffered(1))
    vec = pl.BlockSpec((1, D_MODEL), const)
    return [
        resident((2 * GROUP_W, D_MODEL)),
        vec, vec,
        resident((D_MODEL, D_FF)),
        resident((D_MODEL, D_FF)),
        resident((D_FF, D_MODEL)),
        vec, vec,
        resident((PLE_DIM, D_MODEL)),
        resident((D_MODEL, D_MODEL)),
        vec,
    ]


def _in_proj_steps(x_ref, w_in_ref, proj_ref, xb_ref):
    def in_proj(c):
        if c == 0:
            xb_ref[...] = x_ref[...].astype(BF16)
        cols = slice(c * GROUP_W, (c + 1) * GROUP_W)
        proj_ref[:, cols] = _dot(xb_ref[...], w_in_ref[:, cols])

    return [functools.partial(in_proj, c) for c in range(IN_COLS // GROUP_W)]


def _ret_tables(dm_ref, rd_ref):
    tl = dm_ref.shape[1]
    r = lax.broadcasted_iota(jnp.int32, (tl, tl), 0)
    c = lax.broadcasted_iota(jnp.int32, (tl, tl), 1)
    row = lax.broadcasted_iota(jnp.int32, (tl, HEAD_DIM), 0).astype(F32)
    for h in range(N_HEADS):
        logd = RET_LOG_DECAY[h]
        dm_ref[h] = jnp.where(r >= c, jnp.exp((r - c).astype(F32) * logd), 0.0)
        rd_ref[0, h] = jnp.exp((row + 1.0) * logd)
        rd_ref[1, h] = jnp.exp((tl - 1.0 - row) * logd)


def _prompt_mixer_steps(cos_ref, sin_ref, lb_ref, ag_ref, bg_ref, bb_ref,
                        sa_ref, sb_ref, proj_ref, st_ref, oa_ref, mix_ref,
                        qd_ref, kd_ref, ke_ref, dec_ref, va_ref, kv_ref, sbf_ref, dm_ref, rd_ref, first):
    tl = proj_ref.shape[0]
    shift = REF_CHUNK.bit_length() - 1
    n_chunks = tl // REF_CHUNK

    def prepass():
        q_dec, k_dec, kk, b = _hgrn_prepass(proj_ref, _lower_bound(lb_ref), _causal_in_chunk(tl, shift))
        qd_ref[...] = q_dec
        for h in range(N_HEADS):
            kd_ref[h] = k_dec[:, _head(h)].T.astype(BF16)
        va_ref[...] = proj_ref[:, 2 * GROUP_W:3 * GROUP_W].astype(BF16)
        last = [b[(n + 1) * REF_CHUNK - 1:(n + 1) * REF_CHUNK, :] for n in range(n_chunks)]
        b_last = jnp.concatenate([jnp.broadcast_to(r, (REF_CHUNK, GROUP_W)) for r in last], axis=0)
        k_end = kk * jnp.exp(b_last - b)
        for n in range(n_chunks):
            dec_ref[n:n + 1, :] = jnp.exp(last[n])
        odd = ((lax.broadcasted_iota(jnp.int32, (tl, GROUP_W), 0) >> shift) & 1) == 1
        k_even = jnp.where(odd, 0.0, k_end).astype(BF16)
        k_odd = jnp.where(odd, k_end, 0.0).astype(BF16)
        for h in range(N_HEADS):
            ke_ref[:, 2 * h * HEAD_DIM:(2 * h + 1) * HEAD_DIM] = k_even[:, _head(h)]
            ke_ref[:, (2 * h + 1) * HEAD_DIM:(2 * h + 2) * HEAD_DIM] = k_odd[:, _head(h)]

    def kv_scan(h):
        hs = _head(h)
        pair = 2 * REF_CHUNK
        for r in range(n_chunks // 2):
            rows = slice(r * pair, (r + 1) * pair)
            kv_ref[r] = _dot_tn(va_ref[rows, hs], ke_ref[rows, 2 * h * HEAD_DIM:(2 * h + 2) * HEAD_DIM])
        st = jnp.where(first, 0.0, st_ref[h])
        for n in range(n_chunks):
            sbf_ref[n, h] = st.T.astype(BF16)
            st = st * dec_ref[n:n + 1, hs] + kv_ref[n // 2, :, (n % 2) * HEAD_DIM:(n % 2 + 1) * HEAD_DIM]
        st_ref[h] = st
        sa_ref[0, h] = st.T

    def diag(h):
        hs = _head(h)
        sc = jnp.where(_causal_in_chunk(tl, shift), _dot(qd_ref[:, hs], kd_ref[h]), 0.0)
        oa_ref[:, hs] = _dot(sc.astype(BF16), va_ref[:, hs])

    def inter(n):
        rows = slice(n * REF_CHUNK, (n + 1) * REF_CHUNK)
        for h in range(N_HEADS):
            hs = _head(h)
            oa_ref[rows, hs] += _dot(qd_ref[rows, hs], sbf_ref[n, h])

    def hgrn_out():
        for h in range(N_HEADS):
            hs = _head(h)
            mix_ref[:, hs] = _rms_gate(oa_ref[:, hs], ag_ref[:, hs], proj_ref[:, _head(h, 3)]).astype(BF16)

    def ret(h):
        hs = _head(h)
        cos = cos_ref[...]
        sin = sin_ref[...]
        q = _rope(proj_ref[:, _head(h, 4)], cos, sin)
        k = _rope(proj_ref[:, _head(h, 5)], cos, sin) * K_SCALE
        v = proj_ref[:, _head(h, 6)].astype(BF16)
        a = (_dot(q.astype(BF16), k.T.astype(BF16)) * dm_ref[h]).astype(BF16)
        s = jnp.where(first, 0.0, sb_ref[0, h])
        o = _dot(a, v) + _dot((q * rd_ref[0, h]).astype(BF16), s.astype(BF16))
        k_end_b = (k * rd_ref[1, h]).astype(BF16)
        sb_ref[0, h] = s * math.exp(tl * RET_LOG_DECAY[h]) + _dot_tn(k_end_b, v)
        mix_ref[:, _head(h, 1)] = _ln_gate(o, bg_ref[:, hs], bb_ref[:, hs],
                                           proj_ref[:, _head(h, 7)]).astype(BF16)

    return (prepass,
            [functools.partial(kv_scan, h) for h in range(N_HEADS)],
            [functools.partial(diag, h) for h in range(N_HEADS)],
            [functools.partial(inter, n) for n in range(n_chunks)],
            hgrn_out,
            [functools.partial(ret, h) for h in range(N_HEADS)])


def _interleave(a, b):
    out = []
    for i in range(max(len(a), len(b))):
        out += a[i:i + 1] + b[i:i + 1]
    return out


def _prompt_layer_kernel(xn_ref, xp_ref, p_ref, cos_ref, sin_ref, w_in_ref, lb_ref, ag_ref, bg_ref,
                         bb_ref, *rest, tiles_per_seq):
    tail_w = rest[:11]
    y_ref, sa_ref, sb_ref = rest[11:14]
    mixer_scratch = rest[14:27]
    proj_ref, mix_ref, dm_ref, rd_ref = mixer_scratch[0], mixer_scratch[3], mixer_scratch[11], mixer_scratch[12]
    xb_ref = rest[27]
    tail_scratch = rest[28:]
    g = pl.program_id(0)
    n_tiles = pl.num_programs(0) - 1
    slot = lax.rem(g, 2)

    def mixer_steps():
        return _prompt_mixer_steps(cos_ref, sin_ref, lb_ref, ag_ref, bg_ref, bb_ref, sa_ref, sb_ref,
                                   proj_ref.at[slot], *mixer_scratch[1:],
                                   first=lax.rem(g, tiles_per_seq) == 0)

    def tail_steps():
        return _tail_steps(xp_ref, mix_ref, p_ref, y_ref, *tail_w, *tail_scratch)

    @pl.when(g == 0)
    def _():
        _ret_tables(dm_ref, rd_ref)
        prepass, kv_scan, diag, inter, hgrn_out, ret = mixer_steps()
        steps = _in_proj_steps(xp_ref, w_in_ref, proj_ref.at[slot], xb_ref)
        steps += [prepass] + kv_scan + diag + inter + [hgrn_out] + ret
        steps += _in_proj_steps(xn_ref, w_in_ref, proj_ref.at[1 - slot], xb_ref)
        for step in steps:
            step()

    @pl.when((g > 0) & (g < n_tiles))
    def _():
        prepass, kv_scan, diag, inter, hgrn_out, ret = mixer_steps()
        out_proj, ff, down, final = tail_steps()
        in_proj = _in_proj_steps(xn_ref, w_in_ref, proj_ref.at[1 - slot], xb_ref)
        inter_pairs = [lambda a=a, b=b: (a(), b()) for a, b in zip(inter[0::2], inter[1::2])]
        steps = [out_proj, in_proj[0], prepass, in_proj[1]]
        steps += _interleave(ff + down, kv_scan + diag + ret + inter_pairs + [hgrn_out])
        steps += _interleave(final, in_proj[2:])
        for step in steps:
            step()

    @pl.when(g == n_tiles)
    def _():
        out_proj, ff, down, final = tail_steps()
        for step in [out_proj] + ff + down + final:
            step()


def _prompt_layer(x, p, cos, sin, w_in, lb_logits, a_g, b_g, b_b, tail_w):
    bsz, seq, _ = x.shape
    tl = TOKEN_TILE
    tps = seq // tl
    n_tiles = bsz * tps
    x2 = x.reshape(bsz * seq, D_MODEL)
    p2 = p.reshape(bsz * seq, PLE_DIM)
    const = lambda g: (0, 0)
    nxt = lambda g: (jnp.minimum(g + 1, n_tiles - 1), 0)
    prev = lambda g: (jnp.maximum(g - 1, 0), 0)
    seq_tile = lambda g: (lax.rem(jnp.minimum(g, n_tiles - 1), tps), 0)
    state_spec = pl.BlockSpec((1, N_HEADS, HEAD_DIM, HEAD_DIM),
                              lambda g: (jnp.minimum(g, n_tiles - 1) // tps, 0, 0, 0))
    state_shape = jax.ShapeDtypeStruct((bsz, N_HEADS, HEAD_DIM, HEAD_DIM), F32)
    return pl.pallas_call(
        functools.partial(_prompt_layer_kernel, tiles_per_seq=tps),
        grid=(n_tiles + 1,),
        in_specs=[
            pl.BlockSpec((tl, D_MODEL), nxt),
            pl.BlockSpec((tl, D_MODEL), prev),
            pl.BlockSpec((tl, PLE_DIM), prev),
            pl.BlockSpec((tl, HEAD_DIM), seq_tile),
            pl.BlockSpec((tl, HEAD_DIM), seq_tile),
            pl.BlockSpec((D_MODEL, IN_COLS), const, pipeline_mode=pl.Buffered(1)),
            pl.BlockSpec(lb_logits.shape, const),
            pl.BlockSpec((1, GROUP_W), const),
            pl.BlockSpec((1, GROUP_W), const),
            pl.BlockSpec((1, GROUP_W), const),
        ] + _tail_specs(),
        out_specs=[
            pl.BlockSpec((tl, D_MODEL), prev),
            state_spec,
            state_spec,
        ],
        out_shape=[
            jax.ShapeDtypeStruct((bsz * seq, D_MODEL), F32),
            state_shape,
            state_shape,
        ],
        scratch_shapes=[
            pltpu.VMEM((2, tl, IN_COLS), F32),
            pltpu.VMEM((N_HEADS, HEAD_DIM, HEAD_DIM), F32),
            pltpu.VMEM((tl, GROUP_W), F32),
            pltpu.VMEM((tl, 2 * GROUP_W), BF16),
            pltpu.VMEM((tl, GROUP_W), BF16),
            pltpu.VMEM((N_HEADS, HEAD_DIM, tl), BF16),
            pltpu.VMEM((tl, 2 * GROUP_W), BF16),
            pltpu.VMEM((tl // REF_CHUNK, GROUP_W), F32),
            pltpu.VMEM((tl, GROUP_W), BF16),
            pltpu.VMEM((tl // (2 * REF_CHUNK), HEAD_DIM, 2 * HEAD_DIM), F32),
            pltpu.VMEM((tl // REF_CHUNK, N_HEADS, HEAD_DIM, HEAD_DIM), BF16),
            pltpu.VMEM((N_HEADS, tl, tl), F32),
            pltpu.VMEM((2, N_HEADS, tl, HEAD_DIM), F32),
            pltpu.VMEM((tl, D_MODEL), BF16),
        ] + _tail_scratch(tl),
        compiler_params=pltpu.CompilerParams(
            dimension_semantics=("arbitrary",), vmem_limit_bytes=V7X_VMEM_LIMIT_BYTES),
        name="prompt_layer",
    )(x2, x2, p2, cos, sin, w_in, lb_logits, a_g, b_g, b_b, *tail_w)


def _in_proj_kernel(x_ref, w_ref, o_ref, wb_ref):
    wb_ref[...] = w_ref[...].astype(BF16)
    o_ref[...] = _dot(x_ref[...].astype(BF16), wb_ref[...])


def _in_proj(x, w_in):
    n = x.shape[0]
    tn = SAMPLE_PROJ_COLS
    return pl.pallas_call(
        _in_proj_kernel,
        grid=(IN_COLS // tn,),
        in_specs=[
            pl.BlockSpec((n, D_MODEL), lambda c: (0, 0)),
            pl.BlockSpec((D_MODEL, tn), lambda c: (0, c)),
        ],
        out_specs=[
            pl.BlockSpec((n, tn), lambda c: (0, c)),
            pl.BlockSpec((D_MODEL, tn), lambda c: (0, c)),
        ],
        out_shape=[
            jax.ShapeDtypeStruct((n, IN_COLS), F32),
            jax.ShapeDtypeStruct((D_MODEL, IN_COLS), BF16),
        ],
        compiler_params=pltpu.CompilerParams(
            dimension_semantics=("arbitrary",), vmem_limit_bytes=V7X_VMEM_LIMIT_BYTES),
        name="sample_in_proj",
    )(x, w_in)


def _sample_rec_kernel(proj_ref, sa_in_ref, sb_in_ref, lb_ref, ag_ref, bg_ref, bb_ref,
                       cos_ref, sin_ref, mix_ref, sa_ref, sb_ref, oa_ref, ob_ref, *, seq_len):
    rows_n = proj_ref.shape[0]
    n_seq = rows_n // seq_len
    causal = _causal_in_chunk(rows_n, seq_len.bit_length() - 1)

    q_dec, k_dec, kk, b = _hgrn_prepass(proj_ref, _lower_bound(lb_ref), causal)
    v_a = proj_ref[:, 2 * GROUP_W:3 * GROUP_W]
    for h in range(N_HEADS):
        hs = _head(h)
        sc = jnp.where(causal, _dot_nt(q_dec[:, hs], k_dec[:, hs].astype(BF16)), 0.0).astype(BF16)
        oa_ref[:, hs] = _dot(sc, v_a[:, hs].astype(BF16))
    q_dec32 = q_dec.astype(F32)
    rr = lax.broadcasted_iota(jnp.int32, (seq_len, GROUP_W), 0)
    ones_blk = jnp.ones((seq_len, HEAD_DIM), BF16)
    for s in range(n_seq):
        rows = slice(s * seq_len, (s + 1) * seq_len)
        b_last = b[(s + 1) * seq_len - 1:(s + 1) * seq_len, :]
        k_end = (kk[rows] * jnp.exp(b_last - b[rows])).astype(BF16)
        hi, mid, lo = [t.astype(F32) for t in _split3(jnp.exp(b_last))]
        dec_rows = jnp.where(rr == 0, hi, jnp.where(rr == 1, mid, jnp.where(rr == 2, lo, 0.0)))
        dec_rows = dec_rows.astype(BF16)
        for h in range(N_HEADS):
            hs = _head(h)
            st = sa_in_ref[s, h]
            oa_ref[rows, hs] += _dot(q_dec32[rows, hs].astype(BF16), st.astype(BF16))
            dec_kv = _dot_tn(dec_rows[:, hs], ones_blk)
            sa_ref[s, h] = st * dec_kv + _dot_tn(k_end[:, hs], v_a[rows, hs].astype(BF16))
    for h in range(N_HEADS):
        hs = _head(h)
        mix_ref[:, hs] = _rms_gate(oa_ref[:, hs], ag_ref[:, hs], proj_ref[:, _head(h, 3)]).astype(BF16)

    cos = jnp.concatenate([cos_ref[...]] * n_seq, axis=0)
    sin = jnp.concatenate([sin_ref[...]] * n_seq, axis=0)
    r = lax.broadcasted_iota(jnp.int32, (rows_n, rows_n), 0)
    c = lax.broadcasted_iota(jnp.int32, (rows_n, rows_n), 1)
    diff = ((r & (seq_len - 1)) - (c & (seq_len - 1))).astype(F32)
    row = (lax.broadcasted_iota(jnp.int32, (rows_n, HEAD_DIM), 0) & (seq_len - 1)).astype(F32)
    for h in range(N_HEADS):
        hs = _head(h)
        logd = RET_LOG_DECAY[h]
        q = _rope(proj_ref[:, _head(h, 4)], cos, sin)
        k = _rope(proj_ref[:, _head(h, 5)], cos, sin) * K_SCALE
        v32 = proj_ref[:, _head(h, 6)]
        dmask = jnp.where(causal, jnp.exp(diff * logd), 0.0)
        a = (_dot_nt(q.astype(BF16), k.astype(BF16)) * dmask).astype(BF16)
        ob_ref[...] = _dot(a, v32.astype(BF16))
        q_dec_b = q * jnp.exp((row + 1.0) * logd)
        k_end_b = k * jnp.exp((seq_len - 1.0 - row) * logd)
        for s in range(n_seq):
            rows = slice(s * seq_len, (s + 1) * seq_len)
            st = sb_in_ref[s, h]
            ob_ref[rows, :] += _dot(q_dec_b[rows].astype(BF16), st.astype(BF16))
            sb_ref[s, h] = st * math.exp(seq_len * logd) + _dot_tn(
                k_end_b[rows].astype(BF16), v32[rows].astype(BF16))
        mix_ref[:, _head(h, 1)] = _ln_gate(ob_ref[...], bg_ref[:, hs], bb_ref[:, hs],
                                           proj_ref[:, _head(h, 7)]).astype(BF16)


def _sample_rec(proj, sa, sb, lb_logits, a_g, b_g, b_b, cos, sin, seq_len):
    n_tok = proj.shape[0]
    n_seq = n_tok // seq_len
    bs = SAMPLE_SEQS
    rows = bs * seq_len
    const = lambda i: (0, 0)
    state_spec = pl.BlockSpec((bs, N_HEADS, HEAD_DIM, HEAD_DIM), lambda i: (i, 0, 0, 0))
    state_shape = jax.ShapeDtypeStruct((n_seq, N_HEADS, HEAD_DIM, HEAD_DIM), F32)
    return pl.pallas_call(
        functools.partial(_sample_rec_kernel, seq_len=seq_len),
        grid=(n_seq // bs,),
        in_specs=[
            pl.BlockSpec((rows, IN_COLS), lambda i: (i, 0)),
            state_spec,
            state_spec,
            pl.BlockSpec(lb_logits.shape, const),
            pl.BlockSpec((1, GROUP_W), const),
            pl.BlockSpec((1, GROUP_W), const),
            pl.BlockSpec((1, GROUP_W), const),
            pl.BlockSpec((seq_len, HEAD_DIM), const),
            pl.BlockSpec((seq_len, HEAD_DIM), const),
        ],
        out_specs=[
            pl.BlockSpec((rows, 2 * GROUP_W), lambda i: (i, 0)),
            state_spec,
            state_spec,
        ],
        out_shape=[
            jax.ShapeDtypeStruct((n_tok, 2 * GROUP_W), BF16),
            state_shape,
            state_shape,
        ],
        scratch_shapes=[
            pltpu.VMEM((rows, GROUP_W), F32),
            pltpu.VMEM((rows, HEAD_DIM), F32),
        ],
        compiler_params=pltpu.CompilerParams(
            dimension_semantics=("arbitrary",), vmem_limit_bytes=V7X_VMEM_LIMIT_BYTES),
        name="sample_recurrence",
    )(proj, sa, sb, lb_logits, a_g, b_g, b_b, cos, sin)


_TAIL_PHASE_STEPS = (D_MODEL // DOWN_CHUNK, D_FF // FF_CHUNK, D_MODEL // DOWN_CHUNK, D_MODEL // DOWN_CHUNK)
_TAIL_PHASE_START = tuple(sum(_TAIL_PHASE_STEPS[:i]) for i in range(4))


def _sample_tail_kernel(x_ref, mix_ref, p_ref, wo_ref, wg_ref, wu_ref, wd_ref, wpg_ref, wpp_ref,
                        ln1g_ref, ln1b_ref, ln2g_ref, ln2b_ref, bpg_ref,
                        y_ref, wo_b, wg_b, wu_b, wd_b, wpg_b, wpp_b, act_ref, h_ref, hb_ref):
    s = pl.program_id(0)
    a0, b0, c0, d0 = _TAIL_PHASE_START
    blk = DOWN_CHUNK

    @pl.when(s < b0)
    def _():
        wo_b[...] = wo_ref[...].astype(BF16)
        cols = pl.ds(pl.multiple_of((s - a0) * blk, blk), blk)
        h_ref[:, cols] = DN_ALPHA * x_ref[...] + _dot(mix_ref[...], wo_b[...])

        @pl.when(s == b0 - 1)
        def _():
            h = _layer_norm(h_ref[...], ln1g_ref[...], ln1b_ref[...])
            h_ref[...] = h
            hb_ref[...] = h.astype(BF16)

    @pl.when((s >= b0) & (s < c0))
    def _():
        wg_b[...] = wg_ref[...].astype(BF16)
        wu_b[...] = wu_ref[...].astype(BF16)
        cols = pl.ds(pl.multiple_of((s - b0) * FF_CHUNK, FF_CHUNK), FF_CHUNK)
        hb = hb_ref[...]
        act_ref[:, cols] = (_silu(_dot(hb, wg_b[...])) * _dot(hb, wu_b[...])).astype(BF16)

    @pl.when((s >= c0) & (s < d0))
    def _():
        wd_b[...] = wd_ref[...].astype(BF16)
        cols = pl.ds(pl.multiple_of((s - c0) * blk, blk), blk)
        h_ref[:, cols] = DN_ALPHA * h_ref[:, cols] + _dot(act_ref[...], wd_b[...])

        @pl.when(s == d0 - 1)
        def _():
            h2 = _layer_norm(h_ref[...], ln2g_ref[...], ln2b_ref[...])
            h_ref[...] = h2
            hb_ref[...] = h2.astype(BF16)

    @pl.when(s >= d0)
    def _():
        wpg_b[...] = wpg_ref[...].astype(BF16)
        wpp_b[...] = wpp_ref[...].astype(BF16)
        cols = pl.ds(pl.multiple_of((s - d0) * blk, blk), blk)
        gate = _sigmoid(_dot(hb_ref[...], wpg_b[...]) + bpg_ref[:, cols])
        y_ref[...] = h_ref[:, cols] + gate * _dot(p_ref[...].astype(BF16), wpp_b[...])


def _sample_tail(x, mix, p, w_out, ln1g, ln1b, wg, wu, wd, ln2g, ln2b, wpp, wpg, bpg):
    n = x.shape[0]
    a0, b0, c0, d0 = _TAIL_PHASE_START
    na, nb, nc, nd = _TAIL_PHASE_STEPS
    const = lambda s: (0, 0)
    col = lambda start, count: (lambda s: (0, jnp.clip(s - start, 0, count - 1)))
    vec = pl.BlockSpec((1, D_MODEL), const)
    weight_specs = [
        pl.BlockSpec((2 * GROUP_W, DOWN_CHUNK), col(a0, na)),
        pl.BlockSpec((D_MODEL, FF_CHUNK), col(b0, nb)),
        pl.BlockSpec((D_MODEL, FF_CHUNK), col(b0, nb)),
        pl.BlockSpec((D_FF, DOWN_CHUNK), col(c0, nc)),
        pl.BlockSpec((D_MODEL, DOWN_CHUNK), col(d0, nd)),
        pl.BlockSpec((PLE_DIM, DOWN_CHUNK), col(d0, nd)),
    ]
    weights = (w_out, wg, wu, wd, wpg, wpp)
    return pl.pallas_call(
        _sample_tail_kernel,
        grid=(sum(_TAIL_PHASE_STEPS),),
        in_specs=[
            pl.BlockSpec((n, DOWN_CHUNK), col(a0, na)),
            pl.BlockSpec((n, 2 * GROUP_W), const),
            pl.BlockSpec((n, PLE_DIM), const),
        ] + weight_specs + [vec, vec, vec, vec, vec],
        out_specs=[pl.BlockSpec((n, DOWN_CHUNK), col(d0, nd))] + weight_specs,
        out_shape=[jax.ShapeDtypeStruct((n, D_MODEL), F32)]
        + [jax.ShapeDtypeStruct(w.shape, BF16) for w in weights],
        scratch_shapes=_tail_scratch(n),
        compiler_params=pltpu.CompilerParams(
            dimension_semantics=("arbitrary",), vmem_limit_bytes=V7X_VMEM_LIMIT_BYTES),
        name="sample_tail",
    )(x, mix, p, *weights, ln1g, ln1b, ln2g, ln2b, bpg)


def kernel(x_prompt, x_sample, p_prompt, p_sample, state_hgrn, state_ret, lb_logits, w_in, a_norm_g, b_norm_g, b_norm_b, w_out, ln1_g, ln1_b, w_ffn_gate, w_ffn_up, w_ffn_down, ln2_g, ln2_b, w_ple_proj, w_ple_gate, b_ple_gate):
    assert w_in.shape[0] == DEPTH == 1
    bsz, seq, _ = x_prompt.shape
    n_dec, dec_seq, _ = x_sample.shape

    mixer_vecs = (lb_logits, a_norm_g, b_norm_g, b_norm_b)
    cos_p, sin_p = _rope_tables(seq, 0)
    cos_s, sin_s = _rope_tables(dec_seq, PAST_LEN)

    x_s = x_sample.reshape(n_dec * dec_seq, D_MODEL)
    proj_s, w_in_b = _in_proj(x_s, w_in[0])
    mix_s, sa_s, sb_s = _sample_rec(proj_s, state_hgrn[0], state_ret[0], *mixer_vecs,
                                    cos_s, sin_s, dec_seq)
    y_s, w_out_b, wg_b, wu_b, wd_b, wpg_b, wpp_b = _sample_tail(
        x_s, mix_s, p_sample[0].reshape(n_dec * dec_seq, PLE_DIM), w_out[0], ln1_g, ln1_b,
        w_ffn_gate[0], w_ffn_up[0], w_ffn_down[0], ln2_g, ln2_b, w_ple_proj[0], w_ple_gate[0],
        b_ple_gate)

    tail_w = (w_out_b, ln1_g, ln1_b, wg_b, wu_b, wd_b, ln2_g, ln2_b, wpp_b, wpg_b, b_ple_gate)
    y_p, sa_p, sb_p = _prompt_layer(x_prompt, p_prompt[0], cos_p, sin_p, w_in_b, *mixer_vecs, tail_w)

    return (y_p.reshape(bsz, seq, D_MODEL), y_s.reshape(n_dec, dec_seq, D_MODEL),
            sa_p[None], sb_p[None], sa_s[None], sb_s[None])
```

```python
import functools
import math

import jax
import jax.numpy as jnp
from jax import lax
from jax.experimental import pallas as pl
from jax.experimental.pallas import tpu as pltpu

F32 = jnp.float32
BF16 = jnp.bfloat16

D_MODEL = 1024
N_HEADS = 4
HEAD_DIM = 128
GROUP_W = N_HEADS * HEAD_DIM
IN_COLS = 8 * GROUP_W
D_FF = 2816
PLE_DIM = 256
DEPTH = 1
PAST_LEN = 16384
REF_CHUNK = 32
ROPE_BASE = 10000.0
NORM_EPS = 1e-5
DN_ALPHA = (2.0 * DEPTH) ** 0.25
RET_LOG_DECAY = tuple(math.log1p(-(2.0 ** (-5.0 - h))) for h in range(N_HEADS))
K_SCALE = HEAD_DIM ** -0.5

V7X_VMEM_LIMIT_BYTES = 60 * 1024 * 1024

TOKEN_TILE = 256
SAMPLE_PROJ_COLS = 2048
SAMPLE_SEQS = 16
FF_CHUNK = 256
DOWN_CHUNK = 256


def _dot(a, b):
    return jnp.dot(a, b, preferred_element_type=F32)


def _dot_nt(a, b):
    return lax.dot_general(a, b, (((1,), (1,)), ((), ())), preferred_element_type=F32)


def _dot_tn(a, b):
    return lax.dot_general(a, b, (((0,), (0,)), ((), ())), preferred_element_type=F32)


def _split3(x):
    hi = x.astype(BF16)
    r1 = x - hi.astype(F32)
    mid = r1.astype(BF16)
    lo = (r1 - mid.astype(F32)).astype(BF16)
    return hi, mid, lo


def _dot_exact_lhs01(m01, parts):
    return _dot(jnp.concatenate([m01] * 3, axis=1), jnp.concatenate(list(parts), axis=0))


def _sigmoid(x):
    return 1.0 / (1.0 + jnp.exp(-x))


def _silu(x):
    return x * _sigmoid(x)


def _causal_in_chunk(n, shift):
    r = lax.broadcasted_iota(jnp.int32, (n, n), 0)
    c = lax.broadcasted_iota(jnp.int32, (n, n), 1)
    return ((r >> shift) == (c >> shift)) & (c <= r)


def _lower_bound(lb_ref):
    rows = [lb_ref[i:i + 1, :] for i in range(lb_ref.shape[0])]
    m = functools.reduce(jnp.maximum, rows)
    e = [jnp.exp(r - m) for r in rows]
    return e[0] / functools.reduce(jnp.add, e)


def _hgrn_prepass(proj_ref, lb, causal):
    tri = jnp.where(causal, 1.0, 0.0).astype(BF16)
    f = lb + (1.0 - lb) * _sigmoid(proj_ref[:, GROUP_W:2 * GROUP_W])
    kk = 1.0 - f
    b = _dot_exact_lhs01(tri, _split3(jnp.log(f)))
    q_dec = (_silu(proj_ref[:, 0:GROUP_W]) * jnp.exp(b)).astype(BF16)
    k_dec = kk * jnp.exp(-b)
    return q_dec, k_dec, kk, b


def _rope(x, cos, sin_signed):
    return x * cos + pltpu.roll(x, HEAD_DIM // 2, axis=1) * sin_signed


def _rms_gate(o, g, gate):
    return o * lax.rsqrt(jnp.mean(o * o, axis=-1, keepdims=True) + NORM_EPS) * g * _silu(gate)


def _layer_norm(x, g, b):
    mu = jnp.mean(x, axis=-1, keepdims=True)
    d = x - mu
    var = jnp.mean(d * d, axis=-1, keepdims=True)
    return d * lax.rsqrt(var + NORM_EPS) * g + b


def _ln_gate(o, g, b, gate):
    return _layer_norm(o, g, b) * _silu(gate)


def _head(h, group=0):
    return slice(group * GROUP_W + h * HEAD_DIM, group * GROUP_W + (h + 1) * HEAD_DIM)


def _rope_table_kernel(cos_ref, sin_ref, *, offset):
    n = cos_ref.shape[0]
    half = HEAD_DIM // 2
    packed = n % 16 == 0
    m = n // 2 if packed else n
    row = lax.broadcasted_iota(jnp.int32, (m, HEAD_DIM), 0) + pl.program_id(0) * n
    lane = lax.broadcasted_iota(jnp.int32, (m, HEAD_DIM), 1)
    low = lane < half
    if packed:
        row = row + jnp.where(low, 0, m)
    j = (lane & (half - 1)).astype(F32)
    inv = jnp.exp(-(j / half) * math.log(ROPE_BASE))
    ang = (row.astype(F32) + offset) * inv
    c = jnp.cos(ang)
    s = jnp.sin(ang)
    if not packed:
        cos_ref[...] = c
        sin_ref[...] = jnp.where(low, -s, s)
        return
    c_sw = pltpu.roll(c, half, axis=1)
    s_sw = pltpu.roll(s, half, axis=1)
    cos_ref[0:m, :] = jnp.where(low, c, c_sw)
    cos_ref[m:n, :] = jnp.where(low, c_sw, c)
    sin_ref[0:m, :] = jnp.where(low, -s, s_sw)
    sin_ref[m:n, :] = jnp.where(low, -s_sw, s)


def _rope_tables(n, offset):
    tile = min(n, 512)
    return pl.pallas_call(
        functools.partial(_rope_table_kernel, offset=float(offset)),
        grid=(n // tile,),
        in_specs=[],
        out_specs=[pl.BlockSpec((tile, HEAD_DIM), lambda i: (i, 0))] * 2,
        out_shape=[jax.ShapeDtypeStruct((n, HEAD_DIM), F32)] * 2,
        name="rope_tables",
    )()


def _tail_steps(x_ref, mix_ref, p_ref, y_ref, w_out_ref, ln1g_ref, ln1b_ref, wg_ref, wu_ref, wd_ref,
                ln2g_ref, ln2b_ref, wpp_ref, wpg_ref, bpg_ref, act_ref, h_ref, hb_ref):
    def out_proj():
        for c in range(D_MODEL // DOWN_CHUNK):
            cols = slice(c * DOWN_CHUNK, (c + 1) * DOWN_CHUNK)
            h_ref[:, cols] = DN_ALPHA * x_ref[:, cols] + _dot(mix_ref[...], w_out_ref[:, cols])
        h = _layer_norm(h_ref[...], ln1g_ref[...], ln1b_ref[...])
        h_ref[...] = h
        hb_ref[...] = h.astype(BF16)

    def ff(c):
        cols = slice(c * FF_CHUNK, (c + 1) * FF_CHUNK)
        hb = hb_ref[...]
        act_ref[:, cols] = (_silu(_dot(hb, wg_ref[:, cols])) * _dot(hb, wu_ref[:, cols])).astype(BF16)

    def down(c):
        cols = slice(c * DOWN_CHUNK, (c + 1) * DOWN_CHUNK)
        h_ref[:, cols] = DN_ALPHA * h_ref[:, cols] + _dot(act_ref[...], wd_ref[:, cols])

    def norm2():
        h2 = _layer_norm(h_ref[...], ln2g_ref[...], ln2b_ref[...])
        h_ref[...] = h2
        hb_ref[...] = h2.astype(BF16)

    def ple(c):
        cols = slice(c * DOWN_CHUNK, (c + 1) * DOWN_CHUNK)
        gate = _sigmoid(_dot(hb_ref[...], wpg_ref[:, cols]) + bpg_ref[:, cols])
        y_ref[:, cols] = h_ref[:, cols] + gate * _dot(p_ref[...].astype(BF16), wpp_ref[:, cols])

    return (out_proj,
            [functools.partial(ff, c) for c in range(D_FF // FF_CHUNK)],
            [functools.partial(down, c) for c in range(D_MODEL // DOWN_CHUNK)],
            [norm2] + [functools.partial(ple, c) for c in range(D_MODEL // DOWN_CHUNK)])


def _tail_scratch(tl):
    return [pltpu.VMEM((tl, D_FF), BF16), pltpu.VMEM((tl, D_MODEL), F32), pltpu.VMEM((tl, D_MODEL), BF16)]


def _tail_specs():
    const = lambda i: (0, 0)
    resident = lambda shape: pl.BlockSpec(shape, const, pipeline_mode=pl.Buffered(1))
    vec = pl.BlockSpec((1, D_MODEL), const)
    return [
        resident((2 * GROUP_W, D_MODEL)),
        vec, vec,
        resident((D_MODEL, D_FF)),
        resident((D_MODEL, D_FF)),
        resident((D_FF, D_MODEL)),
        vec, vec,
        resident((PLE_DIM, D_MODEL)),
        resident((D_MODEL, D_MODEL)),
        vec,
    ]


def _in_proj_steps(x_ref, w_in_ref, proj_ref, xb_ref):
    def in_proj(c):
        if c == 0:
            xb_ref[...] = x_ref[...].astype(BF16)
        cols = slice(c * GROUP_W, (c + 1) * GROUP_W)
        proj_ref[:, cols] = _dot(xb_ref[...], w_in_ref[:, cols])

    return [functools.partial(in_proj, c) for c in range(IN_COLS // GROUP_W)]


def _ret_tables(dm_ref, rd_ref):
    tl = dm_ref.shape[1]
    r = lax.broadcasted_iota(jnp.int32, (tl, tl), 0)
    c = lax.broadcasted_iota(jnp.int32, (tl, tl), 1)
    row = lax.broadcasted_iota(jnp.int32, (tl, HEAD_DIM), 0).astype(F32)
    for h in range(N_HEADS):
        logd = RET_LOG_DECAY[h]
        dm_ref[h] = jnp.where(r >= c, jnp.exp((r - c).astype(F32) * logd), 0.0)
        rd_ref[0, h] = jnp.exp((row + 1.0) * logd)
        rd_ref[1, h] = jnp.exp((tl - 1.0 - row) * logd)


def _prompt_mixer_steps(cos_ref, sin_ref, lb_ref, ag_ref, bg_ref, bb_ref,
                        sa_ref, sb_ref, proj_ref, st_ref, oa_ref, mix_ref,
                        qd_ref, kd_ref, ke_ref, dec_ref, va_ref, kv_ref, sbf_ref, dm_ref, rd_ref, first):
    tl = proj_ref.shape[0]
    shift = REF_CHUNK.bit_length() - 1
    n_chunks = tl // REF_CHUNK

    def prepass():
        q_dec, k_dec, kk, b = _hgrn_prepass(proj_ref, _lower_bound(lb_ref), _causal_in_chunk(tl, shift))
        qd_ref[...] = q_dec
        for h in range(N_HEADS):
            kd_ref[h] = k_dec[:, _head(h)].T.astype(BF16)
        va_ref[...] = proj_ref[:, 2 * GROUP_W:3 * GROUP_W].astype(BF16)
        last = [b[(n + 1) * REF_CHUNK - 1:(n + 1) * REF_CHUNK, :] for n in range(n_chunks)]
        b_last = jnp.concatenate([jnp.broadcast_to(r, (REF_CHUNK, GROUP_W)) for r in last], axis=0)
        k_end = kk * jnp.exp(b_last - b)
        for n in range(n_chunks):
            dec_ref[n:n + 1, :] = jnp.exp(last[n])
        odd = ((lax.broadcasted_iota(jnp.int32, (tl, GROUP_W), 0) >> shift) & 1) == 1
        k_even = jnp.where(odd, 0.0, k_end).astype(BF16)
        k_odd = jnp.where(odd, k_end, 0.0).astype(BF16)
        for h in range(N_HEADS):
            ke_ref[:, 2 * h * HEAD_DIM:(2 * h + 1) * HEAD_DIM] = k_even[:, _head(h)]
            ke_ref[:, (2 * h + 1) * HEAD_DIM:(2 * h + 2) * HEAD_DIM] = k_odd[:, _head(h)]

    def kv_scan(h):
        hs = _head(h)
        pair = 2 * REF_CHUNK
        for r in range(n_chunks // 2):
            rows = slice(r * pair, (r + 1) * pair)
            kv_ref[r] = _dot_tn(va_ref[rows, hs], ke_ref[rows, 2 * h * HEAD_DIM:(2 * h + 2) * HEAD_DIM])
        st = jnp.where(first, 0.0, st_ref[h])
        for n in range(n_chunks):
            sbf_ref[n, h] = st.T.astype(BF16)
            st = st * dec_ref[n:n + 1, hs] + kv_ref[n // 2, :, (n % 2) * HEAD_DIM:(n % 2 + 1) * HEAD_DIM]
        st_ref[h] = st
        sa_ref[0, h] = st.T

    def diag(h):
        hs = _head(h)
        sc = jnp.where(_causal_in_chunk(tl, shift), _dot(qd_ref[:, hs], kd_ref[h]), 0.0)
        oa_ref[:, hs] = _dot(sc.astype(BF16), va_ref[:, hs])

    def inter(n):
        rows = slice(n * REF_CHUNK, (n + 1) * REF_CHUNK)
        for h in range(N_HEADS):
            hs = _head(h)
            oa_ref[rows, hs] += _dot(qd_ref[rows, hs], sbf_ref[n, h])

    def hgrn_out():
        for h in range(N_HEADS):
            hs = _head(h)
            mix_ref[:, hs] = _rms_gate(oa_ref[:, hs], ag_ref[:, hs], proj_ref[:, _head(h, 3)]).astype(BF16)

    def ret(h):
        hs = _head(h)
        cos = cos_ref[...]
        sin = sin_ref[...]
        q = _rope(proj_ref[:, _head(h, 4)], cos, sin)
        k = _rope(proj_ref[:, _head(h, 5)], cos, sin) * K_SCALE
        v = proj_ref[:, _head(h, 6)].astype(BF16)
        a = (_dot(q.astype(BF16), k.T.astype(BF16)) * dm_ref[h]).astype(BF16)
        s = jnp.where(first, 0.0, sb_ref[0, h])
        o = _dot(a, v) + _dot((q * rd_ref[0, h]).astype(BF16), s.astype(BF16))
        k_end_b = (k * rd_ref[1, h]).astype(BF16)
        sb_ref[0, h] = s * math.exp(tl * RET_LOG_DECAY[h]) + _dot_tn(k_end_b, v)
        mix_ref[:, _head(h, 1)] = _ln_gate(o, bg_ref[:, hs], bb_ref[:, hs],
                                           proj_ref[:, _head(h, 7)]).astype(BF16)

    return (prepass,
            [functools.partial(kv_scan, h) for h in range(N_HEADS)],
            [functools.partial(diag, h) for h in range(N_HEADS)],
            [functools.partial(inter, n) for n in range(n_chunks)],
            hgrn_out,
            [functools.partial(ret, h) for h in range(N_HEADS)])


def _interleave(a, b):
    out = []
    for i in range(max(len(a), len(b))):
        out += a[i:i + 1] + b[i:i + 1]
    return out


def _prompt_layer_kernel(xn_ref, xp_ref, p_ref, cos_ref, sin_ref, w_in_ref, lb_ref, ag_ref, bg_ref,
                         bb_ref, *rest, tiles_per_seq):
    tail_w = rest[:11]
    y_ref, sa_ref, sb_ref = rest[11:14]
    mixer_scratch = rest[14:27]
    proj_ref, mix_ref, dm_ref, rd_ref = mixer_scratch[0], mixer_scratch[3], mixer_scratch[11], mixer_scratch[12]
    xb_ref = rest[27]
    tail_scratch = rest[28:]
    g = pl.program_id(0)
    n_tiles = pl.num_programs(0) - 1
    slot = lax.rem(g, 2)

    def mixer_steps():
        return _prompt_mixer_steps(cos_ref, sin_ref, lb_ref, ag_ref, bg_ref, bb_ref, sa_ref, sb_ref,
                                   proj_ref.at[slot], *mixer_scratch[1:],
                                   first=lax.rem(g, tiles_per_seq) == 0)

    def tail_steps():
        return _tail_steps(xp_ref, mix_ref, p_ref, y_ref, *tail_w, *tail_scratch)

    @pl.when(g == 0)
    def _():
        _ret_tables(dm_ref, rd_ref)
        prepass, kv_scan, diag, inter, hgrn_out, ret = mixer_steps()
        steps = _in_proj_steps(xp_ref, w_in_ref, proj_ref.at[slot], xb_ref)
        steps += [prepass] + kv_scan + diag + inter + [hgrn_out] + ret
        steps += _in_proj_steps(xn_ref, w_in_ref, proj_ref.at[1 - slot], xb_ref)
        for step in steps:
            step()

    @pl.when((g > 0) & (g < n_tiles))
    def _():
        prepass, kv_scan, diag, inter, hgrn_out, ret = mixer_steps()
        out_proj, ff, down, final = tail_steps()
        in_proj = _in_proj_steps(xn_ref, w_in_ref, proj_ref.at[1 - slot], xb_ref)
        inter_pairs = [lambda a=a, b=b: (a(), b()) for a, b in zip(inter[0::2], inter[1::2])]
        steps = [out_proj, in_proj[0], prepass, in_proj[1]]
        steps += _interleave(ff + down, kv_scan + diag + ret + inter_pairs + [hgrn_out])
        steps += _interleave(final, in_proj[2:])
        for step in steps:
            step()

    @pl.when(g == n_tiles)
    def _():
        out_proj, ff, down, final = tail_steps()
        for step in [out_proj] + ff + down + final:
            step()


def _prompt_layer(x, p, cos, sin, w_in, lb_logits, a_g, b_g, b_b, tail_w):
    bsz, seq, _ = x.shape
    tl = TOKEN_TILE
    tps = seq // tl
    n_tiles = bsz * tps
    x2 = x.reshape(bsz * seq, D_MODEL)
    p2 = p.reshape(bsz * seq, PLE_DIM)
    const = lambda g: (0, 0)
    nxt = lambda g: (jnp.minimum(g + 1, n_tiles - 1), 0)
    prev = lambda g: (jnp.maximum(g - 1, 0), 0)
    seq_tile = lambda g: (lax.rem(jnp.minimum(g, n_tiles - 1), tps), 0)
    state_spec = pl.BlockSpec((1, N_HEADS, HEAD_DIM, HEAD_DIM),
                              lambda g: (jnp.minimum(g, n_tiles - 1) // tps, 0, 0, 0))
    state_shape = jax.ShapeDtypeStruct((bsz, N_HEADS, HEAD_DIM, HEAD_DIM), F32)
    return pl.pallas_call(
        functools.partial(_prompt_layer_kernel, tiles_per_seq=tps),
        grid=(n_tiles + 1,),
        in_specs=[
            pl.BlockSpec((tl, D_MODEL), nxt),
            pl.BlockSpec((tl, D_MODEL), prev),
            pl.BlockSpec((tl, PLE_DIM), prev),
            pl.BlockSpec((tl, HEAD_DIM), seq_tile),
            pl.BlockSpec((tl, HEAD_DIM), seq_tile),
            pl.BlockSpec((D_MODEL, IN_COLS), const, pipeline_mode=pl.Buffered(1)),
            pl.BlockSpec(lb_logits.shape, const),
            pl.BlockSpec((1, GROUP_W), const),
            pl.BlockSpec((1, GROUP_W), const),
            pl.BlockSpec((1, GROUP_W), const),
        ] + _tail_specs(),
        out_specs=[
            pl.BlockSpec((tl, D_MODEL), prev),
            state_spec,
            state_spec,
        ],
        out_shape=[
            jax.ShapeDtypeStruct((bsz * seq, D_MODEL), F32),
            state_shape,
            state_shape,
        ],
        scratch_shapes=[
            pltpu.VMEM((2, tl, IN_COLS), F32),
            pltpu.VMEM((N_HEADS, HEAD_DIM, HEAD_DIM), F32),
            pltpu.VMEM((tl, GROUP_W), F32),
            pltpu.VMEM((tl, 2 * GROUP_W), BF16),
            pltpu.VMEM((tl, GROUP_W), BF16),
            pltpu.VMEM((N_HEADS, HEAD_DIM, tl), BF16),
            pltpu.VMEM((tl, 2 * GROUP_W), BF16),
            pltpu.VMEM((tl // REF_CHUNK, GROUP_W), F32),
            pltpu.VMEM((tl, GROUP_W), BF16),
            pltpu.VMEM((tl // (2 * REF_CHUNK), HEAD_DIM, 2 * HEAD_DIM), F32),
            pltpu.VMEM((tl // REF_CHUNK, N_HEADS, HEAD_DIM, HEAD_DIM), BF16),
            pltpu.VMEM((N_HEADS, tl, tl), F32),
            pltpu.VMEM((2, N_HEADS, tl, HEAD_DIM), F32),
            pltpu.VMEM((tl, D_MODEL), BF16),
        ] + _tail_scratch(tl),
        compiler_params=pltpu.CompilerParams(
            dimension_semantics=("arbitrary",), vmem_limit_bytes=V7X_VMEM_LIMIT_BYTES),
        name="prompt_layer",
    )(x2, x2, p2, cos, sin, w_in, lb_logits, a_g, b_g, b_b, *tail_w)


def _in_proj_kernel(x_ref, w_ref, o_ref, wb_ref):
    wb_ref[...] = w_ref[...].astype(BF16)
    o_ref[...] = _dot(x_ref[...].astype(BF16), wb_ref[...])


def _in_proj(x, w_in):
    n = x.shape[0]
    tn = SAMPLE_PROJ_COLS
    return pl.pallas_call(
        _in_proj_kernel,
        grid=(IN_COLS // tn,),
        in_specs=[
            pl.BlockSpec((n, D_MODEL), lambda c: (0, 0)),
            pl.BlockSpec((D_MODEL, tn), lambda c: (0, c)),
        ],
        out_specs=[
            pl.BlockSpec((n, tn), lambda c: (0, c)),
            pl.BlockSpec((D_MODEL, tn), lambda c: (0, c)),
        ],
        out_shape=[
            jax.ShapeDtypeStruct((n, IN_COLS), F32),
            jax.ShapeDtypeStruct((D_MODEL, IN_COLS), BF16),
        ],
        compiler_params=pltpu.CompilerParams(
            dimension_semantics=("arbitrary",), vmem_limit_bytes=V7X_VMEM_LIMIT_BYTES),
        name="sample_in_proj",
    )(x, w_in)


def _sample_rec_kernel(proj_ref, sa_in_ref, sb_in_ref, lb_ref, ag_ref, bg_ref, bb_ref,
                       cos_ref, sin_ref, mix_ref, sa_ref, sb_ref, oa_ref, ob_ref, *, seq_len):
    rows_n = proj_ref.shape[0]
    n_seq = rows_n // seq_len
    causal = _causal_in_chunk(rows_n, seq_len.bit_length() - 1)

    q_dec, k_dec, kk, b = _hgrn_prepass(proj_ref, _lower_bound(lb_ref), causal)
    v_a = proj_ref[:, 2 * GROUP_W:3 * GROUP_W]
    for h in range(N_HEADS):
        hs = _head(h)
        sc = jnp.where(causal, _dot_nt(q_dec[:, hs], k_dec[:, hs].astype(BF16)), 0.0).astype(BF16)
        oa_ref[:, hs] = _dot(sc, v_a[:, hs].astype(BF16))
    q_dec32 = q_dec.astype(F32)
    rr = lax.broadcasted_iota(jnp.int32, (seq_len, GROUP_W), 0)
    ones_blk = jnp.ones((seq_len, HEAD_DIM), BF16)
    for s in range(n_seq):
        rows = slice(s * seq_len, (s + 1) * seq_len)
        b_last = b[(s + 1) * seq_len - 1:(s + 1) * seq_len, :]
        k_end = (kk[rows] * jnp.exp(b_last - b[rows])).astype(BF16)
        hi, mid, lo = [t.astype(F32) for t in _split3(jnp.exp(b_last))]
        dec_rows = jnp.where(rr == 0, hi, jnp.where(rr == 1, mid, jnp.where(rr == 2, lo, 0.0)))
        dec_rows = dec_rows.astype(BF16)
        for h in range(N_HEADS):
            hs = _head(h)
            st = sa_in_ref[s, h]
            oa_ref[rows, hs] += _dot(q_dec32[rows, hs].astype(BF16), st.astype(BF16))
            dec_kv = _dot_tn(dec_rows[:, hs], ones_blk)
            sa_ref[s, h] = st * dec_kv + _dot_tn(k_end[:, hs], v_a[rows, hs].astype(BF16))
    for h in range(N_HEADS):
        hs = _head(h)
        mix_ref[:, hs] = _rms_gate(oa_ref[:, hs], ag_ref[:, hs], proj_ref[:, _head(h, 3)]).astype(BF16)

    cos = jnp.concatenate([cos_ref[...]] * n_seq, axis=0)
    sin = jnp.concatenate([sin_ref[...]] * n_seq, axis=0)
    r = lax.broadcasted_iota(jnp.int32, (rows_n, rows_n), 0)
    c = lax.broadcasted_iota(jnp.int32, (rows_n, rows_n), 1)
    diff = ((r & (seq_len - 1)) - (c & (seq_len - 1))).astype(F32)
    row = (lax.broadcasted_iota(jnp.int32, (rows_n, HEAD_DIM), 0) & (seq_len - 1)).astype(F32)
    for h in range(N_HEADS):
        hs = _head(h)
        logd = RET_LOG_DECAY[h]
        q = _rope(proj_ref[:, _head(h, 4)], cos, sin)
        k = _rope(proj_ref[:, _head(h, 5)], cos, sin) * K_SCALE
        v32 = proj_ref[:, _head(h, 6)]
        dmask = jnp.where(causal, jnp.exp(diff * logd), 0.0)
        a = (_dot_nt(q.astype(BF16), k.astype(BF16)) * dmask).astype(BF16)
        ob_ref[...] = _dot(a, v32.astype(BF16))
        q_dec_b = q * jnp.exp((row + 1.0) * logd)
        k_end_b = k * jnp.exp((seq_len - 1.0 - row) * logd)
        for s in range(n_seq):
            rows = slice(s * seq_len, (s + 1) * seq_len)
            st = sb_in_ref[s, h]
            ob_ref[rows, :] += _dot(q_dec_b[rows].astype(BF16), st.astype(BF16))
            sb_ref[s, h] = st * math.exp(seq_len * logd) + _dot_tn(
                k_end_b[rows].astype(BF16), v32[rows].astype(BF16))
        mix_ref[:, _head(h, 1)] = _ln_gate(ob_ref[...], bg_ref[:, hs], bb_ref[:, hs],
                                           proj_ref[:, _head(h, 7)]).astype(BF16)


def _sample_rec(proj, sa, sb, lb_logits, a_g, b_g, b_b, cos, sin, seq_len):
    n_tok = proj.shape[0]
    n_seq = n_tok // seq_len
    bs = SAMPLE_SEQS
    rows = bs * seq_len
    const = lambda i: (0, 0)
    state_spec = pl.BlockSpec((bs, N_HEADS, HEAD_DIM, HEAD_DIM), lambda i: (i, 0, 0, 0))
    state_shape = jax.ShapeDtypeStruct((n_seq, N_HEADS, HEAD_DIM, HEAD_DIM), F32)
    return pl.pallas_call(
        functools.partial(_sample_rec_kernel, seq_len=seq_len),
        grid=(n_seq // bs,),
        in_specs=[
            pl.BlockSpec((rows, IN_COLS), lambda i: (i, 0)),
            state_spec,
            state_spec,
            pl.BlockSpec(lb_logits.shape, const),
            pl.BlockSpec((1, GROUP_W), const),
            pl.BlockSpec((1, GROUP_W), const),
            pl.BlockSpec((1, GROUP_W), const),
            pl.BlockSpec((seq_len, HEAD_DIM), const),
            pl.BlockSpec((seq_len, HEAD_DIM), const),
        ],
        out_specs=[
            pl.BlockSpec((rows, 2 * GROUP_W), lambda i: (i, 0)),
            state_spec,
            state_spec,
        ],
        out_shape=[
            jax.ShapeDtypeStruct((n_tok, 2 * GROUP_W), BF16),
            state_shape,
            state_shape,
        ],
        scratch_shapes=[
            pltpu.VMEM((rows, GROUP_W), F32),
            pltpu.VMEM((rows, HEAD_DIM), F32),
        ],
        compiler_params=pltpu.CompilerParams(
            dimension_semantics=("arbitrary",), vmem_limit_bytes=V7X_VMEM_LIMIT_BYTES),
        name="sample_recurrence",
    )(proj, sa, sb, lb_logits, a_g, b_g, b_b, cos, sin)


_TAIL_PHASE_STEPS = (D_MODEL // DOWN_CHUNK, D_FF // FF_CHUNK, D_MODEL // DOWN_CHUNK, D_MODEL // DOWN_CHUNK)
_TAIL_PHASE_START = tuple(sum(_TAIL_PHASE_STEPS[:i]) for i in range(4))


def _sample_tail_kernel(x_ref, mix_ref, p_ref, wo_ref, wg_ref, wu_ref, wd_ref, wpg_ref, wpp_ref,
                        ln1g_ref, ln1b_ref, ln2g_ref, ln2b_ref, bpg_ref,
                        y_ref, wo_b, wg_b, wu_b, wd_b, wpg_b, wpp_b, act_ref, h_ref, hb_ref):
    s = pl.program_id(0)
    a0, b0, c0, d0 = _TAIL_PHASE_START
    blk = DOWN_CHUNK

    @pl.when(s < b0)
    def _():
        wo_b[...] = wo_ref[...].astype(BF16)
        cols = pl.ds(pl.multiple_of((s - a0) * blk, blk), blk)
        h_ref[:, cols] = DN_ALPHA * x_ref[...] + _dot(mix_ref[...], wo_b[...])

        @pl.when(s == b0 - 1)
        def _():
            h = _layer_norm(h_ref[...], ln1g_ref[...], ln1b_ref[...])
            h_ref[...] = h
            hb_ref[...] = h.astype(BF16)

    @pl.when((s >= b0) & (s < c0))
    def _():
        wg_b[...] = wg_ref[...].astype(BF16)
        wu_b[...] = wu_ref[...].astype(BF16)
        cols = pl.ds(pl.multiple_of((s - b0) * FF_CHUNK, FF_CHUNK), FF_CHUNK)
        hb = hb_ref[...]
        act_ref[:, cols] = (_silu(_dot(hb, wg_b[...])) * _dot(hb, wu_b[...])).astype(BF16)

    @pl.when((s >= c0) & (s < d0))
    def _():
        wd_b[...] = wd_ref[...].astype(BF16)
        cols = pl.ds(pl.multiple_of((s - c0) * blk, blk), blk)
        h_ref[:, cols] = DN_ALPHA * h_ref[:, cols] + _dot(act_ref[...], wd_b[...])

        @pl.when(s == d0 - 1)
        def _():
            h2 = _layer_norm(h_ref[...], ln2g_ref[...], ln2b_ref[...])
            h_ref[...] = h2
            hb_ref[...] = h2.astype(BF16)

    @pl.when(s >= d0)
    def _():
        wpg_b[...] = wpg_ref[...].astype(BF16)
        wpp_b[...] = wpp_ref[...].astype(BF16)
        cols = pl.ds(pl.multiple_of((s - d0) * blk, blk), blk)
        gate = _sigmoid(_dot(hb_ref[...], wpg_b[...]) + bpg_ref[:, cols])
        y_ref[...] = h_ref[:, cols] + gate * _dot(p_ref[...].astype(BF16), wpp_b[...])


def _sample_tail(x, mix, p, w_out, ln1g, ln1b, wg, wu, wd, ln2g, ln2b, wpp, wpg, bpg):
    n = x.shape[0]
    a0, b0, c0, d0 = _TAIL_PHASE_START
    na, nb, nc, nd = _TAIL_PHASE_STEPS
    const = lambda s: (0, 0)
    col = lambda start, count: (lambda s: (0, jnp.clip(s - start, 0, count - 1)))
    vec = pl.BlockSpec((1, D_MODEL), const)
    weight_specs = [
        pl.BlockSpec((2 * GROUP_W, DOWN_CHUNK), col(a0, na)),
        pl.BlockSpec((D_MODEL, FF_CHUNK), col(b0, nb)),
        pl.BlockSpec((D_MODEL, FF_CHUNK), col(b0, nb)),
        pl.BlockSpec((D_FF, DOWN_CHUNK), col(c0, nc)),
        pl.BlockSpec((D_MODEL, DOWN_CHUNK), col(d0, nd)),
        pl.BlockSpec((PLE_DIM, DOWN_CHUNK), col(d0, nd)),
    ]
    weights = (w_out, wg, wu, wd, wpg, wpp)
    return pl.pallas_call(
        _sample_tail_kernel,
        grid=(sum(_TAIL_PHASE_STEPS),),
        in_specs=[
            pl.BlockSpec((n, DOWN_CHUNK), col(a0, na)),
            pl.BlockSpec((n, 2 * GROUP_W), const),
            pl.BlockSpec((n, PLE_DIM), const),
        ] + weight_specs + [vec, vec, vec, vec, vec],
        out_specs=[pl.BlockSpec((n, DOWN_CHUNK), col(d0, nd))] + weight_specs,
        out_shape=[jax.ShapeDtypeStruct((n, D_MODEL), F32)]
        + [jax.ShapeDtypeStruct(w.shape, BF16) for w in weights],
        scratch_shapes=_tail_scratch(n),
        compiler_params=pltpu.CompilerParams(
            dimension_semantics=("arbitrary",), vmem_limit_bytes=V7X_VMEM_LIMIT_BYTES),
        name="sample_tail",
    )(x, mix, p, *weights, ln1g, ln1b, ln2g, ln2b, bpg)


def kernel(x_prompt, x_sample, p_prompt, p_sample, state_hgrn, state_ret, lb_logits, w_in, a_norm_g, b_norm_g, b_norm_b, w_out, ln1_g, ln1_b, w_ffn_gate, w_ffn_up, w_ffn_down, ln2_g, ln2_b, w_ple_proj, w_ple_gate, b_ple_gate):
    assert w_in.shape[0] == DEPTH == 1
    bsz, seq, _ = x_prompt.shape
    n_dec, dec_seq, _ = x_sample.shape

    mixer_vecs = (lb_logits, a_norm_g, b_norm_g, b_norm_b)
    cos_p, sin_p = _rope_tables(seq, 0)
    cos_s, sin_s = _rope_tables(dec_seq, PAST_LEN)

    x_s = x_sample.reshape(n_dec * dec_seq, D_MODEL)
    proj_s, w_in_b = _in_proj(x_s, w_in[0])
    mix_s, sa_s, sb_s = _sample_rec(proj_s, state_hgrn[0], state_ret[0], *mixer_vecs,
                                    cos_s, sin_s, dec_seq)
    y_s, w_out_b, wg_b, wu_b, wd_b, wpg_b, wpp_b = _sample_tail(
        x_s, mix_s, p_sample[0].reshape(n_dec * dec_seq, PLE_DIM), w_out[0], ln1_g, ln1_b,
        w_ffn_gate[0], w_ffn_up[0], w_ffn_down[0], ln2_g, ln2_b, w_ple_proj[0], w_ple_gate[0],
        b_ple_gate)

    tail_w = (w_out_b, ln1_g, ln1_b, wg_b, wu_b, wd_b, ln2_g, ln2_b, wpp_b, wpg_b, b_ple_gate)
    y_p, sa_p, sb_p = _prompt_layer(x_prompt, p_prompt[0], cos_p, sin_p, w_in_b, *mixer_vecs, tail_w)

    return (y_p.reshape(bsz, seq, D_MODEL), y_s.reshape(n_dec, dec_seq, D_MODEL),
            sa_p[None], sb_p[None], sa_s[None], sb_s[None])
```

```python
import functools
import math

import jax
import jax.numpy as jnp
from jax import lax
from jax.experimental import pallas as pl
from jax.experimental.pallas import tpu as pltpu

F32 = jnp.float32
BF16 = jnp.bfloat16

D_MODEL = 1024
N_HEADS = 4
HEAD_DIM = 128
GROUP_W = N_HEADS * HEAD_DIM
IN_COLS = 8 * GROUP_W
D_FF = 2816
PLE_DIM = 256
DEPTH = 1
PAST_LEN = 16384
REF_CHUNK = 32
ROPE_BASE = 10000.0
NORM_EPS = 1e-5
DN_ALPHA = (2.0 * DEPTH) ** 0.25
RET_LOG_DECAY = tuple(math.log1p(-(2.0 ** (-5.0 - h))) for h in range(N_HEADS))
K_SCALE = HEAD_DIM ** -0.5

V7X_VMEM_LIMIT_BYTES = 60 * 1024 * 1024

TOKEN_TILE = 256
SAMPLE_PROJ_COLS = 1024
SAMPLE_SEQS = 16
FF_CHUNK = 256
DOWN_CHUNK = 256


def _dot(a, b):
    return jnp.dot(a, b, preferred_element_type=F32)


def _dot_nt(a, b):
    return lax.dot_general(a, b, (((1,), (1,)), ((), ())), preferred_element_type=F32)


def _dot_tn(a, b):
    return lax.dot_general(a, b, (((0,), (0,)), ((), ())), preferred_element_type=F32)


def _split3(x):
    hi = x.astype(BF16)
    r1 = x - hi.astype(F32)
    mid = r1.astype(BF16)
    lo = (r1 - mid.astype(F32)).astype(BF16)
    return hi, mid, lo


def _dot_exact_lhs01(m01, parts):
    return _dot(jnp.concatenate([m01] * 3, axis=1), jnp.concatenate(list(parts), axis=0))


def _sigmoid(x):
    return 1.0 / (1.0 + jnp.exp(-x))


def _silu(x):
    return x * _sigmoid(x)


def _causal_in_chunk(n, shift):
    r = lax.broadcasted_iota(jnp.int32, (n, n), 0)
    c = lax.broadcasted_iota(jnp.int32, (n, n), 1)
    return ((r >> shift) == (c >> shift)) & (c <= r)


def _lower_bound(lb_ref):
    rows = [lb_ref[i:i + 1, :] for i in range(lb_ref.shape[0])]
    m = functools.reduce(jnp.maximum, rows)
    e = [jnp.exp(r - m) for r in rows]
    return e[0] / functools.reduce(jnp.add, e)


def _hgrn_prepass(proj_ref, lb, causal):
    tri = jnp.where(causal, 1.0, 0.0).astype(BF16)
    f = lb + (1.0 - lb) * _sigmoid(proj_ref[:, GROUP_W:2 * GROUP_W])
    kk = 1.0 - f
    b = _dot_exact_lhs01(tri, _split3(jnp.log(f)))
    q_dec = (_silu(proj_ref[:, 0:GROUP_W]) * jnp.exp(b)).astype(BF16)
    k_dec = kk * jnp.exp(-b)
    return q_dec, k_dec, kk, b


def _rope(x, cos, sin_signed):
    return x * cos + pltpu.roll(x, HEAD_DIM // 2, axis=1) * sin_signed


def _rms_gate(o, g, gate):
    return o * lax.rsqrt(jnp.mean(o * o, axis=-1, keepdims=True) + NORM_EPS) * g * _silu(gate)


def _layer_norm(x, g, b):
    mu = jnp.mean(x, axis=-1, keepdims=True)
    d = x - mu
    var = jnp.mean(d * d, axis=-1, keepdims=True)
    return d * lax.rsqrt(var + NORM_EPS) * g + b


def _ln_gate(o, g, b, gate):
    return _layer_norm(o, g, b) * _silu(gate)


def _head(h, group=0):
    return slice(group * GROUP_W + h * HEAD_DIM, group * GROUP_W + (h + 1) * HEAD_DIM)


def _rope_table_kernel(cos_ref, sin_ref, *, offset):
    n = cos_ref.shape[0]
    half = HEAD_DIM // 2
    packed = n % 16 == 0
    m = n // 2 if packed else n
    row = lax.broadcasted_iota(jnp.int32, (m, HEAD_DIM), 0) + pl.program_id(0) * n
    lane = lax.broadcasted_iota(jnp.int32, (m, HEAD_DIM), 1)
    low = lane < half
    if packed:
        row = row + jnp.where(low, 0, m)
    j = (lane & (half - 1)).astype(F32)
    inv = jnp.exp(-(j / half) * math.log(ROPE_BASE))
    ang = (row.astype(F32) + offset) * inv
    c = jnp.cos(ang)
    s = jnp.sin(ang)
    if not packed:
        cos_ref[...] = c
        sin_ref[...] = jnp.where(low, -s, s)
        return
    c_sw = pltpu.roll(c, half, axis=1)
    s_sw = pltpu.roll(s, half, axis=1)
    cos_ref[0:m, :] = jnp.where(low, c, c_sw)
    cos_ref[m:n, :] = jnp.where(low, c_sw, c)
    sin_ref[0:m, :] = jnp.where(low, -s, s_sw)
    sin_ref[m:n, :] = jnp.where(low, -s_sw, s)


def _rope_tables(n, offset):
    tile = min(n, 512)
    return pl.pallas_call(
        functools.partial(_rope_table_kernel, offset=float(offset)),
        grid=(n // tile,),
        in_specs=[],
        out_specs=[pl.BlockSpec((tile, HEAD_DIM), lambda i: (i, 0))] * 2,
        out_shape=[jax.ShapeDtypeStruct((n, HEAD_DIM), F32)] * 2,
        name="rope_tables",
    )()


def _tail_steps(x_ref, mix_ref, p_ref, y_ref, w_out_ref, ln1g_ref, ln1b_ref, wg_ref, wu_ref, wd_ref,
                ln2g_ref, ln2b_ref, wpp_ref, wpg_ref, bpg_ref, act_ref, h_ref, hb_ref):
    def out_proj():
        for c in range(D_MODEL // DOWN_CHUNK):
            cols = slice(c * DOWN_CHUNK, (c + 1) * DOWN_CHUNK)
            h_ref[:, cols] = DN_ALPHA * x_ref[:, cols] + _dot(mix_ref[...], w_out_ref[:, cols])
        h = _layer_norm(h_ref[...], ln1g_ref[...], ln1b_ref[...])
        h_ref[...] = h
        hb_ref[...] = h.astype(BF16)

    def ff(c):
        cols = slice(c * FF_CHUNK, (c + 1) * FF_CHUNK)
        hb = hb_ref[...]
        act_ref[:, cols] = (_silu(_dot(hb, wg_ref[:, cols])) * _dot(hb, wu_ref[:, cols])).astype(BF16)

    def down(c):
        cols = slice(c * DOWN_CHUNK, (c + 1) * DOWN_CHUNK)
        h_ref[:, cols] = DN_ALPHA * h_ref[:, cols] + _dot(act_ref[...], wd_ref[:, cols])

    def norm2():
        h2 = _layer_norm(h_ref[...], ln2g_ref[...], ln2b_ref[...])
        h_ref[...] = h2
        hb_ref[...] = h2.astype(BF16)

    def ple(c):
        cols = slice(c * DOWN_CHUNK, (c + 1) * DOWN_CHUNK)
        gate = _sigmoid(_dot(hb_ref[...], wpg_ref[:, cols]) + bpg_ref[:, cols])
        y_ref[:, cols] = h_ref[:, cols] + gate * _dot(p_ref[...].astype(BF16), wpp_ref[:, cols])

    return (out_proj,
            [functools.partial(ff, c) for c in range(D_FF // FF_CHUNK)],
            [functools.partial(down, c) for c in range(D_MODEL // DOWN_CHUNK)],
            [norm2] + [functools.partial(ple, c) for c in range(D_MODEL // DOWN_CHUNK)])


def _tail_scratch(tl):
    return [pltpu.VMEM((tl, D_FF), BF16), pltpu.VMEM((tl, D_MODEL), F32), pltpu.VMEM((tl, D_MODEL), BF16)]


def _tail_specs():
    const = lambda i: (0, 0)
    resident = lambda shape: pl.BlockSpec(shape, const, pipeline_mode=pl.Buffered(1))
    vec = pl.BlockSpec((1, D_MODEL), const)
    return [
        resident((2 * GROUP_W, D_MODEL)),
        vec, vec,
        resident((D_MODEL, D_FF)),
        resident((D_MODEL, D_FF)),
        resident((D_FF, D_MODEL)),
        vec, vec,
        resident((PLE_DIM, D_MODEL)),
        resident((D_MODEL, D_MODEL)),
        vec,
    ]


def _in_proj_steps(x_ref, w_in_ref, proj_ref, xb_ref):
    def in_proj(c):
        if c == 0:
            xb_ref[...] = x_ref[...].astype(BF16)
        cols = slice(c * GROUP_W, (c + 1) * GROUP_W)
        proj_ref[:, cols] = _dot(xb_ref[...], w_in_ref[:, cols])

    return [functools.partial(in_proj, c) for c in range(IN_COLS // GROUP_W)]


def _ret_tables(dm_ref, rd_ref):
    tl = dm_ref.shape[1]
    r = lax.broadcasted_iota(jnp.int32, (tl, tl), 0)
    c = lax.broadcasted_iota(jnp.int32, (tl, tl), 1)
    row = lax.broadcasted_iota(jnp.int32, (tl, HEAD_DIM), 0).astype(F32)
    for h in range(N_HEADS):
        logd = RET_LOG_DECAY[h]
        dm_ref[h] = jnp.where(r >= c, jnp.exp((r - c).astype(F32) * logd), 0.0)
        rd_ref[0, h] = jnp.exp((row + 1.0) * logd)
        rd_ref[1, h] = jnp.exp((tl - 1.0 - row) * logd)


def _prompt_mixer_steps(cos_ref, sin_ref, lb_ref, ag_ref, bg_ref, bb_ref,
                        sa_ref, sb_ref, proj_ref, st_ref, oa_ref, mix_ref,
                        qd_ref, kd_ref, ke_ref, dec_ref, va_ref, kv_ref, sbf_ref, dm_ref, rd_ref, first):
    tl = proj_ref.shape[0]
    shift = REF_CHUNK.bit_length() - 1
    n_chunks = tl // REF_CHUNK

    def prepass():
        q_dec, k_dec, kk, b = _hgrn_prepass(proj_ref, _lower_bound(lb_ref), _causal_in_chunk(tl, shift))
        qd_ref[...] = q_dec
        for h in range(N_HEADS):
            kd_ref[h] = k_dec[:, _head(h)].T.astype(BF16)
        va_ref[...] = proj_ref[:, 2 * GROUP_W:3 * GROUP_W].astype(BF16)
        last = [b[(n + 1) * REF_CHUNK - 1:(n + 1) * REF_CHUNK, :] for n in range(n_chunks)]
        b_last = jnp.concatenate([jnp.broadcast_to(r, (REF_CHUNK, GROUP_W)) for r in last], axis=0)
        k_end = kk * jnp.exp(b_last - b)
        for n in range(n_chunks):
            dec_ref[n:n + 1, :] = jnp.exp(last[n])
        odd = ((lax.broadcasted_iota(jnp.int32, (tl, GROUP_W), 0) >> shift) & 1) == 1
        k_even = jnp.where(odd, 0.0, k_end).astype(BF16)
        k_odd = jnp.where(odd, k_end, 0.0).astype(BF16)
        for h in range(N_HEADS):
            ke_ref[:, 2 * h * HEAD_DIM:(2 * h + 1) * HEAD_DIM] = k_even[:, _head(h)]
            ke_ref[:, (2 * h + 1) * HEAD_DIM:(2 * h + 2) * HEAD_DIM] = k_odd[:, _head(h)]

    def kv_scan(h):
        hs = _head(h)
        pair = 2 * REF_CHUNK
        for r in range(n_chunks // 2):
            rows = slice(r * pair, (r + 1) * pair)
            kv_ref[r] = _dot_tn(va_ref[rows, hs], ke_ref[rows, 2 * h * HEAD_DIM:(2 * h + 2) * HEAD_DIM])
        st = jnp.where(first, 0.0, st_ref[h])
        for n in range(n_chunks):
            sbf_ref[n, h] = st.T.astype(BF16)
            st = st * dec_ref[n:n + 1, hs] + kv_ref[n // 2, :, (n % 2) * HEAD_DIM:(n % 2 + 1) * HEAD_DIM]
        st_ref[h] = st
        sa_ref[0, h] = st.T

    def diag(h):
        hs = _head(h)
        sc = jnp.where(_causal_in_chunk(tl, shift), _dot(qd_ref[:, hs], kd_ref[h]), 0.0)
        oa_ref[:, hs] = _dot(sc.astype(BF16), va_ref[:, hs])

    def inter(n):
        rows = slice(n * REF_CHUNK, (n + 1) * REF_CHUNK)
        for h in range(N_HEADS):
            hs = _head(h)
            oa_ref[rows, hs] += _dot(qd_ref[rows, hs], sbf_ref[n, h])

    def hgrn_out():
        for h in range(N_HEADS):
            hs = _head(h)
            mix_ref[:, hs] = _rms_gate(oa_ref[:, hs], ag_ref[:, hs], proj_ref[:, _head(h, 3)]).astype(BF16)

    def ret(h):
        hs = _head(h)
        cos = cos_ref[...]
        sin = sin_ref[...]
        q = _rope(proj_ref[:, _head(h, 4)], cos, sin)
        k = _rope(proj_ref[:, _head(h, 5)], cos, sin) * K_SCALE
        v = proj_ref[:, _head(h, 6)].astype(BF16)
        a = (_dot(q.astype(BF16), k.T.astype(BF16)) * dm_ref[h]).astype(BF16)
        s = jnp.where(first, 0.0, sb_ref[0, h])
        o = _dot(a, v) + _dot((q * rd_ref[0, h]).astype(BF16), s.astype(BF16))
        k_end_b = (k * rd_ref[1, h]).astype(BF16)
        sb_ref[0, h] = s * math.exp(tl * RET_LOG_DECAY[h]) + _dot_tn(k_end_b, v)
        mix_ref[:, _head(h, 1)] = _ln_gate(o, bg_ref[:, hs], bb_ref[:, hs],
                                           proj_ref[:, _head(h, 7)]).astype(BF16)

    return (prepass,
            [functools.partial(kv_scan, h) for h in range(N_HEADS)],
            [functools.partial(diag, h) for h in range(N_HEADS)],
            [functools.partial(inter, n) for n in range(n_chunks)],
            hgrn_out,
            [functools.partial(ret, h) for h in range(N_HEADS)])


def _interleave(a, b):
    out = []
    for i in range(max(len(a), len(b))):
        out += a[i:i + 1] + b[i:i + 1]
    return out


def _prompt_layer_kernel(xn_ref, xp_ref, p_ref, cos_ref, sin_ref, w_in_ref, lb_ref, ag_ref, bg_ref,
                         bb_ref, *rest, tiles_per_seq):
    tail_w = rest[:11]
    y_ref, sa_ref, sb_ref = rest[11:14]
    mixer_scratch = rest[14:27]
    proj_ref, mix_ref, dm_ref, rd_ref = mixer_scratch[0], mixer_scratch[3], mixer_scratch[11], mixer_scratch[12]
    xb_ref = rest[27]
    tail_scratch = rest[28:]
    g = pl.program_id(0)
    n_tiles = pl.num_programs(0) - 1
    slot = lax.rem(g, 2)

    def mixer_steps():
        return _prompt_mixer_steps(cos_ref, sin_ref, lb_ref, ag_ref, bg_ref, bb_ref, sa_ref, sb_ref,
                                   proj_ref.at[slot], *mixer_scratch[1:],
                                   first=lax.rem(g, tiles_per_seq) == 0)

    def tail_steps():
        return _tail_steps(xp_ref, mix_ref, p_ref, y_ref, *tail_w, *tail_scratch)

    @pl.when(g == 0)
    def _():
        _ret_tables(dm_ref, rd_ref)
        prepass, kv_scan, diag, inter, hgrn_out, ret = mixer_steps()
        steps = _in_proj_steps(xp_ref, w_in_ref, proj_ref.at[slot], xb_ref)
        steps += [prepass] + kv_scan + diag + inter + [hgrn_out] + ret
        steps += _in_proj_steps(xn_ref, w_in_ref, proj_ref.at[1 - slot], xb_ref)
        for step in steps:
            step()

    @pl.when((g > 0) & (g < n_tiles))
    def _():
        prepass, kv_scan, diag, inter, hgrn_out, ret = mixer_steps()
        out_proj, ff, down, final = tail_steps()
        in_proj = _in_proj_steps(xn_ref, w_in_ref, proj_ref.at[1 - slot], xb_ref)
        inter_pairs = [lambda a=a, b=b: (a(), b()) for a, b in zip(inter[0::2], inter[1::2])]
        steps = [out_proj, in_proj[0], prepass, in_proj[1]]
        steps += _interleave(ff + down, kv_scan + diag + ret + inter_pairs + [hgrn_out])
        steps += _interleave(final, in_proj[2:])
        for step in steps:
            step()

    @pl.when(g == n_tiles)
    def _():
        out_proj, ff, down, final = tail_steps()
        for step in [out_proj] + ff + down + final:
            step()


def _prompt_layer(x, p, cos, sin, w_in, lb_logits, a_g, b_g, b_b, tail_w):
    bsz, seq, _ = x.shape
    tl = TOKEN_TILE
    tps = seq // tl
    n_tiles = bsz * tps
    x2 = x.reshape(bsz * seq, D_MODEL)
    p2 = p.reshape(bsz * seq, PLE_DIM)
    const = lambda g: (0, 0)
    nxt = lambda g: (jnp.minimum(g + 1, n_tiles - 1), 0)
    prev = lambda g: (jnp.maximum(g - 1, 0), 0)
    seq_tile = lambda g: (lax.rem(jnp.minimum(g, n_tiles - 1), tps), 0)
    state_spec = pl.BlockSpec((1, N_HEADS, HEAD_DIM, HEAD_DIM),
                              lambda g: (jnp.minimum(g, n_tiles - 1) // tps, 0, 0, 0))
    state_shape = jax.ShapeDtypeStruct((bsz, N_HEADS, HEAD_DIM, HEAD_DIM), F32)
    return pl.pallas_call(
        functools.partial(_prompt_layer_kernel, tiles_per_seq=tps),
        grid=(n_tiles + 1,),
        in_specs=[
            pl.BlockSpec((tl, D_MODEL), nxt),
            pl.BlockSpec((tl, D_MODEL), prev),
            pl.BlockSpec((tl, PLE_DIM), prev),
            pl.BlockSpec((tl, HEAD_DIM), seq_tile),
            pl.BlockSpec((tl, HEAD_DIM), seq_tile),
            pl.BlockSpec((D_MODEL, IN_COLS), const, pipeline_mode=pl.Buffered(1)),
            pl.BlockSpec(lb_logits.shape, const),
            pl.BlockSpec((1, GROUP_W), const),
            pl.BlockSpec((1, GROUP_W), const),
            pl.BlockSpec((1, GROUP_W), const),
        ] + _tail_specs(),
        out_specs=[
            pl.BlockSpec((tl, D_MODEL), prev),
            state_spec,
            state_spec,
        ],
        out_shape=[
            jax.ShapeDtypeStruct((bsz * seq, D_MODEL), F32),
            state_shape,
            state_shape,
        ],
        scratch_shapes=[
            pltpu.VMEM((2, tl, IN_COLS), F32),
            pltpu.VMEM((N_HEADS, HEAD_DIM, HEAD_DIM), F32),
            pltpu.VMEM((tl, GROUP_W), F32),
            pltpu.VMEM((tl, 2 * GROUP_W), BF16),
            pltpu.VMEM((tl, GROUP_W), BF16),
            pltpu.VMEM((N_HEADS, HEAD_DIM, tl), BF16),
            pltpu.VMEM((tl, 2 * GROUP_W), BF16),
            pltpu.VMEM((tl // REF_CHUNK, GROUP_W), F32),
            pltpu.VMEM((tl, GROUP_W), BF16),
            pltpu.VMEM((tl // (2 * REF_CHUNK), HEAD_DIM, 2 * HEAD_DIM), F32),
            pltpu.VMEM((tl // REF_CHUNK, N_HEADS, HEAD_DIM, HEAD_DIM), BF16),
            pltpu.VMEM((N_HEADS, tl, tl), F32),
            pltpu.VMEM((2, N_HEADS, tl, HEAD_DIM), F32),
            pltpu.VMEM((tl, D_MODEL), BF16),
        ] + _tail_scratch(tl),
        compiler_params=pltpu.CompilerParams(
            dimension_semantics=("arbitrary",), vmem_limit_bytes=V7X_VMEM_LIMIT_BYTES),
        name="prompt_layer",
    )(x2, x2, p2, cos, sin, w_in, lb_logits, a_g, b_g, b_b, *tail_w)


def _in_proj_kernel(x_ref, w_ref, o_ref, wb_ref):
    wb_ref[...] = w_ref[...].astype(BF16)
    o_ref[...] = _dot(x_ref[...].astype(BF16), wb_ref[...])


def _in_proj(x, w_in):
    n = x.shape[0]
    tn = SAMPLE_PROJ_COLS
    return pl.pallas_call(
        _in_proj_kernel,
        grid=(IN_COLS // tn,),
        in_specs=[
            pl.BlockSpec((n, D_MODEL), lambda c: (0, 0)),
            pl.BlockSpec((D_MODEL, tn), lambda c: (0, c)),
        ],
        out_specs=[
            pl.BlockSpec((n, tn), lambda c: (0, c)),
            pl.BlockSpec((D_MODEL, tn), lambda c: (0, c)),
        ],
        out_shape=[
            jax.ShapeDtypeStruct((n, IN_COLS), F32),
            jax.ShapeDtypeStruct((D_MODEL, IN_COLS), BF16),
        ],
        compiler_params=pltpu.CompilerParams(
            dimension_semantics=("arbitrary",), vmem_limit_bytes=V7X_VMEM_LIMIT_BYTES),
        name="sample_in_proj",
    )(x, w_in)


def _sample_rec_kernel(proj_ref, sa_in_ref, sb_in_ref, lb_ref, ag_ref, bg_ref, bb_ref,
                       cos_ref, sin_ref, mix_ref, sa_ref, sb_ref, oa_ref, ob_ref, *, seq_len):
    rows_n = proj_ref.shape[0]
    n_seq = rows_n // seq_len
    causal = _causal_in_chunk(rows_n, seq_len.bit_length() - 1)

    q_dec, k_dec, kk, b = _hgrn_prepass(proj_ref, _lower_bound(lb_ref), causal)
    v_a = proj_ref[:, 2 * GROUP_W:3 * GROUP_W]
    for h in range(N_HEADS):
        hs = _head(h)
        sc = jnp.where(causal, _dot_nt(q_dec[:, hs], k_dec[:, hs].astype(BF16)), 0.0).astype(BF16)
        oa_ref[:, hs] = _dot(sc, v_a[:, hs].astype(BF16))
    q_dec32 = q_dec.astype(F32)
    rr = lax.broadcasted_iota(jnp.int32, (seq_len, GROUP_W), 0)
    ones_blk = jnp.ones((seq_len, HEAD_DIM), BF16)
    for s in range(n_seq):
        rows = slice(s * seq_len, (s + 1) * seq_len)
        b_last = b[(s + 1) * seq_len - 1:(s + 1) * seq_len, :]
        k_end = (kk[rows] * jnp.exp(b_last - b[rows])).astype(BF16)
        hi, mid, lo = [t.astype(F32) for t in _split3(jnp.exp(b_last))]
        dec_rows = jnp.where(rr == 0, hi, jnp.where(rr == 1, mid, jnp.where(rr == 2, lo, 0.0)))
        dec_rows = dec_rows.astype(BF16)
        for h in range(N_HEADS):
            hs = _head(h)
            st = sa_in_ref[s, h]
            oa_ref[rows, hs] += _dot(q_dec32[rows, hs].astype(BF16), st.astype(BF16))
            dec_kv = _dot_tn(dec_rows[:, hs], ones_blk)
            sa_ref[s, h] = st * dec_kv + _dot_tn(k_end[:, hs], v_a[rows, hs].astype(BF16))
    for h in range(N_HEADS):
        hs = _head(h)
        mix_ref[:, hs] = _rms_gate(oa_ref[:, hs], ag_ref[:, hs], proj_ref[:, _head(h, 3)]).astype(BF16)

    cos = jnp.concatenate([cos_ref[...]] * n_seq, axis=0)
    sin = jnp.concatenate([sin_ref[...]] * n_seq, axis=0)
    r = lax.broadcasted_iota(jnp.int32, (rows_n, rows_n), 0)
    c = lax.broadcasted_iota(jnp.int32, (rows_n, rows_n), 1)
    diff = ((r & (seq_len - 1)) - (c & (seq_len - 1))).astype(F32)
    row = (lax.broadcasted_iota(jnp.int32, (rows_n, HEAD_DIM), 0) & (seq_len - 1)).astype(F32)
    for h in range(N_HEADS):
        hs = _head(h)
        logd = RET_LOG_DECAY[h]
        q = _rope(proj_ref[:, _head(h, 4)], cos, sin)
        k = _rope(proj_ref[:, _head(h, 5)], cos, sin) * K_SCALE
        v32 = proj_ref[:, _head(h, 6)]
        dmask = jnp.where(causal, jnp.exp(diff * logd), 0.0)
        a = (_dot_nt(q.astype(BF16), k.astype(BF16)) * dmask).astype(BF16)
        ob_ref[...] = _dot(a, v32.astype(BF16))
        q_dec_b = q * jnp.exp((row + 1.0) * logd)
        k_end_b = k * jnp.exp((seq_len - 1.0 - row) * logd)
        for s in range(n_seq):
            rows = slice(s * seq_len, (s + 1) * seq_len)
            st = sb_in_ref[s, h]
            ob_ref[rows, :] += _dot(q_dec_b[rows].astype(BF16), st.astype(BF16))
            sb_ref[s, h] = st * math.exp(seq_len * logd) + _dot_tn(
                k_end_b[rows].astype(BF16), v32[rows].astype(BF16))
        mix_ref[:, _head(h, 1)] = _ln_gate(ob_ref[...], bg_ref[:, hs], bb_ref[:, hs],
                                           proj_ref[:, _head(h, 7)]).astype(BF16)


def _sample_rec(proj, sa, sb, lb_logits, a_g, b_g, b_b, cos, sin, seq_len):
    n_tok = proj.shape[0]
    n_seq = n_tok // seq_len
    bs = SAMPLE_SEQS
    rows = bs * seq_len
    const = lambda i: (0, 0)
    state_spec = pl.BlockSpec((bs, N_HEADS, HEAD_DIM, HEAD_DIM), lambda i: (i, 0, 0, 0))
    state_shape = jax.ShapeDtypeStruct((n_seq, N_HEADS, HEAD_DIM, HEAD_DIM), F32)
    return pl.pallas_call(
        functools.partial(_sample_rec_kernel, seq_len=seq_len),
        grid=(n_seq // bs,),
        in_specs=[
            pl.BlockSpec((rows, IN_COLS), lambda i: (i, 0)),
            state_spec,
            state_spec,
            pl.BlockSpec(lb_logits.shape, const),
            pl.BlockSpec((1, GROUP_W), const),
            pl.BlockSpec((1, GROUP_W), const),
            pl.BlockSpec((1, GROUP_W), const),
            pl.BlockSpec((seq_len, HEAD_DIM), const),
            pl.BlockSpec((seq_len, HEAD_DIM), const),
        ],
        out_specs=[
            pl.BlockSpec((rows, 2 * GROUP_W), lambda i: (i, 0)),
            state_spec,
            state_spec,
        ],
        out_shape=[
            jax.ShapeDtypeStruct((n_tok, 2 * GROUP_W), BF16),
            state_shape,
            state_shape,
        ],
        scratch_shapes=[
            pltpu.VMEM((rows, GROUP_W), F32),
            pltpu.VMEM((rows, HEAD_DIM), F32),
        ],
        compiler_params=pltpu.CompilerParams(
            dimension_semantics=("arbitrary",), vmem_limit_bytes=V7X_VMEM_LIMIT_BYTES),
        name="sample_recurrence",
    )(proj, sa, sb, lb_logits, a_g, b_g, b_b, cos, sin)


_TAIL_PHASE_STEPS = (D_MODEL // DOWN_CHUNK, D_FF // FF_CHUNK, D_MODEL // DOWN_CHUNK, D_MODEL // DOWN_CHUNK)
_TAIL_PHASE_START = tuple(sum(_TAIL_PHASE_STEPS[:i]) for i in range(4))


def _sample_tail_kernel(x_ref, mix_ref, p_ref, wo_ref, wg_ref, wu_ref, wd_ref, wpg_ref, wpp_ref,
                        ln1g_ref, ln1b_ref, ln2g_ref, ln2b_ref, bpg_ref,
                        y_ref, wo_b, wg_b, wu_b, wd_b, wpg_b, wpp_b, act_ref, h_ref, hb_ref):
    s = pl.program_id(0)
    a0, b0, c0, d0 = _TAIL_PHASE_START
    blk = DOWN_CHUNK

    @pl.when(s < b0)
    def _():
        wo_b[...] = wo_ref[...].astype(BF16)
        cols = pl.ds(pl.multiple_of((s - a0) * blk, blk), blk)
        h_ref[:, cols] = DN_ALPHA * x_ref[...] + _dot(mix_ref[...], wo_b[...])

        @pl.when(s == b0 - 1)
        def _():
            h = _layer_norm(h_ref[...], ln1g_ref[...], ln1b_ref[...])
            h_ref[...] = h
            hb_ref[...] = h.astype(BF16)

    @pl.when((s >= b0) & (s < c0))
    def _():
        wg_b[...] = wg_ref[...].astype(BF16)
        wu_b[...] = wu_ref[...].astype(BF16)
        cols = pl.ds(pl.multiple_of((s - b0) * FF_CHUNK, FF_CHUNK), FF_CHUNK)
        hb = hb_ref[...]
        act_ref[:, cols] = (_silu(_dot(hb, wg_b[...])) * _dot(hb, wu_b[...])).astype(BF16)

    @pl.when((s >= c0) & (s < d0))
    def _():
        wd_b[...] = wd_ref[...].astype(BF16)
        cols = pl.ds(pl.multiple_of((s - c0) * blk, blk), blk)
        h_ref[:, cols] = DN_ALPHA * h_ref[:, cols] + _dot(act_ref[...], wd_b[...])

        @pl.when(s == d0 - 1)
        def _():
            h2 = _layer_norm(h_ref[...], ln2g_ref[...], ln2b_ref[...])
            h_ref[...] = h2
            hb_ref[...] = h2.astype(BF16)

    @pl.when(s >= d0)
    def _():
        wpg_b[...] = wpg_ref[...].astype(BF16)
        wpp_b[...] = wpp_ref[...].astype(BF16)
        cols = pl.ds(pl.multiple_of((s - d0) * blk, blk), blk)
        gate = _sigmoid(_dot(hb_ref[...], wpg_b[...]) + bpg_ref[:, cols])
        y_ref[...] = h_ref[:, cols] + gate * _dot(p_ref[...].astype(BF16), wpp_b[...])


def _sample_tail(x, mix, p, w_out, ln1g, ln1b, wg, wu, wd, ln2g, ln2b, wpp, wpg, bpg):
    n = x.shape[0]
    a0, b0, c0, d0 = _TAIL_PHASE_START
    na, nb, nc, nd = _TAIL_PHASE_STEPS
    const = lambda s: (0, 0)
    col = lambda start, count: (lambda s: (0, jnp.clip(s - start, 0, count - 1)))
    vec = pl.BlockSpec((1, D_MODEL), const)
    weight_specs = [
        pl.BlockSpec((2 * GROUP_W, DOWN_CHUNK), col(a0, na)),
        pl.BlockSpec((D_MODEL, FF_CHUNK), col(b0, nb)),
        pl.BlockSpec((D_MODEL, FF_CHUNK), col(b0, nb)),
        pl.BlockSpec((D_FF, DOWN_CHUNK), col(c0, nc)),
        pl.BlockSpec((D_MODEL, DOWN_CHUNK), col(d0, nd)),
        pl.BlockSpec((PLE_DIM, DOWN_CHUNK), col(d0, nd)),
    ]
    weights = (w_out, wg, wu, wd, wpg, wpp)
    return pl.pallas_call(
        _sample_tail_kernel,
        grid=(sum(_TAIL_PHASE_STEPS),),
        in_specs=[
            pl.BlockSpec((n, DOWN_CHUNK), col(a0, na)),
            pl.BlockSpec((n, 2 * GROUP_W), const),
            pl.BlockSpec((n, PLE_DIM), const),
        ] + weight_specs + [vec, vec, vec, vec, vec],
        out_specs=[pl.BlockSpec((n, DOWN_CHUNK), col(d0, nd))] + weight_specs,
        out_shape=[jax.ShapeDtypeStruct((n, D_MODEL), F32)]
        + [jax.ShapeDtypeStruct(w.shape, BF16) for w in weights],
        scratch_shapes=_tail_scratch(n),
        compiler_params=pltpu.CompilerParams(
            dimension_semantics=("arbitrary",), vmem_limit_bytes=V7X_VMEM_LIMIT_BYTES),
        name="sample_tail",
    )(x, mix, p, *weights, ln1g, ln1b, ln2g, ln2b, bpg)


def kernel(x_prompt, x_sample, p_prompt, p_sample, state_hgrn, state_ret, lb_logits, w_in, a_norm_g, b_norm_g, b_norm_b, w_out, ln1_g, ln1_b, w_ffn_gate, w_ffn_up, w_ffn_down, ln2_g, ln2_b, w_ple_proj, w_ple_gate, b_ple_gate):
    assert w_in.shape[0] == DEPTH == 1
    bsz, seq, _ = x_prompt.shape
    n_dec, dec_seq, _ = x_sample.shape

    mixer_vecs = (lb_logits, a_norm_g, b_norm_g, b_norm_b)
    cos_p, sin_p = _rope_tables(seq, 0)
    cos_s, sin_s = _rope_tables(dec_seq, PAST_LEN)

    x_s = x_sample.reshape(n_dec * dec_seq, D_MODEL)
    proj_s, w_in_b = _in_proj(x_s, w_in[0])
    mix_s, sa_s, sb_s = _sample_rec(proj_s, state_hgrn[0], state_ret[0], *mixer_vecs,
                                    cos_s, sin_s, dec_seq)
    y_s, w_out_b, wg_b, wu_b, wd_b, wpg_b, wpp_b = _sample_tail(
        x_s, mix_s, p_sample[0].reshape(n_dec * dec_seq, PLE_DIM), w_out[0], ln1_g, ln1_b,
        w_ffn_gate[0], w_ffn_up[0], w_ffn_down[0], ln2_g, ln2_b, w_ple_proj[0], w_ple_gate[0],
        b_ple_gate)

    tail_w = (w_out_b, ln1_g, ln1_b, wg_b, wu_b, wd_b, ln2_g, ln2_b, wpp_b, wpg_b, b_ple_gate)
    y_p, sa_p, sb_p = _prompt_layer(x_prompt, p_prompt[0], cos_p, sin_p, w_in_b, *mixer_vecs, tail_w)

    return (y_p.reshape(bsz, seq, D_MODEL), y_s.reshape(n_dec, dec_seq, D_MODEL),
            sa_p[None], sb_p[None], sa_s[None], sb_s[None])
```

```python
import functools
import math

import jax
import jax.numpy as jnp
from jax import lax
from jax.experimental import pallas as pl
from jax.experimental.pallas import tpu as pltpu

F32 = jnp.float32
BF16 = jnp.bfloat16

D_MODEL = 1024
N_HEADS = 4
HEAD_DIM = 128
GROUP_W = N_HEADS * HEAD_DIM
IN_COLS = 8 * GROUP_W
D_FF = 2816
PLE_DIM = 256
DEPTH = 1
PAST_LEN = 16384
REF_CHUNK = 32
ROPE_BASE = 10000.0
NORM_EPS = 1e-5
DN_ALPHA = (2.0 * DEPTH) ** 0.25
RET_LOG_DECAY = tuple(math.log1p(-(2.0 ** (-5.0 - h))) for h in range(N_HEADS))
K_SCALE = HEAD_DIM ** -0.5

V7X_VMEM_LIMIT_BYTES = 60 * 1024 * 1024

TOKEN_TILE = 256
SAMPLE_PROJ_COLS = 1024
SAMPLE_SEQS = 16
FF_CHUNK = 256
DOWN_CHUNK = 256


def _dot(a, b):
    return jnp.dot(a, b, preferred_element_type=F32)


def _dot_nt(a, b):
    return lax.dot_general(a, b, (((1,), (1,)), ((), ())), preferred_element_type=F32)


def _dot_tn(a, b):
    return lax.dot_general(a, b, (((0,), (0,)), ((), ())), preferred_element_type=F32)


def _split3(x):
    hi = x.astype(BF16)
    r1 = x - hi.astype(F32)
    mid = r1.astype(BF16)
    lo = (r1 - mid.astype(F32)).astype(BF16)
    return hi, mid, lo


def _dot_exact_lhs01(m01, parts):
    return _dot(jnp.concatenate([m01] * 3, axis=1), jnp.concatenate(list(parts), axis=0))


def _sigmoid(x):
    return 1.0 / (1.0 + jnp.exp(-x))


def _silu(x):
    return x * _sigmoid(x)


def _causal_in_chunk(n, shift):
    r = lax.broadcasted_iota(jnp.int32, (n, n), 0)
    c = lax.broadcasted_iota(jnp.int32, (n, n), 1)
    return ((r >> shift) == (c >> shift)) & (c <= r)


def _lower_bound(lb_ref):
    rows = [lb_ref[i:i + 1, :] for i in range(lb_ref.shape[0])]
    m = functools.reduce(jnp.maximum, rows)
    e = [jnp.exp(r - m) for r in rows]
    return e[0] / functools.reduce(jnp.add, e)


def _hgrn_prepass(proj_ref, lb, causal):
    tri = jnp.where(causal, 1.0, 0.0).astype(BF16)
    f = lb + (1.0 - lb) * _sigmoid(proj_ref[:, GROUP_W:2 * GROUP_W])
    kk = 1.0 - f
    b = _dot_exact_lhs01(tri, _split3(jnp.log(f)))
    q_dec = (_silu(proj_ref[:, 0:GROUP_W]) * jnp.exp(b)).astype(BF16)
    k_dec = kk * jnp.exp(-b)
    return q_dec, k_dec, kk, b


def _rope(x, cos, sin_signed):
    return x * cos + pltpu.roll(x, HEAD_DIM // 2, axis=1) * sin_signed


def _rms_gate(o, g, gate):
    return o * lax.rsqrt(jnp.mean(o * o, axis=-1, keepdims=True) + NORM_EPS) * g * _silu(gate)


def _layer_norm(x, g, b):
    mu = jnp.mean(x, axis=-1, keepdims=True)
    d = x - mu
    var = jnp.mean(d * d, axis=-1, keepdims=True)
    return d * lax.rsqrt(var + NORM_EPS) * g + b


def _ln_gate(o, g, b, gate):
    return _layer_norm(o, g, b) * _silu(gate)


def _head(h, group=0):
    return slice(group * GROUP_W + h * HEAD_DIM, group * GROUP_W + (h + 1) * HEAD_DIM)


def _rope_table_kernel(cos_ref, sin_ref, *, offset):
    n = cos_ref.shape[0]
    half = HEAD_DIM // 2
    packed = n % 16 == 0
    m = n // 2 if packed else n
    row = lax.broadcasted_iota(jnp.int32, (m, HEAD_DIM), 0) + pl.program_id(0) * n
    lane = lax.broadcasted_iota(jnp.int32, (m, HEAD_DIM), 1)
    low = lane < half
    if packed:
        row = row + jnp.where(low, 0, m)
    j = (lane & (half - 1)).astype(F32)
    inv = jnp.exp(-(j / half) * math.log(ROPE_BASE))
    ang = (row.astype(F32) + offset) * inv
    c = jnp.cos(ang)
    s = jnp.sin(ang)
    if not packed:
        cos_ref[...] = c
        sin_ref[...] = jnp.where(low, -s, s)
        return
    c_sw = pltpu.roll(c, half, axis=1)
    s_sw = pltpu.roll(s, half, axis=1)
    cos_ref[0:m, :] = jnp.where(low, c, c_sw)
    cos_ref[m:n, :] = jnp.where(low, c_sw, c)
    sin_ref[0:m, :] = jnp.where(low, -s, s_sw)
    sin_ref[m:n, :] = jnp.where(low, -s_sw, s)


def _rope_tables(n, offset):
    tile = min(n, 512)
    return pl.pallas_call(
        functools.partial(_rope_table_kernel, offset=float(offset)),
        grid=(n // tile,),
        in_specs=[],
        out_specs=[pl.BlockSpec((tile, HEAD_DIM), lambda i: (i, 0))] * 2,
        out_shape=[jax.ShapeDtypeStruct((n, HEAD_DIM), F32)] * 2,
        name="rope_tables",
    )()


def _tail_steps(x_ref, mix_ref, p_ref, y_ref, w_out_ref, ln1g_ref, ln1b_ref, wg_ref, wu_ref, wd_ref,
                ln2g_ref, ln2b_ref, wpp_ref, wpg_ref, bpg_ref, act_ref, h_ref, hb_ref):
    def out_proj():
        for c in range(D_MODEL // DOWN_CHUNK):
            cols = slice(c * DOWN_CHUNK, (c + 1) * DOWN_CHUNK)
            h_ref[:, cols] = DN_ALPHA * x_ref[:, cols] + _dot(mix_ref[...], w_out_ref[:, cols])
        h = _layer_norm(h_ref[...], ln1g_ref[...], ln1b_ref[...])
        h_ref[...] = h
        hb_ref[...] = h.astype(BF16)

    def ff(c):
        cols = slice(c * FF_CHUNK, (c + 1) * FF_CHUNK)
        hb = hb_ref[...]
        act_ref[:, cols] = (_silu(_dot(hb, wg_ref[:, cols])) * _dot(hb, wu_ref[:, cols])).astype(BF16)

    def down(c):
        cols = slice(c * DOWN_CHUNK, (c + 1) * DOWN_CHUNK)
        h_ref[:, cols] = DN_ALPHA * h_ref[:, cols] + _dot(act_ref[...], wd_ref[:, cols])

    def norm2():
        h2 = _layer_norm(h_ref[...], ln2g_ref[...], ln2b_ref[...])
        h_ref[...] = h2
        hb_ref[...] = h2.astype(BF16)

    def ple(c):
        cols = slice(c * DOWN_CHUNK, (c + 1) * DOWN_CHUNK)
        gate = _sigmoid(_dot(hb_ref[...], wpg_ref[:, cols]) + bpg_ref[:, cols])
        y_ref[:, cols] = h_ref[:, cols] + gate * _dot(p_ref[...].astype(BF16), wpp_ref[:, cols])

    return (out_proj,
            [functools.partial(ff, c) for c in range(D_FF // FF_CHUNK)],
            [functools.partial(down, c) for c in range(D_MODEL // DOWN_CHUNK)],
            [norm2] + [functools.partial(ple, c) for c in range(D_MODEL // DOWN_CHUNK)])


def _tail_scratch(tl):
    return [pltpu.VMEM((tl, D_FF), BF16), pltpu.VMEM((tl, D_MODEL), F32), pltpu.VMEM((tl, D_MODEL), BF16)]


def _tail_specs():
    const = lambda i: (0, 0)
    resident = lambda shape: pl.BlockSpec(shape, const, pipeline_mode=pl.Buffered(1))
    vec = pl.BlockSpec((1, D_MODEL), const)
    return [
        resident((2 * GROUP_W, D_MODEL)),
        vec, vec,
        resident((D_MODEL, D_FF)),
        resident((D_MODEL, D_FF)),
        resident((D_FF, D_MODEL)),
        vec, vec,
        resident((PLE_DIM, D_MODEL)),
        resident((D_MODEL, D_MODEL)),
        vec,
    ]


def _in_proj_steps(x_ref, w_in_ref, proj_ref, xb_ref):
    def in_proj(c):
        if c == 0:
            xb_ref[...] = x_ref[...].astype(BF16)
        cols = slice(c * GROUP_W, (c + 1) * GROUP_W)
        proj_ref[:, cols] = _dot(xb_ref[...], w_in_ref[:, cols])

    return [functools.partial(in_proj, c) for c in range(IN_COLS // GROUP_W)]


def _ret_tables(dm_ref, rd_ref):
    tl = dm_ref.shape[1]
    r = lax.broadcasted_iota(jnp.int32, (tl, tl), 0)
    c = lax.broadcasted_iota(jnp.int32, (tl, tl), 1)
    row = lax.broadcasted_iota(jnp.int32, (tl, HEAD_DIM), 0).astype(F32)
    for h in range(N_HEADS):
        logd = RET_LOG_DECAY[h]
        dm_ref[h] = jnp.where(r >= c, jnp.exp((r - c).astype(F32) * logd), 0.0)
        rd_ref[0, h] = jnp.exp((row + 1.0) * logd)
        rd_ref[1, h] = jnp.exp((tl - 1.0 - row) * logd)


def _prompt_mixer_steps(cos_ref, sin_ref, lb_ref, ag_ref, bg_ref, bb_ref,
                        sa_ref, sb_ref, proj_ref, st_ref, oa_ref, mix_ref,
                        qd_ref, kd_ref, ke_ref, dec_ref, va_ref, kv_ref, sbf_ref, dm_ref, rd_ref, first):
    tl = proj_ref.shape[0]
    shift = REF_CHUNK.bit_length() - 1
    n_chunks = tl // REF_CHUNK

    def prepass():
        q_dec, k_dec, kk, b = _hgrn_prepass(proj_ref, _lower_bound(lb_ref), _causal_in_chunk(tl, shift))
        qd_ref[...] = q_dec
        for h in range(N_HEADS):
            kd_ref[h] = k_dec[:, _head(h)].T.astype(BF16)
        va_ref[...] = proj_ref[:, 2 * GROUP_W:3 * GROUP_W].astype(BF16)
        last = [b[(n + 1) * REF_CHUNK - 1:(n + 1) * REF_CHUNK, :] for n in range(n_chunks)]
        b_last = jnp.concatenate([jnp.broadcast_to(r, (REF_CHUNK, GROUP_W)) for r in last], axis=0)
        k_end = kk * jnp.exp(b_last - b)
        for n in range(n_chunks):
            dec_ref[n:n + 1, :] = jnp.exp(last[n])
        odd = ((lax.broadcasted_iota(jnp.int32, (tl, GROUP_W), 0) >> shift) & 1) == 1
        k_even = jnp.where(odd, 0.0, k_end).astype(BF16)
        k_odd = jnp.where(odd, k_end, 0.0).astype(BF16)
        for h in range(N_HEADS):
            ke_ref[:, 2 * h * HEAD_DIM:(2 * h + 1) * HEAD_DIM] = k_even[:, _head(h)]
            ke_ref[:, (2 * h + 1) * HEAD_DIM:(2 * h + 2) * HEAD_DIM] = k_odd[:, _head(h)]

    def kv_scan(h):
        hs = _head(h)
        pair = 2 * REF_CHUNK
        for r in range(n_chunks // 2):
            rows = slice(r * pair, (r + 1) * pair)
            kv_ref[r] = _dot_tn(va_ref[rows, hs], ke_ref[rows, 2 * h * HEAD_DIM:(2 * h + 2) * HEAD_DIM])
        st = jnp.where(first, 0.0, st_ref[h])
        for n in range(n_chunks):
            sbf_ref[n, h] = st.T.astype(BF16)
            st = st * dec_ref[n:n + 1, hs] + kv_ref[n // 2, :, (n % 2) * HEAD_DIM:(n % 2 + 1) * HEAD_DIM]
        st_ref[h] = st
        sa_ref[0, h] = st.T

    def diag(h):
        hs = _head(h)
        sc = jnp.where(_causal_in_chunk(tl, shift), _dot(qd_ref[:, hs], kd_ref[h]), 0.0)
        oa_ref[:, hs] = _dot(sc.astype(BF16), va_ref[:, hs])

    def inter(n):
        rows = slice(n * REF_CHUNK, (n + 1) * REF_CHUNK)
        for h in range(N_HEADS):
            hs = _head(h)
            oa_ref[rows, hs] += _dot(qd_ref[rows, hs], sbf_ref[n, h])

    def hgrn_out():
        for h in range(N_HEADS):
            hs = _head(h)
            mix_ref[:, hs] = _rms_gate(oa_ref[:, hs], ag_ref[:, hs], proj_ref[:, _head(h, 3)]).astype(BF16)

    def ret(h):
        hs = _head(h)
        cos = cos_ref[...]
        sin = sin_ref[...]
        q = _rope(proj_ref[:, _head(h, 4)], cos, sin)
        k = _rope(proj_ref[:, _head(h, 5)], cos, sin) * K_SCALE
        v = proj_ref[:, _head(h, 6)].astype(BF16)
        a = (_dot(q.astype(BF16), k.T.astype(BF16)) * dm_ref[h]).astype(BF16)
        s = jnp.where(first, 0.0, sb_ref[0, h])
        o = _dot(a, v) + _dot((q * rd_ref[0, h]).astype(BF16), s.astype(BF16))
        k_end_b = (k * rd_ref[1, h]).astype(BF16)
        sb_ref[0, h] = s * math.exp(tl * RET_LOG_DECAY[h]) + _dot_tn(k_end_b, v)
        mix_ref[:, _head(h, 1)] = _ln_gate(o, bg_ref[:, hs], bb_ref[:, hs],
                                           proj_ref[:, _head(h, 7)]).astype(BF16)

    return (prepass,
            [functools.partial(kv_scan, h) for h in range(N_HEADS)],
            [functools.partial(diag, h) for h in range(N_HEADS)],
            [functools.partial(inter, n) for n in range(n_chunks)],
            hgrn_out,
            [functools.partial(ret, h) for h in range(N_HEADS)])


def _interleave(a, b):
    out = []
    for i in range(max(len(a), len(b))):
        out += a[i:i + 1] + b[i:i + 1]
    return out


def _prompt_layer_kernel(xn_ref, xp_ref, p_ref, cos_ref, sin_ref, w_in_ref, lb_ref, ag_ref, bg_ref,
                         bb_ref, *rest, tiles_per_seq):
    tail_w = rest[:11]
    y_ref, sa_ref, sb_ref = rest[11:14]
    mixer_scratch = rest[14:27]
    proj_ref, mix_ref, dm_ref, rd_ref = mixer_scratch[0], mixer_scratch[3], mixer_scratch[11], mixer_scratch[12]
    xb_ref = rest[27]
    tail_scratch = rest[28:]
    g = pl.program_id(0)
    n_tiles = pl.num_programs(0) - 1
    slot = lax.rem(g, 2)

    def mixer_steps():
        return _prompt_mixer_steps(cos_ref, sin_ref, lb_ref, ag_ref, bg_ref, bb_ref, sa_ref, sb_ref,
                                   proj_ref.at[slot], *mixer_scratch[1:],
                                   first=lax.rem(g, tiles_per_seq) == 0)

    def tail_steps():
        return _tail_steps(xp_ref, mix_ref, p_ref, y_ref, *tail_w, *tail_scratch)

    @pl.when(g == 0)
    def _():
        _ret_tables(dm_ref, rd_ref)
        prepass, kv_scan, diag, inter, hgrn_out, ret = mixer_steps()
        steps = _in_proj_steps(xp_ref, w_in_ref, proj_ref.at[slot], xb_ref)
        steps += [prepass] + kv_scan + diag + inter + [hgrn_out] + ret
        steps += _in_proj_steps(xn_ref, w_in_ref, proj_ref.at[1 - slot], xb_ref)
        for step in steps:
            step()

    @pl.when((g > 0) & (g < n_tiles))
    def _():
        prepass, kv_scan, diag, inter, hgrn_out, ret = mixer_steps()
        out_proj, ff, down, final = tail_steps()
        in_proj = _in_proj_steps(xn_ref, w_in_ref, proj_ref.at[1 - slot], xb_ref)
        inter_pairs = [lambda a=a, b=b: (a(), b()) for a, b in zip(inter[0::2], inter[1::2])]
        steps = [out_proj, in_proj[0], prepass, in_proj[1]]
        steps += _interleave(ff + down, kv_scan + diag + ret + inter_pairs + [hgrn_out])
        steps += _interleave(final, in_proj[2:])
        for step in steps:
            step()

    @pl.when(g == n_tiles)
    def _():
        out_proj, ff, down, final = tail_steps()
        for step in [out_proj] + ff + down + final:
            step()


def _prompt_layer(x, p, cos, sin, w_in, lb_logits, a_g, b_g, b_b, tail_w):
    bsz, seq, _ = x.shape
    tl = TOKEN_TILE
    tps = seq // tl
    n_tiles = bsz * tps
    x2 = x.reshape(bsz * seq, D_MODEL)
    p2 = p.reshape(bsz * seq, PLE_DIM)
    const = lambda g: (0, 0)
    nxt = lambda g: (jnp.minimum(g + 1, n_tiles - 1), 0)
    prev = lambda g: (jnp.maximum(g - 1, 0), 0)
    seq_tile = lambda g: (lax.rem(jnp.minimum(g, n_tiles - 1), tps), 0)
    state_spec = pl.BlockSpec((1, N_HEADS, HEAD_DIM, HEAD_DIM),
                              lambda g: (jnp.minimum(g, n_tiles - 1) // tps, 0, 0, 0))
    state_shape = jax.ShapeDtypeStruct((bsz, N_HEADS, HEAD_DIM, HEAD_DIM), F32)
    return pl.pallas_call(
        functools.partial(_prompt_layer_kernel, tiles_per_seq=tps),
        grid=(n_tiles + 1,),
        in_specs=[
            pl.BlockSpec((tl, D_MODEL), nxt),
            pl.BlockSpec((tl, D_MODEL), prev),
            pl.BlockSpec((tl, PLE_DIM), prev),
            pl.BlockSpec((tl, HEAD_DIM), seq_tile),
            pl.BlockSpec((tl, HEAD_DIM), seq_tile),
            pl.BlockSpec((D_MODEL, IN_COLS), const, pipeline_mode=pl.Buffered(1)),
            pl.BlockSpec(lb_logits.shape, const),
            pl.BlockSpec((1, GROUP_W), const),
            pl.BlockSpec((1, GROUP_W), const),
            pl.BlockSpec((1, GROUP_W), const),
        ] + _tail_specs(),
        out_specs=[
            pl.BlockSpec((tl, D_MODEL), prev),
            state_spec,
            state_spec,
        ],
        out_shape=[
            jax.ShapeDtypeStruct((bsz * seq, D_MODEL), F32),
            state_shape,
            state_shape,
        ],
        scratch_shapes=[
            pltpu.VMEM((2, tl, IN_COLS), F32),
            pltpu.VMEM((N_HEADS, HEAD_DIM, HEAD_DIM), F32),
            pltpu.VMEM((tl, GROUP_W), F32),
            pltpu.VMEM((tl, 2 * GROUP_W), BF16),
            pltpu.VMEM((tl, GROUP_W), BF16),
            pltpu.VMEM((N_HEADS, HEAD_DIM, tl), BF16),
            pltpu.VMEM((tl, 2 * GROUP_W), BF16),
            pltpu.VMEM((tl // REF_CHUNK, GROUP_W), F32),
            pltpu.VMEM((tl, GROUP_W), BF16),
            pltpu.VMEM((tl // (2 * REF_CHUNK), HEAD_DIM, 2 * HEAD_DIM), F32),
            pltpu.VMEM((tl // REF_CHUNK, N_HEADS, HEAD_DIM, HEAD_DIM), BF16),
            pltpu.VMEM((N_HEADS, tl, tl), F32),
            pltpu.VMEM((2, N_HEADS, tl, HEAD_DIM), F32),
            pltpu.VMEM((tl, D_MODEL), BF16),
        ] + _tail_scratch(tl),
        compiler_params=pltpu.CompilerParams(
            dimension_semantics=("arbitrary",), vmem_limit_bytes=V7X_VMEM_LIMIT_BYTES),
        name="prompt_layer",
    )(x2, x2, p2, cos, sin, w_in, lb_logits, a_g, b_g, b_b, *tail_w)


def _cast_kernel(w_ref, wb_ref):
    wb_ref[...] = w_ref[...].astype(BF16)


def _w_in_bf16(w_in):
    tn = SAMPLE_PROJ_COLS
    return pl.pallas_call(
        _cast_kernel,
        grid=(IN_COLS // tn,),
        in_specs=[pl.BlockSpec((D_MODEL, tn), lambda c: (0, c))],
        out_specs=pl.BlockSpec((D_MODEL, tn), lambda c: (0, c)),
        out_shape=jax.ShapeDtypeStruct((D_MODEL, IN_COLS), BF16),
        compiler_params=pltpu.CompilerParams(
            dimension_semantics=("arbitrary",), vmem_limit_bytes=V7X_VMEM_LIMIT_BYTES),
        name="w_in_cast",
    )(w_in)


def _sample_rec_kernel(x_ref, w_in_ref, sa_in_ref, sb_in_ref, lb_ref, ag_ref, bg_ref, bb_ref,
                       cos_ref, sin_ref, mix_ref, sa_ref, sb_ref, proj_ref, oa_ref, ob_ref, *, seq_len):
    rows_n = proj_ref.shape[0]
    n_seq = rows_n // seq_len
    causal = _causal_in_chunk(rows_n, seq_len.bit_length() - 1)

    xb = x_ref[...].astype(BF16)
    for c in range(IN_COLS // GROUP_W):
        cols = slice(c * GROUP_W, (c + 1) * GROUP_W)
        proj_ref[:, cols] = _dot(xb, w_in_ref[:, cols])

    q_dec, k_dec, kk, b = _hgrn_prepass(proj_ref, _lower_bound(lb_ref), causal)
    v_a = proj_ref[:, 2 * GROUP_W:3 * GROUP_W]
    for h in range(N_HEADS):
        hs = _head(h)
        sc = jnp.where(causal, _dot_nt(q_dec[:, hs], k_dec[:, hs].astype(BF16)), 0.0).astype(BF16)
        oa_ref[:, hs] = _dot(sc, v_a[:, hs].astype(BF16))
    q_dec32 = q_dec.astype(F32)
    rr = lax.broadcasted_iota(jnp.int32, (seq_len, GROUP_W), 0)
    ones_blk = jnp.ones((seq_len, HEAD_DIM), BF16)
    for s in range(n_seq):
        rows = slice(s * seq_len, (s + 1) * seq_len)
        b_last = b[(s + 1) * seq_len - 1:(s + 1) * seq_len, :]
        k_end = (kk[rows] * jnp.exp(b_last - b[rows])).astype(BF16)
        hi, mid, lo = [t.astype(F32) for t in _split3(jnp.exp(b_last))]
        dec_rows = jnp.where(rr == 0, hi, jnp.where(rr == 1, mid, jnp.where(rr == 2, lo, 0.0)))
        dec_rows = dec_rows.astype(BF16)
        for h in range(N_HEADS):
            hs = _head(h)
            st = sa_in_ref[s, h]
            oa_ref[rows, hs] += _dot(q_dec32[rows, hs].astype(BF16), st.astype(BF16))
            dec_kv = _dot_tn(dec_rows[:, hs], ones_blk)
            sa_ref[s, h] = st * dec_kv + _dot_tn(k_end[:, hs], v_a[rows, hs].astype(BF16))
    for h in range(N_HEADS):
        hs = _head(h)
        mix_ref[:, hs] = _rms_gate(oa_ref[:, hs], ag_ref[:, hs], proj_ref[:, _head(h, 3)]).astype(BF16)

    cos = jnp.concatenate([cos_ref[...]] * n_seq, axis=0)
    sin = jnp.concatenate([sin_ref[...]] * n_seq, axis=0)
    r = lax.broadcasted_iota(jnp.int32, (rows_n, rows_n), 0)
    c = lax.broadcasted_iota(jnp.int32, (rows_n, rows_n), 1)
    diff = ((r & (seq_len - 1)) - (c & (seq_len - 1))).astype(F32)
    row = (lax.broadcasted_iota(jnp.int32, (rows_n, HEAD_DIM), 0) & (seq_len - 1)).astype(F32)
    for h in range(N_HEADS):
        hs = _head(h)
        logd = RET_LOG_DECAY[h]
        q = _rope(proj_ref[:, _head(h, 4)], cos, sin)
        k = _rope(proj_ref[:, _head(h, 5)], cos, sin) * K_SCALE
        v32 = proj_ref[:, _head(h, 6)]
        dmask = jnp.where(causal, jnp.exp(diff * logd), 0.0)
        a = (_dot_nt(q.astype(BF16), k.astype(BF16)) * dmask).astype(BF16)
        ob_ref[...] = _dot(a, v32.astype(BF16))
        q_dec_b = q * jnp.exp((row + 1.0) * logd)
        k_end_b = k * jnp.exp((seq_len - 1.0 - row) * logd)
        for s in range(n_seq):
            rows = slice(s * seq_len, (s + 1) * seq_len)
            st = sb_in_ref[s, h]
            ob_ref[rows, :] += _dot(q_dec_b[rows].astype(BF16), st.astype(BF16))
            sb_ref[s, h] = st * math.exp(seq_len * logd) + _dot_tn(
                k_end_b[rows].astype(BF16), v32[rows].astype(BF16))
        mix_ref[:, _head(h, 1)] = _ln_gate(ob_ref[...], bg_ref[:, hs], bb_ref[:, hs],
                                           proj_ref[:, _head(h, 7)]).astype(BF16)


def _sample_rec(x, w_in, sa, sb, lb_logits, a_g, b_g, b_b, cos, sin, seq_len):
    n_tok = x.shape[0]
    n_seq = n_tok // seq_len
    bs = SAMPLE_SEQS
    rows = bs * seq_len
    const = lambda i: (0, 0)
    state_spec = pl.BlockSpec((bs, N_HEADS, HEAD_DIM, HEAD_DIM), lambda i: (i, 0, 0, 0))
    state_shape = jax.ShapeDtypeStruct((n_seq, N_HEADS, HEAD_DIM, HEAD_DIM), F32)
    return pl.pallas_call(
        functools.partial(_sample_rec_kernel, seq_len=seq_len),
        grid=(n_seq // bs,),
        in_specs=[
            pl.BlockSpec((rows, D_MODEL), lambda i: (i, 0)),
            pl.BlockSpec((D_MODEL, IN_COLS), const, pipeline_mode=pl.Buffered(1)),
            state_spec,
            state_spec,
            pl.BlockSpec(lb_logits.shape, const),
            pl.BlockSpec((1, GROUP_W), const),
            pl.BlockSpec((1, GROUP_W), const),
            pl.BlockSpec((1, GROUP_W), const),
            pl.BlockSpec((seq_len, HEAD_DIM), const),
            pl.BlockSpec((seq_len, HEAD_DIM), const),
        ],
        out_specs=[
            pl.BlockSpec((rows, 2 * GROUP_W), lambda i: (i, 0)),
            state_spec,
            state_spec,
        ],
        out_shape=[
            jax.ShapeDtypeStruct((n_tok, 2 * GROUP_W), BF16),
            state_shape,
            state_shape,
        ],
        scratch_shapes=[
            pltpu.VMEM((rows, IN_COLS), F32),
            pltpu.VMEM((rows, GROUP_W), F32),
            pltpu.VMEM((rows, HEAD_DIM), F32),
        ],
        compiler_params=pltpu.CompilerParams(
            dimension_semantics=("arbitrary",), vmem_limit_bytes=V7X_VMEM_LIMIT_BYTES),
        name="sample_recurrence",
    )(x, w_in, sa, sb, lb_logits, a_g, b_g, b_b, cos, sin)


_TAIL_PHASE_STEPS = (D_MODEL // DOWN_CHUNK, D_FF // FF_CHUNK, D_MODEL // DOWN_CHUNK, D_MODEL // DOWN_CHUNK)
_TAIL_PHASE_START = tuple(sum(_TAIL_PHASE_STEPS[:i]) for i in range(4))


def _sample_tail_kernel(x_ref, mix_ref, p_ref, wo_ref, wg_ref, wu_ref, wd_ref, wpg_ref, wpp_ref,
                        ln1g_ref, ln1b_ref, ln2g_ref, ln2b_ref, bpg_ref,
                        y_ref, wo_b, wg_b, wu_b, wd_b, wpg_b, wpp_b, act_ref, h_ref, hb_ref):
    s = pl.program_id(0)
    a0, b0, c0, d0 = _TAIL_PHASE_START
    blk = DOWN_CHUNK

    @pl.when(s < b0)
    def _():
        wo_b[...] = wo_ref[...].astype(BF16)
        cols = pl.ds(pl.multiple_of((s - a0) * blk, blk), blk)
        h_ref[:, cols] = DN_ALPHA * x_ref[...] + _dot(mix_ref[...], wo_b[...])

        @pl.when(s == b0 - 1)
        def _():
            h = _layer_norm(h_ref[...], ln1g_ref[...], ln1b_ref[...])
            h_ref[...] = h
            hb_ref[...] = h.astype(BF16)

    @pl.when((s >= b0) & (s < c0))
    def _():
        wg_b[...] = wg_ref[...].astype(BF16)
        wu_b[...] = wu_ref[...].astype(BF16)
        cols = pl.ds(pl.multiple_of((s - b0) * FF_CHUNK, FF_CHUNK), FF_CHUNK)
        hb = hb_ref[...]
        act_ref[:, cols] = (_silu(_dot(hb, wg_b[...])) * _dot(hb, wu_b[...])).astype(BF16)

    @pl.when((s >= c0) & (s < d0))
    def _():
        wd_b[...] = wd_ref[...].astype(BF16)
        cols = pl.ds(pl.multiple_of((s - c0) * blk, blk), blk)
        h_ref[:, cols] = DN_ALPHA * h_ref[:, cols] + _dot(act_ref[...], wd_b[...])

        @pl.when(s == d0 - 1)
        def _():
            h2 = _layer_norm(h_ref[...], ln2g_ref[...], ln2b_ref[...])
            h_ref[...] = h2
            hb_ref[...] = h2.astype(BF16)

    @pl.when(s >= d0)
    def _():
        wpg_b[...] = wpg_ref[...].astype(BF16)
        wpp_b[...] = wpp_ref[...].astype(BF16)
        cols = pl.ds(pl.multiple_of((s - d0) * blk, blk), blk)
        gate = _sigmoid(_dot(hb_ref[...], wpg_b[...]) + bpg_ref[:, cols])
        y_ref[...] = h_ref[:, cols] + gate * _dot(p_ref[...].astype(BF16), wpp_b[...])


def _sample_tail(x, mix, p, w_out, ln1g, ln1b, wg, wu, wd, ln2g, ln2b, wpp, wpg, bpg):
    n = x.shape[0]
    a0, b0, c0, d0 = _TAIL_PHASE_START
    na, nb, nc, nd = _TAIL_PHASE_STEPS
    const = lambda s: (0, 0)
    col = lambda start, count: (lambda s: (0, jnp.clip(s - start, 0, count - 1)))
    vec = pl.BlockSpec((1, D_MODEL), const)
    weight_specs = [
        pl.BlockSpec((2 * GROUP_W, DOWN_CHUNK), col(a0, na)),
        pl.BlockSpec((D_MODEL, FF_CHUNK), col(b0, nb)),
        pl.BlockSpec((D_MODEL, FF_CHUNK), col(b0, nb)),
        pl.BlockSpec((D_FF, DOWN_CHUNK), col(c0, nc)),
        pl.BlockSpec((D_MODEL, DOWN_CHUNK), col(d0, nd)),
        pl.BlockSpec((PLE_DIM, DOWN_CHUNK), col(d0, nd)),
    ]
    weights = (w_out, wg, wu, wd, wpg, wpp)
    return pl.pallas_call(
        _sample_tail_kernel,
        grid=(sum(_TAIL_PHASE_STEPS),),
        in_specs=[
            pl.BlockSpec((n, DOWN_CHUNK), col(a0, na)),
            pl.BlockSpec((n, 2 * GROUP_W), const),
            pl.BlockSpec((n, PLE_DIM), const),
        ] + weight_specs + [vec, vec, vec, vec, vec],
        out_specs=[pl.BlockSpec((n, DOWN_CHUNK), col(d0, nd))] + weight_specs,
        out_shape=[jax.ShapeDtypeStruct((n, D_MODEL), F32)]
        + [jax.ShapeDtypeStruct(w.shape, BF16) for w in weights],
        scratch_shapes=_tail_scratch(n),
        compiler_params=pltpu.CompilerParams(
            dimension_semantics=("arbitrary",), vmem_limit_bytes=V7X_VMEM_LIMIT_BYTES),
        name="sample_tail",
    )(x, mix, p, *weights, ln1g, ln1b, ln2g, ln2b, bpg)


def kernel(x_prompt, x_sample, p_prompt, p_sample, state_hgrn, state_ret, lb_logits, w_in, a_norm_g, b_norm_g, b_norm_b, w_out, ln1_g, ln1_b, w_ffn_gate, w_ffn_up, w_ffn_down, ln2_g, ln2_b, w_ple_proj, w_ple_gate, b_ple_gate):
    assert w_in.shape[0] == DEPTH == 1
    bsz, seq, _ = x_prompt.shape
    n_dec, dec_seq, _ = x_sample.shape

    mixer_vecs = (lb_logits, a_norm_g, b_norm_g, b_norm_b)
    cos_p, sin_p = _rope_tables(seq, 0)
    cos_s, sin_s = _rope_tables(dec_seq, PAST_LEN)

    x_s = x_sample.reshape(n_dec * dec_seq, D_MODEL)
    w_in_b = _w_in_bf16(w_in[0])
    mix_s, sa_s, sb_s = _sample_rec(x_s, w_in_b, state_hgrn[0], state_ret[0], *mixer_vecs,
                                    cos_s, sin_s, dec_seq)
    y_s, w_out_b, wg_b, wu_b, wd_b, wpg_b, wpp_b = _sample_tail(
        x_s, mix_s, p_sample[0].reshape(n_dec * dec_seq, PLE_DIM), w_out[0], ln1_g, ln1_b,
        w_ffn_gate[0], w_ffn_up[0], w_ffn_down[0], ln2_g, ln2_b, w_ple_proj[0], w_ple_gate[0],
        b_ple_gate)

    tail_w = (w_out_b, ln1_g, ln1_b, wg_b, wu_b, wd_b, ln2_g, ln2_b, wpp_b, wpg_b, b_ple_gate)
    y_p, sa_p, sb_p = _prompt_layer(x_prompt, p_prompt[0], cos_p, sin_p, w_in_b, *mixer_vecs, tail_w)

    return (y_p.reshape(bsz, seq, D_MODEL), y_s.reshape(n_dec, dec_seq, D_MODEL),
            sa_p[None], sb_p[None], sa_s[None], sb_s[None])
```

```python
import functools
import math

import jax
import jax.numpy as jnp
from jax import lax
from jax.experimental import pallas as pl
from jax.experimental.pallas import tpu as pltpu

F32 = jnp.float32
BF16 = jnp.bfloat16

D_MODEL = 1024
N_HEADS = 4
HEAD_DIM = 128
GROUP_W = N_HEADS * HEAD_DIM
IN_COLS = 8 * GROUP_W
D_FF = 2816
PLE_DIM = 256
DEPTH = 1
PAST_LEN = 16384
REF_CHUNK = 32
ROPE_BASE = 10000.0
NORM_EPS = 1e-5
DN_ALPHA = (2.0 * DEPTH) ** 0.25
RET_LOG_DECAY = tuple(math.log1p(-(2.0 ** (-5.0 - h))) for h in range(N_HEADS))
K_SCALE = HEAD_DIM ** -0.5

V7X_VMEM_LIMIT_BYTES = 60 * 1024 * 1024

TOKEN_TILE = 256
SAMPLE_PROJ_COLS = 1024
SAMPLE_SEQS = 16
FF_CHUNK = 256
DOWN_CHUNK = 256


def _dot(a, b):
    return jnp.dot(a, b, preferred_element_type=F32)


def _dot_nt(a, b):
    return lax.dot_general(a, b, (((1,), (1,)), ((), ())), preferred_element_type=F32)


def _dot_tn(a, b):
    return lax.dot_general(a, b, (((0,), (0,)), ((), ())), preferred_element_type=F32)


def _split3(x):
    hi = x.astype(BF16)
    r1 = x - hi.astype(F32)
    mid = r1.astype(BF16)
    lo = (r1 - mid.astype(F32)).astype(BF16)
    return hi, mid, lo


def _dot_exact_lhs01(m01, parts):
    return _dot(jnp.concatenate([m01] * 3, axis=1), jnp.concatenate(list(parts), axis=0))


def _sigmoid(x):
    return 1.0 / (1.0 + jnp.exp(-x))


def _silu(x):
    return x * _sigmoid(x)


def _causal_in_chunk(n, shift):
    r = lax.broadcasted_iota(jnp.int32, (n, n), 0)
    c = lax.broadcasted_iota(jnp.int32, (n, n), 1)
    return ((r >> shift) == (c >> shift)) & (c <= r)


def _lower_bound(lb_ref):
    rows = [lb_ref[i:i + 1, :] for i in range(lb_ref.shape[0])]
    m = functools.reduce(jnp.maximum, rows)
    e = [jnp.exp(r - m) for r in rows]
    return e[0] / functools.reduce(jnp.add, e)


def _hgrn_prepass(proj_ref, lb, causal):
    tri = jnp.where(causal, 1.0, 0.0).astype(BF16)
    f = lb + (1.0 - lb) * _sigmoid(proj_ref[:, GROUP_W:2 * GROUP_W])
    kk = 1.0 - f
    b = _dot_exact_lhs01(tri, _split3(jnp.log(f)))
    q_dec = (_silu(proj_ref[:, 0:GROUP_W]) * jnp.exp(b)).astype(BF16)
    k_dec = kk * jnp.exp(-b)
    return q_dec, k_dec, kk, b


def _rope(x, cos, sin_signed):
    return x * cos + pltpu.roll(x, HEAD_DIM // 2, axis=1) * sin_signed


def _rms_gate(o, g, gate):
    return o * lax.rsqrt(jnp.mean(o * o, axis=-1, keepdims=True) + NORM_EPS) * g * _silu(gate)


def _layer_norm(x, g, b):
    mu = jnp.mean(x, axis=-1, keepdims=True)
    d = x - mu
    var = jnp.mean(d * d, axis=-1, keepdims=True)
    return d * lax.rsqrt(var + NORM_EPS) * g + b


def _ln_gate(o, g, b, gate):
    return _layer_norm(o, g, b) * _silu(gate)


def _head(h, group=0):
    return slice(group * GROUP_W + h * HEAD_DIM, group * GROUP_W + (h + 1) * HEAD_DIM)


def _rope_table_kernel(cos_ref, sin_ref, *, offset):
    n = cos_ref.shape[0]
    half = HEAD_DIM // 2
    packed = n % 16 == 0
    m = n // 2 if packed else n
    row = lax.broadcasted_iota(jnp.int32, (m, HEAD_DIM), 0) + pl.program_id(0) * n
    lane = lax.broadcasted_iota(jnp.int32, (m, HEAD_DIM), 1)
    low = lane < half
    if packed:
        row = row + jnp.where(low, 0, m)
    j = (lane & (half - 1)).astype(F32)
    inv = jnp.exp(-(j / half) * math.log(ROPE_BASE))
    ang = (row.astype(F32) + offset) * inv
    c = jnp.cos(ang)
    s = jnp.sin(ang)
    if not packed:
        cos_ref[...] = c
        sin_ref[...] = jnp.where(low, -s, s)
        return
    c_sw = pltpu.roll(c, half, axis=1)
    s_sw = pltpu.roll(s, half, axis=1)
    cos_ref[0:m, :] = jnp.where(low, c, c_sw)
    cos_ref[m:n, :] = jnp.where(low, c_sw, c)
    sin_ref[0:m, :] = jnp.where(low, -s, s_sw)
    sin_ref[m:n, :] = jnp.where(low, -s_sw, s)


def _rope_tables(n, offset):
    tile = min(n, 512)
    return pl.pallas_call(
        functools.partial(_rope_table_kernel, offset=float(offset)),
        grid=(n // tile,),
        in_specs=[],
        out_specs=[pl.BlockSpec((tile, HEAD_DIM), lambda i: (i, 0))] * 2,
        out_shape=[jax.ShapeDtypeStruct((n, HEAD_DIM), F32)] * 2,
        name="rope_tables",
    )()


def _tail_steps(x_ref, mix_ref, p_ref, y_ref, w_out_ref, ln1g_ref, ln1b_ref, wg_ref, wu_ref, wd_ref,
                ln2g_ref, ln2b_ref, wpp_ref, wpg_ref, bpg_ref, act_ref, h_ref, hb_ref):
    def out_proj():
        for c in range(D_MODEL // DOWN_CHUNK):
            cols = slice(c * DOWN_CHUNK, (c + 1) * DOWN_CHUNK)
            h_ref[:, cols] = DN_ALPHA * x_ref[:, cols] + _dot(mix_ref[...], w_out_ref[:, cols])
        h = _layer_norm(h_ref[...], ln1g_ref[...], ln1b_ref[...])
        h_ref[...] = h
        hb_ref[...] = h.astype(BF16)

    def ff(c):
        cols = slice(c * FF_CHUNK, (c + 1) * FF_CHUNK)
        hb = hb_ref[...]
        act_ref[:, cols] = (_silu(_dot(hb, wg_ref[:, cols])) * _dot(hb, wu_ref[:, cols])).astype(BF16)

    def down(c):
        cols = slice(c * DOWN_CHUNK, (c + 1) * DOWN_CHUNK)
        h_ref[:, cols] = DN_ALPHA * h_ref[:, cols] + _dot(act_ref[...], wd_ref[:, cols])

    def norm2():
        h2 = _layer_norm(h_ref[...], ln2g_ref[...], ln2b_ref[...])
        h_ref[...] = h2
        hb_ref[...] = h2.astype(BF16)

    def ple(c):
        cols = slice(c * DOWN_CHUNK, (c + 1) * DOWN_CHUNK)
        gate = _sigmoid(_dot(hb_ref[...], wpg_ref[:, cols]) + bpg_ref[:, cols])
        y_ref[:, cols] = h_ref[:, cols] + gate * _dot(p_ref[...].astype(BF16), wpp_ref[:, cols])

    return (out_proj,
            [functools.partial(ff, c) for c in range(D_FF // FF_CHUNK)],
            [functools.partial(down, c) for c in range(D_MODEL // DOWN_CHUNK)],
            [norm2] + [functools.partial(ple, c) for c in range(D_MODEL // DOWN_CHUNK)])


def _tail_scratch(tl):
    return [pltpu.VMEM((tl, D_FF), BF16), pltpu.VMEM((tl, D_MODEL), F32), pltpu.VMEM((tl, D_MODEL), BF16)]


def _tail_specs():
    const = lambda i: (0, 0)
    resident = lambda shape: pl.BlockSpec(shape, const, pipeline_mode=pl.Buffered(1))
    vec = pl.BlockSpec((1, D_MODEL), const)
    return [
        resident((2 * GROUP_W, D_MODEL)),
        vec, vec,
        resident((D_MODEL, D_FF)),
        resident((D_MODEL, D_FF)),
        resident((D_FF, D_MODEL)),
        vec, vec,
        resident((PLE_DIM, D_MODEL)),
        resident((D_MODEL, D_MODEL)),
        vec,
    ]


def _in_proj_steps(x_ref, w_in_ref, proj_ref, xb_ref):
    def in_proj(c):
        if c == 0:
            xb_ref[...] = x_ref[...].astype(BF16)
        cols = slice(c * GROUP_W, (c + 1) * GROUP_W)
        proj_ref[:, cols] = _dot(xb_ref[...], w_in_ref[:, cols])

    return [functools.partial(in_proj, c) for c in range(IN_COLS // GROUP_W)]


def _ret_tables(dm_ref, rd_ref):
    tl = dm_ref.shape[1]
    r = lax.broadcasted_iota(jnp.int32, (tl, tl), 0)
    c = lax.broadcasted_iota(jnp.int32, (tl, tl), 1)
    row = lax.broadcasted_iota(jnp.int32, (tl, HEAD_DIM), 0).astype(F32)
    for h in range(N_HEADS):
        logd = RET_LOG_DECAY[h]
        dm_ref[h] = jnp.where(r >= c, jnp.exp((r - c).astype(F32) * logd), 0.0)
        rd_ref[0, h] = jnp.exp((row + 1.0) * logd)
        rd_ref[1, h] = jnp.exp((tl - 1.0 - row) * logd)


def _prompt_mixer_steps(cos_ref, sin_ref, lb_ref, ag_ref, bg_ref, bb_ref,
                        sa_ref, sb_ref, proj_ref, st_ref, oa_ref, mix_ref,
                        qd_ref, kd_ref, ke_ref, dec_ref, va_ref, kv_ref, sbf_ref, dm_ref, rd_ref, first):
    tl = proj_ref.shape[0]
    shift = REF_CHUNK.bit_length() - 1
    n_chunks = tl // REF_CHUNK

    def prepass():
        q_dec, k_dec, kk, b = _hgrn_prepass(proj_ref, _lower_bound(lb_ref), _causal_in_chunk(tl, shift))
        qd_ref[...] = q_dec
        for h in range(N_HEADS):
            kd_ref[h] = k_dec[:, _head(h)].T.astype(BF16)
        va_ref[...] = proj_ref[:, 2 * GROUP_W:3 * GROUP_W].astype(BF16)
        last = [b[(n + 1) * REF_CHUNK - 1:(n + 1) * REF_CHUNK, :] for n in range(n_chunks)]
        b_last = jnp.concatenate([jnp.broadcast_to(r, (REF_CHUNK, GROUP_W)) for r in last], axis=0)
        k_end = kk * jnp.exp(b_last - b)
        for n in range(n_chunks):
            dec_ref[n:n + 1, :] = jnp.exp(last[n])
        odd = ((lax.broadcasted_iota(jnp.int32, (tl, GROUP_W), 0) >> shift) & 1) == 1
        k_even = jnp.where(odd, 0.0, k_end).astype(BF16)
        k_odd = jnp.where(odd, k_end, 0.0).astype(BF16)
        for h in range(N_HEADS):
            ke_ref[:, 2 * h * HEAD_DIM:(2 * h + 1) * HEAD_DIM] = k_even[:, _head(h)]
            ke_ref[:, (2 * h + 1) * HEAD_DIM:(2 * h + 2) * HEAD_DIM] = k_odd[:, _head(h)]

    def kv_scan(h):
        hs = _head(h)
        pair = 2 * REF_CHUNK
        for r in range(n_chunks // 2):
            rows = slice(r * pair, (r + 1) * pair)
            kv_ref[r] = _dot_tn(va_ref[rows, hs], ke_ref[rows, 2 * h * HEAD_DIM:(2 * h + 2) * HEAD_DIM])
        st = jnp.where(first, 0.0, st_ref[h])
        for n in range(n_chunks):
            sbf_ref[n, h] = st.T.astype(BF16)
            st = st * dec_ref[n:n + 1, hs] + kv_ref[n // 2, :, (n % 2) * HEAD_DIM:(n % 2 + 1) * HEAD_DIM]
        st_ref[h] = st
        sa_ref[0, h] = st.T

    def diag(h):
        hs = _head(h)
        sc = jnp.where(_causal_in_chunk(tl, shift), _dot(qd_ref[:, hs], kd_ref[h]), 0.0)
        oa_ref[:, hs] = _dot(sc.astype(BF16), va_ref[:, hs])

    def inter(n):
        rows = slice(n * REF_CHUNK, (n + 1) * REF_CHUNK)
        for h in range(N_HEADS):
            hs = _head(h)
            oa_ref[rows, hs] += _dot(qd_ref[rows, hs], sbf_ref[n, h])

    def hgrn_out():
        for h in range(N_HEADS):
            hs = _head(h)
            mix_ref[:, hs] = _rms_gate(oa_ref[:, hs], ag_ref[:, hs], proj_ref[:, _head(h, 3)]).astype(BF16)

    def ret(h):
        hs = _head(h)
        cos = cos_ref[...]
        sin = sin_ref[...]
        q = _rope(proj_ref[:, _head(h, 4)], cos, sin)
        k = _rope(proj_ref[:, _head(h, 5)], cos, sin) * K_SCALE
        v = proj_ref[:, _head(h, 6)].astype(BF16)
        a = (_dot(q.astype(BF16), k.T.astype(BF16)) * dm_ref[h]).astype(BF16)
        s = jnp.where(first, 0.0, sb_ref[0, h])
        o = _dot(a, v) + _dot((q * rd_ref[0, h]).astype(BF16), s.astype(BF16))
        k_end_b = (k * rd_ref[1, h]).astype(BF16)
        sb_ref[0, h] = s * math.exp(tl * RET_LOG_DECAY[h]) + _dot_tn(k_end_b, v)
        mix_ref[:, _head(h, 1)] = _ln_gate(o, bg_ref[:, hs], bb_ref[:, hs],
                                           proj_ref[:, _head(h, 7)]).astype(BF16)

    return (prepass,
            [functools.partial(kv_scan, h) for h in range(N_HEADS)],
            [functools.partial(diag, h) for h in range(N_HEADS)],
            [functools.partial(inter, n) for n in range(n_chunks)],
            hgrn_out,
            [functools.partial(ret, h) for h in range(N_HEADS)])


def _interleave(a, b):
    out = []
    for i in range(max(len(a), len(b))):
        out += a[i:i + 1] + b[i:i + 1]
    return out


def _prompt_layer_kernel(xn_ref, xp_ref, p_ref, cos_ref, sin_ref, w_in_ref, lb_ref, ag_ref, bg_ref,
                         bb_ref, *rest, tiles_per_seq):
    tail_w = rest[:11]
    y_ref, sa_ref, sb_ref = rest[11:14]
    mixer_scratch = rest[14:27]
    proj_ref, mix_ref, dm_ref, rd_ref = mixer_scratch[0], mixer_scratch[3], mixer_scratch[11], mixer_scratch[12]
    xb_ref = rest[27]
    tail_scratch = rest[28:]
    g = pl.program_id(0)
    n_tiles = pl.num_programs(0) - 1
    slot = lax.rem(g, 2)

    def mixer_steps():
        return _prompt_mixer_steps(cos_ref, sin_ref, lb_ref, ag_ref, bg_ref, bb_ref, sa_ref, sb_ref,
                                   proj_ref.at[slot], *mixer_scratch[1:],
                                   first=lax.rem(g, tiles_per_seq) == 0)

    def tail_steps():
        return _tail_steps(xp_ref, mix_ref, p_ref, y_ref, *tail_w, *tail_scratch)

    @pl.when(g == 0)
    def _():
        _ret_tables(dm_ref, rd_ref)
        prepass, kv_scan, diag, inter, hgrn_out, ret = mixer_steps()
        steps = _in_proj_steps(xp_ref, w_in_ref, proj_ref.at[slot], xb_ref)
        steps += [prepass] + kv_scan + diag + inter + [hgrn_out] + ret
        steps += _in_proj_steps(xn_ref, w_in_ref, proj_ref.at[1 - slot], xb_ref)
        for step in steps:
            step()

    @pl.when((g > 0) & (g < n_tiles))
    def _():
        prepass, kv_scan, diag, inter, hgrn_out, ret = mixer_steps()
        out_proj, ff, down, final = tail_steps()
        in_proj = _in_proj_steps(xn_ref, w_in_ref, proj_ref.at[1 - slot], xb_ref)
        inter_pairs = [lambda a=a, b=b: (a(), b()) for a, b in zip(inter[0::2], inter[1::2])]
        steps = [out_proj, in_proj[0], prepass, in_proj[1]]
        steps += _interleave(ff + down, kv_scan + diag + ret + inter_pairs + [hgrn_out])
        steps += _interleave(final, in_proj[2:])
        for step in steps:
            step()

    @pl.when(g == n_tiles)
    def _():
        out_proj, ff, down, final = tail_steps()
        for step in [out_proj] + ff + down + final:
            step()


def _prompt_layer(x, p, cos, sin, w_in, lb_logits, a_g, b_g, b_b, tail_w):
    bsz, seq, _ = x.shape
    tl = TOKEN_TILE
    tps = seq // tl
    n_tiles = bsz * tps
    x2 = x.reshape(bsz * seq, D_MODEL)
    p2 = p.reshape(bsz * seq, PLE_DIM)
    const = lambda g: (0, 0)
    nxt = lambda g: (jnp.minimum(g + 1, n_tiles - 1), 0)
    prev = lambda g: (jnp.maximum(g - 1, 0), 0)
    seq_tile = lambda g: (lax.rem(jnp.minimum(g, n_tiles - 1), tps), 0)
    state_spec = pl.BlockSpec((1, N_HEADS, HEAD_DIM, HEAD_DIM),
                              lambda g: (jnp.minimum(g, n_tiles - 1) // tps, 0, 0, 0))
    state_shape = jax.ShapeDtypeStruct((bsz, N_HEADS, HEAD_DIM, HEAD_DIM), F32)
    return pl.pallas_call(
        functools.partial(_prompt_layer_kernel, tiles_per_seq=tps),
        grid=(n_tiles + 1,),
        in_specs=[
            pl.BlockSpec((tl, D_MODEL), nxt),
            pl.BlockSpec((tl, D_MODEL), prev),
            pl.BlockSpec((tl, PLE_DIM), prev),
            pl.BlockSpec((tl, HEAD_DIM), seq_tile),
            pl.BlockSpec((tl, HEAD_DIM), seq_tile),
            pl.BlockSpec((D_MODEL, IN_COLS), const, pipeline_mode=pl.Buffered(1)),
            pl.BlockSpec(lb_logits.shape, const),
            pl.BlockSpec((1, GROUP_W), const),
            pl.BlockSpec((1, GROUP_W), const),
            pl.BlockSpec((1, GROUP_W), const),
        ] + _tail_specs(),
        out_specs=[
            pl.BlockSpec((tl, D_MODEL), prev),
            state_spec,
            state_spec,
        ],
        out_shape=[
            jax.ShapeDtypeStruct((bsz * seq, D_MODEL), F32),
            state_shape,
            state_shape,
        ],
        scratch_shapes=[
            pltpu.VMEM((2, tl, IN_COLS), F32),
            pltpu.VMEM((N_HEADS, HEAD_DIM, HEAD_DIM), F32),
            pltpu.VMEM((tl, GROUP_W), F32),
            pltpu.VMEM((tl, 2 * GROUP_W), BF16),
            pltpu.VMEM((tl, GROUP_W), BF16),
            pltpu.VMEM((N_HEADS, HEAD_DIM, tl), BF16),
            pltpu.VMEM((tl, 2 * GROUP_W), BF16),
            pltpu.VMEM((tl // REF_CHUNK, GROUP_W), F32),
            pltpu.VMEM((tl, GROUP_W), BF16),
            pltpu.VMEM((tl // (2 * REF_CHUNK), HEAD_DIM, 2 * HEAD_DIM), F32),
            pltpu.VMEM((tl // REF_CHUNK, N_HEADS, HEAD_DIM, HEAD_DIM), BF16),
            pltpu.VMEM((N_HEADS, tl, tl), F32),
            pltpu.VMEM((2, N_HEADS, tl, HEAD_DIM), F32),
            pltpu.VMEM((tl, D_MODEL), BF16),
        ] + _tail_scratch(tl),
        compiler_params=pltpu.CompilerParams(
            dimension_semantics=("arbitrary",), vmem_limit_bytes=V7X_VMEM_LIMIT_BYTES),
        name="prompt_layer",
    )(x2, x2, p2, cos, sin, w_in, lb_logits, a_g, b_g, b_b, *tail_w)


def _cast_kernel(w_ref, wb_ref):
    wb_ref[...] = w_ref[...].astype(BF16)


def _w_in_bf16(w_in):
    tn = SAMPLE_PROJ_COLS
    return pl.pallas_call(
        _cast_kernel,
        grid=(IN_COLS // tn,),
        in_specs=[pl.BlockSpec((D_MODEL, tn), lambda c: (0, c))],
        out_specs=pl.BlockSpec((D_MODEL, tn), lambda c: (0, c)),
        out_shape=jax.ShapeDtypeStruct((D_MODEL, IN_COLS), BF16),
        compiler_params=pltpu.CompilerParams(
            dimension_semantics=("arbitrary",), vmem_limit_bytes=V7X_VMEM_LIMIT_BYTES),
        name="w_in_cast",
    )(w_in)


def _sample_rec_kernel(x_ref, w_in_ref, sa_in_ref, sb_in_ref, lb_ref, ag_ref, bg_ref, bb_ref,
                       cos_ref, sin_ref, mix_ref, sa_ref, sb_ref, proj_ref, oa_ref, ob_ref, *, seq_len):
    rows_n = proj_ref.shape[0]
    n_seq = rows_n // seq_len
    causal = _causal_in_chunk(rows_n, seq_len.bit_length() - 1)

    xb = x_ref[...].astype(BF16)
    for c in range(IN_COLS // GROUP_W):
        cols = slice(c * GROUP_W, (c + 1) * GROUP_W)
        proj_ref[:, cols] = _dot(xb, w_in_ref[:, cols])

    q_dec, k_dec, kk, b = _hgrn_prepass(proj_ref, _lower_bound(lb_ref), causal)
    v_a = proj_ref[:, 2 * GROUP_W:3 * GROUP_W]
    for h in range(N_HEADS):
        hs = _head(h)
        sc = jnp.where(causal, _dot_nt(q_dec[:, hs], k_dec[:, hs].astype(BF16)), 0.0).astype(BF16)
        oa_ref[:, hs] = _dot(sc, v_a[:, hs].astype(BF16))
    q_dec32 = q_dec.astype(F32)
    for s in range(n_seq):
        rows = slice(s * seq_len, (s + 1) * seq_len)
        b_last = b[(s + 1) * seq_len - 1:(s + 1) * seq_len, :]
        k_end = (kk[rows] * jnp.exp(b_last - b[rows])).astype(BF16)
        dec = jnp.exp(b_last)
        for h in range(N_HEADS):
            hs = _head(h)
            st = sa_in_ref[s, h]
            oa_ref[rows, hs] += _dot(q_dec32[rows, hs].astype(BF16), st.astype(BF16))
            dec_kv = jnp.broadcast_to(dec[:, hs], (HEAD_DIM, HEAD_DIM)).T
            sa_ref[s, h] = st * dec_kv + _dot_tn(k_end[:, hs], v_a[rows, hs].astype(BF16))
    for h in range(N_HEADS):
        hs = _head(h)
        mix_ref[:, hs] = _rms_gate(oa_ref[:, hs], ag_ref[:, hs], proj_ref[:, _head(h, 3)]).astype(BF16)

    cos = jnp.concatenate([cos_ref[...]] * n_seq, axis=0)
    sin = jnp.concatenate([sin_ref[...]] * n_seq, axis=0)
    r = lax.broadcasted_iota(jnp.int32, (rows_n, rows_n), 0)
    c = lax.broadcasted_iota(jnp.int32, (rows_n, rows_n), 1)
    diff = ((r & (seq_len - 1)) - (c & (seq_len - 1))).astype(F32)
    row = (lax.broadcasted_iota(jnp.int32, (rows_n, HEAD_DIM), 0) & (seq_len - 1)).astype(F32)
    for h in range(N_HEADS):
        hs = _head(h)
        logd = RET_LOG_DECAY[h]
        q = _rope(proj_ref[:, _head(h, 4)], cos, sin)
        k = _rope(proj_ref[:, _head(h, 5)], cos, sin) * K_SCALE
        v32 = proj_ref[:, _head(h, 6)]
        dmask = jnp.where(causal, jnp.exp(diff * logd), 0.0)
        a = (_dot_nt(q.astype(BF16), k.astype(BF16)) * dmask).astype(BF16)
        ob_ref[...] = _dot(a, v32.astype(BF16))
        q_dec_b = q * jnp.exp((row + 1.0) * logd)
        k_end_b = k * jnp.exp((seq_len - 1.0 - row) * logd)
        for s in range(n_seq):
            rows = slice(s * seq_len, (s + 1) * seq_len)
            st = sb_in_ref[s, h]
            ob_ref[rows, :] += _dot(q_dec_b[rows].astype(BF16), st.astype(BF16))
            sb_ref[s, h] = st * math.exp(seq_len * logd) + _dot_tn(
                k_end_b[rows].astype(BF16), v32[rows].astype(BF16))
        mix_ref[:, _head(h, 1)] = _ln_gate(ob_ref[...], bg_ref[:, hs], bb_ref[:, hs],
                                           proj_ref[:, _head(h, 7)]).astype(BF16)


def _sample_rec(x, w_in, sa, sb, lb_logits, a_g, b_g, b_b, cos, sin, seq_len):
    n_tok = x.shape[0]
    n_seq = n_tok // seq_len
    bs = SAMPLE_SEQS
    rows = bs * seq_len
    const = lambda i: (0, 0)
    state_spec = pl.BlockSpec((bs, N_HEADS, HEAD_DIM, HEAD_DIM), lambda i: (i, 0, 0, 0))
    state_shape = jax.ShapeDtypeStruct((n_seq, N_HEADS, HEAD_DIM, HEAD_DIM), F32)
    return pl.pallas_call(
        functools.partial(_sample_rec_kernel, seq_len=seq_len),
        grid=(n_seq // bs,),
        in_specs=[
            pl.BlockSpec((rows, D_MODEL), lambda i: (i, 0)),
            pl.BlockSpec((D_MODEL, IN_COLS), const, pipeline_mode=pl.Buffered(1)),
            state_spec,
            state_spec,
            pl.BlockSpec(lb_logits.shape, const),
            pl.BlockSpec((1, GROUP_W), const),
            pl.BlockSpec((1, GROUP_W), const),
            pl.BlockSpec((1, GROUP_W), const),
            pl.BlockSpec((seq_len, HEAD_DIM), const),
            pl.BlockSpec((seq_len, HEAD_DIM), const),
        ],
        out_specs=[
            pl.BlockSpec((rows, 2 * GROUP_W), lambda i: (i, 0)),
            state_spec,
            state_spec,
        ],
        out_shape=[
            jax.ShapeDtypeStruct((n_tok, 2 * GROUP_W), BF16),
            state_shape,
            state_shape,
        ],
        scratch_shapes=[
            pltpu.VMEM((rows, IN_COLS), F32),
            pltpu.VMEM((rows, GROUP_W), F32),
            pltpu.VMEM((rows, HEAD_DIM), F32),
        ],
        compiler_params=pltpu.CompilerParams(
            dimension_semantics=("arbitrary",), vmem_limit_bytes=V7X_VMEM_LIMIT_BYTES),
        name="sample_recurrence",
    )(x, w_in, sa, sb, lb_logits, a_g, b_g, b_b, cos, sin)


_TAIL_PHASE_STEPS = (D_MODEL // DOWN_CHUNK, D_FF // FF_CHUNK, D_MODEL // DOWN_CHUNK, D_MODEL // DOWN_CHUNK)
_TAIL_PHASE_START = tuple(sum(_TAIL_PHASE_STEPS[:i]) for i in range(4))


def _sample_tail_kernel(x_ref, mix_ref, p_ref, wo_ref, wg_ref, wu_ref, wd_ref, wpg_ref, wpp_ref,
                        ln1g_ref, ln1b_ref, ln2g_ref, ln2b_ref, bpg_ref,
                        y_ref, wo_b, wg_b, wu_b, wd_b, wpg_b, wpp_b, act_ref, h_ref, hb_ref):
    s = pl.program_id(0)
    a0, b0, c0, d0 = _TAIL_PHASE_START
    blk = DOWN_CHUNK

    @pl.when(s < b0)
    def _():
        wo_b[...] = wo_ref[...].astype(BF16)
        cols = pl.ds(pl.multiple_of((s - a0) * blk, blk), blk)
        h_ref[:, cols] = DN_ALPHA * x_ref[...] + _dot(mix_ref[...], wo_b[...])

        @pl.when(s == b0 - 1)
        def _():
            h = _layer_norm(h_ref[...], ln1g_ref[...], ln1b_ref[...])
            h_ref[...] = h
            hb_ref[...] = h.astype(BF16)

    @pl.when((s >= b0) & (s < c0))
    def _():
        wg_b[...] = wg_ref[...].astype(BF16)
        wu_b[...] = wu_ref[...].astype(BF16)
        cols = pl.ds(pl.multiple_of((s - b0) * FF_CHUNK, FF_CHUNK), FF_CHUNK)
        hb = hb_ref[...]
        act_ref[:, cols] = (_silu(_dot(hb, wg_b[...])) * _dot(hb, wu_b[...])).astype(BF16)

    @pl.when((s >= c0) & (s < d0))
    def _():
        wd_b[...] = wd_ref[...].astype(BF16)
        cols = pl.ds(pl.multiple_of((s - c0) * blk, blk), blk)
        h_ref[:, cols] = DN_ALPHA * h_ref[:, cols] + _dot(act_ref[...], wd_b[...])

        @pl.when(s == d0 - 1)
        def _():
            h2 = _layer_norm(h_ref[...], ln2g_ref[...], ln2b_ref[...])
            h_ref[...] = h2
            hb_ref[...] = h2.astype(BF16)

    @pl.when(s >= d0)
    def _():
        wpg_b[...] = wpg_ref[...].astype(BF16)
        wpp_b[...] = wpp_ref[...].astype(BF16)
        cols = pl.ds(pl.multiple_of((s - d0) * blk, blk), blk)
        gate = _sigmoid(_dot(hb_ref[...], wpg_b[...]) + bpg_ref[:, cols])
        y_ref[...] = h_ref[:, cols] + gate * _dot(p_ref[...].astype(BF16), wpp_b[...])


def _sample_tail(x, mix, p, w_out, ln1g, ln1b, wg, wu, wd, ln2g, ln2b, wpp, wpg, bpg):
    n = x.shape[0]
    a0, b0, c0, d0 = _TAIL_PHASE_START
    na, nb, nc, nd = _TAIL_PHASE_STEPS
    const = lambda s: (0, 0)
    col = lambda start, count: (lambda s: (0, jnp.clip(s - start, 0, count - 1)))
    vec = pl.BlockSpec((1, D_MODEL), const)
    weight_specs = [
        pl.BlockSpec((2 * GROUP_W, DOWN_CHUNK), col(a0, na)),
        pl.BlockSpec((D_MODEL, FF_CHUNK), col(b0, nb)),
        pl.BlockSpec((D_MODEL, FF_CHUNK), col(b0, nb)),
        pl.BlockSpec((D_FF, DOWN_CHUNK), col(c0, nc)),
        pl.BlockSpec((D_MODEL, DOWN_CHUNK), col(d0, nd)),
        pl.BlockSpec((PLE_DIM, DOWN_CHUNK), col(d0, nd)),
    ]
    weights = (w_out, wg, wu, wd, wpg, wpp)
    return pl.pallas_call(
        _sample_tail_kernel,
        grid=(sum(_TAIL_PHASE_STEPS),),
        in_specs=[
            pl.BlockSpec((n, DOWN_CHUNK), col(a0, na)),
            pl.BlockSpec((n, 2 * GROUP_W), const),
            pl.BlockSpec((n, PLE_DIM), const),
        ] + weight_specs + [vec, vec, vec, vec, vec],
        out_specs=[pl.BlockSpec((n, DOWN_CHUNK), col(d0, nd))] + weight_specs,
        out_shape=[jax.ShapeDtypeStruct((n, D_MODEL), F32)]
        + [jax.ShapeDtypeStruct(w.shape, BF16) for w in weights],
        scratch_shapes=_tail_scratch(n),
        compiler_params=pltpu.CompilerParams(
            dimension_semantics=("arbitrary",), vmem_limit_bytes=V7X_VMEM_LIMIT_BYTES),
        name="sample_tail",
    )(x, mix, p, *weights, ln1g, ln1b, ln2g, ln2b, bpg)


def kernel(x_prompt, x_sample, p_prompt, p_sample, state_hgrn, state_ret, lb_logits, w_in, a_norm_g, b_norm_g, b_norm_b, w_out, ln1_g, ln1_b, w_ffn_gate, w_ffn_up, w_ffn_down, ln2_g, ln2_b, w_ple_proj, w_ple_gate, b_ple_gate):
    assert w_in.shape[0] == DEPTH == 1
    bsz, seq, _ = x_prompt.shape
    n_dec, dec_seq, _ = x_sample.shape

    mixer_vecs = (lb_logits, a_norm_g, b_norm_g, b_norm_b)
    cos_p, sin_p = _rope_tables(seq, 0)
    cos_s, sin_s = _rope_tables(dec_seq, PAST_LEN)

    x_s = x_sample.reshape(n_dec * dec_seq, D_MODEL)
    w_in_b = _w_in_bf16(w_in[0])
    mix_s, sa_s, sb_s = _sample_rec(x_s, w_in_b, state_hgrn[0], state_ret[0], *mixer_vecs,
                                    cos_s, sin_s, dec_seq)
    y_s, w_out_b, wg_b, wu_b, wd_b, wpg_b, wpp_b = _sample_tail(
        x_s, mix_s, p_sample[0].reshape(n_dec * dec_seq, PLE_DIM), w_out[0], ln1_g, ln1_b,
        w_ffn_gate[0], w_ffn_up[0], w_ffn_down[0], ln2_g, ln2_b, w_ple_proj[0], w_ple_gate[0],
        b_ple_gate)

    tail_w = (w_out_b, ln1_g, ln1_b, wg_b, wu_b, wd_b, ln2_g, ln2_b, wpp_b, wpg_b, b_ple_gate)
    y_p, sa_p, sb_p = _prompt_layer(x_prompt, p_prompt[0], cos_p, sin_p, w_in_b, *mixer_vecs, tail_w)

    return (y_p.reshape(bsz, seq, D_MODEL), y_s.reshape(n_dec, dec_seq, D_MODEL),
            sa_p[None], sb_p[None], sa_s[None], sb_s[None])
```

```python
import functools
import math

import jax
import jax.numpy as jnp
from jax import lax
from jax.experimental import pallas as pl
from jax.experimental.pallas import tpu as pltpu

F32 = jnp.float32
BF16 = jnp.bfloat16

D_MODEL = 1024
N_HEADS = 4
HEAD_DIM = 128
GROUP_W = N_HEADS * HEAD_DIM
IN_COLS = 8 * GROUP_W
D_FF = 2816
PLE_DIM = 256
DEPTH = 1
PAST_LEN = 16384
REF_CHUNK = 32
ROPE_BASE = 10000.0
NORM_EPS = 1e-5
DN_ALPHA = (2.0 * DEPTH) ** 0.25
RET_LOG_DECAY = tuple(math.log1p(-(2.0 ** (-5.0 - h))) for h in range(N_HEADS))
K_SCALE = HEAD_DIM ** -0.5

V7X_VMEM_LIMIT_BYTES = 60 * 1024 * 1024

TOKEN_TILE = 256
SAMPLE_PROJ_COLS = 1024
SAMPLE_SEQS = 16
FF_CHUNK = 256
DOWN_CHUNK = 256


def _dot(a, b):
    return jnp.dot(a, b, preferred_element_type=F32)


def _dot_nt(a, b):
    return lax.dot_general(a, b, (((1,), (1,)), ((), ())), preferred_element_type=F32)


def _dot_tn(a, b):
    return lax.dot_general(a, b, (((0,), (0,)), ((), ())), preferred_element_type=F32)


def _split3(x):
    hi = x.astype(BF16)
    r1 = x - hi.astype(F32)
    mid = r1.astype(BF16)
    lo = (r1 - mid.astype(F32)).astype(BF16)
    return hi, mid, lo


def _dot_exact_lhs01(m01, parts):
    return _dot(jnp.concatenate([m01] * 3, axis=1), jnp.concatenate(list(parts), axis=0))


def _sigmoid(x):
    return 1.0 / (1.0 + jnp.exp(-x))


def _silu(x):
    return x * _sigmoid(x)


def _causal_in_chunk(n, shift):
    r = lax.broadcasted_iota(jnp.int32, (n, n), 0)
    c = lax.broadcasted_iota(jnp.int32, (n, n), 1)
    return ((r >> shift) == (c >> shift)) & (c <= r)


def _lower_bound(lb_ref):
    rows = [lb_ref[i:i + 1, :] for i in range(lb_ref.shape[0])]
    m = functools.reduce(jnp.maximum, rows)
    e = [jnp.exp(r - m) for r in rows]
    return e[0] / functools.reduce(jnp.add, e)


def _hgrn_prepass(proj_ref, lb, causal):
    tri = jnp.where(causal, 1.0, 0.0).astype(BF16)
    f = lb + (1.0 - lb) * _sigmoid(proj_ref[:, GROUP_W:2 * GROUP_W])
    kk = 1.0 - f
    b = _dot_exact_lhs01(tri, _split3(jnp.log(f)))
    q_dec = (_silu(proj_ref[:, 0:GROUP_W]) * jnp.exp(b)).astype(BF16)
    k_dec = kk * jnp.exp(-b)
    return q_dec, k_dec, kk, b


def _rope(x, cos, sin_signed):
    return x * cos + pltpu.roll(x, HEAD_DIM // 2, axis=1) * sin_signed


def _rms_gate(o, g, gate):
    return o * lax.rsqrt(jnp.mean(o * o, axis=-1, keepdims=True) + NORM_EPS) * g * _silu(gate)


def _layer_norm(x, g, b):
    mu = jnp.mean(x, axis=-1, keepdims=True)
    d = x - mu
    var = jnp.mean(d * d, axis=-1, keepdims=True)
    return d * lax.rsqrt(var + NORM_EPS) * g + b


def _ln_gate(o, g, b, gate):
    return _layer_norm(o, g, b) * _silu(gate)


def _head(h, group=0):
    return slice(group * GROUP_W + h * HEAD_DIM, group * GROUP_W + (h + 1) * HEAD_DIM)


def _rope_table_kernel(cos_ref, sin_ref, *, offset):
    n = cos_ref.shape[0]
    half = HEAD_DIM // 2
    packed = n % 16 == 0
    m = n // 2 if packed else n
    row = lax.broadcasted_iota(jnp.int32, (m, HEAD_DIM), 0) + pl.program_id(0) * n
    lane = lax.broadcasted_iota(jnp.int32, (m, HEAD_DIM), 1)
    low = lane < half
    if packed:
        row = row + jnp.where(low, 0, m)
    j = (lane & (half - 1)).astype(F32)
    inv = jnp.exp(-(j / half) * math.log(ROPE_BASE))
    ang = (row.astype(F32) + offset) * inv
    c = jnp.cos(ang)
    s = jnp.sin(ang)
    if not packed:
        cos_ref[...] = c
        sin_ref[...] = jnp.where(low, -s, s)
        return
    c_sw = pltpu.roll(c, half, axis=1)
    s_sw = pltpu.roll(s, half, axis=1)
    cos_ref[0:m, :] = jnp.where(low, c, c_sw)
    cos_ref[m:n, :] = jnp.where(low, c_sw, c)
    sin_ref[0:m, :] = jnp.where(low, -s, s_sw)
    sin_ref[m:n, :] = jnp.where(low, -s_sw, s)


def _rope_tables(n, offset):
    tile = min(n, 512)
    return pl.pallas_call(
        functools.partial(_rope_table_kernel, offset=float(offset)),
        grid=(n // tile,),
        in_specs=[],
        out_specs=[pl.BlockSpec((tile, HEAD_DIM), lambda i: (i, 0))] * 2,
        out_shape=[jax.ShapeDtypeStruct((n, HEAD_DIM), F32)] * 2,
        name="rope_tables",
    )()


def _tail_steps(x_ref, mix_ref, p_ref, y_ref, w_out_ref, ln1g_ref, ln1b_ref, wg_ref, wu_ref, wd_ref,
                ln2g_ref, ln2b_ref, wpp_ref, wpg_ref, bpg_ref, act_ref, h_ref, hb_ref):
    def out_proj():
        for c in range(D_MODEL // DOWN_CHUNK):
            cols = slice(c * DOWN_CHUNK, (c + 1) * DOWN_CHUNK)
            h_ref[:, cols] = DN_ALPHA * x_ref[:, cols] + _dot(mix_ref[...], w_out_ref[:, cols])
        h = _layer_norm(h_ref[...], ln1g_ref[...], ln1b_ref[...])
        h_ref[...] = h
        hb_ref[...] = h.astype(BF16)

    def ff(c):
        cols = slice(c * FF_CHUNK, (c + 1) * FF_CHUNK)
        hb = hb_ref[...]
        act_ref[:, cols] = (_silu(_dot(hb, wg_ref[:, cols])) * _dot(hb, wu_ref[:, cols])).astype(BF16)

    def down(c):
        cols = slice(c * DOWN_CHUNK, (c + 1) * DOWN_CHUNK)
        h_ref[:, cols] = DN_ALPHA * h_ref[:, cols] + _dot(act_ref[...], wd_ref[:, cols])

    def norm2():
        h2 = _layer_norm(h_ref[...], ln2g_ref[...], ln2b_ref[...])
        h_ref[...] = h2
        hb_ref[...] = h2.astype(BF16)

    def ple(c):
        cols = slice(c * DOWN_CHUNK, (c + 1) * DOWN_CHUNK)
        gate = _sigmoid(_dot(hb_ref[...], wpg_ref[:, cols]) + bpg_ref[:, cols])
        y_ref[:, cols] = h_ref[:, cols] + gate * _dot(p_ref[...].astype(BF16), wpp_ref[:, cols])

    return (out_proj,
            [functools.partial(ff, c) for c in range(D_FF // FF_CHUNK)],
            [functools.partial(down, c) for c in range(D_MODEL // DOWN_CHUNK)],
            [norm2] + [functools.partial(ple, c) for c in range(D_MODEL // DOWN_CHUNK)])


def _tail_scratch(tl):
    return [pltpu.VMEM((tl, D_FF), BF16), pltpu.VMEM((tl, D_MODEL), F32), pltpu.VMEM((tl, D_MODEL), BF16)]


_TAIL_MATRIX_POS = (0, 3, 4, 5, 8, 9)
_TAIL_MATRIX_SHAPES = ((2 * GROUP_W, D_MODEL), (D_MODEL, D_FF), (D_MODEL, D_FF), (D_FF, D_MODEL),
                       (PLE_DIM, D_MODEL), (D_MODEL, D_MODEL))


def _tail_specs():
    const = lambda i: (0, 0)
    in_hbm = pl.BlockSpec(memory_space=pl.ANY)
    vec = pl.BlockSpec((1, D_MODEL), const)
    return [in_hbm, vec, vec, in_hbm, in_hbm, in_hbm, vec, vec, in_hbm, in_hbm, vec]


def _tail_matrix_scratch():
    return ([pltpu.VMEM(shape, BF16) for shape in _TAIL_MATRIX_SHAPES]
            + [pltpu.SemaphoreType.DMA((len(_TAIL_MATRIX_SHAPES),))])


def _in_proj_steps(x_ref, w_in_ref, proj_ref, xb_ref):
    def in_proj(c):
        if c == 0:
            xb_ref[...] = x_ref[...].astype(BF16)
        cols = slice(c * GROUP_W, (c + 1) * GROUP_W)
        proj_ref[:, cols] = _dot(xb_ref[...], w_in_ref[:, cols])

    return [functools.partial(in_proj, c) for c in range(IN_COLS // GROUP_W)]


def _ret_tables(dm_ref, rd_ref):
    tl = dm_ref.shape[1]
    r = lax.broadcasted_iota(jnp.int32, (tl, tl), 0)
    c = lax.broadcasted_iota(jnp.int32, (tl, tl), 1)
    row = lax.broadcasted_iota(jnp.int32, (tl, HEAD_DIM), 0).astype(F32)
    for h in range(N_HEADS):
        logd = RET_LOG_DECAY[h]
        dm_ref[h] = jnp.where(r >= c, jnp.exp((r - c).astype(F32) * logd), 0.0)
        rd_ref[0, h] = jnp.exp((row + 1.0) * logd)
        rd_ref[1, h] = jnp.exp((tl - 1.0 - row) * logd)


def _prompt_mixer_steps(cos_ref, sin_ref, lb_ref, ag_ref, bg_ref, bb_ref,
                        sa_ref, sb_ref, proj_ref, st_ref, oa_ref, mix_ref,
                        qd_ref, kd_ref, ke_ref, dec_ref, va_ref, kv_ref, sbf_ref, dm_ref, rd_ref, first):
    tl = proj_ref.shape[0]
    shift = REF_CHUNK.bit_length() - 1
    n_chunks = tl // REF_CHUNK

    def prepass():
        q_dec, k_dec, kk, b = _hgrn_prepass(proj_ref, _lower_bound(lb_ref), _causal_in_chunk(tl, shift))
        qd_ref[...] = q_dec
        for h in range(N_HEADS):
            kd_ref[h] = k_dec[:, _head(h)].T.astype(BF16)
        va_ref[...] = proj_ref[:, 2 * GROUP_W:3 * GROUP_W].astype(BF16)
        last = [b[(n + 1) * REF_CHUNK - 1:(n + 1) * REF_CHUNK, :] for n in range(n_chunks)]
        b_last = jnp.concatenate([jnp.broadcast_to(r, (REF_CHUNK, GROUP_W)) for r in last], axis=0)
        k_end = kk * jnp.exp(b_last - b)
        for n in range(n_chunks):
            dec_ref[n:n + 1, :] = jnp.exp(last[n])
        odd = ((lax.broadcasted_iota(jnp.int32, (tl, GROUP_W), 0) >> shift) & 1) == 1
        k_even = jnp.where(odd, 0.0, k_end).astype(BF16)
        k_odd = jnp.where(odd, k_end, 0.0).astype(BF16)
        for h in range(N_HEADS):
            ke_ref[:, 2 * h * HEAD_DIM:(2 * h + 1) * HEAD_DIM] = k_even[:, _head(h)]
            ke_ref[:, (2 * h + 1) * HEAD_DIM:(2 * h + 2) * HEAD_DIM] = k_odd[:, _head(h)]

    def kv_scan(h):
        hs = _head(h)
        pair = 2 * REF_CHUNK
        for r in range(n_chunks // 2):
            rows = slice(r * pair, (r + 1) * pair)
            kv_ref[r] = _dot_tn(va_ref[rows, hs], ke_ref[rows, 2 * h * HEAD_DIM:(2 * h + 2) * HEAD_DIM])
        st = jnp.where(first, 0.0, st_ref[h])
        for n in range(n_chunks):
            sbf_ref[n, h] = st.T.astype(BF16)
            st = st * dec_ref[n:n + 1, hs] + kv_ref[n // 2, :, (n % 2) * HEAD_DIM:(n % 2 + 1) * HEAD_DIM]
        st_ref[h] = st
        sa_ref[0, h] = st.T

    def diag(h):
        hs = _head(h)
        sc = jnp.where(_causal_in_chunk(tl, shift), _dot(qd_ref[:, hs], kd_ref[h]), 0.0)
        oa_ref[:, hs] = _dot(sc.astype(BF16), va_ref[:, hs])

    def inter(n):
        rows = slice(n * REF_CHUNK, (n + 1) * REF_CHUNK)
        for h in range(N_HEADS):
            hs = _head(h)
            oa_ref[rows, hs] += _dot(qd_ref[rows, hs], sbf_ref[n, h])

    def hgrn_out():
        for h in range(N_HEADS):
            hs = _head(h)
            mix_ref[:, hs] = _rms_gate(oa_ref[:, hs], ag_ref[:, hs], proj_ref[:, _head(h, 3)]).astype(BF16)

    def ret(h):
        hs = _head(h)
        cos = cos_ref[...]
        sin = sin_ref[...]
        q = _rope(proj_ref[:, _head(h, 4)], cos, sin)
        k = _rope(proj_ref[:, _head(h, 5)], cos, sin) * K_SCALE
        v = proj_ref[:, _head(h, 6)].astype(BF16)
        a = (_dot(q.astype(BF16), k.T.astype(BF16)) * dm_ref[h]).astype(BF16)
        s = jnp.where(first, 0.0, sb_ref[0, h])
        o = _dot(a, v) + _dot((q * rd_ref[0, h]).astype(BF16), s.astype(BF16))
        k_end_b = (k * rd_ref[1, h]).astype(BF16)
        sb_ref[0, h] = s * math.exp(tl * RET_LOG_DECAY[h]) + _dot_tn(k_end_b, v)
        mix_ref[:, _head(h, 1)] = _ln_gate(o, bg_ref[:, hs], bb_ref[:, hs],
                                           proj_ref[:, _head(h, 7)]).astype(BF16)

    return (prepass,
            [functools.partial(kv_scan, h) for h in range(N_HEADS)],
            [functools.partial(diag, h) for h in range(N_HEADS)],
            [functools.partial(inter, n) for n in range(n_chunks)],
            hgrn_out,
            [functools.partial(ret, h) for h in range(N_HEADS)])


def _interleave(a, b):
    out = []
    for i in range(max(len(a), len(b))):
        out += a[i:i + 1] + b[i:i + 1]
    return out


def _prompt_layer_kernel(xn_ref, xp_ref, p_ref, cos_ref, sin_ref, w_in_ref, lb_ref, ag_ref, bg_ref,
                         bb_ref, *rest, tiles_per_seq):
    tail_in = rest[:11]
    y_ref, sa_ref, sb_ref = rest[11:14]
    mixer_scratch = rest[14:27]
    proj_ref, mix_ref, dm_ref, rd_ref = mixer_scratch[0], mixer_scratch[3], mixer_scratch[11], mixer_scratch[12]
    xb_ref = rest[27]
    tail_scratch = rest[28:31]
    w_vmem, w_sem = rest[31:-1], rest[-1]
    tail_w = list(tail_in)
    for k, pos in enumerate(_TAIL_MATRIX_POS):
        tail_w[pos] = w_vmem[k]

    def weight_copies():
        return [pltpu.make_async_copy(tail_in[pos], w_vmem[k], w_sem.at[k])
                for k, pos in enumerate(_TAIL_MATRIX_POS)]
    g = pl.program_id(0)
    n_tiles = pl.num_programs(0) - 1
    slot = lax.rem(g, 2)

    def mixer_steps():
        return _prompt_mixer_steps(cos_ref, sin_ref, lb_ref, ag_ref, bg_ref, bb_ref, sa_ref, sb_ref,
                                   proj_ref.at[slot], *mixer_scratch[1:],
                                   first=lax.rem(g, tiles_per_seq) == 0)

    def tail_steps():
        return _tail_steps(xp_ref, mix_ref, p_ref, y_ref, *tail_w, *tail_scratch)

    @pl.when(g == 0)
    def _():
        for copy in weight_copies():
            copy.start()
        _ret_tables(dm_ref, rd_ref)
        prepass, kv_scan, diag, inter, hgrn_out, ret = mixer_steps()
        steps = _in_proj_steps(xp_ref, w_in_ref, proj_ref.at[slot], xb_ref)
        steps += [prepass] + kv_scan + diag + inter + [hgrn_out] + ret
        steps += _in_proj_steps(xn_ref, w_in_ref, proj_ref.at[1 - slot], xb_ref)
        for step in steps:
            step()

    @pl.when((g > 0) & (g < n_tiles))
    def _():
        @pl.when(g == 1)
        def _():
            for copy in weight_copies():
                copy.wait()

        prepass, kv_scan, diag, inter, hgrn_out, ret = mixer_steps()
        out_proj, ff, down, final = tail_steps()
        in_proj = _in_proj_steps(xn_ref, w_in_ref, proj_ref.at[1 - slot], xb_ref)
        inter_pairs = [lambda a=a, b=b: (a(), b()) for a, b in zip(inter[0::2], inter[1::2])]
        steps = [out_proj, in_proj[0], prepass, in_proj[1]]
        steps += _interleave(ff + down, kv_scan + diag + ret + inter_pairs + [hgrn_out])
        steps += _interleave(final, in_proj[2:])
        for step in steps:
            step()

    @pl.when(g == n_tiles)
    def _():
        out_proj, ff, down, final = tail_steps()
        for step in [out_proj] + ff + down + final:
            step()


def _prompt_layer(x, p, cos, sin, w_in, lb_logits, a_g, b_g, b_b, tail_w):
    bsz, seq, _ = x.shape
    tl = TOKEN_TILE
    tps = seq // tl
    n_tiles = bsz * tps
    x2 = x.reshape(bsz * seq, D_MODEL)
    p2 = p.reshape(bsz * seq, PLE_DIM)
    const = lambda g: (0, 0)
    nxt = lambda g: (jnp.minimum(g + 1, n_tiles - 1), 0)
    prev = lambda g: (jnp.maximum(g - 1, 0), 0)
    seq_tile = lambda g: (lax.rem(jnp.minimum(g, n_tiles - 1), tps), 0)
    state_spec = pl.BlockSpec((1, N_HEADS, HEAD_DIM, HEAD_DIM),
                              lambda g: (jnp.minimum(g, n_tiles - 1) // tps, 0, 0, 0))
    state_shape = jax.ShapeDtypeStruct((bsz, N_HEADS, HEAD_DIM, HEAD_DIM), F32)
    return pl.pallas_call(
        functools.partial(_prompt_layer_kernel, tiles_per_seq=tps),
        grid=(n_tiles + 1,),
        in_specs=[
            pl.BlockSpec((tl, D_MODEL), nxt),
            pl.BlockSpec((tl, D_MODEL), prev),
            pl.BlockSpec((tl, PLE_DIM), prev),
            pl.BlockSpec((tl, HEAD_DIM), seq_tile),
            pl.BlockSpec((tl, HEAD_DIM), seq_tile),
            pl.BlockSpec((D_MODEL, IN_COLS), const, pipeline_mode=pl.Buffered(1)),
            pl.BlockSpec(lb_logits.shape, const),
            pl.BlockSpec((1, GROUP_W), const),
            pl.BlockSpec((1, GROUP_W), const),
            pl.BlockSpec((1, GROUP_W), const),
        ] + _tail_specs(),
        out_specs=[
            pl.BlockSpec((tl, D_MODEL), prev),
            state_spec,
            state_spec,
        ],
        out_shape=[
            jax.ShapeDtypeStruct((bsz * seq, D_MODEL), F32),
            state_shape,
            state_shape,
        ],
        scratch_shapes=[
            pltpu.VMEM((2, tl, IN_COLS), F32),
            pltpu.VMEM((N_HEADS, HEAD_DIM, HEAD_DIM), F32),
            pltpu.VMEM((tl, GROUP_W), F32),
            pltpu.VMEM((tl, 2 * GROUP_W), BF16),
            pltpu.VMEM((tl, GROUP_W), BF16),
            pltpu.VMEM((N_HEADS, HEAD_DIM, tl), BF16),
            pltpu.VMEM((tl, 2 * GROUP_W), BF16),
            pltpu.VMEM((tl // REF_CHUNK, GROUP_W), F32),
            pltpu.VMEM((tl, GROUP_W), BF16),
            pltpu.VMEM((tl // (2 * REF_CHUNK), HEAD_DIM, 2 * HEAD_DIM), F32),
            pltpu.VMEM((tl // REF_CHUNK, N_HEADS, HEAD_DIM, HEAD_DIM), BF16),
            pltpu.VMEM((N_HEADS, tl, tl), F32),
            pltpu.VMEM((2, N_HEADS, tl, HEAD_DIM), F32),
            pltpu.VMEM((tl, D_MODEL), BF16),
        ] + _tail_scratch(tl) + _tail_matrix_scratch(),
        compiler_params=pltpu.CompilerParams(
            dimension_semantics=("arbitrary",), vmem_limit_bytes=V7X_VMEM_LIMIT_BYTES),
        name="prompt_layer",
    )(x2, x2, p2, cos, sin, w_in, lb_logits, a_g, b_g, b_b, *tail_w)


def _cast_kernel(w_ref, wb_ref):
    wb_ref[...] = w_ref[...].astype(BF16)


def _w_in_bf16(w_in):
    tn = SAMPLE_PROJ_COLS
    return pl.pallas_call(
        _cast_kernel,
        grid=(IN_COLS // tn,),
        in_specs=[pl.BlockSpec((D_MODEL, tn), lambda c: (0, c))],
        out_specs=pl.BlockSpec((D_MODEL, tn), lambda c: (0, c)),
        out_shape=jax.ShapeDtypeStruct((D_MODEL, IN_COLS), BF16),
        compiler_params=pltpu.CompilerParams(
            dimension_semantics=("arbitrary",), vmem_limit_bytes=V7X_VMEM_LIMIT_BYTES),
        name="w_in_cast",
    )(w_in)


def _sample_rec_kernel(x_ref, w_in_ref, sa_in_ref, sb_in_ref, lb_ref, ag_ref, bg_ref, bb_ref,
                       cos_ref, sin_ref, mix_ref, sa_ref, sb_ref, proj_ref, oa_ref, ob_ref, *, seq_len):
    rows_n = proj_ref.shape[0]
    n_seq = rows_n // seq_len
    causal = _causal_in_chunk(rows_n, seq_len.bit_length() - 1)

    xb = x_ref[...].astype(BF16)
    for c in range(IN_COLS // GROUP_W):
        cols = slice(c * GROUP_W, (c + 1) * GROUP_W)
        proj_ref[:, cols] = _dot(xb, w_in_ref[:, cols])

    q_dec, k_dec, kk, b = _hgrn_prepass(proj_ref, _lower_bound(lb_ref), causal)
    v_a = proj_ref[:, 2 * GROUP_W:3 * GROUP_W]
    for h in range(N_HEADS):
        hs = _head(h)
        sc = jnp.where(causal, _dot_nt(q_dec[:, hs], k_dec[:, hs].astype(BF16)), 0.0).astype(BF16)
        oa_ref[:, hs] = _dot(sc, v_a[:, hs].astype(BF16))
    q_dec32 = q_dec.astype(F32)
    rr = lax.broadcasted_iota(jnp.int32, (seq_len, GROUP_W), 0)
    ones_blk = jnp.ones((seq_len, HEAD_DIM), BF16)
    for s in range(n_seq):
        rows = slice(s * seq_len, (s + 1) * seq_len)
        b_last = b[(s + 1) * seq_len - 1:(s + 1) * seq_len, :]
        k_end = (kk[rows] * jnp.exp(b_last - b[rows])).astype(BF16)
        hi, mid, lo = [t.astype(F32) for t in _split3(jnp.exp(b_last))]
        dec_rows = jnp.where(rr == 0, hi, jnp.where(rr == 1, mid, jnp.where(rr == 2, lo, 0.0)))
        dec_rows = dec_rows.astype(BF16)
        for h in range(N_HEADS):
            hs = _head(h)
            st = sa_in_ref[s, h]
            oa_ref[rows, hs] += _dot(q_dec32[rows, hs].astype(BF16), st.astype(BF16))
            dec_kv = _dot_tn(dec_rows[:, hs], ones_blk)
            sa_ref[s, h] = st * dec_kv + _dot_tn(k_end[:, hs], v_a[rows, hs].astype(BF16))
    for h in range(N_HEADS):
        hs = _head(h)
        mix_ref[:, hs] = _rms_gate(oa_ref[:, hs], ag_ref[:, hs], proj_ref[:, _head(h, 3)]).astype(BF16)

    cos = jnp.concatenate([cos_ref[...]] * n_seq, axis=0)
    sin = jnp.concatenate([sin_ref[...]] * n_seq, axis=0)
    r = lax.broadcasted_iota(jnp.int32, (rows_n, rows_n), 0)
    c = lax.broadcasted_iota(jnp.int32, (rows_n, rows_n), 1)
    diff = ((r & (seq_len - 1)) - (c & (seq_len - 1))).astype(F32)
    row = (lax.broadcasted_iota(jnp.int32, (rows_n, HEAD_DIM), 0) & (seq_len - 1)).astype(F32)
    for h in range(N_HEADS):
        hs = _head(h)
        logd = RET_LOG_DECAY[h]
        q = _rope(proj_ref[:, _head(h, 4)], cos, sin)
        k = _rope(proj_ref[:, _head(h, 5)], cos, sin) * K_SCALE
        v32 = proj_ref[:, _head(h, 6)]
        dmask = jnp.where(causal, jnp.exp(diff * logd), 0.0)
        a = (_dot_nt(q.astype(BF16), k.astype(BF16)) * dmask).astype(BF16)
        ob_ref[...] = _dot(a, v32.astype(BF16))
        q_dec_b = q * jnp.exp((row + 1.0) * logd)
        k_end_b = k * jnp.exp((seq_len - 1.0 - row) * logd)
        for s in range(n_seq):
            rows = slice(s * seq_len, (s + 1) * seq_len)
            st = sb_in_ref[s, h]
            ob_ref[rows, :] += _dot(q_dec_b[rows].astype(BF16), st.astype(BF16))
            sb_ref[s, h] = st * math.exp(seq_len * logd) + _dot_tn(
                k_end_b[rows].astype(BF16), v32[rows].astype(BF16))
        mix_ref[:, _head(h, 1)] = _ln_gate(ob_ref[...], bg_ref[:, hs], bb_ref[:, hs],
                                           proj_ref[:, _head(h, 7)]).astype(BF16)


def _sample_rec(x, w_in, sa, sb, lb_logits, a_g, b_g, b_b, cos, sin, seq_len):
    n_tok = x.shape[0]
    n_seq = n_tok // seq_len
    bs = SAMPLE_SEQS
    rows = bs * seq_len
    const = lambda i: (0, 0)
    state_spec = pl.BlockSpec((bs, N_HEADS, HEAD_DIM, HEAD_DIM), lambda i: (i, 0, 0, 0))
    state_shape = jax.ShapeDtypeStruct((n_seq, N_HEADS, HEAD_DIM, HEAD_DIM), F32)
    return pl.pallas_call(
        functools.partial(_sample_rec_kernel, seq_len=seq_len),
        grid=(n_seq // bs,),
        in_specs=[
            pl.BlockSpec((rows, D_MODEL), lambda i: (i, 0)),
            pl.BlockSpec((D_MODEL, IN_COLS), const, pipeline_mode=pl.Buffered(1)),
            state_spec,
            state_spec,
            pl.BlockSpec(lb_logits.shape, const),
            pl.BlockSpec((1, GROUP_W), const),
            pl.BlockSpec((1, GROUP_W), const),
            pl.BlockSpec((1, GROUP_W), const),
            pl.BlockSpec((seq_len, HEAD_DIM), const),
            pl.BlockSpec((seq_len, HEAD_DIM), const),
        ],
        out_specs=[
            pl.BlockSpec((rows, 2 * GROUP_W), lambda i: (i, 0)),
            state_spec,
            state_spec,
        ],
        out_shape=[
            jax.ShapeDtypeStruct((n_tok, 2 * GROUP_W), BF16),
            state_shape,
            state_shape,
        ],
        scratch_shapes=[
            pltpu.VMEM((rows, IN_COLS), F32),
            pltpu.VMEM((rows, GROUP_W), F32),
            pltpu.VMEM((rows, HEAD_DIM), F32),
        ],
        compiler_params=pltpu.CompilerParams(
            dimension_semantics=("arbitrary",), vmem_limit_bytes=V7X_VMEM_LIMIT_BYTES),
        name="sample_recurrence",
    )(x, w_in, sa, sb, lb_logits, a_g, b_g, b_b, cos, sin)


_TAIL_PHASE_STEPS = (D_MODEL // DOWN_CHUNK, D_FF // FF_CHUNK, D_MODEL // DOWN_CHUNK, D_MODEL // DOWN_CHUNK)
_TAIL_PHASE_START = tuple(sum(_TAIL_PHASE_STEPS[:i]) for i in range(4))


def _sample_tail_kernel(x_ref, mix_ref, p_ref, wo_ref, wg_ref, wu_ref, wd_ref, wpg_ref, wpp_ref,
                        ln1g_ref, ln1b_ref, ln2g_ref, ln2b_ref, bpg_ref,
                        y_ref, wo_b, wg_b, wu_b, wd_b, wpg_b, wpp_b, act_ref, h_ref, hb_ref):
    s = pl.program_id(0)
    a0, b0, c0, d0 = _TAIL_PHASE_START
    blk = DOWN_CHUNK

    @pl.when(s < b0)
    def _():
        wo_b[...] = wo_ref[...].astype(BF16)
        cols = pl.ds(pl.multiple_of((s - a0) * blk, blk), blk)
        h_ref[:, cols] = DN_ALPHA * x_ref[...] + _dot(mix_ref[...], wo_b[...])

        @pl.when(s == b0 - 1)
        def _():
            h = _layer_norm(h_ref[...], ln1g_ref[...], ln1b_ref[...])
            h_ref[...] = h
            hb_ref[...] = h.astype(BF16)

    @pl.when((s >= b0) & (s < c0))
    def _():
        wg_b[...] = wg_ref[...].astype(BF16)
        wu_b[...] = wu_ref[...].astype(BF16)
        cols = pl.ds(pl.multiple_of((s - b0) * FF_CHUNK, FF_CHUNK), FF_CHUNK)
        hb = hb_ref[...]
        act_ref[:, cols] = (_silu(_dot(hb, wg_b[...])) * _dot(hb, wu_b[...])).astype(BF16)

    @pl.when((s >= c0) & (s < d0))
    def _():
        wd_b[...] = wd_ref[...].astype(BF16)
        cols = pl.ds(pl.multiple_of((s - c0) * blk, blk), blk)
        h_ref[:, cols] = DN_ALPHA * h_ref[:, cols] + _dot(act_ref[...], wd_b[...])

        @pl.when(s == d0 - 1)
        def _():
            h2 = _layer_norm(h_ref[...], ln2g_ref[...], ln2b_ref[...])
            h_ref[...] = h2
            hb_ref[...] = h2.astype(BF16)

    @pl.when(s >= d0)
    def _():
        wpg_b[...] = wpg_ref[...].astype(BF16)
        wpp_b[...] = wpp_ref[...].astype(BF16)
        cols = pl.ds(pl.multiple_of((s - d0) * blk, blk), blk)
        gate = _sigmoid(_dot(hb_ref[...], wpg_b[...]) + bpg_ref[:, cols])
        y_ref[...] = h_ref[:, cols] + gate * _dot(p_ref[...].astype(BF16), wpp_b[...])


def _sample_tail(x, mix, p, w_out, ln1g, ln1b, wg, wu, wd, ln2g, ln2b, wpp, wpg, bpg):
    n = x.shape[0]
    a0, b0, c0, d0 = _TAIL_PHASE_START
    na, nb, nc, nd = _TAIL_PHASE_STEPS
    const = lambda s: (0, 0)
    col = lambda start, count: (lambda s: (0, jnp.clip(s - start, 0, count - 1)))
    vec = pl.BlockSpec((1, D_MODEL), const)
    weight_specs = [
        pl.BlockSpec((2 * GROUP_W, DOWN_CHUNK), col(a0, na)),
        pl.BlockSpec((D_MODEL, FF_CHUNK), col(b0, nb)),
        pl.BlockSpec((D_MODEL, FF_CHUNK), col(b0, nb)),
        pl.BlockSpec((D_FF, DOWN_CHUNK), col(c0, nc)),
        pl.BlockSpec((D_MODEL, DOWN_CHUNK), col(d0, nd)),
        pl.BlockSpec((PLE_DIM, DOWN_CHUNK), col(d0, nd)),
    ]
    weights = (w_out, wg, wu, wd, wpg, wpp)
    return pl.pallas_call(
        _sample_tail_kernel,
        grid=(sum(_TAIL_PHASE_STEPS),),
        in_specs=[
            pl.BlockSpec((n, DOWN_CHUNK), col(a0, na)),
            pl.BlockSpec((n, 2 * GROUP_W), const),
            pl.BlockSpec((n, PLE_DIM), const),
        ] + weight_specs + [vec, vec, vec, vec, vec],
        out_specs=[pl.BlockSpec((n, DOWN_CHUNK), col(d0, nd))] + weight_specs,
        out_shape=[jax.ShapeDtypeStruct((n, D_MODEL), F32)]
        + [jax.ShapeDtypeStruct(w.shape, BF16) for w in weights],
        scratch_shapes=_tail_scratch(n),
        compiler_params=pltpu.CompilerParams(
            dimension_semantics=("arbitrary",), vmem_limit_bytes=V7X_VMEM_LIMIT_BYTES),
        name="sample_tail",
    )(x, mix, p, *weights, ln1g, ln1b, ln2g, ln2b, bpg)


def kernel(x_prompt, x_sample, p_prompt, p_sample, state_hgrn, state_ret, lb_logits, w_in, a_norm_g, b_norm_g, b_norm_b, w_out, ln1_g, ln1_b, w_ffn_gate, w_ffn_up, w_ffn_down, ln2_g, ln2_b, w_ple_proj, w_ple_gate, b_ple_gate):
    assert w_in.shape[0] == DEPTH == 1
    bsz, seq, _ = x_prompt.shape
    n_dec, dec_seq, _ = x_sample.shape

    mixer_vecs = (lb_logits, a_norm_g, b_norm_g, b_norm_b)
    cos_p, sin_p = _rope_tables(seq, 0)
    cos_s, sin_s = _rope_tables(dec_seq, PAST_LEN)

    x_s = x_sample.reshape(n_dec * dec_seq, D_MODEL)
    w_in_b = _w_in_bf16(w_in[0])
    mix_s, sa_s, sb_s = _sample_rec(x_s, w_in_b, state_hgrn[0], state_ret[0], *mixer_vecs,
                                    cos_s, sin_s, dec_seq)
    y_s, w_out_b, wg_b, wu_b, wd_b, wpg_b, wpp_b = _sample_tail(
        x_s, mix_s, p_sample[0].reshape(n_dec * dec_seq, PLE_DIM), w_out[0], ln1_g, ln1_b,
        w_ffn_gate[0], w_ffn_up[0], w_ffn_down[0], ln2_g, ln2_b, w_ple_proj[0], w_ple_gate[0],
        b_ple_gate)

    tail_w = (w_out_b, ln1_g, ln1_b, wg_b, wu_b, wd_b, ln2_g, ln2_b, wpp_b, wpg_b, b_ple_gate)
    y_p, sa_p, sb_p = _prompt_layer(x_prompt, p_prompt[0], cos_p, sin_p, w_in_b, *mixer_vecs, tail_w)

    return (y_p.reshape(bsz, seq, D_MODEL), y_s.reshape(n_dec, dec_seq, D_MODEL),
            sa_p[None], sb_p[None], sa_s[None], sb_s[None])
```

```python
import functools
import math

import jax
import jax.numpy as jnp
from jax import lax
from jax.experimental import pallas as pl
from jax.experimental.pallas import tpu as pltpu

F32 = jnp.float32
BF16 = jnp.bfloat16

D_MODEL = 1024
N_HEADS = 4
HEAD_DIM = 128
GROUP_W = N_HEADS * HEAD_DIM
IN_COLS = 8 * GROUP_W
D_FF = 2816
PLE_DIM = 256
DEPTH = 1
PAST_LEN = 16384
REF_CHUNK = 32
ROPE_BASE = 10000.0
NORM_EPS = 1e-5
DN_ALPHA = (2.0 * DEPTH) ** 0.25
RET_LOG_DECAY = tuple(math.log1p(-(2.0 ** (-5.0 - h))) for h in range(N_HEADS))
K_SCALE = HEAD_DIM ** -0.5

V7X_VMEM_LIMIT_BYTES = 60 * 1024 * 1024

TOKEN_TILE = 256
SAMPLE_PROJ_COLS = 1024
SAMPLE_SEQS = 16
FF_CHUNK = 256
DOWN_CHUNK = 256


def _dot(a, b):
    return jnp.dot(a, b, preferred_element_type=F32)


def _dot_nt(a, b):
    return lax.dot_general(a, b, (((1,), (1,)), ((), ())), preferred_element_type=F32)


def _dot_tn(a, b):
    return lax.dot_general(a, b, (((0,), (0,)), ((), ())), preferred_element_type=F32)


def _split3(x):
    hi = x.astype(BF16)
    r1 = x - hi.astype(F32)
    mid = r1.astype(BF16)
    lo = (r1 - mid.astype(F32)).astype(BF16)
    return hi, mid, lo


def _dot_exact_lhs01(m01, parts):
    return _dot(jnp.concatenate([m01] * 3, axis=1), jnp.concatenate(list(parts), axis=0))


def _sigmoid(x):
    return 1.0 / (1.0 + jnp.exp(-x))


def _silu(x):
    return x * _sigmoid(x)


def _causal_in_chunk(n, shift):
    r = lax.broadcasted_iota(jnp.int32, (n, n), 0)
    c = lax.broadcasted_iota(jnp.int32, (n, n), 1)
    return ((r >> shift) == (c >> shift)) & (c <= r)


def _lower_bound(lb_ref):
    rows = [lb_ref[i:i + 1, :] for i in range(lb_ref.shape[0])]
    m = functools.reduce(jnp.maximum, rows)
    e = [jnp.exp(r - m) for r in rows]
    return e[0] / functools.reduce(jnp.add, e)


def _hgrn_prepass(proj_ref, lb, causal):
    tri = jnp.where(causal, 1.0, 0.0).astype(BF16)
    f = lb + (1.0 - lb) * _sigmoid(proj_ref[:, GROUP_W:2 * GROUP_W])
    kk = 1.0 - f
    b = _dot_exact_lhs01(tri, _split3(jnp.log(f)))
    q_dec = (_silu(proj_ref[:, 0:GROUP_W]) * jnp.exp(b)).astype(BF16)
    k_dec = kk * jnp.exp(-b)
    return q_dec, k_dec, kk, b


def _rope(x, cos, sin_signed):
    return x * cos + pltpu.roll(x, HEAD_DIM // 2, axis=1) * sin_signed


def _rms_gate(o, g, gate):
    return o * lax.rsqrt(jnp.mean(o * o, axis=-1, keepdims=True) + NORM_EPS) * g * _silu(gate)


def _layer_norm(x, g, b):
    mu = jnp.mean(x, axis=-1, keepdims=True)
    d = x - mu
    var = jnp.mean(d * d, axis=-1, keepdims=True)
    return d * lax.rsqrt(var + NORM_EPS) * g + b


def _ln_gate(o, g, b, gate):
    return _layer_norm(o, g, b) * _silu(gate)


def _head(h, group=0):
    return slice(group * GROUP_W + h * HEAD_DIM, group * GROUP_W + (h + 1) * HEAD_DIM)


def _rope_table_kernel(cos_ref, sin_ref, *, offset):
    n = cos_ref.shape[0]
    half = HEAD_DIM // 2
    packed = n % 16 == 0
    m = n // 2 if packed else n
    row = lax.broadcasted_iota(jnp.int32, (m, HEAD_DIM), 0) + pl.program_id(0) * n
    lane = lax.broadcasted_iota(jnp.int32, (m, HEAD_DIM), 1)
    low = lane < half
    if packed:
        row = row + jnp.where(low, 0, m)
    j = (lane & (half - 1)).astype(F32)
    inv = jnp.exp(-(j / half) * math.log(ROPE_BASE))
    ang = (row.astype(F32) + offset) * inv
    c = jnp.cos(ang)
    s = jnp.sin(ang)
    if not packed:
        cos_ref[...] = c
        sin_ref[...] = jnp.where(low, -s, s)
        return
    c_sw = pltpu.roll(c, half, axis=1)
    s_sw = pltpu.roll(s, half, axis=1)
    cos_ref[0:m, :] = jnp.where(low, c, c_sw)
    cos_ref[m:n, :] = jnp.where(low, c_sw, c)
    sin_ref[0:m, :] = jnp.where(low, -s, s_sw)
    sin_ref[m:n, :] = jnp.where(low, -s_sw, s)


def _rope_tables(n, offset):
    tile = min(n, 512)
    return pl.pallas_call(
        functools.partial(_rope_table_kernel, offset=float(offset)),
        grid=(n // tile,),
        in_specs=[],
        out_specs=[pl.BlockSpec((tile, HEAD_DIM), lambda i: (i, 0))] * 2,
        out_shape=[jax.ShapeDtypeStruct((n, HEAD_DIM), F32)] * 2,
        name="rope_tables",
    )()


def _tail_steps(x_ref, mix_ref, p_ref, y_ref, w_out_ref, ln1g_ref, ln1b_ref, wg_ref, wu_ref, wd_ref,
                ln2g_ref, ln2b_ref, wpp_ref, wpg_ref, bpg_ref, act_ref, h_ref, hb_ref):
    def out_proj():
        for c in range(D_MODEL // DOWN_CHUNK):
            cols = slice(c * DOWN_CHUNK, (c + 1) * DOWN_CHUNK)
            h_ref[:, cols] = DN_ALPHA * x_ref[:, cols] + _dot(mix_ref[...], w_out_ref[:, cols])
        h = _layer_norm(h_ref[...], ln1g_ref[...], ln1b_ref[...])
        h_ref[...] = h
        hb_ref[...] = h.astype(BF16)

    def ff(c):
        cols = slice(c * FF_CHUNK, (c + 1) * FF_CHUNK)
        hb = hb_ref[...]
        act_ref[:, cols] = (_silu(_dot(hb, wg_ref[:, cols])) * _dot(hb, wu_ref[:, cols])).astype(BF16)

    def down(c):
        cols = slice(c * DOWN_CHUNK, (c + 1) * DOWN_CHUNK)
        h_ref[:, cols] = DN_ALPHA * h_ref[:, cols] + _dot(act_ref[...], wd_ref[:, cols])

    def norm2():
        h2 = _layer_norm(h_ref[...], ln2g_ref[...], ln2b_ref[...])
        h_ref[...] = h2
        hb_ref[...] = h2.astype(BF16)

    def ple(c):
        cols = slice(c * DOWN_CHUNK, (c + 1) * DOWN_CHUNK)
        gate = _sigmoid(_dot(hb_ref[...], wpg_ref[:, cols]) + bpg_ref[:, cols])
        y_ref[:, cols] = h_ref[:, cols] + gate * _dot(p_ref[...].astype(BF16), wpp_ref[:, cols])

    return (out_proj,
            [functools.partial(ff, c) for c in range(D_FF // FF_CHUNK)],
            [functools.partial(down, c) for c in range(D_MODEL // DOWN_CHUNK)],
            [norm2] + [functools.partial(ple, c) for c in range(D_MODEL // DOWN_CHUNK)])


def _tail_scratch(tl):
    return [pltpu.VMEM((tl, D_FF), BF16), pltpu.VMEM((tl, D_MODEL), F32), pltpu.VMEM((tl, D_MODEL), BF16)]


_TAIL_MATRIX_POS = (0, 3, 4, 5, 8, 9)
_TAIL_MATRIX_SHAPES = ((2 * GROUP_W, D_MODEL), (D_MODEL, D_FF), (D_MODEL, D_FF), (D_FF, D_MODEL),
                       (PLE_DIM, D_MODEL), (D_MODEL, D_MODEL))


def _tail_specs():
    const = lambda i: (0, 0)
    in_hbm = pl.BlockSpec(memory_space=pl.ANY)
    vec = pl.BlockSpec((1, D_MODEL), const)
    return [in_hbm, vec, vec, in_hbm, in_hbm, in_hbm, vec, vec, in_hbm, in_hbm, vec]


def _tail_matrix_scratch():
    return ([pltpu.VMEM(shape, BF16) for shape in _TAIL_MATRIX_SHAPES]
            + [pltpu.SemaphoreType.DMA((len(_TAIL_MATRIX_SHAPES),))])


def _in_proj_steps(x_ref, w_in_ref, proj_ref, xb_ref):
    def in_proj(c):
        if c == 0:
            xb_ref[...] = x_ref[...].astype(BF16)
        cols = slice(c * GROUP_W, (c + 1) * GROUP_W)
        proj_ref[:, cols] = _dot(xb_ref[...], w_in_ref[:, cols])

    return [functools.partial(in_proj, c) for c in range(IN_COLS // GROUP_W)]


def _ret_tables(dm_ref, rd_ref):
    tl = dm_ref.shape[1]
    r = lax.broadcasted_iota(jnp.int32, (tl, tl), 0)
    c = lax.broadcasted_iota(jnp.int32, (tl, tl), 1)
    row = lax.broadcasted_iota(jnp.int32, (tl, HEAD_DIM), 0).astype(F32)
    for h in range(N_HEADS):
        logd = RET_LOG_DECAY[h]
        dm_ref[h] = jnp.where(r >= c, jnp.exp((r - c).astype(F32) * logd), 0.0)
        rd_ref[0, h] = jnp.exp((row + 1.0) * logd)
        rd_ref[1, h] = jnp.exp((tl - 1.0 - row) * logd)


def _prompt_mixer_steps(cos_ref, sin_ref, lb_ref, ag_ref, bg_ref, bb_ref,
                        sa_ref, sb_ref, proj_ref, st_ref, oa_ref, mix_ref,
                        qd_ref, kd_ref, ke_ref, dec_ref, va_ref, kv_ref, sbf_ref, dm_ref, rd_ref, first):
    tl = proj_ref.shape[0]
    shift = REF_CHUNK.bit_length() - 1
    n_chunks = tl // REF_CHUNK

    def prepass():
        q_dec, k_dec, kk, b = _hgrn_prepass(proj_ref, _lower_bound(lb_ref), _causal_in_chunk(tl, shift))
        qd_ref[...] = q_dec
        for h in range(N_HEADS):
            kd_ref[h] = k_dec[:, _head(h)].T.astype(BF16)
        va_ref[...] = proj_ref[:, 2 * GROUP_W:3 * GROUP_W].astype(BF16)
        last = [b[(n + 1) * REF_CHUNK - 1:(n + 1) * REF_CHUNK, :] for n in range(n_chunks)]
        b_last = jnp.concatenate([jnp.broadcast_to(r, (REF_CHUNK, GROUP_W)) for r in last], axis=0)
        k_end = kk * jnp.exp(b_last - b)
        for n in range(n_chunks):
            dec_ref[n:n + 1, :] = jnp.exp(last[n])
        odd = ((lax.broadcasted_iota(jnp.int32, (tl, GROUP_W), 0) >> shift) & 1) == 1
        k_even = jnp.where(odd, 0.0, k_end).astype(BF16)
        k_odd = jnp.where(odd, k_end, 0.0).astype(BF16)
        for h in range(N_HEADS):
            ke_ref[:, 2 * h * HEAD_DIM:(2 * h + 1) * HEAD_DIM] = k_even[:, _head(h)]
            ke_ref[:, (2 * h + 1) * HEAD_DIM:(2 * h + 2) * HEAD_DIM] = k_odd[:, _head(h)]

    def kv_scan(h):
        hs = _head(h)
        pair = 2 * REF_CHUNK
        for r in range(n_chunks // 2):
            rows = slice(r * pair, (r + 1) * pair)
            kv_ref[r] = _dot_tn(va_ref[rows, hs], ke_ref[rows, 2 * h * HEAD_DIM:(2 * h + 2) * HEAD_DIM])
        st = jnp.where(first, 0.0, st_ref[h])
        for n in range(n_chunks):
            sbf_ref[n, h] = st.T.astype(BF16)
            st = st * dec_ref[n:n + 1, hs] + kv_ref[n // 2, :, (n % 2) * HEAD_DIM:(n % 2 + 1) * HEAD_DIM]
        st_ref[h] = st
        sa_ref[0, h] = st.T

    def diag(h):
        hs = _head(h)
        sc = jnp.where(_causal_in_chunk(tl, shift), _dot(qd_ref[:, hs], kd_ref[h]), 0.0)
        oa_ref[:, hs] = _dot(sc.astype(BF16), va_ref[:, hs])

    def inter(n):
        rows = slice(n * REF_CHUNK, (n + 1) * REF_CHUNK)
        for h in range(N_HEADS):
            hs = _head(h)
            oa_ref[rows, hs] += _dot(qd_ref[rows, hs], sbf_ref[n, h])

    def hgrn_out():
        for h in range(N_HEADS):
            hs = _head(h)
            mix_ref[:, hs] = _rms_gate(oa_ref[:, hs], ag_ref[:, hs], proj_ref[:, _head(h, 3)]).astype(BF16)

    def ret(h):
        hs = _head(h)
        cos = cos_ref[...]
        sin = sin_ref[...]
        q = _rope(proj_ref[:, _head(h, 4)], cos, sin)
        k = _rope(proj_ref[:, _head(h, 5)], cos, sin) * K_SCALE
        v = proj_ref[:, _head(h, 6)].astype(BF16)
        a = (_dot(q.astype(BF16), k.T.astype(BF16)) * dm_ref[h]).astype(BF16)
        s = jnp.where(first, 0.0, sb_ref[0, h])
        o = _dot(a, v) + _dot((q * rd_ref[0, h]).astype(BF16), s.astype(BF16))
        k_end_b = (k * rd_ref[1, h]).astype(BF16)
        sb_ref[0, h] = s * math.exp(tl * RET_LOG_DECAY[h]) + _dot_tn(k_end_b, v)
        mix_ref[:, _head(h, 1)] = _ln_gate(o, bg_ref[:, hs], bb_ref[:, hs],
                                           proj_ref[:, _head(h, 7)]).astype(BF16)

    return (prepass,
            [functools.partial(kv_scan, h) for h in range(N_HEADS)],
            [functools.partial(diag, h) for h in range(N_HEADS)],
            [functools.partial(inter, n) for n in range(n_chunks)],
            hgrn_out,
            [functools.partial(ret, h) for h in range(N_HEADS)])


def _interleave(a, b):
    out = []
    for i in range(max(len(a), len(b))):
        out += a[i:i + 1] + b[i:i + 1]
    return out


def _prompt_layer_kernel(xn_ref, xp_ref, p_ref, cos_ref, sin_ref, w_in_ref, lb_ref, ag_ref, bg_ref,
                         bb_ref, *rest, tiles_per_seq):
    tail_in = rest[:11]
    y_ref, sa_ref, sb_ref = rest[11:14]
    mixer_scratch = rest[14:27]
    proj_ref, mix_ref, dm_ref, rd_ref = mixer_scratch[0], mixer_scratch[3], mixer_scratch[11], mixer_scratch[12]
    xb_ref = rest[27]
    tail_scratch = rest[28:31]
    w_vmem, w_sem = rest[31:-1], rest[-1]
    tail_w = list(tail_in)
    for k, pos in enumerate(_TAIL_MATRIX_POS):
        tail_w[pos] = w_vmem[k]

    def weight_copies():
        return [pltpu.make_async_copy(tail_in[pos], w_vmem[k], w_sem.at[k])
                for k, pos in enumerate(_TAIL_MATRIX_POS)]
    g = pl.program_id(0)
    n_tiles = pl.num_programs(0) - 1
    slot = lax.rem(g, 2)

    def mixer_steps():
        return _prompt_mixer_steps(cos_ref, sin_ref, lb_ref, ag_ref, bg_ref, bb_ref, sa_ref, sb_ref,
                                   proj_ref.at[slot], *mixer_scratch[1:],
                                   first=lax.rem(g, tiles_per_seq) == 0)

    def tail_steps():
        return _tail_steps(xp_ref, mix_ref, p_ref, y_ref, *tail_w, *tail_scratch)

    @pl.when(g == 0)
    def _():
        for copy in weight_copies():
            copy.start()
        _ret_tables(dm_ref, rd_ref)
        prepass, kv_scan, diag, inter, hgrn_out, ret = mixer_steps()
        steps = _in_proj_steps(xp_ref, w_in_ref, proj_ref.at[slot], xb_ref)
        steps += [prepass] + kv_scan + diag + inter + [hgrn_out] + ret
        steps += _in_proj_steps(xn_ref, w_in_ref, proj_ref.at[1 - slot], xb_ref)
        for step in steps:
            step()

    @pl.when((g > 0) & (g < n_tiles))
    def _():
        @pl.when(g == 1)
        def _():
            for copy in weight_copies():
                copy.wait()

        prepass, kv_scan, diag, inter, hgrn_out, ret = mixer_steps()
        out_proj, ff, down, final = tail_steps()
        in_proj = _in_proj_steps(xn_ref, w_in_ref, proj_ref.at[1 - slot], xb_ref)
        inter_pairs = [lambda a=a, b=b: (a(), b()) for a, b in zip(inter[0::2], inter[1::2])]
        steps = [out_proj, in_proj[0], prepass, in_proj[1]]
        steps += _interleave(ff + down, kv_scan + diag + ret + inter_pairs + [hgrn_out])
        steps += _interleave(final, in_proj[2:])
        for step in steps:
            step()

    @pl.when(g == n_tiles)
    def _():
        out_proj, ff, down, final = tail_steps()
        for step in [out_proj] + ff + down + final:
            step()


def _prompt_layer(x, p, cos, sin, w_in, lb_logits, a_g, b_g, b_b, tail_w):
    bsz, seq, _ = x.shape
    tl = TOKEN_TILE
    tps = seq // tl
    n_tiles = bsz * tps
    assert n_tiles >= 2, "the tail weights are awaited in the second grid step"
    x2 = x.reshape(bsz * seq, D_MODEL)
    p2 = p.reshape(bsz * seq, PLE_DIM)
    const = lambda g: (0, 0)
    nxt = lambda g: (jnp.minimum(g + 1, n_tiles - 1), 0)
    prev = lambda g: (jnp.maximum(g - 1, 0), 0)
    seq_tile = lambda g: (lax.rem(jnp.minimum(g, n_tiles - 1), tps), 0)
    state_spec = pl.BlockSpec((1, N_HEADS, HEAD_DIM, HEAD_DIM),
                              lambda g: (jnp.minimum(g, n_tiles - 1) // tps, 0, 0, 0))
    state_shape = jax.ShapeDtypeStruct((bsz, N_HEADS, HEAD_DIM, HEAD_DIM), F32)
    return pl.pallas_call(
        functools.partial(_prompt_layer_kernel, tiles_per_seq=tps),
        grid=(n_tiles + 1,),
        in_specs=[
            pl.BlockSpec((tl, D_MODEL), nxt),
            pl.BlockSpec((tl, D_MODEL), prev),
            pl.BlockSpec((tl, PLE_DIM), prev),
            pl.BlockSpec((tl, HEAD_DIM), seq_tile),
            pl.BlockSpec((tl, HEAD_DIM), seq_tile),
            pl.BlockSpec((D_MODEL, IN_COLS), const, pipeline_mode=pl.Buffered(1)),
            pl.BlockSpec(lb_logits.shape, const),
            pl.BlockSpec((1, GROUP_W), const),
            pl.BlockSpec((1, GROUP_W), const),
            pl.BlockSpec((1, GROUP_W), const),
        ] + _tail_specs(),
        out_specs=[
            pl.BlockSpec((tl, D_MODEL), prev),
            state_spec,
            state_spec,
        ],
        out_shape=[
            jax.ShapeDtypeStruct((bsz * seq, D_MODEL), F32),
            state_shape,
            state_shape,
        ],
        scratch_shapes=[
            pltpu.VMEM((2, tl, IN_COLS), F32),
            pltpu.VMEM((N_HEADS, HEAD_DIM, HEAD_DIM), F32),
            pltpu.VMEM((tl, GROUP_W), F32),
            pltpu.VMEM((tl, 2 * GROUP_W), BF16),
            pltpu.VMEM((tl, GROUP_W), BF16),
            pltpu.VMEM((N_HEADS, HEAD_DIM, tl), BF16),
            pltpu.VMEM((tl, 2 * GROUP_W), BF16),
            pltpu.VMEM((tl // REF_CHUNK, GROUP_W), F32),
            pltpu.VMEM((tl, GROUP_W), BF16),
            pltpu.VMEM((tl // (2 * REF_CHUNK), HEAD_DIM, 2 * HEAD_DIM), F32),
            pltpu.VMEM((tl // REF_CHUNK, N_HEADS, HEAD_DIM, HEAD_DIM), BF16),
            pltpu.VMEM((N_HEADS, tl, tl), F32),
            pltpu.VMEM((2, N_HEADS, tl, HEAD_DIM), F32),
            pltpu.VMEM((tl, D_MODEL), BF16),
        ] + _tail_scratch(tl) + _tail_matrix_scratch(),
        compiler_params=pltpu.CompilerParams(
            dimension_semantics=("arbitrary",), vmem_limit_bytes=V7X_VMEM_LIMIT_BYTES),
        name="prompt_layer",
    )(x2, x2, p2, cos, sin, w_in, lb_logits, a_g, b_g, b_b, *tail_w)


def _cast_kernel(w_ref, wb_ref):
    wb_ref[...] = w_ref[...].astype(BF16)


def _w_in_bf16(w_in):
    tn = SAMPLE_PROJ_COLS
    return pl.pallas_call(
        _cast_kernel,
        grid=(IN_COLS // tn,),
        in_specs=[pl.BlockSpec((D_MODEL, tn), lambda c: (0, c))],
        out_specs=pl.BlockSpec((D_MODEL, tn), lambda c: (0, c)),
        out_shape=jax.ShapeDtypeStruct((D_MODEL, IN_COLS), BF16),
        compiler_params=pltpu.CompilerParams(
            dimension_semantics=("arbitrary",), vmem_limit_bytes=V7X_VMEM_LIMIT_BYTES),
        name="w_in_cast",
    )(w_in)


def _sample_rec_kernel(x_ref, w_in_ref, sa_in_ref, sb_in_ref, lb_ref, ag_ref, bg_ref, bb_ref,
                       cos_ref, sin_ref, mix_ref, sa_ref, sb_ref, proj_ref, oa_ref, ob_ref, *, seq_len):
    rows_n = proj_ref.shape[0]
    n_seq = rows_n // seq_len
    causal = _causal_in_chunk(rows_n, seq_len.bit_length() - 1)

    xb = x_ref[...].astype(BF16)

    def in_proj(c):
        cols = slice(c * GROUP_W, (c + 1) * GROUP_W)
        proj_ref[:, cols] = _dot(xb, w_in_ref[:, cols])

    half = IN_COLS // GROUP_W // 2
    for c in range(half):
        in_proj(c)

    q_dec, k_dec, kk, b = _hgrn_prepass(proj_ref, _lower_bound(lb_ref), causal)
    for c in range(half, 2 * half):
        in_proj(c)
    v_a = proj_ref[:, 2 * GROUP_W:3 * GROUP_W]
    for h in range(N_HEADS):
        hs = _head(h)
        sc = jnp.where(causal, _dot_nt(q_dec[:, hs], k_dec[:, hs].astype(BF16)), 0.0).astype(BF16)
        oa_ref[:, hs] = _dot(sc, v_a[:, hs].astype(BF16))
    q_dec32 = q_dec.astype(F32)
    rr = lax.broadcasted_iota(jnp.int32, (seq_len, GROUP_W), 0)
    ones_blk = jnp.ones((seq_len, HEAD_DIM), BF16)
    for s in range(n_seq):
        rows = slice(s * seq_len, (s + 1) * seq_len)
        b_last = b[(s + 1) * seq_len - 1:(s + 1) * seq_len, :]
        k_end = (kk[rows] * jnp.exp(b_last - b[rows])).astype(BF16)
        hi, mid, lo = [t.astype(F32) for t in _split3(jnp.exp(b_last))]
        dec_rows = jnp.where(rr == 0, hi, jnp.where(rr == 1, mid, jnp.where(rr == 2, lo, 0.0)))
        dec_rows = dec_rows.astype(BF16)
        for h in range(N_HEADS):
            hs = _head(h)
            st = sa_in_ref[s, h]
            oa_ref[rows, hs] += _dot(q_dec32[rows, hs].astype(BF16), st.astype(BF16))
            dec_kv = _dot_tn(dec_rows[:, hs], ones_blk)
            sa_ref[s, h] = st * dec_kv + _dot_tn(k_end[:, hs], v_a[rows, hs].astype(BF16))
    for h in range(N_HEADS):
        hs = _head(h)
        mix_ref[:, hs] = _rms_gate(oa_ref[:, hs], ag_ref[:, hs], proj_ref[:, _head(h, 3)]).astype(BF16)

    cos = jnp.concatenate([cos_ref[...]] * n_seq, axis=0)
    sin = jnp.concatenate([sin_ref[...]] * n_seq, axis=0)
    r = lax.broadcasted_iota(jnp.int32, (rows_n, rows_n), 0)
    c = lax.broadcasted_iota(jnp.int32, (rows_n, rows_n), 1)
    diff = ((r & (seq_len - 1)) - (c & (seq_len - 1))).astype(F32)
    row = (lax.broadcasted_iota(jnp.int32, (rows_n, HEAD_DIM), 0) & (seq_len - 1)).astype(F32)
    for h in range(N_HEADS):
        hs = _head(h)
        logd = RET_LOG_DECAY[h]
        q = _rope(proj_ref[:, _head(h, 4)], cos, sin)
        k = _rope(proj_ref[:, _head(h, 5)], cos, sin) * K_SCALE
        v32 = proj_ref[:, _head(h, 6)]
        dmask = jnp.where(causal, jnp.exp(diff * logd), 0.0)
        a = (_dot_nt(q.astype(BF16), k.astype(BF16)) * dmask).astype(BF16)
        ob_ref[...] = _dot(a, v32.astype(BF16))
        q_dec_b = q * jnp.exp((row + 1.0) * logd)
        k_end_b = k * jnp.exp((seq_len - 1.0 - row) * logd)
        for s in range(n_seq):
            rows = slice(s * seq_len, (s + 1) * seq_len)
            st = sb_in_ref[s, h]
            ob_ref[rows, :] += _dot(q_dec_b[rows].astype(BF16), st.astype(BF16))
            sb_ref[s, h] = st * math.exp(seq_len * logd) + _dot_tn(
                k_end_b[rows].astype(BF16), v32[rows].astype(BF16))
        mix_ref[:, _head(h, 1)] = _ln_gate(ob_ref[...], bg_ref[:, hs], bb_ref[:, hs],
                                           proj_ref[:, _head(h, 7)]).astype(BF16)


def _sample_rec(x, w_in, sa, sb, lb_logits, a_g, b_g, b_b, cos, sin, seq_len):
    n_tok = x.shape[0]
    n_seq = n_tok // seq_len
    bs = SAMPLE_SEQS
    rows = bs * seq_len
    const = lambda i: (0, 0)
    state_spec = pl.BlockSpec((bs, N_HEADS, HEAD_DIM, HEAD_DIM), lambda i: (i, 0, 0, 0))
    state_shape = jax.ShapeDtypeStruct((n_seq, N_HEADS, HEAD_DIM, HEAD_DIM), F32)
    return pl.pallas_call(
        functools.partial(_sample_rec_kernel, seq_len=seq_len),
        grid=(n_seq // bs,),
        in_specs=[
            pl.BlockSpec((rows, D_MODEL), lambda i: (i, 0)),
            pl.BlockSpec((D_MODEL, IN_COLS), const, pipeline_mode=pl.Buffered(1)),
            state_spec,
            state_spec,
            pl.BlockSpec(lb_logits.shape, const),
            pl.BlockSpec((1, GROUP_W), const),
            pl.BlockSpec((1, GROUP_W), const),
            pl.BlockSpec((1, GROUP_W), const),
            pl.BlockSpec((seq_len, HEAD_DIM), const),
            pl.BlockSpec((seq_len, HEAD_DIM), const),
        ],
        out_specs=[
            pl.BlockSpec((rows, 2 * GROUP_W), lambda i: (i, 0)),
            state_spec,
            state_spec,
        ],
        out_shape=[
            jax.ShapeDtypeStruct((n_tok, 2 * GROUP_W), BF16),
            state_shape,
            state_shape,
        ],
        scratch_shapes=[
            pltpu.VMEM((rows, IN_COLS), F32),
            pltpu.VMEM((rows, GROUP_W), F32),
            pltpu.VMEM((rows, HEAD_DIM), F32),
        ],
        compiler_params=pltpu.CompilerParams(
            dimension_semantics=("arbitrary",), vmem_limit_bytes=V7X_VMEM_LIMIT_BYTES),
        name="sample_recurrence",
    )(x, w_in, sa, sb, lb_logits, a_g, b_g, b_b, cos, sin)


_TAIL_PHASE_STEPS = (D_MODEL // DOWN_CHUNK, D_FF // FF_CHUNK, D_MODEL // DOWN_CHUNK, D_MODEL // DOWN_CHUNK)
_TAIL_PHASE_START = tuple(sum(_TAIL_PHASE_STEPS[:i]) for i in range(4))


def _sample_tail_kernel(x_ref, mix_ref, p_ref, wo_ref, wg_ref, wu_ref, wd_ref, wpg_ref, wpp_ref,
                        ln1g_ref, ln1b_ref, ln2g_ref, ln2b_ref, bpg_ref,
                        y_ref, wo_b, wg_b, wu_b, wd_b, wpg_b, wpp_b, act_ref, h_ref, hb_ref):
    s = pl.program_id(0)
    a0, b0, c0, d0 = _TAIL_PHASE_START
    blk = DOWN_CHUNK

    @pl.when(s < b0)
    def _():
        wo_b[...] = wo_ref[...].astype(BF16)
        cols = pl.ds(pl.multiple_of((s - a0) * blk, blk), blk)
        h_ref[:, cols] = DN_ALPHA * x_ref[...] + _dot(mix_ref[...], wo_b[...])

        @pl.when(s == b0 - 1)
        def _():
            h = _layer_norm(h_ref[...], ln1g_ref[...], ln1b_ref[...])
            h_ref[...] = h
            hb_ref[...] = h.astype(BF16)

    @pl.when((s >= b0) & (s < c0))
    def _():
        wg_b[...] = wg_ref[...].astype(BF16)
        wu_b[...] = wu_ref[...].astype(BF16)
        cols = pl.ds(pl.multiple_of((s - b0) * FF_CHUNK, FF_CHUNK), FF_CHUNK)
        hb = hb_ref[...]
        act_ref[:, cols] = (_silu(_dot(hb, wg_b[...])) * _dot(hb, wu_b[...])).astype(BF16)

    @pl.when((s >= c0) & (s < d0))
    def _():
        wd_b[...] = wd_ref[...].astype(BF16)
        cols = pl.ds(pl.multiple_of((s - c0) * blk, blk), blk)
        h_ref[:, cols] = DN_ALPHA * h_ref[:, cols] + _dot(act_ref[...], wd_b[...])

        @pl.when(s == d0 - 1)
        def _():
            h2 = _layer_norm(h_ref[...], ln2g_ref[...], ln2b_ref[...])
            h_ref[...] = h2
            hb_ref[...] = h2.astype(BF16)

    @pl.when(s >= d0)
    def _():
        wpg_b[...] = wpg_ref[...].astype(BF16)
        wpp_b[...] = wpp_ref[...].astype(BF16)
        cols = pl.ds(pl.multiple_of((s - d0) * blk, blk), blk)
        gate = _sigmoid(_dot(hb_ref[...], wpg_b[...]) + bpg_ref[:, cols])
        y_ref[...] = h_ref[:, cols] + gate * _dot(p_ref[...].astype(BF16), wpp_b[...])


def _sample_tail(x, mix, p, w_out, ln1g, ln1b, wg, wu, wd, ln2g, ln2b, wpp, wpg, bpg):
    n = x.shape[0]
    a0, b0, c0, d0 = _TAIL_PHASE_START
    na, nb, nc, nd = _TAIL_PHASE_STEPS
    const = lambda s: (0, 0)
    col = lambda start, count: (lambda s: (0, jnp.clip(s - start, 0, count - 1)))
    vec = pl.BlockSpec((1, D_MODEL), const)
    weight_specs = [
        pl.BlockSpec((2 * GROUP_W, DOWN_CHUNK), col(a0, na)),
        pl.BlockSpec((D_MODEL, FF_CHUNK), col(b0, nb)),
        pl.BlockSpec((D_MODEL, FF_CHUNK), col(b0, nb)),
        pl.BlockSpec((D_FF, DOWN_CHUNK), col(c0, nc)),
        pl.BlockSpec((D_MODEL, DOWN_CHUNK), col(d0, nd)),
        pl.BlockSpec((PLE_DIM, DOWN_CHUNK), col(d0, nd)),
    ]
    weights = (w_out, wg, wu, wd, wpg, wpp)
    return pl.pallas_call(
        _sample_tail_kernel,
        grid=(sum(_TAIL_PHASE_STEPS),),
        in_specs=[
            pl.BlockSpec((n, DOWN_CHUNK), col(a0, na)),
            pl.BlockSpec((n, 2 * GROUP_W), const),
            pl.BlockSpec((n, PLE_DIM), const),
        ] + weight_specs + [vec, vec, vec, vec, vec],
        out_specs=[pl.BlockSpec((n, DOWN_CHUNK), col(d0, nd))] + weight_specs,
        out_shape=[jax.ShapeDtypeStruct((n, D_MODEL), F32)]
        + [jax.ShapeDtypeStruct(w.shape, BF16) for w in weights],
        scratch_shapes=_tail_scratch(n),
        compiler_params=pltpu.CompilerParams(
            dimension_semantics=("arbitrary",), vmem_limit_bytes=V7X_VMEM_LIMIT_BYTES),
        name="sample_tail",
    )(x, mix, p, *weights, ln1g, ln1b, ln2g, ln2b, bpg)


def kernel(x_prompt, x_sample, p_prompt, p_sample, state_hgrn, state_ret, lb_logits, w_in, a_norm_g, b_norm_g, b_norm_b, w_out, ln1_g, ln1_b, w_ffn_gate, w_ffn_up, w_ffn_down, ln2_g, ln2_b, w_ple_proj, w_ple_gate, b_ple_gate):
    assert w_in.shape[0] == DEPTH == 1
    bsz, seq, _ = x_prompt.shape
    n_dec, dec_seq, _ = x_sample.shape

    mixer_vecs = (lb_logits, a_norm_g, b_norm_g, b_norm_b)
    cos_p, sin_p = _rope_tables(seq, 0)
    cos_s, sin_s = _rope_tables(dec_seq, PAST_LEN)

    x_s = x_sample.reshape(n_dec * dec_seq, D_MODEL)
    w_in_b = _w_in_bf16(w_in[0])
    mix_s, sa_s, sb_s = _sample_rec(x_s, w_in_b, state_hgrn[0], state_ret[0], *mixer_vecs,
                                    cos_s, sin_s, dec_seq)
    y_s, w_out_b, wg_b, wu_b, wd_b, wpg_b, wpp_b = _sample_tail(
        x_s, mix_s, p_sample[0].reshape(n_dec * dec_seq, PLE_DIM), w_out[0], ln1_g, ln1_b,
        w_ffn_gate[0], w_ffn_up[0], w_ffn_down[0], ln2_g, ln2_b, w_ple_proj[0], w_ple_gate[0],
        b_ple_gate)

    tail_w = (w_out_b, ln1_g, ln1_b, wg_b, wu_b, wd_b, ln2_g, ln2_b, wpp_b, wpg_b, b_ple_gate)
    y_p, sa_p, sb_p = _prompt_layer(x_prompt, p_prompt[0], cos_p, sin_p, w_in_b, *mixer_vecs, tail_w)

    return (y_p.reshape(bsz, seq, D_MODEL), y_s.reshape(n_dec, dec_seq, D_MODEL),
            sa_p[None], sb_p[None], sa_s[None], sb_s[None])
```

```python
import functools
import math

import jax
import jax.numpy as jnp
from jax import lax
from jax.experimental import pallas as pl
from jax.experimental.pallas import tpu as pltpu

F32 = jnp.float32
BF16 = jnp.bfloat16

D_MODEL = 1024
N_HEADS = 4
HEAD_DIM = 128
GROUP_W = N_HEADS * HEAD_DIM
IN_COLS = 8 * GROUP_W
D_FF = 2816
PLE_DIM = 256
DEPTH = 1
PAST_LEN = 16384
REF_CHUNK = 32
ROPE_BASE = 10000.0
NORM_EPS = 1e-5
DN_ALPHA = (2.0 * DEPTH) ** 0.25
RET_LOG_DECAY = tuple(math.log1p(-(2.0 ** (-5.0 - h))) for h in range(N_HEADS))
K_SCALE = HEAD_DIM ** -0.5

V7X_VMEM_LIMIT_BYTES = 60 * 1024 * 1024

TOKEN_TILE = 256
SAMPLE_PROJ_COLS = 1024
SAMPLE_SEQS = 16
FF_CHUNK = 256
DOWN_CHUNK = 256


def _dot(a, b):
    return jnp.dot(a, b, preferred_element_type=F32)


def _dot_nt(a, b):
    return lax.dot_general(a, b, (((1,), (1,)), ((), ())), preferred_element_type=F32)


def _dot_tn(a, b):
    return lax.dot_general(a, b, (((0,), (0,)), ((), ())), preferred_element_type=F32)


def _split3(x):
    hi = x.astype(BF16)
    r1 = x - hi.astype(F32)
    mid = r1.astype(BF16)
    lo = (r1 - mid.astype(F32)).astype(BF16)
    return hi, mid, lo


def _dot_exact_lhs01(m01, parts):
    return _dot(jnp.concatenate([m01] * 3, axis=1), jnp.concatenate(list(parts), axis=0))


def _sigmoid(x):
    return 1.0 / (1.0 + jnp.exp(-x))


def _silu(x):
    return x * _sigmoid(x)


def _causal_in_chunk(n, shift):
    r = lax.broadcasted_iota(jnp.int32, (n, n), 0)
    c = lax.broadcasted_iota(jnp.int32, (n, n), 1)
    return ((r >> shift) == (c >> shift)) & (c <= r)


def _lower_bound(lb_ref):
    rows = [lb_ref[i:i + 1, :] for i in range(lb_ref.shape[0])]
    m = functools.reduce(jnp.maximum, rows)
    e = [jnp.exp(r - m) for r in rows]
    return e[0] / functools.reduce(jnp.add, e)


def _hgrn_prepass(proj_ref, lb, causal):
    tri = jnp.where(causal, 1.0, 0.0).astype(BF16)
    f = lb + (1.0 - lb) * _sigmoid(proj_ref[:, GROUP_W:2 * GROUP_W])
    kk = 1.0 - f
    b = _dot_exact_lhs01(tri, _split3(jnp.log(f)))
    q_dec = (_silu(proj_ref[:, 0:GROUP_W]) * jnp.exp(b)).astype(BF16)
    k_dec = kk * jnp.exp(-b)
    return q_dec, k_dec, kk, b


def _rope(x, cos, sin_signed):
    return x * cos + pltpu.roll(x, HEAD_DIM // 2, axis=1) * sin_signed


def _rms_gate(o, g, gate):
    return o * lax.rsqrt(jnp.mean(o * o, axis=-1, keepdims=True) + NORM_EPS) * g * _silu(gate)


def _layer_norm(x, g, b):
    mu = jnp.mean(x, axis=-1, keepdims=True)
    d = x - mu
    var = jnp.mean(d * d, axis=-1, keepdims=True)
    return d * lax.rsqrt(var + NORM_EPS) * g + b


def _ln_gate(o, g, b, gate):
    return _layer_norm(o, g, b) * _silu(gate)


def _head(h, group=0):
    return slice(group * GROUP_W + h * HEAD_DIM, group * GROUP_W + (h + 1) * HEAD_DIM)


def _rope_table_kernel(cos_ref, sin_ref, *, offset):
    n = cos_ref.shape[0]
    half = HEAD_DIM // 2
    packed = n % 16 == 0
    m = n // 2 if packed else n
    row = lax.broadcasted_iota(jnp.int32, (m, HEAD_DIM), 0) + pl.program_id(0) * n
    lane = lax.broadcasted_iota(jnp.int32, (m, HEAD_DIM), 1)
    low = lane < half
    if packed:
        row = row + jnp.where(low, 0, m)
    j = (lane & (half - 1)).astype(F32)
    inv = jnp.exp(-(j / half) * math.log(ROPE_BASE))
    ang = (row.astype(F32) + offset) * inv
    c = jnp.cos(ang)
    s = jnp.sin(ang)
    if not packed:
        cos_ref[...] = c
        sin_ref[...] = jnp.where(low, -s, s)
        return
    c_sw = pltpu.roll(c, half, axis=1)
    s_sw = pltpu.roll(s, half, axis=1)
    cos_ref[0:m, :] = jnp.where(low, c, c_sw)
    cos_ref[m:n, :] = jnp.where(low, c_sw, c)
    sin_ref[0:m, :] = jnp.where(low, -s, s_sw)
    sin_ref[m:n, :] = jnp.where(low, -s_sw, s)


def _rope_tables(n, offset):
    tile = min(n, 512)
    return pl.pallas_call(
        functools.partial(_rope_table_kernel, offset=float(offset)),
        grid=(n // tile,),
        in_specs=[],
        out_specs=[pl.BlockSpec((tile, HEAD_DIM), lambda i: (i, 0))] * 2,
        out_shape=[jax.ShapeDtypeStruct((n, HEAD_DIM), F32)] * 2,
        name="rope_tables",
    )()


def _tail_steps(x_ref, mix_ref, p_ref, y_ref, w_out_ref, ln1g_ref, ln1b_ref, wg_ref, wu_ref, wd_ref,
                ln2g_ref, ln2b_ref, wpp_ref, wpg_ref, bpg_ref, act_ref, h_ref, hb_ref):
    def out_proj():
        for c in range(D_MODEL // DOWN_CHUNK):
            cols = slice(c * DOWN_CHUNK, (c + 1) * DOWN_CHUNK)
            h_ref[:, cols] = DN_ALPHA * x_ref[:, cols] + _dot(mix_ref[...], w_out_ref[:, cols])
        h = _layer_norm(h_ref[...], ln1g_ref[...], ln1b_ref[...])
        h_ref[...] = h
        hb_ref[...] = h.astype(BF16)

    def ff(c):
        cols = slice(c * FF_CHUNK, (c + 1) * FF_CHUNK)
        hb = hb_ref[...]
        act_ref[:, cols] = (_silu(_dot(hb, wg_ref[:, cols])) * _dot(hb, wu_ref[:, cols])).astype(BF16)

    def down(c):
        cols = slice(c * DOWN_CHUNK, (c + 1) * DOWN_CHUNK)
        h_ref[:, cols] = DN_ALPHA * h_ref[:, cols] + _dot(act_ref[...], wd_ref[:, cols])

    def norm2():
        h2 = _layer_norm(h_ref[...], ln2g_ref[...], ln2b_ref[...])
        h_ref[...] = h2
        hb_ref[...] = h2.astype(BF16)

    def ple(c):
        cols = slice(c * DOWN_CHUNK, (c + 1) * DOWN_CHUNK)
        gate = _sigmoid(_dot(hb_ref[...], wpg_ref[:, cols]) + bpg_ref[:, cols])
        y_ref[:, cols] = h_ref[:, cols] + gate * _dot(p_ref[...].astype(BF16), wpp_ref[:, cols])

    return (out_proj,
            [functools.partial(ff, c) for c in range(D_FF // FF_CHUNK)],
            [functools.partial(down, c) for c in range(D_MODEL // DOWN_CHUNK)],
            [norm2] + [functools.partial(ple, c) for c in range(D_MODEL // DOWN_CHUNK)])


def _tail_scratch(tl):
    return [pltpu.VMEM((tl, D_FF), BF16), pltpu.VMEM((tl, D_MODEL), F32), pltpu.VMEM((tl, D_MODEL), BF16)]


_TAIL_MATRIX_POS = (0, 3, 4, 5, 8, 9)
_TAIL_MATRIX_SHAPES = ((2 * GROUP_W, D_MODEL), (D_MODEL, D_FF), (D_MODEL, D_FF), (D_FF, D_MODEL),
                       (PLE_DIM, D_MODEL), (D_MODEL, D_MODEL))


def _tail_specs():
    const = lambda i: (0, 0)
    in_hbm = pl.BlockSpec(memory_space=pl.ANY)
    vec = pl.BlockSpec((1, D_MODEL), const)
    return [in_hbm, vec, vec, in_hbm, in_hbm, in_hbm, vec, vec, in_hbm, in_hbm, vec]


def _tail_matrix_scratch():
    return ([pltpu.VMEM(shape, BF16) for shape in _TAIL_MATRIX_SHAPES]
            + [pltpu.SemaphoreType.DMA((len(_TAIL_MATRIX_SHAPES),))])


def _in_proj_steps(x_ref, w_in_ref, proj_ref, xb_ref):
    def in_proj(c):
        if c == 0:
            xb_ref[...] = x_ref[...].astype(BF16)
        cols = slice(c * GROUP_W, (c + 1) * GROUP_W)
        proj_ref[:, cols] = _dot(xb_ref[...], w_in_ref[:, cols])

    return [functools.partial(in_proj, c) for c in range(IN_COLS // GROUP_W)]


def _ret_tables(dm_ref, rd_ref):
    tl = dm_ref.shape[1]
    r = lax.broadcasted_iota(jnp.int32, (tl, tl), 0)
    c = lax.broadcasted_iota(jnp.int32, (tl, tl), 1)
    row = lax.broadcasted_iota(jnp.int32, (tl, HEAD_DIM), 0).astype(F32)
    for h in range(N_HEADS):
        logd = RET_LOG_DECAY[h]
        dm_ref[h] = jnp.where(r >= c, jnp.exp((r - c).astype(F32) * logd), 0.0)
        rd_ref[0, h] = jnp.exp((row + 1.0) * logd)
        rd_ref[1, h] = jnp.exp((tl - 1.0 - row) * logd)


def _prompt_mixer_steps(cos_ref, sin_ref, lb_ref, ag_ref, bg_ref, bb_ref,
                        sa_ref, sb_ref, proj_ref, st_ref, oa_ref, mix_ref,
                        qd_ref, kd_ref, ke_ref, dec_ref, va_ref, kv_ref, sbf_ref, dm_ref, rd_ref, first):
    tl = proj_ref.shape[0]
    shift = REF_CHUNK.bit_length() - 1
    n_chunks = tl // REF_CHUNK

    def prepass():
        q_dec, k_dec, kk, b = _hgrn_prepass(proj_ref, _lower_bound(lb_ref), _causal_in_chunk(tl, shift))
        qd_ref[...] = q_dec
        for h in range(N_HEADS):
            kd_ref[h] = k_dec[:, _head(h)].T.astype(BF16)
        va_ref[...] = proj_ref[:, 2 * GROUP_W:3 * GROUP_W].astype(BF16)
        last = [b[(n + 1) * REF_CHUNK - 1:(n + 1) * REF_CHUNK, :] for n in range(n_chunks)]
        b_last = jnp.concatenate([jnp.broadcast_to(r, (REF_CHUNK, GROUP_W)) for r in last], axis=0)
        k_end = kk * jnp.exp(b_last - b)
        for n in range(n_chunks):
            dec_ref[n:n + 1, :] = jnp.exp(last[n])
        odd = ((lax.broadcasted_iota(jnp.int32, (tl, GROUP_W), 0) >> shift) & 1) == 1
        k_even = jnp.where(odd, 0.0, k_end).astype(BF16)
        k_odd = jnp.where(odd, k_end, 0.0).astype(BF16)
        for h in range(N_HEADS):
            ke_ref[:, 2 * h * HEAD_DIM:(2 * h + 1) * HEAD_DIM] = k_even[:, _head(h)]
            ke_ref[:, (2 * h + 1) * HEAD_DIM:(2 * h + 2) * HEAD_DIM] = k_odd[:, _head(h)]

    def kv_scan(h):
        hs = _head(h)
        pair = 2 * REF_CHUNK
        for r in range(n_chunks // 2):
            rows = slice(r * pair, (r + 1) * pair)
            kv_ref[r] = _dot_tn(va_ref[rows, hs], ke_ref[rows, 2 * h * HEAD_DIM:(2 * h + 2) * HEAD_DIM])
        st = jnp.where(first, 0.0, st_ref[h])
        for n in range(n_chunks):
            sbf_ref[n, h] = st.T.astype(BF16)
            st = st * dec_ref[n:n + 1, hs] + kv_ref[n // 2, :, (n % 2) * HEAD_DIM:(n % 2 + 1) * HEAD_DIM]
        st_ref[h] = st
        sa_ref[0, h] = st.T

    def diag(h):
        hs = _head(h)
        sc = jnp.where(_causal_in_chunk(tl, shift), _dot(qd_ref[:, hs], kd_ref[h]), 0.0)
        oa_ref[:, hs] = _dot(sc.astype(BF16), va_ref[:, hs])

    def inter(n):
        rows = slice(n * REF_CHUNK, (n + 1) * REF_CHUNK)
        for h in range(N_HEADS):
            hs = _head(h)
            oa_ref[rows, hs] += _dot(qd_ref[rows, hs], sbf_ref[n, h])

    def hgrn_out():
        for h in range(N_HEADS):
            hs = _head(h)
            mix_ref[:, hs] = _rms_gate(oa_ref[:, hs], ag_ref[:, hs], proj_ref[:, _head(h, 3)]).astype(BF16)

    def ret(h):
        hs = _head(h)
        cos = cos_ref[...]
        sin = sin_ref[...]
        q = _rope(proj_ref[:, _head(h, 4)], cos, sin)
        k = _rope(proj_ref[:, _head(h, 5)], cos, sin) * K_SCALE
        v = proj_ref[:, _head(h, 6)].astype(BF16)
        a = (_dot(q.astype(BF16), k.T.astype(BF16)) * dm_ref[h]).astype(BF16)
        s = jnp.where(first, 0.0, sb_ref[0, h])
        o = _dot(a, v) + _dot((q * rd_ref[0, h]).astype(BF16), s.astype(BF16))
        k_end_b = (k * rd_ref[1, h]).astype(BF16)
        sb_ref[0, h] = s * math.exp(tl * RET_LOG_DECAY[h]) + _dot_tn(k_end_b, v)
        mix_ref[:, _head(h, 1)] = _ln_gate(o, bg_ref[:, hs], bb_ref[:, hs],
                                           proj_ref[:, _head(h, 7)]).astype(BF16)

    return (prepass,
            [functools.partial(kv_scan, h) for h in range(N_HEADS)],
            [functools.partial(diag, h) for h in range(N_HEADS)],
            [functools.partial(inter, n) for n in range(n_chunks)],
            hgrn_out,
            [functools.partial(ret, h) for h in range(N_HEADS)])


def _interleave(a, b):
    out = []
    for i in range(max(len(a), len(b))):
        out += a[i:i + 1] + b[i:i + 1]
    return out


def _prompt_layer_kernel(xn_ref, xp_ref, p_ref, cos_ref, sin_ref, w_in_ref, lb_ref, ag_ref, bg_ref,
                         bb_ref, *rest, tiles_per_seq):
    tail_in = rest[:11]
    y_ref, sa_ref, sb_ref = rest[11:14]
    mixer_scratch = rest[14:27]
    proj_ref, mix_ref, dm_ref, rd_ref = mixer_scratch[0], mixer_scratch[3], mixer_scratch[11], mixer_scratch[12]
    xb_ref = rest[27]
    tail_scratch = rest[28:31]
    w_vmem, w_sem = rest[31:-1], rest[-1]
    tail_w = list(tail_in)
    for k, pos in enumerate(_TAIL_MATRIX_POS):
        tail_w[pos] = w_vmem[k]

    def weight_copies():
        return [pltpu.make_async_copy(tail_in[pos], w_vmem[k], w_sem.at[k])
                for k, pos in enumerate(_TAIL_MATRIX_POS)]
    g = pl.program_id(0)
    n_tiles = pl.num_programs(0) - 1
    slot = lax.rem(g, 2)

    def mixer_steps():
        return _prompt_mixer_steps(cos_ref, sin_ref, lb_ref, ag_ref, bg_ref, bb_ref, sa_ref, sb_ref,
                                   proj_ref.at[slot], *mixer_scratch[1:],
                                   first=lax.rem(g, tiles_per_seq) == 0)

    def tail_steps():
        return _tail_steps(xp_ref, mix_ref, p_ref, y_ref, *tail_w, *tail_scratch)

    @pl.when(g == 0)
    def _():
        for copy in weight_copies():
            copy.start()
        _ret_tables(dm_ref, rd_ref)
        prepass, kv_scan, diag, inter, hgrn_out, ret = mixer_steps()
        steps = _in_proj_steps(xp_ref, w_in_ref, proj_ref.at[slot], xb_ref)
        steps += [prepass] + kv_scan + diag + inter + [hgrn_out] + ret
        steps += _in_proj_steps(xn_ref, w_in_ref, proj_ref.at[1 - slot], xb_ref)
        for step in steps:
            step()

    @pl.when((g > 0) & (g < n_tiles))
    def _():
        @pl.when(g == 1)
        def _():
            for copy in weight_copies():
                copy.wait()

        prepass, kv_scan, diag, inter, hgrn_out, ret = mixer_steps()
        out_proj, ff, down, final = tail_steps()
        in_proj = _in_proj_steps(xn_ref, w_in_ref, proj_ref.at[1 - slot], xb_ref)
        inter_pairs = [lambda a=a, b=b: (a(), b()) for a, b in zip(inter[0::2], inter[1::2])]
        steps = [out_proj, in_proj[0], prepass, in_proj[1]]
        steps += _interleave(ff + down, kv_scan + diag + ret + inter_pairs + [hgrn_out])
        steps += _interleave(final, in_proj[2:])
        for step in steps:
            step()

    @pl.when(g == n_tiles)
    def _():
        out_proj, ff, down, final = tail_steps()
        for step in [out_proj] + ff + down + final:
            step()


def _prompt_layer(x, p, cos, sin, w_in, lb_logits, a_g, b_g, b_b, tail_w):
    bsz, seq, _ = x.shape
    tl = TOKEN_TILE
    tps = seq // tl
    n_tiles = bsz * tps
    assert n_tiles >= 2, "the tail weights are awaited in the second grid step"
    x2 = x.reshape(bsz * seq, D_MODEL)
    p2 = p.reshape(bsz * seq, PLE_DIM)
    const = lambda g: (0, 0)
    nxt = lambda g: (jnp.minimum(g + 1, n_tiles - 1), 0)
    prev = lambda g: (jnp.maximum(g - 1, 0), 0)
    seq_tile = lambda g: (lax.rem(jnp.minimum(g, n_tiles - 1), tps), 0)
    state_spec = pl.BlockSpec((1, N_HEADS, HEAD_DIM, HEAD_DIM),
                              lambda g: (jnp.minimum(g, n_tiles - 1) // tps, 0, 0, 0))
    state_shape = jax.ShapeDtypeStruct((bsz, N_HEADS, HEAD_DIM, HEAD_DIM), F32)
    return pl.pallas_call(
        functools.partial(_prompt_layer_kernel, tiles_per_seq=tps),
        grid=(n_tiles + 1,),
        in_specs=[
            pl.BlockSpec((tl, D_MODEL), nxt),
            pl.BlockSpec((tl, D_MODEL), prev),
            pl.BlockSpec((tl, PLE_DIM), prev),
            pl.BlockSpec((tl, HEAD_DIM), seq_tile),
            pl.BlockSpec((tl, HEAD_DIM), seq_tile),
            pl.BlockSpec((D_MODEL, IN_COLS), const, pipeline_mode=pl.Buffered(1)),
            pl.BlockSpec(lb_logits.shape, const),
            pl.BlockSpec((1, GROUP_W), const),
            pl.BlockSpec((1, GROUP_W), const),
            pl.BlockSpec((1, GROUP_W), const),
        ] + _tail_specs(),
        out_specs=[
            pl.BlockSpec((tl, D_MODEL), prev),
            state_spec,
            state_spec,
        ],
        out_shape=[
            jax.ShapeDtypeStruct((bsz * seq, D_MODEL), F32),
            state_shape,
            state_shape,
        ],
        scratch_shapes=[
            pltpu.VMEM((2, tl, IN_COLS), F32),
            pltpu.VMEM((N_HEADS, HEAD_DIM, HEAD_DIM), F32),
            pltpu.VMEM((tl, GROUP_W), F32),
            pltpu.VMEM((tl, 2 * GROUP_W), BF16),
            pltpu.VMEM((tl, GROUP_W), BF16),
            pltpu.VMEM((N_HEADS, HEAD_DIM, tl), BF16),
            pltpu.VMEM((tl, 2 * GROUP_W), BF16),
            pltpu.VMEM((tl // REF_CHUNK, GROUP_W), F32),
            pltpu.VMEM((tl, GROUP_W), BF16),
            pltpu.VMEM((tl // (2 * REF_CHUNK), HEAD_DIM, 2 * HEAD_DIM), F32),
            pltpu.VMEM((tl // REF_CHUNK, N_HEADS, HEAD_DIM, HEAD_DIM), BF16),
            pltpu.VMEM((N_HEADS, tl, tl), F32),
            pltpu.VMEM((2, N_HEADS, tl, HEAD_DIM), F32),
            pltpu.VMEM((tl, D_MODEL), BF16),
        ] + _tail_scratch(tl) + _tail_matrix_scratch(),
        compiler_params=pltpu.CompilerParams(
            dimension_semantics=("arbitrary",), vmem_limit_bytes=V7X_VMEM_LIMIT_BYTES),
        name="prompt_layer",
    )(x2, x2, p2, cos, sin, w_in, lb_logits, a_g, b_g, b_b, *tail_w)


def _cast_kernel(w_ref, wb_ref):
    wb_ref[...] = w_ref[...].astype(BF16)


def _w_in_bf16(w_in):
    tn = SAMPLE_PROJ_COLS
    return pl.pallas_call(
        _cast_kernel,
        grid=(IN_COLS // tn,),
        in_specs=[pl.BlockSpec((D_MODEL, tn), lambda c: (0, c))],
        out_specs=pl.BlockSpec((D_MODEL, tn), lambda c: (0, c)),
        out_shape=jax.ShapeDtypeStruct((D_MODEL, IN_COLS), BF16),
        compiler_params=pltpu.CompilerParams(
            dimension_semantics=("arbitrary",), vmem_limit_bytes=V7X_VMEM_LIMIT_BYTES),
        name="w_in_cast",
    )(w_in)


def _sample_rec_kernel(x_ref, w_in_ref, sa_in_ref, sb_in_ref, lb_ref, ag_ref, bg_ref, bb_ref,
                       cos_ref, sin_ref, mix_ref, sa_ref, sb_ref, proj_ref, oa_ref, ob_ref, *, seq_len):
    rows_n = proj_ref.shape[0]
    n_seq = rows_n // seq_len
    causal = _causal_in_chunk(rows_n, seq_len.bit_length() - 1)

    xb = x_ref[...].astype(BF16)

    def in_proj(c):
        cols = slice(c * GROUP_W, (c + 1) * GROUP_W)
        proj_ref[:, cols] = _dot(xb, w_in_ref[:, cols])

    half = IN_COLS // GROUP_W // 2
    for c in range(half):
        in_proj(c)

    q_dec, k_dec, kk, b = _hgrn_prepass(proj_ref, _lower_bound(lb_ref), causal)
    v_a = proj_ref[:, 2 * GROUP_W:3 * GROUP_W]
    for h in range(N_HEADS):
        hs = _head(h)
        in_proj(half + h)
        sc = jnp.where(causal, _dot_nt(q_dec[:, hs], k_dec[:, hs].astype(BF16)), 0.0).astype(BF16)
        oa_ref[:, hs] = _dot(sc, v_a[:, hs].astype(BF16))
    q_dec32 = q_dec.astype(F32)
    rr = lax.broadcasted_iota(jnp.int32, (seq_len, GROUP_W), 0)
    ones_blk = jnp.ones((seq_len, HEAD_DIM), BF16)
    for s in range(n_seq):
        rows = slice(s * seq_len, (s + 1) * seq_len)
        b_last = b[(s + 1) * seq_len - 1:(s + 1) * seq_len, :]
        k_end = (kk[rows] * jnp.exp(b_last - b[rows])).astype(BF16)
        hi, mid, lo = [t.astype(F32) for t in _split3(jnp.exp(b_last))]
        dec_rows = jnp.where(rr == 0, hi, jnp.where(rr == 1, mid, jnp.where(rr == 2, lo, 0.0)))
        dec_rows = dec_rows.astype(BF16)
        for h in range(N_HEADS):
            hs = _head(h)
            st = sa_in_ref[s, h]
            oa_ref[rows, hs] += _dot(q_dec32[rows, hs].astype(BF16), st.astype(BF16))
            dec_kv = _dot_tn(dec_rows[:, hs], ones_blk)
            sa_ref[s, h] = st * dec_kv + _dot_tn(k_end[:, hs], v_a[rows, hs].astype(BF16))
    for h in range(N_HEADS):
        hs = _head(h)
        mix_ref[:, hs] = _rms_gate(oa_ref[:, hs], ag_ref[:, hs], proj_ref[:, _head(h, 3)]).astype(BF16)

    cos = jnp.concatenate([cos_ref[...]] * n_seq, axis=0)
    sin = jnp.concatenate([sin_ref[...]] * n_seq, axis=0)
    r = lax.broadcasted_iota(jnp.int32, (rows_n, rows_n), 0)
    c = lax.broadcasted_iota(jnp.int32, (rows_n, rows_n), 1)
    diff = ((r & (seq_len - 1)) - (c & (seq_len - 1))).astype(F32)
    row = (lax.broadcasted_iota(jnp.int32, (rows_n, HEAD_DIM), 0) & (seq_len - 1)).astype(F32)
    for h in range(N_HEADS):
        hs = _head(h)
        logd = RET_LOG_DECAY[h]
        q = _rope(proj_ref[:, _head(h, 4)], cos, sin)
        k = _rope(proj_ref[:, _head(h, 5)], cos, sin) * K_SCALE
        v32 = proj_ref[:, _head(h, 6)]
        dmask = jnp.where(causal, jnp.exp(diff * logd), 0.0)
        a = (_dot_nt(q.astype(BF16), k.astype(BF16)) * dmask).astype(BF16)
        ob_ref[...] = _dot(a, v32.astype(BF16))
        q_dec_b = q * jnp.exp((row + 1.0) * logd)
        k_end_b = k * jnp.exp((seq_len - 1.0 - row) * logd)
        for s in range(n_seq):
            rows = slice(s * seq_len, (s + 1) * seq_len)
            st = sb_in_ref[s, h]
            ob_ref[rows, :] += _dot(q_dec_b[rows].astype(BF16), st.astype(BF16))
            sb_ref[s, h] = st * math.exp(seq_len * logd) + _dot_tn(
                k_end_b[rows].astype(BF16), v32[rows].astype(BF16))
        mix_ref[:, _head(h, 1)] = _ln_gate(ob_ref[...], bg_ref[:, hs], bb_ref[:, hs],
                                           proj_ref[:, _head(h, 7)]).astype(BF16)


def _sample_rec(x, w_in, sa, sb, lb_logits, a_g, b_g, b_b, cos, sin, seq_len):
    n_tok = x.shape[0]
    n_seq = n_tok // seq_len
    bs = SAMPLE_SEQS
    rows = bs * seq_len
    const = lambda i: (0, 0)
    state_spec = pl.BlockSpec((bs, N_HEADS, HEAD_DIM, HEAD_DIM), lambda i: (i, 0, 0, 0))
    state_shape = jax.ShapeDtypeStruct((n_seq, N_HEADS, HEAD_DIM, HEAD_DIM), F32)
    return pl.pallas_call(
        functools.partial(_sample_rec_kernel, seq_len=seq_len),
        grid=(n_seq // bs,),
        in_specs=[
            pl.BlockSpec((rows, D_MODEL), lambda i: (i, 0)),
            pl.BlockSpec((D_MODEL, IN_COLS), const, pipeline_mode=pl.Buffered(1)),
            state_spec,
            state_spec,
            pl.BlockSpec(lb_logits.shape, const),
            pl.BlockSpec((1, GROUP_W), const),
            pl.BlockSpec((1, GROUP_W), const),
            pl.BlockSpec((1, GROUP_W), const),
            pl.BlockSpec((seq_len, HEAD_DIM), const),
            pl.BlockSpec((seq_len, HEAD_DIM), const),
        ],
        out_specs=[
            pl.BlockSpec((rows, 2 * GROUP_W), lambda i: (i, 0)),
            state_spec,
            state_spec,
        ],
        out_shape=[
            jax.ShapeDtypeStruct((n_tok, 2 * GROUP_W), BF16),
            state_shape,
            state_shape,
        ],
        scratch_shapes=[
            pltpu.VMEM((rows, IN_COLS), F32),
            pltpu.VMEM((rows, GROUP_W), F32),
            pltpu.VMEM((rows, HEAD_DIM), F32),
        ],
        compiler_params=pltpu.CompilerParams(
            dimension_semantics=("arbitrary",), vmem_limit_bytes=V7X_VMEM_LIMIT_BYTES),
        name="sample_recurrence",
    )(x, w_in, sa, sb, lb_logits, a_g, b_g, b_b, cos, sin)


_TAIL_PHASE_STEPS = (D_MODEL // DOWN_CHUNK, D_FF // FF_CHUNK, D_MODEL // DOWN_CHUNK, D_MODEL // DOWN_CHUNK)
_TAIL_PHASE_START = tuple(sum(_TAIL_PHASE_STEPS[:i]) for i in range(4))


def _sample_tail_kernel(x_ref, mix_ref, p_ref, wo_ref, wg_ref, wu_ref, wd_ref, wpg_ref, wpp_ref,
                        ln1g_ref, ln1b_ref, ln2g_ref, ln2b_ref, bpg_ref,
                        y_ref, wo_b, wg_b, wu_b, wd_b, wpg_b, wpp_b, act_ref, h_ref, hb_ref):
    s = pl.program_id(0)
    a0, b0, c0, d0 = _TAIL_PHASE_START
    blk = DOWN_CHUNK

    @pl.when(s < b0)
    def _():
        wo_b[...] = wo_ref[...].astype(BF16)
        cols = pl.ds(pl.multiple_of((s - a0) * blk, blk), blk)
        h_ref[:, cols] = DN_ALPHA * x_ref[...] + _dot(mix_ref[...], wo_b[...])

        @pl.when(s == b0 - 1)
        def _():
            h = _layer_norm(h_ref[...], ln1g_ref[...], ln1b_ref[...])
            h_ref[...] = h
            hb_ref[...] = h.astype(BF16)

    @pl.when((s >= b0) & (s < c0))
    def _():
        wg_b[...] = wg_ref[...].astype(BF16)
        wu_b[...] = wu_ref[...].astype(BF16)
        cols = pl.ds(pl.multiple_of((s - b0) * FF_CHUNK, FF_CHUNK), FF_CHUNK)
        hb = hb_ref[...]
        act_ref[:, cols] = (_silu(_dot(hb, wg_b[...])) * _dot(hb, wu_b[...])).astype(BF16)

    @pl.when((s >= c0) & (s < d0))
    def _():
        wd_b[...] = wd_ref[...].astype(BF16)
        cols = pl.ds(pl.multiple_of((s - c0) * blk, blk), blk)
        h_ref[:, cols] = DN_ALPHA * h_ref[:, cols] + _dot(act_ref[...], wd_b[...])

        @pl.when(s == d0 - 1)
        def _():
            h2 = _layer_norm(h_ref[...], ln2g_ref[...], ln2b_ref[...])
            h_ref[...] = h2
            hb_ref[...] = h2.astype(BF16)

    @pl.when(s >= d0)
    def _():
        wpg_b[...] = wpg_ref[...].astype(BF16)
        wpp_b[...] = wpp_ref[...].astype(BF16)
        cols = pl.ds(pl.multiple_of((s - d0) * blk, blk), blk)
        gate = _sigmoid(_dot(hb_ref[...], wpg_b[...]) + bpg_ref[:, cols])
        y_ref[...] = h_ref[:, cols] + gate * _dot(p_ref[...].astype(BF16), wpp_b[...])


def _sample_tail(x, mix, p, w_out, ln1g, ln1b, wg, wu, wd, ln2g, ln2b, wpp, wpg, bpg):
    n = x.shape[0]
    a0, b0, c0, d0 = _TAIL_PHASE_START
    na, nb, nc, nd = _TAIL_PHASE_STEPS
    const = lambda s: (0, 0)
    col = lambda start, count: (lambda s: (0, jnp.clip(s - start, 0, count - 1)))
    vec = pl.BlockSpec((1, D_MODEL), const)
    weight_specs = [
        pl.BlockSpec((2 * GROUP_W, DOWN_CHUNK), col(a0, na)),
        pl.BlockSpec((D_MODEL, FF_CHUNK), col(b0, nb)),
        pl.BlockSpec((D_MODEL, FF_CHUNK), col(b0, nb)),
        pl.BlockSpec((D_FF, DOWN_CHUNK), col(c0, nc)),
        pl.BlockSpec((D_MODEL, DOWN_CHUNK), col(d0, nd)),
        pl.BlockSpec((PLE_DIM, DOWN_CHUNK), col(d0, nd)),
    ]
    weights = (w_out, wg, wu, wd, wpg, wpp)
    return pl.pallas_call(
        _sample_tail_kernel,
        grid=(sum(_TAIL_PHASE_STEPS),),
        in_specs=[
            pl.BlockSpec((n, DOWN_CHUNK), col(a0, na)),
            pl.BlockSpec((n, 2 * GROUP_W), const),
            pl.BlockSpec((n, PLE_DIM), const),
        ] + weight_specs + [vec, vec, vec, vec, vec],
        out_specs=[pl.BlockSpec((n, DOWN_CHUNK), col(d0, nd))] + weight_specs,
        out_shape=[jax.ShapeDtypeStruct((n, D_MODEL), F32)]
        + [jax.ShapeDtypeStruct(w.shape, BF16) for w in weights],
        scratch_shapes=_tail_scratch(n),
        compiler_params=pltpu.CompilerParams(
            dimension_semantics=("arbitrary",), vmem_limit_bytes=V7X_VMEM_LIMIT_BYTES),
        name="sample_tail",
    )(x, mix, p, *weights, ln1g, ln1b, ln2g, ln2b, bpg)


def kernel(x_prompt, x_sample, p_prompt, p_sample, state_hgrn, state_ret, lb_logits, w_in, a_norm_g, b_norm_g, b_norm_b, w_out, ln1_g, ln1_b, w_ffn_gate, w_ffn_up, w_ffn_down, ln2_g, ln2_b, w_ple_proj, w_ple_gate, b_ple_gate):
    assert w_in.shape[0] == DEPTH == 1
    bsz, seq, _ = x_prompt.shape
    n_dec, dec_seq, _ = x_sample.shape

    mixer_vecs = (lb_logits, a_norm_g, b_norm_g, b_norm_b)
    cos_p, sin_p = _rope_tables(seq, 0)
    cos_s, sin_s = _rope_tables(dec_seq, PAST_LEN)

    x_s = x_sample.reshape(n_dec * dec_seq, D_MODEL)
    w_in_b = _w_in_bf16(w_in[0])
    mix_s, sa_s, sb_s = _sample_rec(x_s, w_in_b, state_hgrn[0], state_ret[0], *mixer_vecs,
                                    cos_s, sin_s, dec_seq)
    y_s, w_out_b, wg_b, wu_b, wd_b, wpg_b, wpp_b = _sample_tail(
        x_s, mix_s, p_sample[0].reshape(n_dec * dec_seq, PLE_DIM), w_out[0], ln1_g, ln1_b,
        w_ffn_gate[0], w_ffn_up[0], w_ffn_down[0], ln2_g, ln2_b, w_ple_proj[0], w_ple_gate[0],
        b_ple_gate)

    tail_w = (w_out_b, ln1_g, ln1_b, wg_b, wu_b, wd_b, ln2_g, ln2_b, wpp_b, wpg_b, b_ple_gate)
    y_p, sa_p, sb_p = _prompt_layer(x_prompt, p_prompt[0], cos_p, sin_p, w_in_b, *mixer_vecs, tail_w)

    return (y_p.reshape(bsz, seq, D_MODEL), y_s.reshape(n_dec, dec_seq, D_MODEL),
            sa_p[None], sb_p[None], sa_s[None], sb_s[None])
```

```python
import functools
import math

import jax
import jax.numpy as jnp
from jax import lax
from jax.experimental import pallas as pl
from jax.experimental.pallas import tpu as pltpu

F32 = jnp.float32
BF16 = jnp.bfloat16

D_MODEL = 1024
N_HEADS = 4
HEAD_DIM = 128
GROUP_W = N_HEADS * HEAD_DIM
IN_COLS = 8 * GROUP_W
D_FF = 2816
PLE_DIM = 256
DEPTH = 1
PAST_LEN = 16384
REF_CHUNK = 32
ROPE_BASE = 10000.0
NORM_EPS = 1e-5
DN_ALPHA = (2.0 * DEPTH) ** 0.25
RET_LOG_DECAY = tuple(math.log1p(-(2.0 ** (-5.0 - h))) for h in range(N_HEADS))
K_SCALE = HEAD_DIM ** -0.5

V7X_VMEM_LIMIT_BYTES = 60 * 1024 * 1024

TOKEN_TILE = 256
SAMPLE_PROJ_COLS = 1024
SAMPLE_SEQS = 16
FF_CHUNK = 256
DOWN_CHUNK = 256


def _dot(a, b):
    return jnp.dot(a, b, preferred_element_type=F32)


def _dot_nt(a, b):
    return lax.dot_general(a, b, (((1,), (1,)), ((), ())), preferred_element_type=F32)


def _dot_tn(a, b):
    return lax.dot_general(a, b, (((0,), (0,)), ((), ())), preferred_element_type=F32)


def _split3(x):
    hi = x.astype(BF16)
    r1 = x - hi.astype(F32)
    mid = r1.astype(BF16)
    lo = (r1 - mid.astype(F32)).astype(BF16)
    return hi, mid, lo


def _dot_exact_lhs01(m01, parts):
    return _dot(jnp.concatenate([m01] * 3, axis=1), jnp.concatenate(list(parts), axis=0))


def _sigmoid(x):
    return 1.0 / (1.0 + jnp.exp(-x))


def _silu(x):
    return x * _sigmoid(x)


def _causal_in_chunk(n, shift):
    r = lax.broadcasted_iota(jnp.int32, (n, n), 0)
    c = lax.broadcasted_iota(jnp.int32, (n, n), 1)
    return ((r >> shift) == (c >> shift)) & (c <= r)


def _lower_bound(lb_ref):
    rows = [lb_ref[i:i + 1, :] for i in range(lb_ref.shape[0])]
    m = functools.reduce(jnp.maximum, rows)
    e = [jnp.exp(r - m) for r in rows]
    return e[0] / functools.reduce(jnp.add, e)


def _hgrn_prepass(proj_ref, lb, causal):
    tri = jnp.where(causal, 1.0, 0.0).astype(BF16)
    f = lb + (1.0 - lb) * _sigmoid(proj_ref[:, GROUP_W:2 * GROUP_W])
    kk = 1.0 - f
    b = _dot_exact_lhs01(tri, _split3(jnp.log(f)))
    q_dec = (_silu(proj_ref[:, 0:GROUP_W]) * jnp.exp(b)).astype(BF16)
    k_dec = kk * jnp.exp(-b)
    return q_dec, k_dec, kk, b


def _rope(x, cos, sin_signed):
    return x * cos + pltpu.roll(x, HEAD_DIM // 2, axis=1) * sin_signed


def _rms_gate(o, g, gate):
    return o * lax.rsqrt(jnp.mean(o * o, axis=-1, keepdims=True) + NORM_EPS) * g * _silu(gate)


def _layer_norm(x, g, b):
    mu = jnp.mean(x, axis=-1, keepdims=True)
    d = x - mu
    var = jnp.mean(d * d, axis=-1, keepdims=True)
    return d * lax.rsqrt(var + NORM_EPS) * g + b


def _ln_gate(o, g, b, gate):
    return _layer_norm(o, g, b) * _silu(gate)


def _head(h, group=0):
    return slice(group * GROUP_W + h * HEAD_DIM, group * GROUP_W + (h + 1) * HEAD_DIM)


def _rope_table_kernel(cos_ref, sin_ref, *, offset):
    n = cos_ref.shape[0]
    half = HEAD_DIM // 2
    packed = n % 16 == 0
    m = n // 2 if packed else n
    row = lax.broadcasted_iota(jnp.int32, (m, HEAD_DIM), 0) + pl.program_id(0) * n
    lane = lax.broadcasted_iota(jnp.int32, (m, HEAD_DIM), 1)
    low = lane < half
    if packed:
        row = row + jnp.where(low, 0, m)
    j = (lane & (half - 1)).astype(F32)
    inv = jnp.exp(-(j / half) * math.log(ROPE_BASE))
    ang = (row.astype(F32) + offset) * inv
    c = jnp.cos(ang)
    s = jnp.sin(ang)
    if not packed:
        cos_ref[...] = c
        sin_ref[...] = jnp.where(low, -s, s)
        return
    c_sw = pltpu.roll(c, half, axis=1)
    s_sw = pltpu.roll(s, half, axis=1)
    cos_ref[0:m, :] = jnp.where(low, c, c_sw)
    cos_ref[m:n, :] = jnp.where(low, c_sw, c)
    sin_ref[0:m, :] = jnp.where(low, -s, s_sw)
    sin_ref[m:n, :] = jnp.where(low, -s_sw, s)


def _rope_tables(n, offset):
    tile = min(n, 512)
    return pl.pallas_call(
        functools.partial(_rope_table_kernel, offset=float(offset)),
        grid=(n // tile,),
        in_specs=[],
        out_specs=[pl.BlockSpec((tile, HEAD_DIM), lambda i: (i, 0))] * 2,
        out_shape=[jax.ShapeDtypeStruct((n, HEAD_DIM), F32)] * 2,
        name="rope_tables",
    )()


def _tail_steps(x_ref, mix_ref, p_ref, y_ref, w_out_ref, ln1g_ref, ln1b_ref, wg_ref, wu_ref, wd_ref,
                ln2g_ref, ln2b_ref, wpp_ref, wpg_ref, bpg_ref, act_ref, h_ref, hb_ref):
    def out_proj():
        for c in range(D_MODEL // DOWN_CHUNK):
            cols = slice(c * DOWN_CHUNK, (c + 1) * DOWN_CHUNK)
            h_ref[:, cols] = DN_ALPHA * x_ref[:, cols] + _dot(mix_ref[...], w_out_ref[:, cols])
        h = _layer_norm(h_ref[...], ln1g_ref[...], ln1b_ref[...])
        h_ref[...] = h
        hb_ref[...] = h.astype(BF16)

    def ff(c):
        cols = slice(c * FF_CHUNK, (c + 1) * FF_CHUNK)
        hb = hb_ref[...]
        act_ref[:, cols] = (_silu(_dot(hb, wg_ref[:, cols])) * _dot(hb, wu_ref[:, cols])).astype(BF16)

    def down(c):
        cols = slice(c * DOWN_CHUNK, (c + 1) * DOWN_CHUNK)
        h_ref[:, cols] = DN_ALPHA * h_ref[:, cols] + _dot(act_ref[...], wd_ref[:, cols])

    def norm2():
        h2 = _layer_norm(h_ref[...], ln2g_ref[...], ln2b_ref[...])
        h_ref[...] = h2
        hb_ref[...] = h2.astype(BF16)

    def ple(c):
        cols = slice(c * DOWN_CHUNK, (c + 1) * DOWN_CHUNK)
        gate = _sigmoid(_dot(hb_ref[...], wpg_ref[:, cols]) + bpg_ref[:, cols])
        y_ref[:, cols] = h_ref[:, cols] + gate * _dot(p_ref[...].astype(BF16), wpp_ref[:, cols])

    return (out_proj,
            [functools.partial(ff, c) for c in range(D_FF // FF_CHUNK)],
            [functools.partial(down, c) for c in range(D_MODEL // DOWN_CHUNK)],
            [norm2] + [functools.partial(ple, c) for c in range(D_MODEL // DOWN_CHUNK)])


def _tail_scratch(tl):
    return [pltpu.VMEM((tl, D_FF), BF16), pltpu.VMEM((tl, D_MODEL), F32), pltpu.VMEM((tl, D_MODEL), BF16)]


_TAIL_MATRIX_POS = (0, 3, 4, 5, 8, 9)
_TAIL_MATRIX_SHAPES = ((2 * GROUP_W, D_MODEL), (D_MODEL, D_FF), (D_MODEL, D_FF), (D_FF, D_MODEL),
                       (PLE_DIM, D_MODEL), (D_MODEL, D_MODEL))


def _tail_specs():
    const = lambda i: (0, 0)
    in_hbm = pl.BlockSpec(memory_space=pl.ANY)
    vec = pl.BlockSpec((1, D_MODEL), const)
    return [in_hbm, vec, vec, in_hbm, in_hbm, in_hbm, vec, vec, in_hbm, in_hbm, vec]


def _tail_matrix_scratch():
    return ([pltpu.VMEM(shape, BF16) for shape in _TAIL_MATRIX_SHAPES]
            + [pltpu.SemaphoreType.DMA((len(_TAIL_MATRIX_SHAPES),))])


def _in_proj_steps(x_ref, w_in_ref, proj_ref, xb_ref):
    def in_proj(c):
        if c == 0:
            xb_ref[...] = x_ref[...].astype(BF16)
        cols = slice(c * GROUP_W, (c + 1) * GROUP_W)
        proj_ref[:, cols] = _dot(xb_ref[...], w_in_ref[:, cols])

    return [functools.partial(in_proj, c) for c in range(IN_COLS // GROUP_W)]


def _ret_tables(dm_ref, rd_ref):
    tl = dm_ref.shape[1]
    r = lax.broadcasted_iota(jnp.int32, (tl, tl), 0)
    c = lax.broadcasted_iota(jnp.int32, (tl, tl), 1)
    row = lax.broadcasted_iota(jnp.int32, (tl, HEAD_DIM), 0).astype(F32)
    for h in range(N_HEADS):
        logd = RET_LOG_DECAY[h]
        dm_ref[h] = jnp.where(r >= c, jnp.exp((r - c).astype(F32) * logd), 0.0)
        rd_ref[0, h] = jnp.exp((row + 1.0) * logd)
        rd_ref[1, h] = jnp.exp((tl - 1.0 - row) * logd)


def _prompt_mixer_steps(cos_ref, sin_ref, lb_ref, ag_ref, bg_ref, bb_ref,
                        sa_ref, sb_ref, proj_ref, st_ref, oa_ref, mix_ref,
                        qd_ref, kd_ref, ke_ref, dec_ref, va_ref, kv_ref, sbf_ref, dm_ref, rd_ref, first):
    tl = proj_ref.shape[0]
    shift = REF_CHUNK.bit_length() - 1
    n_chunks = tl // REF_CHUNK

    def prepass():
        q_dec, k_dec, kk, b = _hgrn_prepass(proj_ref, _lower_bound(lb_ref), _causal_in_chunk(tl, shift))
        qd_ref[...] = q_dec
        for h in range(N_HEADS):
            kd_ref[h] = k_dec[:, _head(h)].T.astype(BF16)
        va_ref[...] = proj_ref[:, 2 * GROUP_W:3 * GROUP_W].astype(BF16)
        last = [b[(n + 1) * REF_CHUNK - 1:(n + 1) * REF_CHUNK, :] for n in range(n_chunks)]
        b_last = jnp.concatenate([jnp.broadcast_to(r, (REF_CHUNK, GROUP_W)) for r in last], axis=0)
        k_end = kk * jnp.exp(b_last - b)
        for n in range(n_chunks):
            dec_ref[n:n + 1, :] = jnp.exp(last[n])
        odd = ((lax.broadcasted_iota(jnp.int32, (tl, GROUP_W), 0) >> shift) & 1) == 1
        k_even = jnp.where(odd, 0.0, k_end).astype(BF16)
        k_odd = jnp.where(odd, k_end, 0.0).astype(BF16)
        for h in range(N_HEADS):
            ke_ref[:, 2 * h * HEAD_DIM:(2 * h + 1) * HEAD_DIM] = k_even[:, _head(h)]
            ke_ref[:, (2 * h + 1) * HEAD_DIM:(2 * h + 2) * HEAD_DIM] = k_odd[:, _head(h)]

    def kv_scan(h):
        hs = _head(h)
        pair = 2 * REF_CHUNK
        for r in range(n_chunks // 2):
            rows = slice(r * pair, (r + 1) * pair)
            kv_ref[r] = _dot_tn(va_ref[rows, hs], ke_ref[rows, 2 * h * HEAD_DIM:(2 * h + 2) * HEAD_DIM])
        st = jnp.where(first, 0.0, st_ref[h])
        for n in range(n_chunks):
            sbf_ref[n, h] = st.T.astype(BF16)
            st = st * dec_ref[n:n + 1, hs] + kv_ref[n // 2, :, (n % 2) * HEAD_DIM:(n % 2 + 1) * HEAD_DIM]
        st_ref[h] = st
        sa_ref[0, h] = st.T

    def diag(h):
        hs = _head(h)
        sc = jnp.where(_causal_in_chunk(tl, shift), _dot(qd_ref[:, hs], kd_ref[h]), 0.0)
        oa_ref[:, hs] = _dot(sc.astype(BF16), va_ref[:, hs])

    def inter(n):
        rows = slice(n * REF_CHUNK, (n + 1) * REF_CHUNK)
        for h in range(N_HEADS):
            hs = _head(h)
            oa_ref[rows, hs] += _dot(qd_ref[rows, hs], sbf_ref[n, h])

    def hgrn_out():
        for h in range(N_HEADS):
            hs = _head(h)
            mix_ref[:, hs] = _rms_gate(oa_ref[:, hs], ag_ref[:, hs], proj_ref[:, _head(h, 3)]).astype(BF16)

    def ret(h):
        hs = _head(h)
        cos = cos_ref[...]
        sin = sin_ref[...]
        q = _rope(proj_ref[:, _head(h, 4)], cos, sin)
        k = _rope(proj_ref[:, _head(h, 5)], cos, sin) * K_SCALE
        v = proj_ref[:, _head(h, 6)].astype(BF16)
        a = (_dot(q.astype(BF16), k.T.astype(BF16)) * dm_ref[h]).astype(BF16)
        s = jnp.where(first, 0.0, sb_ref[0, h])
        o = _dot(a, v) + _dot((q * rd_ref[0, h]).astype(BF16), s.astype(BF16))
        k_end_b = (k * rd_ref[1, h]).astype(BF16)
        sb_ref[0, h] = s * math.exp(tl * RET_LOG_DECAY[h]) + _dot_tn(k_end_b, v)
        mix_ref[:, _head(h, 1)] = _ln_gate(o, bg_ref[:, hs], bb_ref[:, hs],
                                           proj_ref[:, _head(h, 7)]).astype(BF16)

    return (prepass,
            [functools.partial(kv_scan, h) for h in range(N_HEADS)],
            [functools.partial(diag, h) for h in range(N_HEADS)],
            [functools.partial(inter, n) for n in range(n_chunks)],
            hgrn_out,
            [functools.partial(ret, h) for h in range(N_HEADS)])


def _interleave(a, b):
    out = []
    for i in range(max(len(a), len(b))):
        out += a[i:i + 1] + b[i:i + 1]
    return out


def _prompt_layer_kernel(xn_ref, xp_ref, p_ref, cos_ref, sin_ref, w_in_ref, lb_ref, ag_ref, bg_ref,
                         bb_ref, *rest, tiles_per_seq):
    tail_in = rest[:11]
    y_ref, sa_ref, sb_ref = rest[11:14]
    mixer_scratch = rest[14:27]
    proj_ref, mix_ref, dm_ref, rd_ref = mixer_scratch[0], mixer_scratch[3], mixer_scratch[11], mixer_scratch[12]
    xb_ref = rest[27]
    tail_scratch = rest[28:31]
    w_vmem, w_sem = rest[31:-1], rest[-1]
    tail_w = list(tail_in)
    for k, pos in enumerate(_TAIL_MATRIX_POS):
        tail_w[pos] = w_vmem[k]

    def weight_copies():
        return [pltpu.make_async_copy(tail_in[pos], w_vmem[k], w_sem.at[k])
                for k, pos in enumerate(_TAIL_MATRIX_POS)]
    g = pl.program_id(0)
    n_tiles = pl.num_programs(0) - 1
    slot = lax.rem(g, 2)

    def mixer_steps():
        return _prompt_mixer_steps(cos_ref, sin_ref, lb_ref, ag_ref, bg_ref, bb_ref, sa_ref, sb_ref,
                                   proj_ref.at[slot], *mixer_scratch[1:],
                                   first=lax.rem(g, tiles_per_seq) == 0)

    def tail_steps():
        return _tail_steps(xp_ref, mix_ref, p_ref, y_ref, *tail_w, *tail_scratch)

    @pl.when(g == 0)
    def _():
        for copy in weight_copies():
            copy.start()
        _ret_tables(dm_ref, rd_ref)
        prepass, kv_scan, diag, inter, hgrn_out, ret = mixer_steps()
        steps = _in_proj_steps(xp_ref, w_in_ref, proj_ref.at[slot], xb_ref)
        steps += [prepass] + kv_scan + diag + inter + [hgrn_out] + ret
        steps += _in_proj_steps(xn_ref, w_in_ref, proj_ref.at[1 - slot], xb_ref)
        for step in steps:
            step()

    @pl.when((g > 0) & (g < n_tiles))
    def _():
        @pl.when(g == 1)
        def _():
            for copy in weight_copies():
                copy.wait()

        prepass, kv_scan, diag, inter, hgrn_out, ret = mixer_steps()
        out_proj, ff, down, final = tail_steps()
        in_proj = _in_proj_steps(xn_ref, w_in_ref, proj_ref.at[1 - slot], xb_ref)
        inter_pairs = [lambda a=a, b=b: (a(), b()) for a, b in zip(inter[0::2], inter[1::2])]
        steps = [out_proj, in_proj[0], prepass, in_proj[1]]
        steps += _interleave(ff + down, kv_scan + diag + ret + inter_pairs + [hgrn_out])
        steps += _interleave(final, in_proj[2:])
        for step in steps:
            step()

    @pl.when(g == n_tiles)
    def _():
        out_proj, ff, down, final = tail_steps()
        for step in [out_proj] + ff + down + final:
            step()


def _prompt_layer(x, p, cos, sin, w_in, lb_logits, a_g, b_g, b_b, tail_w):
    bsz, seq, _ = x.shape
    tl = TOKEN_TILE
    tps = seq // tl
    n_tiles = bsz * tps
    assert n_tiles >= 2, "the tail weights are awaited in the second grid step"
    x2 = x.reshape(bsz * seq, D_MODEL)
    p2 = p.reshape(bsz * seq, PLE_DIM)
    const = lambda g: (0, 0)
    nxt = lambda g: (jnp.minimum(g + 1, n_tiles - 1), 0)
    prev = lambda g: (jnp.maximum(g - 1, 0), 0)
    seq_tile = lambda g: (lax.rem(jnp.minimum(g, n_tiles - 1), tps), 0)
    state_spec = pl.BlockSpec((1, N_HEADS, HEAD_DIM, HEAD_DIM),
                              lambda g: (jnp.minimum(g, n_tiles - 1) // tps, 0, 0, 0))
    state_shape = jax.ShapeDtypeStruct((bsz, N_HEADS, HEAD_DIM, HEAD_DIM), F32)
    return pl.pallas_call(
        functools.partial(_prompt_layer_kernel, tiles_per_seq=tps),
        grid=(n_tiles + 1,),
        in_specs=[
            pl.BlockSpec((tl, D_MODEL), nxt),
            pl.BlockSpec((tl, D_MODEL), prev),
            pl.BlockSpec((tl, PLE_DIM), prev),
            pl.BlockSpec((tl, HEAD_DIM), seq_tile),
            pl.BlockSpec((tl, HEAD_DIM), seq_tile),
            pl.BlockSpec((D_MODEL, IN_COLS), const, pipeline_mode=pl.Buffered(1)),
            pl.BlockSpec(lb_logits.shape, const),
            pl.BlockSpec((1, GROUP_W), const),
            pl.BlockSpec((1, GROUP_W), const),
            pl.BlockSpec((1, GROUP_W), const),
        ] + _tail_specs(),
        out_specs=[
            pl.BlockSpec((tl, D_MODEL), prev),
            state_spec,
            state_spec,
        ],
        out_shape=[
            jax.ShapeDtypeStruct((bsz * seq, D_MODEL), F32),
            state_shape,
            state_shape,
        ],
        scratch_shapes=[
            pltpu.VMEM((2, tl, IN_COLS), F32),
            pltpu.VMEM((N_HEADS, HEAD_DIM, HEAD_DIM), F32),
            pltpu.VMEM((tl, GROUP_W), F32),
            pltpu.VMEM((tl, 2 * GROUP_W), BF16),
            pltpu.VMEM((tl, GROUP_W), BF16),
            pltpu.VMEM((N_HEADS, HEAD_DIM, tl), BF16),
            pltpu.VMEM((tl, 2 * GROUP_W), BF16),
            pltpu.VMEM((tl // REF_CHUNK, GROUP_W), F32),
            pltpu.VMEM((tl, GROUP_W), BF16),
            pltpu.VMEM((tl // (2 * REF_CHUNK), HEAD_DIM, 2 * HEAD_DIM), F32),
            pltpu.VMEM((tl // REF_CHUNK, N_HEADS, HEAD_DIM, HEAD_DIM), BF16),
            pltpu.VMEM((N_HEADS, tl, tl), F32),
            pltpu.VMEM((2, N_HEADS, tl, HEAD_DIM), F32),
            pltpu.VMEM((tl, D_MODEL), BF16),
        ] + _tail_scratch(tl) + _tail_matrix_scratch(),
        compiler_params=pltpu.CompilerParams(
            dimension_semantics=("arbitrary",), vmem_limit_bytes=V7X_VMEM_LIMIT_BYTES),
        name="prompt_layer",
    )(x2, x2, p2, cos, sin, w_in, lb_logits, a_g, b_g, b_b, *tail_w)


def _cast_kernel(w_ref, wb_ref):
    wb_ref[...] = w_ref[...].astype(BF16)


def _w_in_bf16(w_in):
    tn = SAMPLE_PROJ_COLS
    return pl.pallas_call(
        _cast_kernel,
        grid=(IN_COLS // tn,),
        in_specs=[pl.BlockSpec((D_MODEL, tn), lambda c: (0, c))],
        out_specs=pl.BlockSpec((D_MODEL, tn), lambda c: (0, c)),
        out_shape=jax.ShapeDtypeStruct((D_MODEL, IN_COLS), BF16),
        compiler_params=pltpu.CompilerParams(
            dimension_semantics=("arbitrary",), vmem_limit_bytes=V7X_VMEM_LIMIT_BYTES),
        name="w_in_cast",
    )(w_in)


def _sample_rec_kernel(x_ref, w_in_ref, sa_in_ref, sb_in_ref, lb_ref, ag_ref, bg_ref, bb_ref,
                       cos_ref, sin_ref, mix_ref, sa_ref, sb_ref, proj_ref, oa_ref, ob_ref, *, seq_len):
    rows_n = proj_ref.shape[0]
    n_seq = rows_n // seq_len
    causal = _causal_in_chunk(rows_n, seq_len.bit_length() - 1)

    xb = x_ref[...].astype(BF16)

    def in_proj(c):
        cols = slice(c * GROUP_W, (c + 1) * GROUP_W)
        proj_ref[:, cols] = _dot(xb, w_in_ref[:, cols])

    half = IN_COLS // GROUP_W // 2
    assert half == N_HEADS
    for c in range(half):
        in_proj(c)

    q_dec, k_dec, kk, b = _hgrn_prepass(proj_ref, _lower_bound(lb_ref), causal)
    v_a = proj_ref[:, 2 * GROUP_W:3 * GROUP_W]
    for h in range(N_HEADS):
        hs = _head(h)
        in_proj(half + h)
        sc = jnp.where(causal, _dot_nt(q_dec[:, hs], k_dec[:, hs].astype(BF16)), 0.0).astype(BF16)
        oa_ref[:, hs] = _dot(sc, v_a[:, hs].astype(BF16))
    q_dec32 = q_dec.astype(F32)
    rr = lax.broadcasted_iota(jnp.int32, (seq_len, GROUP_W), 0)
    ones_blk = jnp.ones((seq_len, HEAD_DIM), BF16)
    for s in range(n_seq):
        rows = slice(s * seq_len, (s + 1) * seq_len)
        b_last = b[(s + 1) * seq_len - 1:(s + 1) * seq_len, :]
        k_end = (kk[rows] * jnp.exp(b_last - b[rows])).astype(BF16)
        hi, mid, lo = [t.astype(F32) for t in _split3(jnp.exp(b_last))]
        dec_rows = jnp.where(rr == 0, hi, jnp.where(rr == 1, mid, jnp.where(rr == 2, lo, 0.0)))
        dec_rows = dec_rows.astype(BF16)
        for h in range(N_HEADS):
            hs = _head(h)
            st = sa_in_ref[s, h]
            oa_ref[rows, hs] += _dot(q_dec32[rows, hs].astype(BF16), st.astype(BF16))
            dec_kv = _dot_tn(dec_rows[:, hs], ones_blk)
            sa_ref[s, h] = st * dec_kv + _dot_tn(k_end[:, hs], v_a[rows, hs].astype(BF16))
    for h in range(N_HEADS):
        hs = _head(h)
        mix_ref[:, hs] = _rms_gate(oa_ref[:, hs], ag_ref[:, hs], proj_ref[:, _head(h, 3)]).astype(BF16)

    cos = jnp.concatenate([cos_ref[...]] * n_seq, axis=0)
    sin = jnp.concatenate([sin_ref[...]] * n_seq, axis=0)
    r = lax.broadcasted_iota(jnp.int32, (rows_n, rows_n), 0)
    c = lax.broadcasted_iota(jnp.int32, (rows_n, rows_n), 1)
    diff = ((r & (seq_len - 1)) - (c & (seq_len - 1))).astype(F32)
    row = (lax.broadcasted_iota(jnp.int32, (rows_n, HEAD_DIM), 0) & (seq_len - 1)).astype(F32)
    for h in range(N_HEADS):
        hs = _head(h)
        logd = RET_LOG_DECAY[h]
        q = _rope(proj_ref[:, _head(h, 4)], cos, sin)
        k = _rope(proj_ref[:, _head(h, 5)], cos, sin) * K_SCALE
        v32 = proj_ref[:, _head(h, 6)]
        dmask = jnp.where(causal, jnp.exp(diff * logd), 0.0)
        a = (_dot_nt(q.astype(BF16), k.astype(BF16)) * dmask).astype(BF16)
        ob_ref[...] = _dot(a, v32.astype(BF16))
        q_dec_b = q * jnp.exp((row + 1.0) * logd)
        k_end_b = k * jnp.exp((seq_len - 1.0 - row) * logd)
        for s in range(n_seq):
            rows = slice(s * seq_len, (s + 1) * seq_len)
            st = sb_in_ref[s, h]
            ob_ref[rows, :] += _dot(q_dec_b[rows].astype(BF16), st.astype(BF16))
            sb_ref[s, h] = st * math.exp(seq_len * logd) + _dot_tn(
                k_end_b[rows].astype(BF16), v32[rows].astype(BF16))
        mix_ref[:, _head(h, 1)] = _ln_gate(ob_ref[...], bg_ref[:, hs], bb_ref[:, hs],
                                           proj_ref[:, _head(h, 7)]).astype(BF16)


def _sample_rec(x, w_in, sa, sb, lb_logits, a_g, b_g, b_b, cos, sin, seq_len):
    n_tok = x.shape[0]
    n_seq = n_tok // seq_len
    bs = SAMPLE_SEQS
    rows = bs * seq_len
    const = lambda i: (0, 0)
    state_spec = pl.BlockSpec((bs, N_HEADS, HEAD_DIM, HEAD_DIM), lambda i: (i, 0, 0, 0))
    state_shape = jax.ShapeDtypeStruct((n_seq, N_HEADS, HEAD_DIM, HEAD_DIM), F32)
    return pl.pallas_call(
        functools.partial(_sample_rec_kernel, seq_len=seq_len),
        grid=(n_seq // bs,),
        in_specs=[
            pl.BlockSpec((rows, D_MODEL), lambda i: (i, 0)),
            pl.BlockSpec((D_MODEL, IN_COLS), const, pipeline_mode=pl.Buffered(1)),
            state_spec,
            state_spec,
            pl.BlockSpec(lb_logits.shape, const),
            pl.BlockSpec((1, GROUP_W), const),
            pl.BlockSpec((1, GROUP_W), const),
            pl.BlockSpec((1, GROUP_W), const),
            pl.BlockSpec((seq_len, HEAD_DIM), const),
            pl.BlockSpec((seq_len, HEAD_DIM), const),
        ],
        out_specs=[
            pl.BlockSpec((rows, 2 * GROUP_W), lambda i: (i, 0)),
            state_spec,
            state_spec,
        ],
        out_shape=[
            jax.ShapeDtypeStruct((n_tok, 2 * GROUP_W), BF16),
            state_shape,
            state_shape,
        ],
        scratch_shapes=[
            pltpu.VMEM((rows, IN_COLS), F32),
            pltpu.VMEM((rows, GROUP_W), F32),
            pltpu.VMEM((rows, HEAD_DIM), F32),
        ],
        compiler_params=pltpu.CompilerParams(
            dimension_semantics=("arbitrary",), vmem_limit_bytes=V7X_VMEM_LIMIT_BYTES),
        name="sample_recurrence",
    )(x, w_in, sa, sb, lb_logits, a_g, b_g, b_b, cos, sin)


_TAIL_PHASE_STEPS = (D_MODEL // DOWN_CHUNK, D_FF // FF_CHUNK, D_MODEL // DOWN_CHUNK, D_MODEL // DOWN_CHUNK)
_TAIL_PHASE_START = tuple(sum(_TAIL_PHASE_STEPS[:i]) for i in range(4))


def _sample_tail_kernel(x_ref, mix_ref, p_ref, wo_ref, wg_ref, wu_ref, wd_ref, wpg_ref, wpp_ref,
                        ln1g_ref, ln1b_ref, ln2g_ref, ln2b_ref, bpg_ref,
                        y_ref, wo_b, wg_b, wu_b, wd_b, wpg_b, wpp_b, act_ref, h_ref, hb_ref):
    s = pl.program_id(0)
    a0, b0, c0, d0 = _TAIL_PHASE_START
    blk = DOWN_CHUNK

    @pl.when(s < b0)
    def _():
        wo_b[...] = wo_ref[...].astype(BF16)
        cols = pl.ds(pl.multiple_of((s - a0) * blk, blk), blk)
        h_ref[:, cols] = DN_ALPHA * x_ref[...] + _dot(mix_ref[...], wo_b[...])

        @pl.when(s == b0 - 1)
        def _():
            h = _layer_norm(h_ref[...], ln1g_ref[...], ln1b_ref[...])
            h_ref[...] = h
            hb_ref[...] = h.astype(BF16)

    @pl.when((s >= b0) & (s < c0))
    def _():
        wg_b[...] = wg_ref[...].astype(BF16)
        wu_b[...] = wu_ref[...].astype(BF16)
        cols = pl.ds(pl.multiple_of((s - b0) * FF_CHUNK, FF_CHUNK), FF_CHUNK)
        hb = hb_ref[...]
        act_ref[:, cols] = (_silu(_dot(hb, wg_b[...])) * _dot(hb, wu_b[...])).astype(BF16)

    @pl.when((s >= c0) & (s < d0))
    def _():
        wd_b[...] = wd_ref[...].astype(BF16)
        cols = pl.ds(pl.multiple_of((s - c0) * blk, blk), blk)
        h_ref[:, cols] = DN_ALPHA * h_ref[:, cols] + _dot(act_ref[...], wd_b[...])

        @pl.when(s == d0 - 1)
        def _():
            h2 = _layer_norm(h_ref[...], ln2g_ref[...], ln2b_ref[...])
            h_ref[...] = h2
            hb_ref[...] = h2.astype(BF16)

    @pl.when(s >= d0)
    def _():
        wpg_b[...] = wpg_ref[...].astype(BF16)
        wpp_b[...] = wpp_ref[...].astype(BF16)
        cols = pl.ds(pl.multiple_of((s - d0) * blk, blk), blk)
        gate = _sigmoid(_dot(hb_ref[...], wpg_b[...]) + bpg_ref[:, cols])
        y_ref[...] = h_ref[:, cols] + gate * _dot(p_ref[...].astype(BF16), wpp_b[...])


def _sample_tail(x, mix, p, w_out, ln1g, ln1b, wg, wu, wd, ln2g, ln2b, wpp, wpg, bpg):
    n = x.shape[0]
    a0, b0, c0, d0 = _TAIL_PHASE_START
    na, nb, nc, nd = _TAIL_PHASE_STEPS
    const = lambda s: (0, 0)
    col = lambda start, count: (lambda s: (0, jnp.clip(s - start, 0, count - 1)))
    vec = pl.BlockSpec((1, D_MODEL), const)
    weight_specs = [
        pl.BlockSpec((2 * GROUP_W, DOWN_CHUNK), col(a0, na)),
        pl.BlockSpec((D_MODEL, FF_CHUNK), col(b0, nb)),
        pl.BlockSpec((D_MODEL, FF_CHUNK), col(b0, nb)),
        pl.BlockSpec((D_FF, DOWN_CHUNK), col(c0, nc)),
        pl.BlockSpec((D_MODEL, DOWN_CHUNK), col(d0, nd)),
        pl.BlockSpec((PLE_DIM, DOWN_CHUNK), col(d0, nd)),
    ]
    weights = (w_out, wg, wu, wd, wpg, wpp)
    return pl.pallas_call(
        _sample_tail_kernel,
        grid=(sum(_TAIL_PHASE_STEPS),),
        in_specs=[
            pl.BlockSpec((n, DOWN_CHUNK), col(a0, na)),
            pl.BlockSpec((n, 2 * GROUP_W), const),
            pl.BlockSpec((n, PLE_DIM), const),
        ] + weight_specs + [vec, vec, vec, vec, vec],
        out_specs=[pl.BlockSpec((n, DOWN_CHUNK), col(d0, nd))] + weight_specs,
        out_shape=[jax.ShapeDtypeStruct((n, D_MODEL), F32)]
        + [jax.ShapeDtypeStruct(w.shape, BF16) for w in weights],
        scratch_shapes=_tail_scratch(n),
        compiler_params=pltpu.CompilerParams(
            dimension_semantics=("arbitrary",), vmem_limit_bytes=V7X_VMEM_LIMIT_BYTES),
        name="sample_tail",
    )(x, mix, p, *weights, ln1g, ln1b, ln2g, ln2b, bpg)


def kernel(x_prompt, x_sample, p_prompt, p_sample, state_hgrn, state_ret, lb_logits, w_in, a_norm_g, b_norm_g, b_norm_b, w_out, ln1_g, ln1_b, w_ffn_gate, w_ffn_up, w_ffn_down, ln2_g, ln2_b, w_ple_proj, w_ple_gate, b_ple_gate):
    assert w_in.shape[0] == DEPTH == 1
    bsz, seq, _ = x_prompt.shape
    n_dec, dec_seq, _ = x_sample.shape

    mixer_vecs = (lb_logits, a_norm_g, b_norm_g, b_norm_b)
    cos_p, sin_p = _rope_tables(seq, 0)
    cos_s, sin_s = _rope_tables(dec_seq, PAST_LEN)

    x_s = x_sample.reshape(n_dec * dec_seq, D_MODEL)
    w_in_b = _w_in_bf16(w_in[0])
    mix_s, sa_s, sb_s = _sample_rec(x_s, w_in_b, state_hgrn[0], state_ret[0], *mixer_vecs,
                                    cos_s, sin_s, dec_seq)
    y_s, w_out_b, wg_b, wu_b, wd_b, wpg_b, wpp_b = _sample_tail(
        x_s, mix_s, p_sample[0].reshape(n_dec * dec_seq, PLE_DIM), w_out[0], ln1_g, ln1_b,
        w_ffn_gate[0], w_ffn_up[0], w_ffn_down[0], ln2_g, ln2_b, w_ple_proj[0], w_ple_gate[0],
        b_ple_gate)

    tail_w = (w_out_b, ln1_g, ln1_b, wg_b, wu_b, wd_b, ln2_g, ln2_b, wpp_b, wpg_b, b_ple_gate)
    y_p, sa_p, sb_p = _prompt_layer(x_prompt, p_prompt[0], cos_p, sin_p, w_in_b, *mixer_vecs, tail_w)

    return (y_p.reshape(bsz, seq, D_MODEL), y_s.reshape(n_dec, dec_seq, D_MODEL),
            sa_p[None], sb_p[None], sa_s[None], sb_s[None])
```
